```python
import math
import jax
import jax.numpy as jnp
from jax import lax
import numpy as np

D_MODEL = 1024
BATCH = 8
SEQ = 4096
DEPTH = 4

N_META = 16
M_HEADS = 4
M_HEAD_DIM = D_MODEL // 8
M_WIDTH = M_HEADS * M_HEAD_DIM
M_CHUNK = 64
CONV_WIDTH = 4
A_HEADS = 8
A_HEAD_DIM = D_MODEL // 16
A_WIDTH = A_HEADS * A_HEAD_DIM
KV_RANK = D_MODEL // 8
IDX_HEADS = 4
IDX_DIM = 64
IDX_SCALE = (IDX_HEADS * IDX_DIM) ** -0.5
TOPK_MAX = 256
Q_BLOCK = 128
REL_BUCKETS = 32
REL_MAX_EXACT = 16
REL_MAX_DIST = 128
N_EXPERTS = 16
N_GROUPS = 4
TOPK_EXPERTS = 2
D_EXPERT = D_MODEL // 2
DN_ALPHA = (2 * DEPTH) ** 0.25
DN_BETA = (8 * DEPTH) ** -0.25
IN_SPLITS = (M_WIDTH, M_WIDTH, M_WIDTH, M_WIDTH, M_HEADS, M_HEADS,
             A_WIDTH, KV_RANK, IDX_HEADS * IDX_DIM, IDX_DIM, IDX_HEADS,
             D_MODEL, D_MODEL)
N_IN = sum(IN_SPLITS)

kernel_name = 'hybrid_mlstm_dsa_moe_deepnorm'


def _layer_norm(x, g, b, eps=1e-5):
    xf = x.astype(jnp.float32)
    mu = xf.mean(-1, keepdims=True)
    var = jnp.mean(jnp.square(xf - mu), -1, keepdims=True)
    return ((xf - mu) * lax.rsqrt(var + eps) * g + b).astype(x.dtype)


def _rms_norm(x, g, eps=1e-6):
    xf = x.astype(jnp.float32)
    return (xf * lax.rsqrt(jnp.mean(xf * xf, -1, keepdims=True) + eps) * g).astype(x.dtype)


def _head_norm(h, eps=1e-5):
    mu = h.mean(-1, keepdims=True)
    var = jnp.mean(jnp.square(h - mu), -1, keepdims=True)
    return (h - mu) * lax.rsqrt(var + eps)


def _causal_dwconv(x, w):
    c = x.shape[-1]
    return lax.conv_general_dilated(x, w[:, None, :].astype(x.dtype), window_strides=(1,),
                                    padding=[(w.shape[0] - 1, 0)],
                                    dimension_numbers=('NWC', 'WIO', 'NWC'),
                                    feature_group_count=c)


def _rel_bucket(dist):
    n = jnp.maximum(dist, 0)
    nf = jnp.maximum(n, REL_MAX_EXACT).astype(jnp.float32)
    large = REL_MAX_EXACT + (jnp.log(nf / REL_MAX_EXACT) / math.log(REL_MAX_DIST / REL_MAX_EXACT)
                             * (REL_BUCKETS - REL_MAX_EXACT)).astype(jnp.int32)
    large = jnp.minimum(large, REL_BUCKETS - 1)
    return jnp.where(n < REL_MAX_EXACT, n, large)


def _mlstm_chunk(state, inp):
    c_st, n_st, m_st = state
    q, k, v, li, lf = inp
    L = q.shape[2]
    b = jnp.cumsum(lf, axis=-1)
    causal = jnp.tril(jnp.ones((L, L), bool))
    d_mat = jnp.where(causal, b[..., :, None] - b[..., None, :] + li[..., None, :], -jnp.inf)
    inter = b + m_st[..., None]
    m_t = jnp.maximum(inter, d_mat.max(-1))
    a = jnp.exp(inter - m_t)
    w = jnp.exp(d_mat - m_t[..., None]) * jnp.einsum('bhtd,bhsd->bhts', q, k)
    num = a[..., None] * jnp.einsum('bhvk,bhtk->bhtv', c_st, q) + jnp.einsum('bhts,bhsv->bhtv', w, v)
    den = a * jnp.einsum('bhk,bhtk->bht', n_st, q) + w.sum(-1)
    h = num / jnp.maximum(jnp.abs(den), jnp.exp(-m_t))[..., None]
    b_last = b[..., -1]
    g = b_last[..., None] - b + li
    m_new = jnp.maximum(b_last + m_st, g.max(-1))
    decay = jnp.exp(b_last + m_st - m_new)
    wk = jnp.exp(g - m_new[..., None])
    c_new = decay[..., None, None] * c_st + jnp.einsum('bhs,bhsv,bhsk->bhvk', wk, v, k)
    n_new = decay[..., None] * n_st + jnp.einsum('bhs,bhsk->bhk', wk, k)
    return (c_new, n_new, m_new), h


def _mlstm_branch(q, k, v, i_pre, f_pre):
    B, T, _ = q.shape
    f32 = jnp.float32
    heads = lambda z: z.astype(f32).reshape(B, T, M_HEADS, M_HEAD_DIM).transpose(0, 2, 1, 3)
    q = heads(q) * (M_HEAD_DIM ** -0.5)
    k = heads(k)
    v = heads(v)
    li = i_pre.astype(f32).transpose(0, 2, 1)
    lf = jax.nn.log_sigmoid(f_pre.astype(f32)).transpose(0, 2, 1)
    state0 = (jnp.zeros((B, M_HEADS, M_HEAD_DIM, M_HEAD_DIM), f32),
              jnp.zeros((B, M_HEADS, M_HEAD_DIM), f32),
              jnp.zeros((B, M_HEADS), f32))
    state, h_meta = _mlstm_chunk(state0, tuple(z[:, :, :N_META] for z in (q, k, v, li, lf)))
    n_chunks = (T - N_META) // M_CHUNK

    def to_chunks(z):
        z = z[:, :, N_META:]
        return jnp.moveaxis(z.reshape(B, M_HEADS, n_chunks, M_CHUNK, *z.shape[3:]), 2, 0)

    _, h_real = lax.scan(_mlstm_chunk, state, tuple(to_chunks(z) for z in (q, k, v, li, lf)))
    h_real = jnp.moveaxis(h_real, 0, 2).reshape(B, M_HEADS, T - N_META, M_HEAD_DIM)
    return jnp.concatenate([h_meta, h_real], axis=2).transpose(0, 2, 1, 3)


def _dsa_branch(q_a, c_kv, q_idx, k_idx, w_idx, w_uk, w_uv, rel_bias):
    B, T, _ = q_a.shape
    S = T - N_META
    top_k = min(TOPK_MAX, S // 4)
    q_a = q_a.reshape(B, T, A_HEADS, A_HEAD_DIM)
    q_lat = jnp.einsum('bthd,hrd->bthr', q_a, w_uk) * (A_HEAD_DIM ** -0.5)
    c_meta, c_real = c_kv[:, :N_META], c_kv[:, N_META:]
    k_idx_real = k_idx[:, N_META:]
    q_idx = q_idx.reshape(B, T, IDX_HEADS, IDX_DIM)
    w_idx = w_idx * IDX_SCALE
    meta_pos = jnp.arange(N_META)
    real_pos = jnp.arange(S)

    lm = jnp.einsum('bqhr,bmr->bhqm', q_lat[:, :N_META], c_meta) + \
        rel_bias[_rel_bucket(meta_pos[:, None] - meta_pos[None, :])].transpose(2, 0, 1)
    lm = jnp.where(meta_pos[:, None] >= meta_pos[None, :], lm.astype(jnp.float32), -jnp.inf)
    pm = jax.nn.softmax(lm, axis=-1).astype(c_kv.dtype)
    o_meta = jnp.einsum('bhqm,bmr->bqhr', pm, c_meta)

    n_blocks = S // Q_BLOCK

    def blocks(z):
        z = z[:, N_META:]
        return jnp.moveaxis(z.reshape(B, n_blocks, Q_BLOCK, *z.shape[2:]), 1, 0)

    def block_fn(args):
        ql, qi, wi, t0 = args
        t = t0 + jnp.arange(Q_BLOCK)
        sc = jax.nn.relu(jnp.einsum('bqhd,bsd->bqhs', qi, k_idx_real))
        idx_score = jnp.einsum('bqh,bqhs->bqs', wi, sc).astype(jnp.float32)
        idx_score = jnp.where(real_pos[None, None, :] <= t[None, :, None], idx_score, -jnp.inf)
        _, sel = lax.top_k(idx_score, top_k)
        c_sel = jax.vmap(lambda c, i: c[i])(c_real, sel)
        valid = sel <= t[None, :, None]
        l_sel = jnp.einsum('bqhr,bqkr->bqhk', ql, c_sel) + \
            rel_bias[_rel_bucket(t[None, :, None] - sel)].transpose(0, 1, 3, 2)
        l_sel = jnp.where(valid[:, :, None, :], l_sel.astype(jnp.float32), -jnp.inf)
        l_meta = jnp.einsum('bqhr,bmr->bqhm', ql, c_meta) + \
            rel_bias[_rel_bucket(t[:, None] + N_META - meta_pos[None, :])].transpose(0, 2, 1)[None]
        p = jax.nn.softmax(jnp.concatenate([l_meta.astype(jnp.float32), l_sel], axis=-1), axis=-1)
        p = p.astype(c_kv.dtype)
        return jnp.einsum('bqhm,bmr->bqhr', p[..., :N_META], c_meta) + \
            jnp.einsum('bqhk,bqkr->bqhr', p[..., N_META:], c_sel)

    o_real = lax.map(block_fn, (blocks(q_lat), blocks(q_idx), blocks(w_idx),
                                jnp.arange(n_blocks, dtype=jnp.int32) * Q_BLOCK))
    o_real = jnp.moveaxis(o_real, 0, 1).reshape(B, S, A_HEADS, KV_RANK)
    o_lat = jnp.concatenate([o_meta, o_real], axis=1)
    return jnp.einsum('bthr,hrd->bthd', o_lat, w_uv).reshape(B, T, A_WIDTH)


def _moe(h, w_router, b_router, w_gate, w_up, w_down):
    B, T, D = h.shape
    hf = h.reshape(-1, D)
    n_tok = hf.shape[0]
    scores = jax.nn.sigmoid((hf @ w_router).astype(jnp.float32))
    sel_score = scores + b_router
    grouped = sel_score.reshape(n_tok, N_GROUPS, N_EXPERTS // N_GROUPS)
    group_score = lax.top_k(grouped, TOPK_EXPERTS)[0].sum(-1)
    best_group = jnp.argmax(group_score, axis=-1)
    in_group = (jnp.arange(N_EXPERTS) // (N_EXPERTS // N_GROUPS))[None, :] == best_group[:, None]
    _, idx = lax.top_k(jnp.where(in_group, sel_score, -jnp.inf), TOPK_EXPERTS)
    w_sel = jnp.take_along_axis(scores, idx, axis=-1)
    w_sel = w_sel / w_sel.sum(-1, keepdims=True)
    combine = (jax.nn.one_hot(idx, N_EXPERTS, dtype=jnp.float32) * w_sel[..., None]).sum(1)
    y = jnp.zeros((n_tok, D), jnp.float32)
    for e in range(N_EXPERTS):
        he = jax.nn.silu(hf @ w_gate[e]) * (hf @ w_up[e])
        y = y + combine[:, e:e + 1] * (he @ w_down[e])
    return y.reshape(B, T, D).astype(h.dtype)


def setup_inputs(seed: int = 0) -> dict:
    key = jax.random.key(seed)
    ks = jax.random.split(key, 26)
    nrm = lambda k, shape, s: jax.random.normal(k, shape, jnp.float32) * s
    gain = lambda k, shape: 1.0 + nrm(k, shape, 0.02)
    L, D, E = DEPTH, D_MODEL, N_EXPERTS
    b_if = jnp.concatenate([nrm(ks[5], (L, M_HEADS), 0.1),
                            jnp.linspace(3.0, 6.0, M_HEADS)[None] + nrm(ks[6], (L, M_HEADS), 0.1)], axis=-1)
    return {
        'x': nrm(ks[0], (BATCH, SEQ, D), 1.0),
        'meta_tokens': nrm(ks[1], (N_META, D), 1.0),
        'ln_in_g': gain(ks[2], (D,)),
        'ln_in_b': nrm(ks[3], (D,), 0.02),
        'w_in': nrm(ks[4], (L, D, N_IN), D ** -0.5),
        'conv_w': nrm(ks[7], (L, CONV_WIDTH, 2 * M_WIDTH), CONV_WIDTH ** -0.5),
        'b_if': b_if,
        'mnorm_g': gain(ks[8], (L, M_WIDTH)),
        'kv_norm_g': gain(ks[9], (L, KV_RANK)),
        'w_uk': nrm(ks[10], (L, A_HEADS, KV_RANK, A_HEAD_DIM), KV_RANK ** -0.5),
        'w_uv': nrm(ks[11], (L, A_HEADS, KV_RANK, A_HEAD_DIM), KV_RANK ** -0.5),
        'w_branch_m': nrm(ks[12], (L, M_WIDTH, D), M_WIDTH ** -0.5),
        'w_branch_a': nrm(ks[13], (L, A_WIDTH, D), A_WIDTH ** -0.5),
        'w_out': nrm(ks[14], (L, D, D), D ** -0.5 * DN_BETA),
        'ln1_g': gain(ks[15], (L, D)),
        'ln1_b': nrm(ks[16], (L, D), 0.02),
        'w_router': nrm(ks[17], (D, E), D ** -0.5),
        'b_router': nrm(ks[18], (E,), 0.01),
        'w_gate': nrm(ks[19], (L, E, D, D_EXPERT), D ** -0.5),
        'w_up': nrm(ks[20], (L, E, D, D_EXPERT), D ** -0.5),
        'w_down': nrm(ks[21], (L, E, D_EXPERT, D), D_EXPERT ** -0.5 * DN_BETA),
        'ln2_g': gain(ks[22], (L, D)),
        'ln2_b': nrm(ks[23], (L, D), 0.02),
        'rel_bias': nrm(ks[24], (REL_BUCKETS, A_HEADS), 0.2),
    }


def reference(x, meta_tokens, ln_in_g, ln_in_b, w_in, conv_w, b_if, mnorm_g, kv_norm_g,
              w_uk, w_uv, w_branch_m, w_branch_a, w_out, ln1_g, ln1_b, w_router, b_router,
              w_gate, w_up, w_down, ln2_g, ln2_b, rel_bias):
    B = x.shape[0]
    h = jnp.concatenate([jnp.broadcast_to(meta_tokens.astype(x.dtype)[None], (B, N_META, D_MODEL)), x], axis=1)
    h = _layer_norm(h, ln_in_g, ln_in_b)
    T = h.shape[1]
    offsets = np.cumsum(IN_SPLITS)[:-1].tolist()
    for l in range(DEPTH):
        p = h @ w_in[l]
        (q_m, k_m, v_m, o_m, i_pre, f_pre, q_a, c_kv, q_idx, k_idx, w_idx, g_m, g_a) = \
            jnp.split(p, offsets, axis=-1)
        qk = jax.nn.silu(_causal_dwconv(jnp.concatenate([q_m, k_m], axis=-1), conv_w[l]))
        q_m, k_m = jnp.split(qk, 2, axis=-1)
        h_m = _mlstm_branch(q_m, k_m, v_m, i_pre + b_if[l, :M_HEADS], f_pre + b_if[l, M_HEADS:])
        h_m = (_head_norm(h_m).reshape(B, T, M_WIDTH) * mnorm_g[l]
               * jax.nn.sigmoid(o_m.astype(jnp.float32))).astype(h.dtype)
        h_a = _dsa_branch(q_a, _rms_norm(c_kv, kv_norm_g[l]), q_idx, k_idx, w_idx,
                          w_uk[l], w_uv[l], rel_bias).astype(h.dtype)
        y = jax.nn.sigmoid(g_m) * (h_m @ w_branch_m[l]) + jax.nn.sigmoid(g_a) * (h_a @ w_branch_a[l])
        h = _layer_norm(DN_ALPHA * h + y @ w_out[l], ln1_g[l], ln1_b[l])
        h = _layer_norm(DN_ALPHA * h + _moe(h, w_router, b_router, w_gate[l], w_up[l], w_down[l]),
                        ln2_g[l], ln2_b[l])
    return h[:, N_META:]
```

```python
import functools
import math

import numpy as np
import jax
import jax.numpy as jnp
from jax import lax
from jax.experimental import pallas as pl
from jax.experimental.pallas import tpu as pltpu

F32 = jnp.float32
BF16 = jnp.bfloat16

N_META = 16
M_HEADS = 4
M_HEAD_DIM = 128
M_WIDTH = M_HEADS * M_HEAD_DIM
CONV_WIDTH = 4
A_HEADS = 8
A_HEAD_DIM = 64
A_WIDTH = A_HEADS * A_HEAD_DIM
KV_RANK = 128
IDX_HEADS = 4
IDX_DIM = 64
IDX_SCALE = (IDX_HEADS * IDX_DIM) ** -0.5
TOPK_MAX = 256
REL_BUCKETS = 32
REL_MAX_EXACT = 16
REL_MAX_DIST = 128
N_EXPERTS = 16
N_GROUPS = 4
GROUP_SIZE = N_EXPERTS // N_GROUPS

LANES = 128
ROW_TILE = 256
MOE_ROW_TILE = 768
M_CHUNK = 256
SCORE_CHUNK = 4
VMEM_LIMIT = 56 * 1024 * 1024
NEG_BIG = -1e30
INT_MIN = -2 ** 31

PK_PM = 0
PK_QA = 2048
PK_QI = 2560
PK_CKV = 2816
PK_SM = 2944
PK_G = 3072
PK_TOTAL = 5120
SM_KIDX = 0
SM_WIDX = 64
SM_IPRE = 68
SM_FPRE = 72


def _params(sem):
    return pltpu.CompilerParams(dimension_semantics=sem, vmem_limit_bytes=VMEM_LIMIT)


def _sigmoid(x):
    return 1.0 / (1.0 + jnp.exp(-x))


def _layer_norm_rows(x, g, b, eps):
    mu = jnp.mean(x, axis=-1, keepdims=True)
    xc = x - mu
    var = jnp.mean(xc * xc, axis=-1, keepdims=True)
    return xc * lax.rsqrt(var + eps) * g + b


def _ln_kernel(x_ref, g_ref, b_ref, o_ref):
    o_ref[...] = _layer_norm_rows(x_ref[...], g_ref[...], b_ref[...], 1e-5)


def _input_ln(x, g, b):
    n, d = x.shape
    return pl.pallas_call(
        _ln_kernel,
        grid=(n // ROW_TILE,),
        in_specs=[pl.BlockSpec((ROW_TILE, d), lambda r: (r, 0)),
                  pl.BlockSpec((1, d), lambda r: (0, 0)),
                  pl.BlockSpec((1, d), lambda r: (0, 0))],
        out_specs=pl.BlockSpec((ROW_TILE, d), lambda r: (r, 0)),
        out_shape=jax.ShapeDtypeStruct((n, d), F32),
        compiler_params=_params(("parallel",)),
        name="input_ln",
    )(x, g.reshape(1, d), b.reshape(1, d))


def _proj_kernel(h_ref, w_ref, kvg_ref, brow_ref, pm_ref, qa_ref, qi_ref, c_ref, sm_ref, g_ref):
    x = h_ref[...].astype(BF16)

    def mm(lo, width):
        return jnp.dot(x, w_ref[:, lo:lo + width], preferred_element_type=F32)

    pm_ref[...] = mm(PK_PM, 2048).astype(BF16)
    qa_ref[...] = mm(PK_QA, 512).astype(BF16)
    qi_ref[...] = mm(PK_QI, 256).astype(BF16)
    ckv = mm(PK_CKV, 128)
    c_ref[...] = ckv * lax.rsqrt(jnp.mean(ckv * ckv, axis=-1, keepdims=True) + 1e-6) * kvg_ref[...]
    sm_ref[...] = mm(PK_SM, 128) + brow_ref[...]
    g_ref[...] = mm(PK_G, 2048).astype(BF16)


def _pack_w_in(w):
    d = w.shape[0]
    cols = [w[:, 0:2048], w[:, 2056:2568], w[:, 2696:2952], w[:, 2568:2696],
            w[:, 2952:3016], w[:, 3016:3020], w[:, 2048:2056],
            jnp.zeros((d, LANES - 76), w.dtype), w[:, 3020:5068]]
    return jnp.concatenate(cols, axis=1).astype(BF16)


def _project(h, w_packed, kv_g, b_if):
    n, d = h.shape
    brow = jnp.concatenate([jnp.zeros((SM_IPRE,), F32), b_if.astype(F32),
                            jnp.zeros((LANES - SM_IPRE - 2 * M_HEADS,), F32)]).reshape(1, LANES)
    row = lambda width: pl.BlockSpec((ROW_TILE, width), lambda r: (r, 0))
    shp = lambda width, dt: jax.ShapeDtypeStruct((n, width), dt)
    return pl.pallas_call(
        _proj_kernel,
        grid=(n // ROW_TILE,),
        in_specs=[row(d),
                  pl.BlockSpec((d, PK_TOTAL), lambda r: (0, 0)),
                  pl.BlockSpec((1, KV_RANK), lambda r: (0, 0)),
                  pl.BlockSpec((1, LANES), lambda r: (0, 0))],
        out_specs=[row(2048), row(512), row(256), row(128), row(128), row(2048)],
        out_shape=[shp(2048, BF16), shp(512, BF16), shp(256, BF16), shp(128, F32), shp(128, F32),
                   shp(2048, BF16)],
        compiler_params=_params(("parallel",)),
        name="in_proj",
    )(h, w_packed, kv_g.reshape(1, KV_RANK), brow)


def _mlstm_chunk_math(q, k, v, li, lf, ct, n, m):
    L = q.shape[0]
    r = lax.broadcasted_iota(jnp.int32, (L, L), 0)
    c = lax.broadcasted_iota(jnp.int32, (L, L), 1)
    eye = r == c
    tril = c <= r
    lf_b = jnp.broadcast_to(lf, (L, L))
    li_b = jnp.broadcast_to(li, (L, L))
    lf_row = jnp.sum(jnp.where(eye, lf_b, 0.0), axis=0, keepdims=True)
    li_row = jnp.sum(jnp.where(eye, li_b, 0.0), axis=0, keepdims=True)
    b_col = jnp.sum(jnp.where(tril, jnp.broadcast_to(lf_row, (L, L)), 0.0), axis=1, keepdims=True)
    b_row = jnp.sum(jnp.where(r <= c, lf_b, 0.0), axis=0, keepdims=True)
    d = jnp.where(tril, b_col - b_row + li_row, -jnp.inf)
    inter = b_col + m
    m_t = jnp.maximum(inter, jnp.max(d, axis=1, keepdims=True))
    a = jnp.exp(inter - m_t)
    qb, kb, vb = q.astype(BF16), k.astype(BF16), v.astype(BF16)
    qk = lax.dot_general(qb, kb, (((1,), (1,)), ((), ())), preferred_element_type=F32)
    w = jnp.exp(d - m_t) * qk
    num = a * jnp.dot(qb, ct.astype(BF16), preferred_element_type=F32) + \
        jnp.dot(w.astype(BF16), vb, preferred_element_type=F32)
    den = a * jnp.sum(q * n, axis=1, keepdims=True) + jnp.sum(w, axis=1, keepdims=True)
    h = num / jnp.maximum(jnp.abs(den), jnp.exp(-m_t))
    b_last = b_col[L - 1:L, :]
    g = b_last - b_col + li
    m_new = jnp.maximum(b_last + m, jnp.max(g, axis=0, keepdims=True))
    decay = jnp.exp(b_last + m - m_new)
    kw = k * jnp.exp(g - m_new)
    ct_new = decay * ct + lax.dot_general(kw.astype(BF16), vb, (((0,), (0,)), ((), ())),
                                          preferred_element_type=F32)
    n_new = decay * n + jnp.sum(kw, axis=0, keepdims=True)
    return h, ct_new, n_new, m_new


def _mlstm_kernel(pm_ref, sm_ref, pmm_ref, smm_ref, cw_ref, mg_ref, out_ref, outm_ref,
                  ct_scr, n_scr, m_scr, x_scr):
    ci = pl.program_id(1)
    tail = 8

    def run_chunk(p_ref, s_ref, o_ref):
        L = p_ref.shape[0]
        x_scr[tail:tail + L, :] = p_ref[:, 0:2 * M_WIDTH].astype(F32)
        conv = cw_ref[0:1, :] * x_scr[tail - 3:tail - 3 + L, :]
        for j in range(1, CONV_WIDTH):
            conv = conv + cw_ref[j:j + 1, :] * x_scr[tail - 3 + j:tail - 3 + j + L, :]
        x_scr[0:tail, :] = x_scr[L:L + tail, :]
        qk = conv * _sigmoid(conv)
        for hd in range(M_HEADS):
            lo = hd * M_HEAD_DIM
            q = qk[:, lo:lo + M_HEAD_DIM] * (M_HEAD_DIM ** -0.5)
            k = qk[:, M_WIDTH + lo:M_WIDTH + lo + M_HEAD_DIM]
            v = p_ref[:, 2 * M_WIDTH + lo:2 * M_WIDTH + lo + M_HEAD_DIM].astype(F32)
            li = s_ref[:, SM_IPRE + hd:SM_IPRE + hd + 1]
            f = s_ref[:, SM_FPRE + hd:SM_FPRE + hd + 1]
            lf = jnp.minimum(f, 0.0) - jnp.log1p(jnp.exp(-jnp.abs(f)))
            h, ct_new, n_new, m_new = _mlstm_chunk_math(
                q, k, v, li, lf, ct_scr[hd], n_scr[hd:hd + 1, :], m_scr[hd:hd + 1, 0:1])
            ct_scr[hd] = ct_new
            n_scr[hd:hd + 1, :] = n_new
            m_scr[hd:hd + 1, :] = jnp.broadcast_to(m_new, (1, LANES))
            mu = jnp.mean(h, axis=-1, keepdims=True)
            hc = h - mu
            var = jnp.mean(hc * hc, axis=-1, keepdims=True)
            o_gate = _sigmoid(p_ref[:, 3 * M_WIDTH + lo:3 * M_WIDTH + lo + M_HEAD_DIM].astype(F32))
            o_ref[:, lo:lo + M_HEAD_DIM] = (hc * lax.rsqrt(var + 1e-5) * mg_ref[:, lo:lo + M_HEAD_DIM]
                                            * o_gate).astype(o_ref.dtype)

    @pl.when(ci == 0)
    def _():
        ct_scr[...] = jnp.zeros_like(ct_scr)
        n_scr[...] = jnp.zeros_like(n_scr)
        m_scr[...] = jnp.zeros_like(m_scr)
        x_scr[...] = jnp.zeros_like(x_scr)
        run_chunk(pmm_ref, smm_ref, outm_ref)

    run_chunk(pm_ref, sm_ref, out_ref)


def _mlstm(pm, sm, conv_w, mnorm_g, batch, seq):
    n_real = batch * seq
    chunk = min(M_CHUNK, seq)
    nc = seq // chunk
    meta_blk = n_real // N_META
    out_real, out_meta = pl.pallas_call(
        _mlstm_kernel,
        grid=(batch, nc),
        in_specs=[pl.BlockSpec((chunk, 4 * M_WIDTH), lambda b, c: (b * nc + c, 0)),
                  pl.BlockSpec((chunk, LANES), lambda b, c: (b * nc + c, 0)),
                  pl.BlockSpec((N_META, 4 * M_WIDTH), lambda b, c: (meta_blk + b, 0)),
                  pl.BlockSpec((N_META, LANES), lambda b, c: (meta_blk + b, 0)),
                  pl.BlockSpec((CONV_WIDTH, 2 * M_WIDTH), lambda b, c: (0, 0)),
                  pl.BlockSpec((1, M_WIDTH), lambda b, c: (0, 0))],
        out_specs=[pl.BlockSpec((chunk, M_WIDTH), lambda b, c: (b * nc + c, 0)),
                   pl.BlockSpec((N_META, M_WIDTH), lambda b, c: (b, 0))],
        out_shape=[jax.ShapeDtypeStruct((n_real, M_WIDTH), BF16),
                   jax.ShapeDtypeStruct((batch * N_META, M_WIDTH), BF16)],
        scratch_shapes=[pltpu.VMEM((M_HEADS, M_HEAD_DIM, M_HEAD_DIM), F32),
                        pltpu.VMEM((8, LANES), F32),
                        pltpu.VMEM((8, LANES), F32),
                        pltpu.VMEM((chunk + 8, 2 * M_WIDTH), F32)],
        compiler_params=_params(("arbitrary", "arbitrary")),
        name="mlstm",
    )(pm, sm, pm, sm, conv_w.astype(F32), mnorm_g.reshape(1, M_WIDTH).astype(F32))
    return out_real, out_meta


def _rel_bucket_np(dist):
    n = np.maximum(dist, 0)
    nf = np.maximum(n, REL_MAX_EXACT).astype(np.float32)
    large = REL_MAX_EXACT + (np.log(nf / np.float32(REL_MAX_EXACT)) /
                             np.float32(math.log(REL_MAX_DIST / REL_MAX_EXACT))
                             * np.float32(REL_BUCKETS - REL_MAX_EXACT)).astype(np.int32)
    large = np.minimum(large, REL_BUCKETS - 1)
    return np.where(n < REL_MAX_EXACT, n, large).astype(np.int32)


def _bias_tables(rel_bias):
    q = np.arange(LANES)[:, None]
    k = np.arange(LANES)[None, :]
    far = 4 * LANES
    near_idx = np.stack([_rel_bucket_np(q - k), _rel_bucket_np(LANES + q - k),
                         _rel_bucket_np(np.full((LANES, LANES), far))])
    assert (_rel_bucket_np(np.arange(LANES + 1, far)) == REL_BUCKETS - 1).all()
    meta_idx = np.stack([_rel_bucket_np(q + N_META - np.minimum(k, N_META - 1)),
                         _rel_bucket_np(np.full((LANES, LANES), far))])
    mq = np.arange(N_META)[:, None]
    mm_idx = _rel_bucket_np(mq - np.minimum(k, N_META - 1))
    rb = rel_bias.astype(F32)
    gather = lambda idx: jnp.moveaxis(rb[idx], -1, -3)
    return gather(near_idx), gather(meta_idx), gather(mm_idx)


def _dsa_kernel(top_k, qa_ref, qi_ref, sm_ref, cb_ref, smb_ref, cm_ref, qam_ref,
                wuk_ref, wuv_ref, near_ref, metab_ref, mmb_ref,
                out_ref, outm_ref,
                keys_scr, cbf_scr, kbf_scr, cmeta_scr, qs_scr, m_scr, l_scr, acc_scr):
    i = pl.program_id(1)
    T = LANES
    col = lax.broadcasted_iota(jnp.int32, (T, T), 1)
    row = lax.broadcasted_iota(jnp.int32, (T, T), 0)

    def q_latent(qa, hd):
        ql = jnp.dot(qa[:, hd * A_HEAD_DIM:(hd + 1) * A_HEAD_DIM], wuk_ref[hd],
                     preferred_element_type=F32)
        return (ql * (A_HEAD_DIM ** -0.5)).astype(BF16)

    @pl.when(i == 0)
    def _():
        cbf_scr[...] = cb_ref[...].astype(BF16)
        kbf_scr[...] = smb_ref[:, SM_KIDX:SM_KIDX + IDX_DIM].astype(BF16)
        cmeta_scr[...] = jnp.zeros_like(cmeta_scr)
        cmeta_scr[0:N_META, :] = cm_ref[...].astype(BF16)
        cmk = cmeta_scr[...]
        qam = qam_ref[...]
        mrow = lax.broadcasted_iota(jnp.int32, (N_META, T), 0)
        mcol = lax.broadcasted_iota(jnp.int32, (N_META, T), 1)
        for hd in range(A_HEADS):
            lg = lax.dot_general(q_latent(qam, hd), cmk, (((1,), (1,)), ((), ())),
                                 preferred_element_type=F32) + mmb_ref[hd]
            lg = jnp.where(mcol <= mrow, lg, NEG_BIG)
            p = jnp.exp(lg - jnp.max(lg, axis=1, keepdims=True))
            p = p / jnp.sum(p, axis=1, keepdims=True)
            o = jnp.dot(p.astype(BF16), cmk, preferred_element_type=F32)
            outm_ref[:, hd * A_HEAD_DIM:(hd + 1) * A_HEAD_DIM] = jnp.dot(
                o.astype(BF16), wuv_ref[hd], preferred_element_type=F32).astype(outm_ref.dtype)

    qa = qa_ref[...]
    for hd in range(A_HEADS):
        qs_scr[hd * T:(hd + 1) * T, :] = q_latent(qa, hd)
    qi = qi_ref[...]
    wv = sm_ref[:, SM_WIDX:SM_WIDX + IDX_HEADS] * IDX_SCALE
    t_col = i * T + lax.broadcasted_iota(jnp.int32, (T, 1), 0)
    n_chunks = (i + SCORE_CHUNK) // SCORE_CHUNK
    CW = SCORE_CHUNK * T

    def score_body(cix, carry):
        kc = kbf_scr[pl.ds(pl.multiple_of(cix * CW, CW), CW), :]
        acc = jnp.zeros((T, CW), F32)
        for hh in range(IDX_HEADS):
            s = lax.dot_general(qi[:, hh * IDX_DIM:(hh + 1) * IDX_DIM], kc, (((1,), (1,)), ((), ())),
                                preferred_element_type=F32)
            acc = acc + wv[:, hh:hh + 1] * jnp.maximum(s, 0.0)
        acc = jnp.where(acc == 0.0, 0.0, acc)
        bits = lax.bitcast_convert_type(acc, jnp.int32)
        key = jnp.where(bits < 0, bits ^ jnp.int32(0x7FFFFFFF), bits)
        s_idx = cix * CW + lax.broadcasted_iota(jnp.int32, (T, CW), 1)
        key = jnp.where(s_idx <= t_col, key, jnp.int32(INT_MIN))
        for u in range(SCORE_CHUNK):
            keys_scr[cix * SCORE_CHUNK + u] = key[:, u * T:(u + 1) * T]
        return carry

    lax.fori_loop(0, n_chunks, score_body, 0)

    def count_where(pred_fn):
        def body(cix, cnt):
            for u in range(SCORE_CHUNK):
                cnt = cnt + jnp.where(pred_fn(keys_scr[cix * SCORE_CHUNK + u]), 1.0, 0.0)
            return cnt
        cnt = lax.fori_loop(0, n_chunks, body, jnp.zeros((T, T), F32))
        return jnp.sum(cnt, axis=1, keepdims=True)

    def bit_body(bi, ans):
        cand_u = ans | lax.shift_left(jnp.int32(1), jnp.int32(31) - bi)
        cand = cand_u ^ jnp.int32(INT_MIN)
        total = count_where(lambda kk: kk >= cand)
        return jnp.where(total >= float(top_k), cand_u, ans)

    ans = lax.fori_loop(0, 32, bit_body, jnp.zeros((T, 1), jnp.int32))
    thr = ans ^ jnp.int32(INT_MIN)
    need = float(top_k) - count_where(lambda kk: kk > thr)

    m_scr[...] = jnp.full_like(m_scr, NEG_BIG)
    l_scr[...] = jnp.zeros_like(l_scr)
    acc_scr[...] = jnp.zeros_like(acc_scr)

    def tile_update(c_tile, bias_fn, sel):
        for hd in range(A_HEADS):
            rs = slice(hd * T, (hd + 1) * T)
            lg = lax.dot_general(qs_scr[rs, :], c_tile, (((1,), (1,)), ((), ())),
                                 preferred_element_type=F32) + bias_fn(hd)
            lg = jnp.where(sel, lg, NEG_BIG)
            m_old = m_scr[rs, :]
            m_new = jnp.maximum(m_old, jnp.max(lg, axis=1, keepdims=True))
            alpha = jnp.exp(m_old - m_new)
            p = jnp.exp(lg - m_new)
            l_scr[rs, :] = alpha * l_scr[rs, :] + jnp.sum(p, axis=1, keepdims=True)
            acc_scr[rs, :] = alpha * acc_scr[rs, :] + jnp.dot(p.astype(BF16), c_tile,
                                                              preferred_element_type=F32)
            m_scr[rs, :] = m_new

    meta_sel = jnp.minimum(i, 1)
    tile_update(cmeta_scr[...], lambda hd: metab_ref[meta_sel, hd], col < N_META)

    upper = (row < col).astype(BF16)

    def key_body(j, seen):
        kk = keys_scr[j]
        eq = kk == thr
        eqf = jnp.where(eq, 1.0, 0.0)
        before = jnp.dot(eqf.astype(BF16), upper, preferred_element_type=F32) + seen
        causal = (j * T + col) <= (i * T + row)
        sel = causal & ((kk > thr) | (eq & (before < need)))
        c_tile = cbf_scr[pl.ds(pl.multiple_of(j * T, T), T), :]
        dsel = jnp.minimum(i - j, 2)
        tile_update(c_tile, lambda hd: near_ref[dsel, hd], sel)
        return seen + jnp.sum(eqf, axis=1, keepdims=True)

    lax.fori_loop(0, i + 1, key_body, jnp.zeros((T, 1), F32))

    for hd in range(A_HEADS):
        rs = slice(hd * T, (hd + 1) * T)
        o = acc_scr[rs, :] / l_scr[rs, :]
        out_ref[:, hd * A_HEAD_DIM:(hd + 1) * A_HEAD_DIM] = jnp.dot(
            o.astype(BF16), wuv_ref[hd], preferred_element_type=F32).astype(out_ref.dtype)


def _dsa(qa, qi, sm, c, w_uk, w_uv, tables, batch, seq):
    n_real = batch * seq
    nq = seq // LANES
    n_tiles = ((nq + SCORE_CHUNK - 1) // SCORE_CHUNK) * SCORE_CHUNK
    top_k = min(TOPK_MAX, seq // 4)
    meta_blk = n_real // N_META
    near, metab, mmb = tables
    wuk_t = jnp.swapaxes(w_uk, 1, 2).astype(BF16)
    wuv = w_uv.astype(BF16)
    full = lambda a: pl.BlockSpec(a.shape, lambda b, i: (0,) * a.ndim)
    assert seq % (SCORE_CHUNK * LANES) == 0
    out_real, out_meta = pl.pallas_call(
        functools.partial(_dsa_kernel, top_k),
        grid=(batch, nq),
        in_specs=[pl.BlockSpec((LANES, A_WIDTH), lambda b, i: (b * nq + i, 0)),
                  pl.BlockSpec((LANES, IDX_HEADS * IDX_DIM), lambda b, i: (b * nq + i, 0)),
                  pl.BlockSpec((LANES, LANES), lambda b, i: (b * nq + i, 0)),
                  pl.BlockSpec((seq, KV_RANK), lambda b, i: (b, 0)),
                  pl.BlockSpec((seq, LANES), lambda b, i: (b, 0)),
                  pl.BlockSpec((N_META, KV_RANK), lambda b, i: (meta_blk + b, 0)),
                  pl.BlockSpec((N_META, A_WIDTH), lambda b, i: (meta_blk + b, 0)),
                  full(wuk_t), full(wuv), full(near), full(metab), full(mmb)],
        out_specs=[pl.BlockSpec((LANES, A_WIDTH), lambda b, i: (b * nq + i, 0)),
                   pl.BlockSpec((N_META, A_WIDTH), lambda b, i: (b, 0))],
        out_shape=[jax.ShapeDtypeStruct((n_real, A_WIDTH), BF16),
                   jax.ShapeDtypeStruct((batch * N_META, A_WIDTH), BF16)],
        scratch_shapes=[pltpu.VMEM((n_tiles, LANES, LANES), jnp.int32),
                        pltpu.VMEM((seq, KV_RANK), BF16),
                        pltpu.VMEM((seq, IDX_DIM), BF16),
                        pltpu.VMEM((LANES, KV_RANK), BF16),
                        pltpu.VMEM((A_HEADS * LANES, KV_RANK), BF16),
                        pltpu.VMEM((A_HEADS * LANES, 1), F32),
                        pltpu.VMEM((A_HEADS * LANES, 1), F32),
                        pltpu.VMEM((A_HEADS * LANES, KV_RANK), F32)],
        compiler_params=_params(("arbitrary", "arbitrary")),
        name="dsa",
    )(qa, qi, sm, c, sm, c, qa, wuk_t, wuv, near, metab, mmb)
    return out_real, out_meta


def _merge_kernel(alpha, h_ref, hm_ref, ha_ref, g_ref, wbm_ref, wba_ref, wo_ref, lg_ref, lb_ref,
                  wr_ref, br_ref, h1_ref, comb_ref):
    d = h_ref.shape[1]
    gm = _sigmoid(g_ref[:, 0:d].astype(F32))
    ga = _sigmoid(g_ref[:, d:2 * d].astype(F32))
    y = gm * jnp.dot(hm_ref[...], wbm_ref[...], preferred_element_type=F32) + \
        ga * jnp.dot(ha_ref[...], wba_ref[...], preferred_element_type=F32)
    z = alpha * h_ref[...] + jnp.dot(y.astype(BF16), wo_ref[...], preferred_element_type=F32)
    h1 = _layer_norm_rows(z, lg_ref[...], lb_ref[...], 1e-5)
    h1_ref[...] = h1

    tm = h1.shape[0]
    logits_t = lax.dot_general(wr_ref[...], h1.astype(BF16), (((1,), (1,)), ((), ())),
                               preferred_element_type=F32)
    scores = _sigmoid(logits_t[0:N_EXPERTS, :])
    sel = scores + br_ref[0:N_EXPERTS, :]
    best = None
    for gidx in range(N_GROUPS):
        r0, r1, r2, r3 = (sel[gidx * GROUP_SIZE + u:gidx * GROUP_SIZE + u + 1, :] for u in range(4))
        a, b = jnp.maximum(r0, r1), jnp.minimum(r0, r1)
        c, dd = jnp.maximum(r2, r3), jnp.minimum(r2, r3)
        gs = jnp.maximum(a, c) + jnp.maximum(jnp.minimum(a, c), jnp.maximum(b, dd))
        if best is None:
            best, bg = gs, jnp.zeros((1, tm), jnp.int32)
        else:
            upd = gs > best
            bg = jnp.where(upd, gidx, bg)
            best = jnp.where(upd, gs, best)
    eidx = lax.broadcasted_iota(jnp.int32, (N_EXPERTS, tm), 0)
    masked = jnp.where((eidx // GROUP_SIZE) == bg, sel, -jnp.inf)
    v1 = jnp.max(masked, axis=0, keepdims=True)
    i1 = jnp.min(jnp.where(masked == v1, eidx, N_EXPERTS), axis=0, keepdims=True)
    masked2 = jnp.where(eidx == i1, -jnp.inf, masked)
    v2 = jnp.max(masked2, axis=0, keepdims=True)
    i2 = jnp.min(jnp.where(masked2 == v2, eidx, N_EXPERTS), axis=0, keepdims=True)
    s1 = jnp.sum(jnp.where(eidx == i1, scores, 0.0), axis=0, keepdims=True)
    s2 = jnp.sum(jnp.where(eidx == i2, scores, 0.0), axis=0, keepdims=True)
    tot = s1 + s2
    comb_t = jnp.where(eidx == i1, s1 / tot, 0.0) + jnp.where(eidx == i2, s2 / tot, 0.0)
    comb_pad = jnp.concatenate([comb_t, jnp.zeros((LANES - N_EXPERTS, tm), F32)], axis=0)
    comb_ref[...] = comb_pad.T


def _merge(h, hm, ha, g, w_bm, w_ba, w_o, ln_g, ln_b, w_router, b_router, alpha):
    n, d = h.shape
    wr_t = jnp.zeros((LANES, d), F32).at[0:N_EXPERTS].set(w_router.T).astype(BF16)
    br = jnp.zeros((LANES, 1), F32).at[0:N_EXPERTS, 0].set(b_router)
    row = lambda width: pl.BlockSpec((ROW_TILE, width), lambda r: (r, 0))
    full = lambda a: pl.BlockSpec(a.shape, lambda r: (0,) * a.ndim)
    args = (h, hm, ha, g, w_bm.astype(BF16), w_ba.astype(BF16), w_o.astype(BF16),
            ln_g.reshape(1, d), ln_b.reshape(1, d), wr_t, br)
    return pl.pallas_call(
        functools.partial(_merge_kernel, alpha),
        grid=(n // ROW_TILE,),
        in_specs=[row(d), row(M_WIDTH), row(A_WIDTH), row(2 * d)] + [full(a) for a in args[4:]],
        out_specs=[row(d), row(LANES)],
        out_shape=[jax.ShapeDtypeStruct((n, d), F32), jax.ShapeDtypeStruct((n, LANES), F32)],
        compiler_params=_params(("parallel",)),
        name="merge",
    )(*args)


def _moe_kernel(alpha, h_ref, comb_ref, wg_ref, wu_ref, wd_ref, lg_ref, lb_ref, out_ref,
                xb_scr, acc_scr):
    e = pl.program_id(1)

    @pl.when(e == 0)
    def _():
        xb_scr[...] = h_ref[...].astype(BF16)
        acc_scr[...] = jnp.zeros_like(acc_scr)

    xb = xb_scr[...]
    gate = jnp.dot(xb, wg_ref[...], preferred_element_type=F32)
    up = jnp.dot(xb, wu_ref[...], preferred_element_type=F32)
    he = gate * _sigmoid(gate) * up
    o = jnp.dot(he.astype(BF16), wd_ref[...], preferred_element_type=F32)
    comb = comb_ref[...]
    lane = lax.broadcasted_iota(jnp.int32, comb.shape, 1)
    ce = jnp.sum(jnp.where(lane == e, comb, 0.0), axis=1, keepdims=True)
    acc_scr[...] += ce * o

    @pl.when(e == pl.num_programs(1) - 1)
    def _():
        z = alpha * h_ref[...] + acc_scr[...]
        out_ref[...] = _layer_norm_rows(z, lg_ref[...], lb_ref[...], 1e-5)


def _moe(h, comb, w_gate, w_up, w_down, ln_g, ln_b, alpha):
    n, d = h.shape
    de = w_gate.shape[-1]
    tm = MOE_ROW_TILE if n % MOE_ROW_TILE == 0 else ROW_TILE
    return pl.pallas_call(
        functools.partial(_moe_kernel, alpha),
        grid=(n // tm, N_EXPERTS),
        in_specs=[pl.BlockSpec((tm, d), lambda r, e: (r, 0)),
                  pl.BlockSpec((tm, LANES), lambda r, e: (r, 0)),
                  pl.BlockSpec((None, d, de), lambda r, e: (e, 0, 0)),
                  pl.BlockSpec((None, d, de), lambda r, e: (e, 0, 0)),
                  pl.BlockSpec((None, de, d), lambda r, e: (e, 0, 0)),
                  pl.BlockSpec((1, d), lambda r, e: (0, 0)),
                  pl.BlockSpec((1, d), lambda r, e: (0, 0))],
        out_specs=pl.BlockSpec((tm, d), lambda r, e: (r, 0)),
        out_shape=jax.ShapeDtypeStruct((n, d), F32),
        scratch_shapes=[pltpu.VMEM((tm, d), BF16), pltpu.VMEM((tm, d), F32)],
        compiler_params=_params(("parallel", "arbitrary")),
        name="moe",
    )(h, comb, w_gate.astype(BF16), w_up.astype(BF16), w_down.astype(BF16),
      ln_g.reshape(1, d), ln_b.reshape(1, d))


def _with_meta(real, meta, n_pad):
    pad = n_pad - real.shape[0] - meta.shape[0]
    return jnp.concatenate([real, meta, jnp.zeros((pad, real.shape[1]), real.dtype)], axis=0)


def kernel(x, meta_tokens, ln_in_g, ln_in_b, w_in, conv_w, b_if, mnorm_g, kv_norm_g, w_uk, w_uv,
           w_branch_m, w_branch_a, w_out, ln1_g, ln1_b, w_router, b_router, w_gate, w_up, w_down,
           ln2_g, ln2_b, rel_bias):
    batch, seq, d = x.shape
    depth = w_in.shape[0]
    alpha = (2 * depth) ** 0.25
    n_real = batch * seq
    n_meta = batch * N_META
    tile = math.lcm(ROW_TILE, MOE_ROW_TILE)
    n_pad = -(-(n_real + n_meta) // tile) * tile
    assert seq % LANES == 0 and n_real % N_META == 0

    h = _with_meta(x.reshape(n_real, d), jnp.tile(meta_tokens.astype(x.dtype), (batch, 1)), n_pad)
    h = _input_ln(h, ln_in_g, ln_in_b)
    tables = _bias_tables(rel_bias)
    for l in range(depth):
        pm, qa, qi, c, sm, g = _project(h, _pack_w_in(w_in[l]), kv_norm_g[l], b_if[l])
        hm_real, hm_meta = _mlstm(pm, sm, conv_w[l], mnorm_g[l], batch, seq)
        ha_real, ha_meta = _dsa(qa, qi, sm, c, w_uk[l], w_uv[l], tables, batch, seq)
        hm = _with_meta(hm_real, hm_meta, n_pad)
        ha = _with_meta(ha_real, ha_meta, n_pad)
        h1, comb = _merge(h, hm, ha, g, w_branch_m[l], w_branch_a[l], w_out[l], ln1_g[l], ln1_b[l],
                          w_router, b_router, alpha)
        h = _moe(h1, comb, w_gate[l], w_up[l], w_down[l], ln2_g[l], ln2_b[l], alpha)
    return h[:n_real].reshape(batch, seq, d)
```

```python
import functools
import math

import numpy as np
import jax
import jax.numpy as jnp
from jax import lax
from jax.experimental import pallas as pl
from jax.experimental.pallas import tpu as pltpu

F32 = jnp.float32
BF16 = jnp.bfloat16

N_META = 16
M_HEADS = 4
M_HEAD_DIM = 128
M_WIDTH = M_HEADS * M_HEAD_DIM
CONV_WIDTH = 4
A_HEADS = 8
A_HEAD_DIM = 64
A_WIDTH = A_HEADS * A_HEAD_DIM
KV_RANK = 128
IDX_HEADS = 4
IDX_DIM = 64
IDX_SCALE = (IDX_HEADS * IDX_DIM) ** -0.5
TOPK_MAX = 256
REL_BUCKETS = 32
REL_MAX_EXACT = 16
REL_MAX_DIST = 128
N_EXPERTS = 16
N_GROUPS = 4
GROUP_SIZE = N_EXPERTS // N_GROUPS

LANES = 128
ROW_TILE = 256
MOE_ROW_TILE = 768
M_CHUNK = 256
SCORE_CHUNK = 4
VMEM_LIMIT = 56 * 1024 * 1024
NEG_BIG = -1e30
INT_MIN = -2 ** 31
LOG2E = math.log2(math.e)

PK_PM = 0
PK_QA = 2048
PK_QI = 2560
PK_CKV = 2816
PK_SM = 2944
PK_G = 3072
PK_TOTAL = 5120
SM_KIDX = 0
SM_WIDX = 64
SM_IPRE = 68
SM_FPRE = 72


def _params(sem):
    return pltpu.CompilerParams(dimension_semantics=sem, vmem_limit_bytes=VMEM_LIMIT)


def _sigmoid(x):
    return 1.0 / (1.0 + jnp.exp(-x))


def _layer_norm_rows(x, g, b, eps):
    mu = jnp.mean(x, axis=-1, keepdims=True)
    xc = x - mu
    var = jnp.mean(xc * xc, axis=-1, keepdims=True)
    return xc * lax.rsqrt(var + eps) * g + b


def _ln_kernel(x_ref, g_ref, b_ref, o_ref):
    o_ref[...] = _layer_norm_rows(x_ref[...], g_ref[...], b_ref[...], 1e-5)


def _input_ln(x, g, b):
    n, d = x.shape
    return pl.pallas_call(
        _ln_kernel,
        grid=(n // ROW_TILE,),
        in_specs=[pl.BlockSpec((ROW_TILE, d), lambda r: (r, 0)),
                  pl.BlockSpec((1, d), lambda r: (0, 0)),
                  pl.BlockSpec((1, d), lambda r: (0, 0))],
        out_specs=pl.BlockSpec((ROW_TILE, d), lambda r: (r, 0)),
        out_shape=jax.ShapeDtypeStruct((n, d), F32),
        compiler_params=_params(("parallel",)),
        name="input_ln",
    )(x, g.reshape(1, d), b.reshape(1, d))


def _proj_kernel(h_ref, w_ref, kvg_ref, brow_ref, pm_ref, qa_ref, qi_ref, c_ref, sm_ref, g_ref):
    x = h_ref[...].astype(BF16)

    def mm(lo, width):
        return jnp.dot(x, w_ref[:, lo:lo + width], preferred_element_type=F32)

    pm_ref[...] = mm(PK_PM, 2048).astype(BF16)
    qa_ref[...] = mm(PK_QA, 512).astype(BF16)
    qi_ref[...] = mm(PK_QI, 256).astype(BF16)
    ckv = mm(PK_CKV, 128)
    c_ref[...] = ckv * lax.rsqrt(jnp.mean(ckv * ckv, axis=-1, keepdims=True) + 1e-6) * kvg_ref[...]
    sm_ref[...] = mm(PK_SM, 128) + brow_ref[...]
    g_ref[...] = mm(PK_G, 2048).astype(BF16)


def _pack_w_in(w):
    d = w.shape[0]
    cols = [w[:, 0:2048], w[:, 2056:2568], w[:, 2696:2952], w[:, 2568:2696],
            w[:, 2952:3016], w[:, 3016:3020], w[:, 2048:2056],
            jnp.zeros((d, LANES - 76), w.dtype), w[:, 3020:5068]]
    return jnp.concatenate(cols, axis=1).astype(BF16)


def _project(h, w_packed, kv_g, b_if):
    n, d = h.shape
    brow = jnp.concatenate([jnp.zeros((SM_IPRE,), F32), b_if.astype(F32),
                            jnp.zeros((LANES - SM_IPRE - 2 * M_HEADS,), F32)]).reshape(1, LANES)
    row = lambda width: pl.BlockSpec((ROW_TILE, width), lambda r: (r, 0))
    shp = lambda width, dt: jax.ShapeDtypeStruct((n, width), dt)
    return pl.pallas_call(
        _proj_kernel,
        grid=(n // ROW_TILE,),
        in_specs=[row(d),
                  pl.BlockSpec((d, PK_TOTAL), lambda r: (0, 0)),
                  pl.BlockSpec((1, KV_RANK), lambda r: (0, 0)),
                  pl.BlockSpec((1, LANES), lambda r: (0, 0))],
        out_specs=[row(2048), row(512), row(256), row(128), row(128), row(2048)],
        out_shape=[shp(2048, BF16), shp(512, BF16), shp(256, BF16), shp(128, F32), shp(128, F32),
                   shp(2048, BF16)],
        compiler_params=_params(("parallel",)),
        name="in_proj",
    )(h, w_packed, kv_g.reshape(1, KV_RANK), brow)


def _mlstm_chunk_math(q, k, v, li, lf, ct, n, m):
    L = q.shape[0]
    r = lax.broadcasted_iota(jnp.int32, (L, L), 0)
    c = lax.broadcasted_iota(jnp.int32, (L, L), 1)
    eye = r == c
    tril = c <= r
    lf_b = jnp.broadcast_to(lf, (L, L))
    li_b = jnp.broadcast_to(li, (L, L))
    lf_row = jnp.sum(jnp.where(eye, lf_b, 0.0), axis=0, keepdims=True)
    li_row = jnp.sum(jnp.where(eye, li_b, 0.0), axis=0, keepdims=True)
    b_col = jnp.sum(jnp.where(tril, jnp.broadcast_to(lf_row, (L, L)), 0.0), axis=1, keepdims=True)
    b_row = jnp.sum(jnp.where(r <= c, lf_b, 0.0), axis=0, keepdims=True)
    d = jnp.where(tril, b_col - b_row + li_row, -jnp.inf)
    inter = b_col + m
    m_t = jnp.maximum(inter, jnp.max(d, axis=1, keepdims=True))
    a = jnp.exp(inter - m_t)
    qb, kb, vb = q.astype(BF16), k.astype(BF16), v.astype(BF16)
    qk = lax.dot_general(qb, kb, (((1,), (1,)), ((), ())), preferred_element_type=F32)
    w = jnp.exp(d - m_t) * qk
    num = a * jnp.dot(qb, ct.astype(BF16), preferred_element_type=F32) + \
        jnp.dot(w.astype(BF16), vb, preferred_element_type=F32)
    den = a * jnp.sum(q * n, axis=1, keepdims=True) + jnp.sum(w, axis=1, keepdims=True)
    h = num / jnp.maximum(jnp.abs(den), jnp.exp(-m_t))
    b_last = b_col[L - 1:L, :]
    g = b_last - b_col + li
    m_new = jnp.maximum(b_last + m, jnp.max(g, axis=0, keepdims=True))
    decay = jnp.exp(b_last + m - m_new)
    kw = k * jnp.exp(g - m_new)
    ct_new = decay * ct + lax.dot_general(kw.astype(BF16), vb, (((0,), (0,)), ((), ())),
                                          preferred_element_type=F32)
    n_new = decay * n + jnp.sum(kw, axis=0, keepdims=True)
    return h, ct_new, n_new, m_new


def _mlstm_kernel(pm_ref, sm_ref, pmm_ref, smm_ref, cw_ref, mg_ref, out_ref, outm_ref,
                  ct_scr, n_scr, m_scr, x_scr):
    ci = pl.program_id(1)
    tail = 8

    def run_chunk(p_ref, s_ref, o_ref):
        L = p_ref.shape[0]
        x_scr[tail:tail + L, :] = p_ref[:, 0:2 * M_WIDTH].astype(F32)
        conv = cw_ref[0:1, :] * x_scr[tail - 3:tail - 3 + L, :]
        for j in range(1, CONV_WIDTH):
            conv = conv + cw_ref[j:j + 1, :] * x_scr[tail - 3 + j:tail - 3 + j + L, :]
        x_scr[0:tail, :] = x_scr[L:L + tail, :]
        qk = conv * _sigmoid(conv)
        for hd in range(M_HEADS):
            lo = hd * M_HEAD_DIM
            q = qk[:, lo:lo + M_HEAD_DIM] * (M_HEAD_DIM ** -0.5)
            k = qk[:, M_WIDTH + lo:M_WIDTH + lo + M_HEAD_DIM]
            v = p_ref[:, 2 * M_WIDTH + lo:2 * M_WIDTH + lo + M_HEAD_DIM].astype(F32)
            li = s_ref[:, SM_IPRE + hd:SM_IPRE + hd + 1]
            f = s_ref[:, SM_FPRE + hd:SM_FPRE + hd + 1]
            lf = jnp.minimum(f, 0.0) - jnp.log1p(jnp.exp(-jnp.abs(f)))
            h, ct_new, n_new, m_new = _mlstm_chunk_math(
                q, k, v, li, lf, ct_scr[hd], n_scr[hd:hd + 1, :], m_scr[hd:hd + 1, 0:1])
            ct_scr[hd] = ct_new
            n_scr[hd:hd + 1, :] = n_new
            m_scr[hd:hd + 1, :] = jnp.broadcast_to(m_new, (1, LANES))
            mu = jnp.mean(h, axis=-1, keepdims=True)
            hc = h - mu
            var = jnp.mean(hc * hc, axis=-1, keepdims=True)
            o_gate = _sigmoid(p_ref[:, 3 * M_WIDTH + lo:3 * M_WIDTH + lo + M_HEAD_DIM].astype(F32))
            o_ref[:, lo:lo + M_HEAD_DIM] = (hc * lax.rsqrt(var + 1e-5) * mg_ref[:, lo:lo + M_HEAD_DIM]
                                            * o_gate).astype(o_ref.dtype)

    @pl.when(ci == 0)
    def _():
        ct_scr[...] = jnp.zeros_like(ct_scr)
        n_scr[...] = jnp.zeros_like(n_scr)
        m_scr[...] = jnp.zeros_like(m_scr)
        x_scr[...] = jnp.zeros_like(x_scr)
        run_chunk(pmm_ref, smm_ref, outm_ref)

    run_chunk(pm_ref, sm_ref, out_ref)


def _mlstm(pm, sm, conv_w, mnorm_g, batch, seq):
    n_real = batch * seq
    chunk = min(M_CHUNK, seq)
    nc = seq // chunk
    meta_blk = n_real // N_META
    out_real, out_meta = pl.pallas_call(
        _mlstm_kernel,
        grid=(batch, nc),
        in_specs=[pl.BlockSpec((chunk, 4 * M_WIDTH), lambda b, c: (b * nc + c, 0)),
                  pl.BlockSpec((chunk, LANES), lambda b, c: (b * nc + c, 0)),
                  pl.BlockSpec((N_META, 4 * M_WIDTH), lambda b, c: (meta_blk + b, 0)),
                  pl.BlockSpec((N_META, LANES), lambda b, c: (meta_blk + b, 0)),
                  pl.BlockSpec((CONV_WIDTH, 2 * M_WIDTH), lambda b, c: (0, 0)),
                  pl.BlockSpec((1, M_WIDTH), lambda b, c: (0, 0))],
        out_specs=[pl.BlockSpec((chunk, M_WIDTH), lambda b, c: (b * nc + c, 0)),
                   pl.BlockSpec((N_META, M_WIDTH), lambda b, c: (b, 0))],
        out_shape=[jax.ShapeDtypeStruct((n_real, M_WIDTH), BF16),
                   jax.ShapeDtypeStruct((batch * N_META, M_WIDTH), BF16)],
        scratch_shapes=[pltpu.VMEM((M_HEADS, M_HEAD_DIM, M_HEAD_DIM), F32),
                        pltpu.VMEM((8, LANES), F32),
                        pltpu.VMEM((8, LANES), F32),
                        pltpu.VMEM((chunk + 8, 2 * M_WIDTH), F32)],
        compiler_params=_params(("arbitrary", "arbitrary")),
        name="mlstm",
    )(pm, sm, pm, sm, conv_w.astype(F32), mnorm_g.reshape(1, M_WIDTH).astype(F32))
    return out_real, out_meta


def _rel_bucket_np(dist):
    n = np.maximum(dist, 0)
    nf = np.maximum(n, REL_MAX_EXACT).astype(np.float32)
    large = REL_MAX_EXACT + (np.log(nf / np.float32(REL_MAX_EXACT)) /
                             np.float32(math.log(REL_MAX_DIST / REL_MAX_EXACT))
                             * np.float32(REL_BUCKETS - REL_MAX_EXACT)).astype(np.int32)
    large = np.minimum(large, REL_BUCKETS - 1)
    return np.where(n < REL_MAX_EXACT, n, large).astype(np.int32)


def _bias_tables(rel_bias):
    q = np.arange(LANES)[:, None]
    k = np.arange(LANES)[None, :]
    far = 4 * LANES
    assert (_rel_bucket_np(np.arange(LANES + 1, far)) == REL_BUCKETS - 1).all()
    far_idx = np.full((LANES, LANES), REL_BUCKETS - 1, np.int32)
    near_idx = np.stack([_rel_bucket_np(q - k), _rel_bucket_np(LANES + q - k), far_idx])
    meta_idx = np.stack([_rel_bucket_np(q + N_META - np.minimum(k, N_META - 1)), far_idx])
    mq = np.arange(N_META)[:, None]
    mm_idx = _rel_bucket_np(mq - np.minimum(k, N_META - 1))
    rb = rel_bias.astype(F32)
    gather = lambda idx: jnp.moveaxis(rb[idx], -1, -3)
    rel = lambda idx: (gather(idx) - rb[REL_BUCKETS - 1][:, None, None]) * LOG2E
    return rel(near_idx), rel(meta_idx), gather(mm_idx)


def _dsa_kernel(top_k, qa_ref, qi_ref, sm_ref, cb_ref, smb_ref, cm_ref, qam_ref,
                wuk_ref, wuv_ref, near_ref, metab_ref, mmb_ref,
                out_ref, outm_ref,
                keys_scr, caug_scr, kbf_scr, cmaug_scr, qs_scr, m_scr, acc_scr):
    i = pl.program_id(1)
    T = LANES
    H = A_HEADS
    col = lax.broadcasted_iota(jnp.int32, (T, T), 1)
    row = lax.broadcasted_iota(jnp.int32, (T, T), 0)
    nt = (((1,), (1,)), ((), ()))

    def q_latent(qa, hd, scale):
        ql = jnp.dot(qa[:, hd * A_HEAD_DIM:(hd + 1) * A_HEAD_DIM], wuk_ref[hd],
                     preferred_element_type=F32)
        return (ql * scale).astype(BF16)

    def ones_column(n):
        return jnp.where(lax.broadcasted_iota(jnp.int32, (n, T), 1) == 0, 1.0, 0.0).astype(BF16)

    @pl.when(i == 0)
    def _():
        caug_scr[:, 0:KV_RANK] = cb_ref[...].astype(BF16)
        caug_scr[:, KV_RANK:KV_RANK + T] = ones_column(caug_scr.shape[0])
        kbf_scr[...] = smb_ref[:, SM_KIDX:SM_KIDX + IDX_DIM].astype(BF16)
        cm_pad = jnp.concatenate([cm_ref[...], jnp.zeros((T - N_META, KV_RANK), F32)], axis=0)
        cmaug_scr[:, 0:KV_RANK] = cm_pad.astype(BF16)
        cmaug_scr[:, KV_RANK:KV_RANK + T] = ones_column(T)
        cmk = cmaug_scr[:, 0:KV_RANK]
        qam = qam_ref[...]
        mrow = lax.broadcasted_iota(jnp.int32, (N_META, T), 0)
        mcol = lax.broadcasted_iota(jnp.int32, (N_META, T), 1)
        for hd in range(H):
            lg = lax.dot_general(q_latent(qam, hd, A_HEAD_DIM ** -0.5), cmk, nt,
                                 preferred_element_type=F32) + mmb_ref[hd]
            lg = jnp.where(mcol <= mrow, lg, NEG_BIG)
            p = jnp.exp(lg - jnp.max(lg, axis=1, keepdims=True))
            p = p / jnp.sum(p, axis=1, keepdims=True)
            o = jnp.dot(p.astype(BF16), cmk, preferred_element_type=F32)
            outm_ref[:, hd * A_HEAD_DIM:(hd + 1) * A_HEAD_DIM] = jnp.dot(
                o.astype(BF16), wuv_ref[hd], preferred_element_type=F32).astype(outm_ref.dtype)

    qa = qa_ref[...]
    for hd in range(H):
        qs_scr[hd * T:(hd + 1) * T, :] = q_latent(qa, hd, A_HEAD_DIM ** -0.5 * LOG2E)
    qi = qi_ref[...]
    wv = sm_ref[:, SM_WIDX:SM_WIDX + IDX_HEADS] * IDX_SCALE
    t_col = i * T + lax.broadcasted_iota(jnp.int32, (T, 1), 0)
    n_chunks = (i + SCORE_CHUNK) // SCORE_CHUNK
    CW = SCORE_CHUNK * T

    def score_body(cix, carry):
        kc = kbf_scr[pl.ds(pl.multiple_of(cix * CW, CW), CW), :]
        acc = jnp.zeros((T, CW), F32)
        for hh in range(IDX_HEADS):
            s = lax.dot_general(qi[:, hh * IDX_DIM:(hh + 1) * IDX_DIM], kc, nt,
                                preferred_element_type=F32)
            acc = acc + wv[:, hh:hh + 1] * jnp.maximum(s, 0.0)
        acc = jnp.where(acc == 0.0, 0.0, acc)
        bits = lax.bitcast_convert_type(acc, jnp.int32)
        key = jnp.where(bits < 0, bits ^ jnp.int32(0x7FFFFFFF), bits)
        s_idx = cix * CW + lax.broadcasted_iota(jnp.int32, (T, CW), 1)
        key = jnp.where(s_idx <= t_col, key, jnp.int32(INT_MIN))
        for u in range(SCORE_CHUNK):
            keys_scr[cix * SCORE_CHUNK + u] = key[:, u * T:(u + 1) * T]
        return carry

    lax.fori_loop(0, n_chunks, score_body, 0)

    def count_where(pred_fn):
        def body(cix, cnt):
            for u in range(SCORE_CHUNK):
                cnt = cnt + jnp.where(pred_fn(keys_scr[cix * SCORE_CHUNK + u]), 1.0, 0.0)
            return cnt
        cnt = lax.fori_loop(0, n_chunks, body, jnp.zeros((T, T), F32))
        return jnp.sum(cnt, axis=1, keepdims=True)

    def bit_body(bi, ans):
        cand_u = ans | lax.shift_left(jnp.int32(1), jnp.int32(31) - bi)
        cand = cand_u ^ jnp.int32(INT_MIN)
        total = count_where(lambda kk: kk >= cand)
        return jnp.where(total >= float(top_k), cand_u, ans)

    ans = lax.fori_loop(0, 32, bit_body, jnp.zeros((T, 1), jnp.int32))
    thr = ans ^ jnp.int32(INT_MIN)
    need = float(top_k) - count_where(lambda kk: kk > thr)

    m_scr[...] = jnp.full_like(m_scr, NEG_BIG)
    acc_scr[...] = jnp.zeros_like(acc_scr)

    def attend(c_aug, sels, bias_fns):
        n_sub = len(sels)
        lg = lax.dot_general(qs_scr[...], c_aug[:, 0:KV_RANK], nt, preferred_element_type=F32)
        m_old = m_scr[...]
        pieces, maxes = [], []
        for hd in range(H):
            mx = None
            for u in range(n_sub):
                x = lg[hd * T:(hd + 1) * T, u * T:(u + 1) * T]
                if bias_fns[u] is not None:
                    x = x + bias_fns[u](hd)
                x = jnp.where(sels[u], x, NEG_BIG)
                pieces.append(x)
                mx = x if mx is None else jnp.maximum(mx, x)
            maxes.append(jnp.max(mx, axis=1, keepdims=True))
        m_new = jnp.maximum(m_old, jnp.concatenate(maxes, axis=0))
        alpha = jnp.exp2(m_old - m_new)
        p_rows = []
        for hd in range(H):
            mh = m_new[hd * T:(hd + 1) * T, :]
            ph = [jnp.exp2(pieces[hd * n_sub + u] - mh).astype(BF16) for u in range(n_sub)]
            p_rows.append(ph[0] if n_sub == 1 else jnp.concatenate(ph, axis=1))
        p = jnp.concatenate(p_rows, axis=0)
        acc_scr[...] = alpha * acc_scr[...] + jnp.dot(p, c_aug, preferred_element_type=F32)
        m_scr[...] = m_new

    meta_sel = jnp.minimum(i, 1)
    attend(cmaug_scr[...], [col < N_META], [lambda hd: metab_ref[meta_sel, hd]])

    upper = (row < col).astype(BF16)

    def key_step(step, seen, near):
        sels, bias_fns = [], []
        for u in range(2):
            j = 2 * step + u
            kk = keys_scr[j]
            eq = kk == thr
            eqf = jnp.where(eq, 1.0, 0.0)
            before = jnp.dot(eqf.astype(BF16), upper, preferred_element_type=F32) + seen
            sel = (kk > thr) | (eq & (before < need))
            if near:
                sel = sel & ((j * T + col) <= (i * T + row))
                dsel = jnp.clip(i - j, 0, 2)
                bias_fns.append(lambda hd, dsel=dsel: near_ref[dsel, hd])
            else:
                bias_fns.append(None)
            seen = seen + jnp.sum(eqf, axis=1, keepdims=True)
            sels.append(sel)
        attend(caug_scr[pl.ds(pl.multiple_of(step * 2 * T, 2 * T), 2 * T), :], sels, bias_fns)
        return seen

    n_far = jnp.maximum(i - 1, 0) // 2
    n_steps = (i + 2) // 2
    seen = lax.fori_loop(0, n_far, lambda s, c: key_step(s, c, False), jnp.zeros((T, 1), F32))
    lax.fori_loop(n_far, n_steps, lambda s, c: key_step(s, c, True), seen)

    for hd in range(H):
        rs = slice(hd * T, (hd + 1) * T)
        o = acc_scr[rs, 0:KV_RANK] / acc_scr[rs, KV_RANK:KV_RANK + 1]
        out_ref[:, hd * A_HEAD_DIM:(hd + 1) * A_HEAD_DIM] = jnp.dot(
            o.astype(BF16), wuv_ref[hd], preferred_element_type=F32).astype(out_ref.dtype)


def _dsa(qa, qi, sm, c, w_uk, w_uv, tables, batch, seq):
    n_real = batch * seq
    nq = seq // LANES
    n_tiles = ((nq + SCORE_CHUNK - 1) // SCORE_CHUNK) * SCORE_CHUNK
    top_k = min(TOPK_MAX, seq // 4)
    meta_blk = n_real // N_META
    near, metab, mmb = tables
    wuk_t = jnp.swapaxes(w_uk, 1, 2).astype(BF16)
    wuv = w_uv.astype(BF16)
    full = lambda a: pl.BlockSpec(a.shape, lambda b, i: (0,) * a.ndim)
    assert seq % (SCORE_CHUNK * LANES) == 0 and nq % 2 == 0
    out_real, out_meta = pl.pallas_call(
        functools.partial(_dsa_kernel, top_k),
        grid=(batch, nq),
        in_specs=[pl.BlockSpec((LANES, A_WIDTH), lambda b, i: (b * nq + i, 0)),
                  pl.BlockSpec((LANES, IDX_HEADS * IDX_DIM), lambda b, i: (b * nq + i, 0)),
                  pl.BlockSpec((LANES, LANES), lambda b, i: (b * nq + i, 0)),
                  pl.BlockSpec((seq, KV_RANK), lambda b, i: (b, 0)),
                  pl.BlockSpec((seq, LANES), lambda b, i: (b, 0)),
                  pl.BlockSpec((N_META, KV_RANK), lambda b, i: (meta_blk + b, 0)),
                  pl.BlockSpec((N_META, A_WIDTH), lambda b, i: (meta_blk + b, 0)),
                  full(wuk_t), full(wuv), full(near), full(metab), full(mmb)],
        out_specs=[pl.BlockSpec((LANES, A_WIDTH), lambda b, i: (b * nq + i, 0)),
                   pl.BlockSpec((N_META, A_WIDTH), lambda b, i: (b, 0))],
        out_shape=[jax.ShapeDtypeStruct((n_real, A_WIDTH), BF16),
                   jax.ShapeDtypeStruct((batch * N_META, A_WIDTH), BF16)],
        scratch_shapes=[pltpu.VMEM((n_tiles, LANES, LANES), jnp.int32),
                        pltpu.VMEM((seq, KV_RANK + LANES), BF16),
                        pltpu.VMEM((seq, IDX_DIM), BF16),
                        pltpu.VMEM((LANES, KV_RANK + LANES), BF16),
                        pltpu.VMEM((A_HEADS * LANES, KV_RANK), BF16),
                        pltpu.VMEM((A_HEADS * LANES, 1), F32),
                        pltpu.VMEM((A_HEADS * LANES, KV_RANK + LANES), F32)],
        compiler_params=_params(("arbitrary", "arbitrary")),
        name="dsa",
    )(qa, qi, sm, c, sm, c, qa, wuk_t, wuv, near, metab, mmb)
    return out_real, out_meta


def _merge_kernel(alpha, h_ref, hm_ref, ha_ref, g_ref, wbm_ref, wba_ref, wo_ref, lg_ref, lb_ref,
                  wr_ref, br_ref, h1_ref, comb_ref):
    d = h_ref.shape[1]
    gm = _sigmoid(g_ref[:, 0:d].astype(F32))
    ga = _sigmoid(g_ref[:, d:2 * d].astype(F32))
    y = gm * jnp.dot(hm_ref[...], wbm_ref[...], preferred_element_type=F32) + \
        ga * jnp.dot(ha_ref[...], wba_ref[...], preferred_element_type=F32)
    z = alpha * h_ref[...] + jnp.dot(y.astype(BF16), wo_ref[...], preferred_element_type=F32)
    h1 = _layer_norm_rows(z, lg_ref[...], lb_ref[...], 1e-5)
    h1_ref[...] = h1

    tm = h1.shape[0]
    logits_t = lax.dot_general(wr_ref[...], h1.astype(BF16), (((1,), (1,)), ((), ())),
                               preferred_element_type=F32)
    scores = _sigmoid(logits_t[0:N_EXPERTS, :])
    sel = scores + br_ref[0:N_EXPERTS, :]
    best = None
    for gidx in range(N_GROUPS):
        r0, r1, r2, r3 = (sel[gidx * GROUP_SIZE + u:gidx * GROUP_SIZE + u + 1, :] for u in range(4))
        a, b = jnp.maximum(r0, r1), jnp.minimum(r0, r1)
        c, dd = jnp.maximum(r2, r3), jnp.minimum(r2, r3)
        gs = jnp.maximum(a, c) + jnp.maximum(jnp.minimum(a, c), jnp.maximum(b, dd))
        if best is None:
            best, bg = gs, jnp.zeros((1, tm), jnp.int32)
        else:
            upd = gs > best
            bg = jnp.where(upd, gidx, bg)
            best = jnp.where(upd, gs, best)
    eidx = lax.broadcasted_iota(jnp.int32, (N_EXPERTS, tm), 0)
    masked = jnp.where((eidx // GROUP_SIZE) == bg, sel, -jnp.inf)
    v1 = jnp.max(masked, axis=0, keepdims=True)
    i1 = jnp.min(jnp.where(masked == v1, eidx, N_EXPERTS), axis=0, keepdims=True)
    masked2 = jnp.where(eidx == i1, -jnp.inf, masked)
    v2 = jnp.max(masked2, axis=0, keepdims=True)
    i2 = jnp.min(jnp.where(masked2 == v2, eidx, N_EXPERTS), axis=0, keepdims=True)
    s1 = jnp.sum(jnp.where(eidx == i1, scores, 0.0), axis=0, keepdims=True)
    s2 = jnp.sum(jnp.where(eidx == i2, scores, 0.0), axis=0, keepdims=True)
    tot = s1 + s2
    comb_t = jnp.where(eidx == i1, s1 / tot, 0.0) + jnp.where(eidx == i2, s2 / tot, 0.0)
    comb_pad = jnp.concatenate([comb_t, jnp.zeros((LANES - N_EXPERTS, tm), F32)], axis=0)
    comb_ref[...] = comb_pad.T


def _merge(h, hm, ha, g, w_bm, w_ba, w_o, ln_g, ln_b, w_router, b_router, alpha):
    n, d = h.shape
    wr_t = jnp.zeros((LANES, d), F32).at[0:N_EXPERTS].set(w_router.T).astype(BF16)
    br = jnp.zeros((LANES, 1), F32).at[0:N_EXPERTS, 0].set(b_router)
    row = lambda width: pl.BlockSpec((ROW_TILE, width), lambda r: (r, 0))
    full = lambda a: pl.BlockSpec(a.shape, lambda r: (0,) * a.ndim)
    args = (h, hm, ha, g, w_bm.astype(BF16), w_ba.astype(BF16), w_o.astype(BF16),
            ln_g.reshape(1, d), ln_b.reshape(1, d), wr_t, br)
    return pl.pallas_call(
        functools.partial(_merge_kernel, alpha),
        grid=(n // ROW_TILE,),
        in_specs=[row(d), row(M_WIDTH), row(A_WIDTH), row(2 * d)] + [full(a) for a in args[4:]],
        out_specs=[row(d), row(LANES)],
        out_shape=[jax.ShapeDtypeStruct((n, d), F32), jax.ShapeDtypeStruct((n, LANES), F32)],
        compiler_params=_params(("parallel",)),
        name="merge",
    )(*args)


def _moe_kernel(alpha, h_ref, comb_ref, wg_ref, wu_ref, wd_ref, lg_ref, lb_ref, out_ref,
                xb_scr, acc_scr):
    e = pl.program_id(1)

    @pl.when(e == 0)
    def _():
        xb_scr[...] = h_ref[...].astype(BF16)
        acc_scr[...] = jnp.zeros_like(acc_scr)

    xb = xb_scr[...]
    gate = jnp.dot(xb, wg_ref[...], preferred_element_type=F32)
    up = jnp.dot(xb, wu_ref[...], preferred_element_type=F32)
    he = gate * _sigmoid(gate) * up
    o = jnp.dot(he.astype(BF16), wd_ref[...], preferred_element_type=F32)
    comb = comb_ref[...]
    lane = lax.broadcasted_iota(jnp.int32, comb.shape, 1)
    ce = jnp.sum(jnp.where(lane == e, comb, 0.0), axis=1, keepdims=True)
    acc_scr[...] += ce * o

    @pl.when(e == pl.num_programs(1) - 1)
    def _():
        z = alpha * h_ref[...] + acc_scr[...]
        out_ref[...] = _layer_norm_rows(z, lg_ref[...], lb_ref[...], 1e-5)


def _moe(h, comb, w_gate, w_up, w_down, ln_g, ln_b, alpha):
    n, d = h.shape
    de = w_gate.shape[-1]
    tm = MOE_ROW_TILE if n % MOE_ROW_TILE == 0 else ROW_TILE
    return pl.pallas_call(
        functools.partial(_moe_kernel, alpha),
        grid=(n // tm, N_EXPERTS),
        in_specs=[pl.BlockSpec((tm, d), lambda r, e: (r, 0)),
                  pl.BlockSpec((tm, LANES), lambda r, e: (r, 0)),
                  pl.BlockSpec((None, d, de), lambda r, e: (e, 0, 0)),
                  pl.BlockSpec((None, d, de), lambda r, e: (e, 0, 0)),
                  pl.BlockSpec((None, de, d), lambda r, e: (e, 0, 0)),
                  pl.BlockSpec((1, d), lambda r, e: (0, 0)),
                  pl.BlockSpec((1, d), lambda r, e: (0, 0))],
        out_specs=pl.BlockSpec((tm, d), lambda r, e: (r, 0)),
        out_shape=jax.ShapeDtypeStruct((n, d), F32),
        scratch_shapes=[pltpu.VMEM((tm, d), BF16), pltpu.VMEM((tm, d), F32)],
        compiler_params=_params(("parallel", "arbitrary")),
        name="moe",
    )(h, comb, w_gate.astype(BF16), w_up.astype(BF16), w_down.astype(BF16),
      ln_g.reshape(1, d), ln_b.reshape(1, d))


def _with_meta(real, meta, n_pad):
    pad = n_pad - real.shape[0] - meta.shape[0]
    return jnp.concatenate([real, meta, jnp.zeros((pad, real.shape[1]), real.dtype)], axis=0)


def kernel(x, meta_tokens, ln_in_g, ln_in_b, w_in, conv_w, b_if, mnorm_g, kv_norm_g, w_uk, w_uv,
           w_branch_m, w_branch_a, w_out, ln1_g, ln1_b, w_router, b_router, w_gate, w_up, w_down,
           ln2_g, ln2_b, rel_bias):
    batch, seq, d = x.shape
    depth = w_in.shape[0]
    alpha = (2 * depth) ** 0.25
    n_real = batch * seq
    n_meta = batch * N_META
    tile = math.lcm(ROW_TILE, MOE_ROW_TILE)
    n_pad = -(-(n_real + n_meta) // tile) * tile
    assert seq % LANES == 0 and n_real % N_META == 0

    h = _with_meta(x.reshape(n_real, d), jnp.tile(meta_tokens.astype(x.dtype), (batch, 1)), n_pad)
    h = _input_ln(h, ln_in_g, ln_in_b)
    tables = _bias_tables(rel_bias)
    for l in range(depth):
        pm, qa, qi, c, sm, g = _project(h, _pack_w_in(w_in[l]), kv_norm_g[l], b_if[l])
        hm_real, hm_meta = _mlstm(pm, sm, conv_w[l], mnorm_g[l], batch, seq)
        ha_real, ha_meta = _dsa(qa, qi, sm, c, w_uk[l], w_uv[l], tables, batch, seq)
        hm = _with_meta(hm_real, hm_meta, n_pad)
        ha = _with_meta(ha_real, ha_meta, n_pad)
        h1, comb = _merge(h, hm, ha, g, w_branch_m[l], w_branch_a[l], w_out[l], ln1_g[l], ln1_b[l],
                          w_router, b_router, alpha)
        h = _moe(h1, comb, w_gate[l], w_up[l], w_down[l], ln2_g[l], ln2_b[l], alpha)
    return h[:n_real].reshape(batch, seq, d)
```

```python
import functools
import math

import numpy as np
import jax
import jax.numpy as jnp
from jax import lax
from jax.experimental import pallas as pl
from jax.experimental.pallas import tpu as pltpu

F32 = jnp.float32
BF16 = jnp.bfloat16

N_META = 16
M_HEADS = 4
M_HEAD_DIM = 128
M_WIDTH = M_HEADS * M_HEAD_DIM
CONV_WIDTH = 4
A_HEADS = 8
A_HEAD_DIM = 64
A_WIDTH = A_HEADS * A_HEAD_DIM
KV_RANK = 128
IDX_HEADS = 4
IDX_DIM = 64
IDX_SCALE = (IDX_HEADS * IDX_DIM) ** -0.5
TOPK_MAX = 256
REL_BUCKETS = 32
REL_MAX_EXACT = 16
REL_MAX_DIST = 128
N_EXPERTS = 16
N_GROUPS = 4
GROUP_SIZE = N_EXPERTS // N_GROUPS

LANES = 128
ROW_TILE = 256
MOE_ROW_TILE = 768
M_CHUNK = 256
SCORE_CHUNK = 4
KEY_SUB = 4
VMEM_LIMIT = 56 * 1024 * 1024
NEG_BIG = -1e30
INT_MIN = -2 ** 31
LOG2E = math.log2(math.e)

PK_PM = 0
PK_QA = 2048
PK_QI = 2560
PK_CKV = 2816
PK_SM = 2944
PK_G = 3072
PK_TOTAL = 5120
SM_KIDX = 0
SM_WIDX = 64
SM_IPRE = 68
SM_FPRE = 72


def _params(sem):
    return pltpu.CompilerParams(dimension_semantics=sem, vmem_limit_bytes=VMEM_LIMIT)


def _sigmoid(x):
    return 1.0 / (1.0 + jnp.exp(-x))


def _layer_norm_rows(x, g, b, eps):
    mu = jnp.mean(x, axis=-1, keepdims=True)
    xc = x - mu
    var = jnp.mean(xc * xc, axis=-1, keepdims=True)
    return xc * lax.rsqrt(var + eps) * g + b


def _ln_kernel(x_ref, g_ref, b_ref, o_ref):
    o_ref[...] = _layer_norm_rows(x_ref[...], g_ref[...], b_ref[...], 1e-5)


def _input_ln(x, g, b):
    n, d = x.shape
    return pl.pallas_call(
        _ln_kernel,
        grid=(n // ROW_TILE,),
        in_specs=[pl.BlockSpec((ROW_TILE, d), lambda r: (r, 0)),
                  pl.BlockSpec((1, d), lambda r: (0, 0)),
                  pl.BlockSpec((1, d), lambda r: (0, 0))],
        out_specs=pl.BlockSpec((ROW_TILE, d), lambda r: (r, 0)),
        out_shape=jax.ShapeDtypeStruct((n, d), F32),
        compiler_params=_params(("parallel",)),
        name="input_ln",
    )(x, g.reshape(1, d), b.reshape(1, d))


def _proj_kernel(h_ref, w_ref, kvg_ref, brow_ref, pm_ref, qa_ref, qi_ref, c_ref, sm_ref, g_ref):
    x = h_ref[...].astype(BF16)

    def mm(lo, width):
        return jnp.dot(x, w_ref[:, lo:lo + width], preferred_element_type=F32)

    pm_ref[...] = mm(PK_PM, 2048).astype(BF16)
    qa_ref[...] = mm(PK_QA, 512).astype(BF16)
    qi_ref[...] = mm(PK_QI, 256).astype(BF16)
    ckv = mm(PK_CKV, 128)
    c_ref[...] = ckv * lax.rsqrt(jnp.mean(ckv * ckv, axis=-1, keepdims=True) + 1e-6) * kvg_ref[...]
    sm_ref[...] = mm(PK_SM, 128) + brow_ref[...]
    g_ref[...] = mm(PK_G, 2048).astype(BF16)


def _pack_w_in(w):
    d = w.shape[0]
    cols = [w[:, 0:2048], w[:, 2056:2568], w[:, 2696:2952], w[:, 2568:2696],
            w[:, 2952:3016], w[:, 3016:3020], w[:, 2048:2056],
            jnp.zeros((d, LANES - 76), w.dtype), w[:, 3020:5068]]
    return jnp.concatenate(cols, axis=1).astype(BF16)


def _project(h, w_packed, kv_g, b_if):
    n, d = h.shape
    brow = jnp.concatenate([jnp.zeros((SM_IPRE,), F32), b_if.astype(F32),
                            jnp.zeros((LANES - SM_IPRE - 2 * M_HEADS,), F32)]).reshape(1, LANES)
    row = lambda width: pl.BlockSpec((ROW_TILE, width), lambda r: (r, 0))
    shp = lambda width, dt: jax.ShapeDtypeStruct((n, width), dt)
    return pl.pallas_call(
        _proj_kernel,
        grid=(n // ROW_TILE,),
        in_specs=[row(d),
                  pl.BlockSpec((d, PK_TOTAL), lambda r: (0, 0)),
                  pl.BlockSpec((1, KV_RANK), lambda r: (0, 0)),
                  pl.BlockSpec((1, LANES), lambda r: (0, 0))],
        out_specs=[row(2048), row(512), row(256), row(128), row(128), row(2048)],
        out_shape=[shp(2048, BF16), shp(512, BF16), shp(256, BF16), shp(128, F32), shp(128, F32),
                   shp(2048, BF16)],
        compiler_params=_params(("parallel",)),
        name="in_proj",
    )(h, w_packed, kv_g.reshape(1, KV_RANK), brow)


def _mlstm_chunk_math(q, k, v, li, lf, ct, n, m):
    L = q.shape[0]
    r = lax.broadcasted_iota(jnp.int32, (L, L), 0)
    c = lax.broadcasted_iota(jnp.int32, (L, L), 1)
    eye = r == c
    tril = c <= r
    lf_b = jnp.broadcast_to(lf, (L, L))
    li_b = jnp.broadcast_to(li, (L, L))
    lf_row = jnp.sum(jnp.where(eye, lf_b, 0.0), axis=0, keepdims=True)
    li_row = jnp.sum(jnp.where(eye, li_b, 0.0), axis=0, keepdims=True)
    b_col = jnp.sum(jnp.where(tril, jnp.broadcast_to(lf_row, (L, L)), 0.0), axis=1, keepdims=True)
    b_row = jnp.sum(jnp.where(r <= c, lf_b, 0.0), axis=0, keepdims=True)
    d = jnp.where(tril, b_col - b_row + li_row, -jnp.inf)
    inter = b_col + m
    m_t = jnp.maximum(inter, jnp.max(d, axis=1, keepdims=True))
    a = jnp.exp(inter - m_t)
    qb, kb, vb = q.astype(BF16), k.astype(BF16), v.astype(BF16)
    qk = lax.dot_general(qb, kb, (((1,), (1,)), ((), ())), preferred_element_type=F32)
    w = jnp.exp(d - m_t) * qk
    num = a * jnp.dot(qb, ct.astype(BF16), preferred_element_type=F32) + \
        jnp.dot(w.astype(BF16), vb, preferred_element_type=F32)
    den = a * jnp.sum(q * n, axis=1, keepdims=True) + jnp.sum(w, axis=1, keepdims=True)
    h = num / jnp.maximum(jnp.abs(den), jnp.exp(-m_t))
    b_last = b_col[L - 1:L, :]
    g = b_last - b_col + li
    m_new = jnp.maximum(b_last + m, jnp.max(g, axis=0, keepdims=True))
    decay = jnp.exp(b_last + m - m_new)
    kw = k * jnp.exp(g - m_new)
    ct_new = decay * ct + lax.dot_general(kw.astype(BF16), vb, (((0,), (0,)), ((), ())),
                                          preferred_element_type=F32)
    n_new = decay * n + jnp.sum(kw, axis=0, keepdims=True)
    return h, ct_new, n_new, m_new


def _mlstm_kernel(pm_ref, sm_ref, pmm_ref, smm_ref, cw_ref, mg_ref, out_ref, outm_ref,
                  ct_scr, n_scr, m_scr, x_scr):
    ci = pl.program_id(1)
    tail = 8

    def run_chunk(p_ref, s_ref, o_ref):
        L = p_ref.shape[0]
        x_scr[tail:tail + L, :] = p_ref[:, 0:2 * M_WIDTH].astype(F32)
        conv = cw_ref[0:1, :] * x_scr[tail - 3:tail - 3 + L, :]
        for j in range(1, CONV_WIDTH):
            conv = conv + cw_ref[j:j + 1, :] * x_scr[tail - 3 + j:tail - 3 + j + L, :]
        x_scr[0:tail, :] = x_scr[L:L + tail, :]
        qk = conv * _sigmoid(conv)
        for hd in range(M_HEADS):
            lo = hd * M_HEAD_DIM
            q = qk[:, lo:lo + M_HEAD_DIM] * (M_HEAD_DIM ** -0.5)
            k = qk[:, M_WIDTH + lo:M_WIDTH + lo + M_HEAD_DIM]
            v = p_ref[:, 2 * M_WIDTH + lo:2 * M_WIDTH + lo + M_HEAD_DIM].astype(F32)
            li = s_ref[:, SM_IPRE + hd:SM_IPRE + hd + 1]
            f = s_ref[:, SM_FPRE + hd:SM_FPRE + hd + 1]
            lf = jnp.minimum(f, 0.0) - jnp.log1p(jnp.exp(-jnp.abs(f)))
            h, ct_new, n_new, m_new = _mlstm_chunk_math(
                q, k, v, li, lf, ct_scr[hd], n_scr[hd:hd + 1, :], m_scr[hd:hd + 1, 0:1])
            ct_scr[hd] = ct_new
            n_scr[hd:hd + 1, :] = n_new
            m_scr[hd:hd + 1, :] = jnp.broadcast_to(m_new, (1, LANES))
            mu = jnp.mean(h, axis=-1, keepdims=True)
            hc = h - mu
            var = jnp.mean(hc * hc, axis=-1, keepdims=True)
            o_gate = _sigmoid(p_ref[:, 3 * M_WIDTH + lo:3 * M_WIDTH + lo + M_HEAD_DIM].astype(F32))
            o_ref[:, lo:lo + M_HEAD_DIM] = (hc * lax.rsqrt(var + 1e-5) * mg_ref[:, lo:lo + M_HEAD_DIM]
                                            * o_gate).astype(o_ref.dtype)

    @pl.when(ci == 0)
    def _():
        ct_scr[...] = jnp.zeros_like(ct_scr)
        n_scr[...] = jnp.zeros_like(n_scr)
        m_scr[...] = jnp.zeros_like(m_scr)
        x_scr[...] = jnp.zeros_like(x_scr)
        run_chunk(pmm_ref, smm_ref, outm_ref)

    run_chunk(pm_ref, sm_ref, out_ref)


def _mlstm(pm, sm, conv_w, mnorm_g, batch, seq):
    n_real = batch * seq
    chunk = min(M_CHUNK, seq)
    nc = seq // chunk
    meta_blk = n_real // N_META
    out_real, out_meta = pl.pallas_call(
        _mlstm_kernel,
        grid=(batch, nc),
        in_specs=[pl.BlockSpec((chunk, 4 * M_WIDTH), lambda b, c: (b * nc + c, 0)),
                  pl.BlockSpec((chunk, LANES), lambda b, c: (b * nc + c, 0)),
                  pl.BlockSpec((N_META, 4 * M_WIDTH), lambda b, c: (meta_blk + b, 0)),
                  pl.BlockSpec((N_META, LANES), lambda b, c: (meta_blk + b, 0)),
                  pl.BlockSpec((CONV_WIDTH, 2 * M_WIDTH), lambda b, c: (0, 0)),
                  pl.BlockSpec((1, M_WIDTH), lambda b, c: (0, 0))],
        out_specs=[pl.BlockSpec((chunk, M_WIDTH), lambda b, c: (b * nc + c, 0)),
                   pl.BlockSpec((N_META, M_WIDTH), lambda b, c: (b, 0))],
        out_shape=[jax.ShapeDtypeStruct((n_real, M_WIDTH), BF16),
                   jax.ShapeDtypeStruct((batch * N_META, M_WIDTH), BF16)],
        scratch_shapes=[pltpu.VMEM((M_HEADS, M_HEAD_DIM, M_HEAD_DIM), F32),
                        pltpu.VMEM((8, LANES), F32),
                        pltpu.VMEM((8, LANES), F32),
                        pltpu.VMEM((chunk + 8, 2 * M_WIDTH), F32)],
        compiler_params=_params(("arbitrary", "arbitrary")),
        name="mlstm",
    )(pm, sm, pm, sm, conv_w.astype(F32), mnorm_g.reshape(1, M_WIDTH).astype(F32))
    return out_real, out_meta


def _rel_bucket_np(dist):
    n = np.maximum(dist, 0)
    nf = np.maximum(n, REL_MAX_EXACT).astype(np.float32)
    large = REL_MAX_EXACT + (np.log(nf / np.float32(REL_MAX_EXACT)) /
                             np.float32(math.log(REL_MAX_DIST / REL_MAX_EXACT))
                             * np.float32(REL_BUCKETS - REL_MAX_EXACT)).astype(np.int32)
    large = np.minimum(large, REL_BUCKETS - 1)
    return np.where(n < REL_MAX_EXACT, n, large).astype(np.int32)


def _bias_tables(rel_bias):
    q = np.arange(LANES)[:, None]
    k = np.arange(LANES)[None, :]
    far = 4 * LANES
    assert (_rel_bucket_np(np.arange(LANES + 1, far)) == REL_BUCKETS - 1).all()
    far_idx = np.full((LANES, LANES), REL_BUCKETS - 1, np.int32)
    near_idx = np.stack([_rel_bucket_np(q - k), _rel_bucket_np(LANES + q - k), far_idx])
    meta_idx = np.stack([_rel_bucket_np(q + N_META - np.minimum(k, N_META - 1)), far_idx])
    mq = np.arange(N_META)[:, None]
    mm_idx = _rel_bucket_np(mq - np.minimum(k, N_META - 1))
    rb = rel_bias.astype(F32)
    gather = lambda idx: jnp.moveaxis(rb[idx], -1, -3)
    rel = lambda idx: (gather(idx) - rb[REL_BUCKETS - 1][:, None, None]) * LOG2E
    return rel(near_idx), rel(meta_idx), gather(mm_idx)


def _dsa_kernel(top_k, qa_ref, qi_ref, sm_ref, cb_ref, smb_ref, cm_ref, qam_ref,
                wuk_ref, wuv_ref, near_ref, metab_ref, mmb_ref,
                out_ref, outm_ref,
                keys_scr, caug_scr, kbf_scr, cmaug_scr, qs_scr, m_scr, acc_scr):
    i = pl.program_id(1)
    T = LANES
    H = A_HEADS
    col = lax.broadcasted_iota(jnp.int32, (T, T), 1)
    row = lax.broadcasted_iota(jnp.int32, (T, T), 0)
    nt = (((1,), (1,)), ((), ()))

    def q_latent(qa, hd, scale):
        ql = jnp.dot(qa[:, hd * A_HEAD_DIM:(hd + 1) * A_HEAD_DIM], wuk_ref[hd],
                     preferred_element_type=F32)
        return (ql * scale).astype(BF16)

    def ones_column(n):
        return jnp.where(lax.broadcasted_iota(jnp.int32, (n, T), 1) == 0, 1.0, 0.0).astype(BF16)

    @pl.when(i == 0)
    def _():
        caug_scr[:, 0:KV_RANK] = cb_ref[...].astype(BF16)
        caug_scr[:, KV_RANK:KV_RANK + T] = ones_column(caug_scr.shape[0])
        kbf_scr[...] = smb_ref[:, SM_KIDX:SM_KIDX + IDX_DIM].astype(BF16)
        cm_pad = jnp.concatenate([cm_ref[...], jnp.zeros((T - N_META, KV_RANK), F32)], axis=0)
        cmaug_scr[:, 0:KV_RANK] = cm_pad.astype(BF16)
        cmaug_scr[:, KV_RANK:KV_RANK + T] = ones_column(T)
        cmk = cmaug_scr[:, 0:KV_RANK]
        qam = qam_ref[...]
        mrow = lax.broadcasted_iota(jnp.int32, (N_META, T), 0)
        mcol = lax.broadcasted_iota(jnp.int32, (N_META, T), 1)
        for hd in range(H):
            lg = lax.dot_general(q_latent(qam, hd, A_HEAD_DIM ** -0.5), cmk, nt,
                                 preferred_element_type=F32) + mmb_ref[hd]
            lg = jnp.where(mcol <= mrow, lg, NEG_BIG)
            p = jnp.exp(lg - jnp.max(lg, axis=1, keepdims=True))
            p = p / jnp.sum(p, axis=1, keepdims=True)
            o = jnp.dot(p.astype(BF16), cmk, preferred_element_type=F32)
            outm_ref[:, hd * A_HEAD_DIM:(hd + 1) * A_HEAD_DIM] = jnp.dot(
                o.astype(BF16), wuv_ref[hd], preferred_element_type=F32).astype(outm_ref.dtype)

    qa = qa_ref[...]
    for hd in range(H):
        qs_scr[hd * T:(hd + 1) * T, :] = q_latent(qa, hd, A_HEAD_DIM ** -0.5 * LOG2E)
    qi = qi_ref[...]
    wv = sm_ref[:, SM_WIDX:SM_WIDX + IDX_HEADS] * IDX_SCALE
    t_col = i * T + lax.broadcasted_iota(jnp.int32, (T, 1), 0)
    n_chunks = (i + SCORE_CHUNK) // SCORE_CHUNK
    CW = SCORE_CHUNK * T

    def score_body(cix, carry):
        kc = kbf_scr[pl.ds(pl.multiple_of(cix * CW, CW), CW), :]
        acc = jnp.zeros((T, CW), F32)
        for hh in range(IDX_HEADS):
            s = lax.dot_general(qi[:, hh * IDX_DIM:(hh + 1) * IDX_DIM], kc, nt,
                                preferred_element_type=F32)
            acc = acc + wv[:, hh:hh + 1] * jnp.maximum(s, 0.0)
        acc = jnp.where(acc == 0.0, 0.0, acc)
        bits = lax.bitcast_convert_type(acc, jnp.int32)
        key = jnp.where(bits < 0, bits ^ jnp.int32(0x7FFFFFFF), bits)
        s_idx = cix * CW + lax.broadcasted_iota(jnp.int32, (T, CW), 1)
        key = jnp.where(s_idx <= t_col, key, jnp.int32(INT_MIN))
        for u in range(SCORE_CHUNK):
            keys_scr[cix * SCORE_CHUNK + u] = key[:, u * T:(u + 1) * T]
        return carry

    lax.fori_loop(0, n_chunks, score_body, 0)

    def count_where(pred_fn):
        def body(cix, cnt):
            for u in range(SCORE_CHUNK):
                cnt = cnt + jnp.where(pred_fn(keys_scr[cix * SCORE_CHUNK + u]), 1.0, 0.0)
            return cnt
        cnt = lax.fori_loop(0, n_chunks, body, jnp.zeros((T, T), F32))
        return jnp.sum(cnt, axis=1, keepdims=True)

    def bit_body(bi, ans):
        cand_u = ans | lax.shift_left(jnp.int32(1), jnp.int32(31) - bi)
        cand = cand_u ^ jnp.int32(INT_MIN)
        total = count_where(lambda kk: kk >= cand)
        return jnp.where(total >= float(top_k), cand_u, ans)

    ans = lax.fori_loop(0, 32, bit_body, jnp.zeros((T, 1), jnp.int32))
    thr = ans ^ jnp.int32(INT_MIN)
    need = float(top_k) - count_where(lambda kk: kk > thr)

    m_scr[...] = jnp.full_like(m_scr, NEG_BIG)
    acc_scr[...] = jnp.zeros_like(acc_scr)

    def attend(c_aug, sels, bias_fns):
        n_sub = len(sels)
        lg = lax.dot_general(qs_scr[...], c_aug[:, 0:KV_RANK], nt, preferred_element_type=F32)
        m_old = m_scr[...]
        pieces, maxes = [], []
        for hd in range(H):
            mx = None
            for u in range(n_sub):
                x = lg[hd * T:(hd + 1) * T, u * T:(u + 1) * T]
                if bias_fns[u] is not None:
                    x = x + bias_fns[u](hd)
                x = jnp.where(sels[u], x, NEG_BIG)
                pieces.append(x)
                mx = x if mx is None else jnp.maximum(mx, x)
            maxes.append(jnp.max(mx, axis=1, keepdims=True))
        m_new = jnp.maximum(m_old, jnp.concatenate(maxes, axis=0))
        alpha = jnp.exp2(m_old - m_new)
        p_rows = []
        for hd in range(H):
            mh = m_new[hd * T:(hd + 1) * T, :]
            ph = [jnp.exp2(pieces[hd * n_sub + u] - mh).astype(BF16) for u in range(n_sub)]
            p_rows.append(ph[0] if n_sub == 1 else jnp.concatenate(ph, axis=1))
        p = jnp.concatenate(p_rows, axis=0)
        acc_scr[...] = alpha * acc_scr[...] + jnp.dot(p, c_aug, preferred_element_type=F32)
        m_scr[...] = m_new

    meta_sel = jnp.minimum(i, 1)
    attend(cmaug_scr[...], [col < N_META], [lambda hd: metab_ref[meta_sel, hd]])

    upper = (row < col).astype(BF16)

    def key_step(step, seen, near):
        sels, bias_fns = [], []
        for u in range(KEY_SUB):
            j = KEY_SUB * step + u
            kk = keys_scr[j]
            eq = kk == thr
            eqf = jnp.where(eq, 1.0, 0.0)
            before = jnp.dot(eqf.astype(BF16), upper, preferred_element_type=F32) + seen
            sel = (kk > thr) | (eq & (before < need))
            if near:
                sel = sel & ((j * T + col) <= (i * T + row))
                dsel = jnp.clip(i - j, 0, 2)
                bias_fns.append(lambda hd, dsel=dsel: near_ref[dsel, hd])
            else:
                bias_fns.append(None)
            seen = seen + jnp.sum(eqf, axis=1, keepdims=True)
            sels.append(sel)
        attend(caug_scr[pl.ds(pl.multiple_of(step * KEY_SUB * T, KEY_SUB * T), KEY_SUB * T), :], sels, bias_fns)
        return seen

    n_far = jnp.maximum(i - 1, 0) // KEY_SUB
    n_steps = (i + KEY_SUB) // KEY_SUB
    seen = lax.fori_loop(0, n_far, lambda s, c: key_step(s, c, False), jnp.zeros((T, 1), F32))
    lax.fori_loop(n_far, n_steps, lambda s, c: key_step(s, c, True), seen)

    for hd in range(H):
        rs = slice(hd * T, (hd + 1) * T)
        o = acc_scr[rs, 0:KV_RANK] / acc_scr[rs, KV_RANK:KV_RANK + 1]
        out_ref[:, hd * A_HEAD_DIM:(hd + 1) * A_HEAD_DIM] = jnp.dot(
            o.astype(BF16), wuv_ref[hd], preferred_element_type=F32).astype(out_ref.dtype)


def _dsa(qa, qi, sm, c, w_uk, w_uv, tables, batch, seq):
    n_real = batch * seq
    nq = seq // LANES
    n_tiles = ((nq + SCORE_CHUNK - 1) // SCORE_CHUNK) * SCORE_CHUNK
    top_k = min(TOPK_MAX, seq // 4)
    meta_blk = n_real // N_META
    near, metab, mmb = tables
    wuk_t = jnp.swapaxes(w_uk, 1, 2).astype(BF16)
    wuv = w_uv.astype(BF16)
    full = lambda a: pl.BlockSpec(a.shape, lambda b, i: (0,) * a.ndim)
    assert seq % (SCORE_CHUNK * LANES) == 0 and nq % KEY_SUB == 0 and SCORE_CHUNK % KEY_SUB == 0
    out_real, out_meta = pl.pallas_call(
        functools.partial(_dsa_kernel, top_k),
        grid=(batch, nq),
        in_specs=[pl.BlockSpec((LANES, A_WIDTH), lambda b, i: (b * nq + i, 0)),
                  pl.BlockSpec((LANES, IDX_HEADS * IDX_DIM), lambda b, i: (b * nq + i, 0)),
                  pl.BlockSpec((LANES, LANES), lambda b, i: (b * nq + i, 0)),
                  pl.BlockSpec((seq, KV_RANK), lambda b, i: (b, 0)),
                  pl.BlockSpec((seq, LANES), lambda b, i: (b, 0)),
                  pl.BlockSpec((N_META, KV_RANK), lambda b, i: (meta_blk + b, 0)),
                  pl.BlockSpec((N_META, A_WIDTH), lambda b, i: (meta_blk + b, 0)),
                  full(wuk_t), full(wuv), full(near), full(metab), full(mmb)],
        out_specs=[pl.BlockSpec((LANES, A_WIDTH), lambda b, i: (b * nq + i, 0)),
                   pl.BlockSpec((N_META, A_WIDTH), lambda b, i: (b, 0))],
        out_shape=[jax.ShapeDtypeStruct((n_real, A_WIDTH), BF16),
                   jax.ShapeDtypeStruct((batch * N_META, A_WIDTH), BF16)],
        scratch_shapes=[pltpu.VMEM((n_tiles, LANES, LANES), jnp.int32),
                        pltpu.VMEM((seq, KV_RANK + LANES), BF16),
                        pltpu.VMEM((seq, IDX_DIM), BF16),
                        pltpu.VMEM((LANES, KV_RANK + LANES), BF16),
                        pltpu.VMEM((A_HEADS * LANES, KV_RANK), BF16),
                        pltpu.VMEM((A_HEADS * LANES, 1), F32),
                        pltpu.VMEM((A_HEADS * LANES, KV_RANK + LANES), F32)],
        compiler_params=_params(("arbitrary", "arbitrary")),
        name="dsa",
    )(qa, qi, sm, c, sm, c, qa, wuk_t, wuv, near, metab, mmb)
    return out_real, out_meta


def _merge_kernel(alpha, h_ref, hm_ref, ha_ref, g_ref, wbm_ref, wba_ref, wo_ref, lg_ref, lb_ref,
                  wr_ref, br_ref, h1_ref, comb_ref):
    d = h_ref.shape[1]
    gm = _sigmoid(g_ref[:, 0:d].astype(F32))
    ga = _sigmoid(g_ref[:, d:2 * d].astype(F32))
    y = gm * jnp.dot(hm_ref[...], wbm_ref[...], preferred_element_type=F32) + \
        ga * jnp.dot(ha_ref[...], wba_ref[...], preferred_element_type=F32)
    z = alpha * h_ref[...] + jnp.dot(y.astype(BF16), wo_ref[...], preferred_element_type=F32)
    h1 = _layer_norm_rows(z, lg_ref[...], lb_ref[...], 1e-5)
    h1_ref[...] = h1

    tm = h1.shape[0]
    logits_t = lax.dot_general(wr_ref[...], h1.astype(BF16), (((1,), (1,)), ((), ())),
                               preferred_element_type=F32)
    scores = _sigmoid(logits_t[0:N_EXPERTS, :])
    sel = scores + br_ref[0:N_EXPERTS, :]
    best = None
    for gidx in range(N_GROUPS):
        r0, r1, r2, r3 = (sel[gidx * GROUP_SIZE + u:gidx * GROUP_SIZE + u + 1, :] for u in range(4))
        a, b = jnp.maximum(r0, r1), jnp.minimum(r0, r1)
        c, dd = jnp.maximum(r2, r3), jnp.minimum(r2, r3)
        gs = jnp.maximum(a, c) + jnp.maximum(jnp.minimum(a, c), jnp.maximum(b, dd))
        if best is None:
            best, bg = gs, jnp.zeros((1, tm), jnp.int32)
        else:
            upd = gs > best
            bg = jnp.where(upd, gidx, bg)
            best = jnp.where(upd, gs, best)
    eidx = lax.broadcasted_iota(jnp.int32, (N_EXPERTS, tm), 0)
    masked = jnp.where((eidx // GROUP_SIZE) == bg, sel, -jnp.inf)
    v1 = jnp.max(masked, axis=0, keepdims=True)
    i1 = jnp.min(jnp.where(masked == v1, eidx, N_EXPERTS), axis=0, keepdims=True)
    masked2 = jnp.where(eidx == i1, -jnp.inf, masked)
    v2 = jnp.max(masked2, axis=0, keepdims=True)
    i2 = jnp.min(jnp.where(masked2 == v2, eidx, N_EXPERTS), axis=0, keepdims=True)
    s1 = jnp.sum(jnp.where(eidx == i1, scores, 0.0), axis=0, keepdims=True)
    s2 = jnp.sum(jnp.where(eidx == i2, scores, 0.0), axis=0, keepdims=True)
    tot = s1 + s2
    comb_t = jnp.where(eidx == i1, s1 / tot, 0.0) + jnp.where(eidx == i2, s2 / tot, 0.0)
    comb_pad = jnp.concatenate([comb_t, jnp.zeros((LANES - N_EXPERTS, tm), F32)], axis=0)
    comb_ref[...] = comb_pad.T


def _merge(h, hm, ha, g, w_bm, w_ba, w_o, ln_g, ln_b, w_router, b_router, alpha):
    n, d = h.shape
    wr_t = jnp.zeros((LANES, d), F32).at[0:N_EXPERTS].set(w_router.T).astype(BF16)
    br = jnp.zeros((LANES, 1), F32).at[0:N_EXPERTS, 0].set(b_router)
    row = lambda width: pl.BlockSpec((ROW_TILE, width), lambda r: (r, 0))
    full = lambda a: pl.BlockSpec(a.shape, lambda r: (0,) * a.ndim)
    args = (h, hm, ha, g, w_bm.astype(BF16), w_ba.astype(BF16), w_o.astype(BF16),
            ln_g.reshape(1, d), ln_b.reshape(1, d), wr_t, br)
    return pl.pallas_call(
        functools.partial(_merge_kernel, alpha),
        grid=(n // ROW_TILE,),
        in_specs=[row(d), row(M_WIDTH), row(A_WIDTH), row(2 * d)] + [full(a) for a in args[4:]],
        out_specs=[row(d), row(LANES)],
        out_shape=[jax.ShapeDtypeStruct((n, d), F32), jax.ShapeDtypeStruct((n, LANES), F32)],
        compiler_params=_params(("parallel",)),
        name="merge",
    )(*args)


def _moe_kernel(alpha, h_ref, comb_ref, wg_ref, wu_ref, wd_ref, lg_ref, lb_ref, out_ref,
                xb_scr, acc_scr):
    e = pl.program_id(1)

    @pl.when(e == 0)
    def _():
        xb_scr[...] = h_ref[...].astype(BF16)
        acc_scr[...] = jnp.zeros_like(acc_scr)

    xb = xb_scr[...]
    gate = jnp.dot(xb, wg_ref[...], preferred_element_type=F32)
    up = jnp.dot(xb, wu_ref[...], preferred_element_type=F32)
    he = gate * _sigmoid(gate) * up
    o = jnp.dot(he.astype(BF16), wd_ref[...], preferred_element_type=F32)
    comb = comb_ref[...]
    lane = lax.broadcasted_iota(jnp.int32, comb.shape, 1)
    ce = jnp.sum(jnp.where(lane == e, comb, 0.0), axis=1, keepdims=True)
    acc_scr[...] += ce * o

    @pl.when(e == pl.num_programs(1) - 1)
    def _():
        z = alpha * h_ref[...] + acc_scr[...]
        out_ref[...] = _layer_norm_rows(z, lg_ref[...], lb_ref[...], 1e-5)


def _moe(h, comb, w_gate, w_up, w_down, ln_g, ln_b, alpha):
    n, d = h.shape
    de = w_gate.shape[-1]
    tm = MOE_ROW_TILE if n % MOE_ROW_TILE == 0 else ROW_TILE
    return pl.pallas_call(
        functools.partial(_moe_kernel, alpha),
        grid=(n // tm, N_EXPERTS),
        in_specs=[pl.BlockSpec((tm, d), lambda r, e: (r, 0)),
                  pl.BlockSpec((tm, LANES), lambda r, e: (r, 0)),
                  pl.BlockSpec((None, d, de), lambda r, e: (e, 0, 0)),
                  pl.BlockSpec((None, d, de), lambda r, e: (e, 0, 0)),
                  pl.BlockSpec((None, de, d), lambda r, e: (e, 0, 0)),
                  pl.BlockSpec((1, d), lambda r, e: (0, 0)),
                  pl.BlockSpec((1, d), lambda r, e: (0, 0))],
        out_specs=pl.BlockSpec((tm, d), lambda r, e: (r, 0)),
        out_shape=jax.ShapeDtypeStruct((n, d), F32),
        scratch_shapes=[pltpu.VMEM((tm, d), BF16), pltpu.VMEM((tm, d), F32)],
        compiler_params=_params(("parallel", "arbitrary")),
        name="moe",
    )(h, comb, w_gate.astype(BF16), w_up.astype(BF16), w_down.astype(BF16),
      ln_g.reshape(1, d), ln_b.reshape(1, d))


def _with_meta(real, meta, n_pad):
    pad = n_pad - real.shape[0] - meta.shape[0]
    return jnp.concatenate([real, meta, jnp.zeros((pad, real.shape[1]), real.dtype)], axis=0)


def kernel(x, meta_tokens, ln_in_g, ln_in_b, w_in, conv_w, b_if, mnorm_g, kv_norm_g, w_uk, w_uv,
           w_branch_m, w_branch_a, w_out, ln1_g, ln1_b, w_router, b_router, w_gate, w_up, w_down,
           ln2_g, ln2_b, rel_bias):
    batch, seq, d = x.shape
    depth = w_in.shape[0]
    alpha = (2 * depth) ** 0.25
    n_real = batch * seq
    n_meta = batch * N_META
    tile = math.lcm(ROW_TILE, MOE_ROW_TILE)
    n_pad = -(-(n_real + n_meta) // tile) * tile
    assert seq % LANES == 0 and n_real % N_META == 0

    h = _with_meta(x.reshape(n_real, d), jnp.tile(meta_tokens.astype(x.dtype), (batch, 1)), n_pad)
    h = _input_ln(h, ln_in_g, ln_in_b)
    tables = _bias_tables(rel_bias)
    for l in range(depth):
        pm, qa, qi, c, sm, g = _project(h, _pack_w_in(w_in[l]), kv_norm_g[l], b_if[l])
        hm_real, hm_meta = _mlstm(pm, sm, conv_w[l], mnorm_g[l], batch, seq)
        ha_real, ha_meta = _dsa(qa, qi, sm, c, w_uk[l], w_uv[l], tables, batch, seq)
        hm = _with_meta(hm_real, hm_meta, n_pad)
        ha = _with_meta(ha_real, ha_meta, n_pad)
        h1, comb = _merge(h, hm, ha, g, w_branch_m[l], w_branch_a[l], w_out[l], ln1_g[l], ln1_b[l],
                          w_router, b_router, alpha)
        h = _moe(h1, comb, w_gate[l], w_up[l], w_down[l], ln2_g[l], ln2_b[l], alpha)
    return h[:n_real].reshape(batch, seq, d)
```

```python
import functools
import math

import numpy as np
import jax
import jax.numpy as jnp
from jax import lax
from jax.experimental import pallas as pl
from jax.experimental.pallas import tpu as pltpu

F32 = jnp.float32
BF16 = jnp.bfloat16

N_META = 16
M_HEADS = 4
M_HEAD_DIM = 128
M_WIDTH = M_HEADS * M_HEAD_DIM
CONV_WIDTH = 4
A_HEADS = 8
A_HEAD_DIM = 64
A_WIDTH = A_HEADS * A_HEAD_DIM
KV_RANK = 128
IDX_HEADS = 4
IDX_DIM = 64
IDX_SCALE = (IDX_HEADS * IDX_DIM) ** -0.5
TOPK_MAX = 256
REL_BUCKETS = 32
REL_MAX_EXACT = 16
REL_MAX_DIST = 128
N_EXPERTS = 16
N_GROUPS = 4
GROUP_SIZE = N_EXPERTS // N_GROUPS

LANES = 128
ROW_TILE = 256
MOE_ROW_TILE = 768
M_CHUNK = 256
SCORE_CHUNK = 4
KEY_SUB = 4
ATT_GROUPS = 2
VMEM_LIMIT = 56 * 1024 * 1024
NEG_BIG = -1e30
INT_MIN = -2 ** 31
LOG2E = math.log2(math.e)

PK_PM = 0
PK_QA = 2048
PK_QI = 2560
PK_CKV = 2816
PK_SM = 2944
PK_G = 3072
PK_TOTAL = 5120
SM_KIDX = 0
SM_WIDX = 64
SM_IPRE = 68
SM_FPRE = 72


def _params(sem):
    return pltpu.CompilerParams(dimension_semantics=sem, vmem_limit_bytes=VMEM_LIMIT)


def _sigmoid(x):
    return 1.0 / (1.0 + jnp.exp(-x))


def _layer_norm_rows(x, g, b, eps):
    mu = jnp.mean(x, axis=-1, keepdims=True)
    xc = x - mu
    var = jnp.mean(xc * xc, axis=-1, keepdims=True)
    return xc * lax.rsqrt(var + eps) * g + b


def _ln_kernel(x_ref, g_ref, b_ref, o_ref):
    o_ref[...] = _layer_norm_rows(x_ref[...], g_ref[...], b_ref[...], 1e-5)


def _input_ln(x, g, b):
    n, d = x.shape
    return pl.pallas_call(
        _ln_kernel,
        grid=(n // ROW_TILE,),
        in_specs=[pl.BlockSpec((ROW_TILE, d), lambda r: (r, 0)),
                  pl.BlockSpec((1, d), lambda r: (0, 0)),
                  pl.BlockSpec((1, d), lambda r: (0, 0))],
        out_specs=pl.BlockSpec((ROW_TILE, d), lambda r: (r, 0)),
        out_shape=jax.ShapeDtypeStruct((n, d), F32),
        compiler_params=_params(("parallel",)),
        name="input_ln",
    )(x, g.reshape(1, d), b.reshape(1, d))


def _proj_kernel(h_ref, w_ref, kvg_ref, brow_ref, pm_ref, qa_ref, qi_ref, c_ref, sm_ref, g_ref):
    x = h_ref[...].astype(BF16)

    def mm(lo, width):
        return jnp.dot(x, w_ref[:, lo:lo + width], preferred_element_type=F32)

    pm_ref[...] = mm(PK_PM, 2048).astype(BF16)
    qa_ref[...] = mm(PK_QA, 512).astype(BF16)
    qi_ref[...] = mm(PK_QI, 256).astype(BF16)
    ckv = mm(PK_CKV, 128)
    c_ref[...] = ckv * lax.rsqrt(jnp.mean(ckv * ckv, axis=-1, keepdims=True) + 1e-6) * kvg_ref[...]
    sm_ref[...] = mm(PK_SM, 128) + brow_ref[...]
    g_ref[...] = mm(PK_G, 2048).astype(BF16)


def _pack_w_in(w):
    d = w.shape[0]
    cols = [w[:, 0:2048], w[:, 2056:2568], w[:, 2696:2952], w[:, 2568:2696],
            w[:, 2952:3016], w[:, 3016:3020], w[:, 2048:2056],
            jnp.zeros((d, LANES - 76), w.dtype), w[:, 3020:5068]]
    return jnp.concatenate(cols, axis=1).astype(BF16)


def _project(h, w_packed, kv_g, b_if):
    n, d = h.shape
    brow = jnp.concatenate([jnp.zeros((SM_IPRE,), F32), b_if.astype(F32),
                            jnp.zeros((LANES - SM_IPRE - 2 * M_HEADS,), F32)]).reshape(1, LANES)
    row = lambda width: pl.BlockSpec((ROW_TILE, width), lambda r: (r, 0))
    shp = lambda width, dt: jax.ShapeDtypeStruct((n, width), dt)
    return pl.pallas_call(
        _proj_kernel,
        grid=(n // ROW_TILE,),
        in_specs=[row(d),
                  pl.BlockSpec((d, PK_TOTAL), lambda r: (0, 0)),
                  pl.BlockSpec((1, KV_RANK), lambda r: (0, 0)),
                  pl.BlockSpec((1, LANES), lambda r: (0, 0))],
        out_specs=[row(2048), row(512), row(256), row(128), row(128), row(2048)],
        out_shape=[shp(2048, BF16), shp(512, BF16), shp(256, BF16), shp(128, F32), shp(128, F32),
                   shp(2048, BF16)],
        compiler_params=_params(("parallel",)),
        name="in_proj",
    )(h, w_packed, kv_g.reshape(1, KV_RANK), brow)


def _mlstm_chunk_math(q, k, v, li, lf, ct, n, m):
    L = q.shape[0]
    r = lax.broadcasted_iota(jnp.int32, (L, L), 0)
    c = lax.broadcasted_iota(jnp.int32, (L, L), 1)
    eye = r == c
    tril = c <= r
    lf_b = jnp.broadcast_to(lf, (L, L))
    li_b = jnp.broadcast_to(li, (L, L))
    lf_row = jnp.sum(jnp.where(eye, lf_b, 0.0), axis=0, keepdims=True)
    li_row = jnp.sum(jnp.where(eye, li_b, 0.0), axis=0, keepdims=True)
    b_col = jnp.sum(jnp.where(tril, jnp.broadcast_to(lf_row, (L, L)), 0.0), axis=1, keepdims=True)
    b_row = jnp.sum(jnp.where(r <= c, lf_b, 0.0), axis=0, keepdims=True)
    d = jnp.where(tril, b_col - b_row + li_row, -jnp.inf)
    inter = b_col + m
    m_t = jnp.maximum(inter, jnp.max(d, axis=1, keepdims=True))
    a = jnp.exp(inter - m_t)
    qb, kb, vb = q.astype(BF16), k.astype(BF16), v.astype(BF16)
    qk = lax.dot_general(qb, kb, (((1,), (1,)), ((), ())), preferred_element_type=F32)
    w = jnp.exp(d - m_t) * qk
    num = a * jnp.dot(qb, ct.astype(BF16), preferred_element_type=F32) + \
        jnp.dot(w.astype(BF16), vb, preferred_element_type=F32)
    den = a * jnp.sum(q * n, axis=1, keepdims=True) + jnp.sum(w, axis=1, keepdims=True)
    h = num / jnp.maximum(jnp.abs(den), jnp.exp(-m_t))
    b_last = b_col[L - 1:L, :]
    g = b_last - b_col + li
    m_new = jnp.maximum(b_last + m, jnp.max(g, axis=0, keepdims=True))
    decay = jnp.exp(b_last + m - m_new)
    kw = k * jnp.exp(g - m_new)
    ct_new = decay * ct + lax.dot_general(kw.astype(BF16), vb, (((0,), (0,)), ((), ())),
                                          preferred_element_type=F32)
    n_new = decay * n + jnp.sum(kw, axis=0, keepdims=True)
    return h, ct_new, n_new, m_new


def _mlstm_kernel(pm_ref, sm_ref, pmm_ref, smm_ref, cw_ref, mg_ref, out_ref, outm_ref,
                  ct_scr, n_scr, m_scr, x_scr):
    ci = pl.program_id(1)
    tail = 8

    def run_chunk(p_ref, s_ref, o_ref):
        L = p_ref.shape[0]
        x_scr[tail:tail + L, :] = p_ref[:, 0:2 * M_WIDTH].astype(F32)
        conv = cw_ref[0:1, :] * x_scr[tail - 3:tail - 3 + L, :]
        for j in range(1, CONV_WIDTH):
            conv = conv + cw_ref[j:j + 1, :] * x_scr[tail - 3 + j:tail - 3 + j + L, :]
        x_scr[0:tail, :] = x_scr[L:L + tail, :]
        qk = conv * _sigmoid(conv)
        for hd in range(M_HEADS):
            lo = hd * M_HEAD_DIM
            q = qk[:, lo:lo + M_HEAD_DIM] * (M_HEAD_DIM ** -0.5)
            k = qk[:, M_WIDTH + lo:M_WIDTH + lo + M_HEAD_DIM]
            v = p_ref[:, 2 * M_WIDTH + lo:2 * M_WIDTH + lo + M_HEAD_DIM].astype(F32)
            li = s_ref[:, SM_IPRE + hd:SM_IPRE + hd + 1]
            f = s_ref[:, SM_FPRE + hd:SM_FPRE + hd + 1]
            lf = jnp.minimum(f, 0.0) - jnp.log1p(jnp.exp(-jnp.abs(f)))
            h, ct_new, n_new, m_new = _mlstm_chunk_math(
                q, k, v, li, lf, ct_scr[hd], n_scr[hd:hd + 1, :], m_scr[hd:hd + 1, 0:1])
            ct_scr[hd] = ct_new
            n_scr[hd:hd + 1, :] = n_new
            m_scr[hd:hd + 1, :] = jnp.broadcast_to(m_new, (1, LANES))
            mu = jnp.mean(h, axis=-1, keepdims=True)
            hc = h - mu
            var = jnp.mean(hc * hc, axis=-1, keepdims=True)
            o_gate = _sigmoid(p_ref[:, 3 * M_WIDTH + lo:3 * M_WIDTH + lo + M_HEAD_DIM].astype(F32))
            o_ref[:, lo:lo + M_HEAD_DIM] = (hc * lax.rsqrt(var + 1e-5) * mg_ref[:, lo:lo + M_HEAD_DIM]
                                            * o_gate).astype(o_ref.dtype)

    @pl.when(ci == 0)
    def _():
        ct_scr[...] = jnp.zeros_like(ct_scr)
        n_scr[...] = jnp.zeros_like(n_scr)
        m_scr[...] = jnp.zeros_like(m_scr)
        x_scr[...] = jnp.zeros_like(x_scr)
        run_chunk(pmm_ref, smm_ref, outm_ref)

    run_chunk(pm_ref, sm_ref, out_ref)


def _mlstm(pm, sm, conv_w, mnorm_g, batch, seq):
    n_real = batch * seq
    chunk = min(M_CHUNK, seq)
    nc = seq // chunk
    meta_blk = n_real // N_META
    out_real, out_meta = pl.pallas_call(
        _mlstm_kernel,
        grid=(batch, nc),
        in_specs=[pl.BlockSpec((chunk, 4 * M_WIDTH), lambda b, c: (b * nc + c, 0)),
                  pl.BlockSpec((chunk, LANES), lambda b, c: (b * nc + c, 0)),
                  pl.BlockSpec((N_META, 4 * M_WIDTH), lambda b, c: (meta_blk + b, 0)),
                  pl.BlockSpec((N_META, LANES), lambda b, c: (meta_blk + b, 0)),
                  pl.BlockSpec((CONV_WIDTH, 2 * M_WIDTH), lambda b, c: (0, 0)),
                  pl.BlockSpec((1, M_WIDTH), lambda b, c: (0, 0))],
        out_specs=[pl.BlockSpec((chunk, M_WIDTH), lambda b, c: (b * nc + c, 0)),
                   pl.BlockSpec((N_META, M_WIDTH), lambda b, c: (b, 0))],
        out_shape=[jax.ShapeDtypeStruct((n_real, M_WIDTH), BF16),
                   jax.ShapeDtypeStruct((batch * N_META, M_WIDTH), BF16)],
        scratch_shapes=[pltpu.VMEM((M_HEADS, M_HEAD_DIM, M_HEAD_DIM), F32),
                        pltpu.VMEM((8, LANES), F32),
                        pltpu.VMEM((8, LANES), F32),
                        pltpu.VMEM((chunk + 8, 2 * M_WIDTH), F32)],
        compiler_params=_params(("arbitrary", "arbitrary")),
        name="mlstm",
    )(pm, sm, pm, sm, conv_w.astype(F32), mnorm_g.reshape(1, M_WIDTH).astype(F32))
    return out_real, out_meta


def _rel_bucket_np(dist):
    n = np.maximum(dist, 0)
    nf = np.maximum(n, REL_MAX_EXACT).astype(np.float32)
    large = REL_MAX_EXACT + (np.log(nf / np.float32(REL_MAX_EXACT)) /
                             np.float32(math.log(REL_MAX_DIST / REL_MAX_EXACT))
                             * np.float32(REL_BUCKETS - REL_MAX_EXACT)).astype(np.int32)
    large = np.minimum(large, REL_BUCKETS - 1)
    return np.where(n < REL_MAX_EXACT, n, large).astype(np.int32)


def _bias_tables(rel_bias):
    q = np.arange(LANES)[:, None]
    k = np.arange(LANES)[None, :]
    far = 4 * LANES
    assert (_rel_bucket_np(np.arange(LANES + 1, far)) == REL_BUCKETS - 1).all()
    far_idx = np.full((LANES, LANES), REL_BUCKETS - 1, np.int32)
    near_idx = np.stack([_rel_bucket_np(q - k), _rel_bucket_np(LANES + q - k), far_idx])
    meta_idx = np.stack([_rel_bucket_np(q + N_META - np.minimum(k, N_META - 1)), far_idx])
    mq = np.arange(N_META)[:, None]
    mm_idx = _rel_bucket_np(mq - np.minimum(k, N_META - 1))
    rb = rel_bias.astype(F32)
    gather = lambda idx: jnp.moveaxis(rb[idx], -1, -3)
    rel = lambda idx: (gather(idx) - rb[REL_BUCKETS - 1][:, None, None]) * LOG2E
    return rel(near_idx), rel(meta_idx), gather(mm_idx)


def _dsa_kernel(top_k, qa_ref, qi_ref, sm_ref, cb_ref, smb_ref, cm_ref, qam_ref,
                wuk_ref, wuv_ref, near_ref, metab_ref, mmb_ref,
                out_ref, outm_ref,
                keys_scr, lg_scr, lgm_scr, caug_scr, kbf_scr, cmaug_scr, qs_scr, mx_scr, mrep_scr, acc_scr):
    i = pl.program_id(1)
    T = LANES
    H = A_HEADS
    col = lax.broadcasted_iota(jnp.int32, (T, T), 1)
    row = lax.broadcasted_iota(jnp.int32, (T, T), 0)
    nt = (((1,), (1,)), ((), ()))

    def q_latent(qa, hd, scale):
        ql = jnp.dot(qa[:, hd * A_HEAD_DIM:(hd + 1) * A_HEAD_DIM], wuk_ref[hd],
                     preferred_element_type=F32)
        return (ql * scale).astype(BF16)

    def ones_column(n):
        return jnp.where(lax.broadcasted_iota(jnp.int32, (n, T), 1) == 0, 1.0, 0.0).astype(BF16)

    @pl.when(i == 0)
    def _():
        caug_scr[:, 0:KV_RANK] = cb_ref[...].astype(BF16)
        caug_scr[:, KV_RANK:KV_RANK + T] = ones_column(caug_scr.shape[0])
        kbf_scr[...] = smb_ref[:, SM_KIDX:SM_KIDX + IDX_DIM].astype(BF16)
        cm_pad = jnp.concatenate([cm_ref[...], jnp.zeros((T - N_META, KV_RANK), F32)], axis=0)
        cmaug_scr[:, 0:KV_RANK] = cm_pad.astype(BF16)
        cmaug_scr[:, KV_RANK:KV_RANK + T] = ones_column(T)
        cmk = cmaug_scr[:, 0:KV_RANK]
        qam = qam_ref[...]
        mrow = lax.broadcasted_iota(jnp.int32, (N_META, T), 0)
        mcol = lax.broadcasted_iota(jnp.int32, (N_META, T), 1)
        for hd in range(H):
            lg = lax.dot_general(q_latent(qam, hd, A_HEAD_DIM ** -0.5), cmk, nt,
                                 preferred_element_type=F32) + mmb_ref[hd]
            lg = jnp.where(mcol <= mrow, lg, NEG_BIG)
            p = jnp.exp(lg - jnp.max(lg, axis=1, keepdims=True))
            p = p / jnp.sum(p, axis=1, keepdims=True)
            o = jnp.dot(p.astype(BF16), cmk, preferred_element_type=F32)
            outm_ref[:, hd * A_HEAD_DIM:(hd + 1) * A_HEAD_DIM] = jnp.dot(
                o.astype(BF16), wuv_ref[hd], preferred_element_type=F32).astype(outm_ref.dtype)

    qa = qa_ref[...]
    for hd in range(H):
        qs_scr[hd * T:(hd + 1) * T, :] = q_latent(qa, hd, A_HEAD_DIM ** -0.5 * LOG2E)
    qi = qi_ref[...]
    wv = sm_ref[:, SM_WIDX:SM_WIDX + IDX_HEADS] * IDX_SCALE
    t_col = i * T + lax.broadcasted_iota(jnp.int32, (T, 1), 0)
    n_chunks = (i + SCORE_CHUNK) // SCORE_CHUNK
    CW = SCORE_CHUNK * T

    def score_body(cix, carry):
        kc = kbf_scr[pl.ds(pl.multiple_of(cix * CW, CW), CW), :]
        acc = jnp.zeros((T, CW), F32)
        for hh in range(IDX_HEADS):
            s = lax.dot_general(qi[:, hh * IDX_DIM:(hh + 1) * IDX_DIM], kc, nt,
                                preferred_element_type=F32)
            acc = acc + wv[:, hh:hh + 1] * jnp.maximum(s, 0.0)
        acc = jnp.where(acc == 0.0, 0.0, acc)
        bits = lax.bitcast_convert_type(acc, jnp.int32)
        key = jnp.where(bits < 0, bits ^ jnp.int32(0x7FFFFFFF), bits)
        s_idx = cix * CW + lax.broadcasted_iota(jnp.int32, (T, CW), 1)
        key = jnp.where(s_idx <= t_col, key, jnp.int32(INT_MIN))
        for u in range(SCORE_CHUNK):
            keys_scr[cix * SCORE_CHUNK + u] = key[:, u * T:(u + 1) * T]
        return carry

    lax.fori_loop(0, n_chunks, score_body, 0)

    def count_where(pred_fn):
        def body(cix, cnt):
            for u in range(SCORE_CHUNK):
                cnt = cnt + jnp.where(pred_fn(keys_scr[cix * SCORE_CHUNK + u]), 1.0, 0.0)
            return cnt
        cnt = lax.fori_loop(0, n_chunks, body, jnp.zeros((T, T), F32))
        return jnp.sum(cnt, axis=1, keepdims=True)

    def bit_body(bi, ans):
        cand_u = ans | lax.shift_left(jnp.int32(1), jnp.int32(31) - bi)
        cand = cand_u ^ jnp.int32(INT_MIN)
        total = count_where(lambda kk: kk >= cand)
        return jnp.where(total >= float(top_k), cand_u, ans)

    ans = lax.fori_loop(0, 32, bit_body, jnp.zeros((T, 1), jnp.int32))
    thr = ans ^ jnp.int32(INT_MIN)
    need = float(top_k) - count_where(lambda kk: kk > thr)

    mx_scr[...] = jnp.full_like(mx_scr, NEG_BIG)
    acc_scr[...] = jnp.zeros_like(acc_scr)
    hg = H // ATT_GROUPS
    groups = [slice(g * hg * T, (g + 1) * hg * T) for g in range(ATT_GROUPS)]
    meta_sel = jnp.minimum(i, 1)

    def max_pass(c_aug, madds, bias_fns, store):
        ck = c_aug[:, 0:KV_RANK]
        lgs = [lax.dot_general(qs_scr[rs, :], ck, nt, preferred_element_type=F32) for rs in groups]
        for hd in range(H):
            rs = slice(hd * T, (hd + 1) * T)
            lo = (hd % hg) * T
            mx = mx_scr[rs, :]
            for u in range(len(madds)):
                x = lgs[hd // hg][lo:lo + T, u * T:(u + 1) * T] + madds[u]
                if bias_fns[u] is not None:
                    x = x + bias_fns[u](hd)
                store(rs, u, x)
                mx = jnp.maximum(mx, x)
            mx_scr[rs, :] = mx

    def sum_pass(c_aug, n_sub, load):
        for grp in groups:
            m_rep = mrep_scr[grp, :]
            ph = [jnp.exp2(load(grp, u) - m_rep).astype(BF16) for u in range(n_sub)]
            p = ph[0] if n_sub == 1 else jnp.concatenate(ph, axis=1)
            acc_scr[grp, :] += jnp.dot(p, c_aug, preferred_element_type=F32)

    def key_rows(step):
        return pl.ds(pl.multiple_of(step * KEY_SUB * T, KEY_SUB * T), KEY_SUB * T)

    upper = (row < col).astype(BF16)

    def mask_step(step, seen, near):
        madds, bias_fns = [], []
        for u in range(KEY_SUB):
            j = KEY_SUB * step + u
            kk = keys_scr[j]
            eq = kk == thr
            eqf = jnp.where(eq, 1.0, 0.0)
            before = jnp.dot(eqf.astype(BF16), upper, preferred_element_type=F32) + seen
            sel = (kk > thr) | (eq & (before < need))
            if near:
                sel = sel & ((j * T + col) <= (i * T + row))
                dsel = jnp.clip(i - j, 0, 2)
                bias_fns.append(lambda hd, dsel=dsel: near_ref[dsel, hd])
            else:
                bias_fns.append(None)
            seen = seen + jnp.sum(eqf, axis=1, keepdims=True)
            madds.append(jnp.where(sel, 0.0, NEG_BIG))

        def store(rs, u, x):
            lg_scr[step, rs, u * T:(u + 1) * T] = x

        max_pass(caug_scr[key_rows(step), :], madds, bias_fns, store)
        return seen

    def sum_step(step, carry):
        sum_pass(caug_scr[key_rows(step), :], KEY_SUB,
                 lambda grp, u: lg_scr[step, grp, u * T:(u + 1) * T])
        return carry

    def store_meta(rs, u, x):
        lgm_scr[rs, :] = x

    n_far = jnp.maximum(i - 1, 0) // KEY_SUB
    n_steps = (i + KEY_SUB) // KEY_SUB
    max_pass(cmaug_scr[...], [jnp.where(col < N_META, 0.0, NEG_BIG)],
             [lambda hd: metab_ref[meta_sel, hd]], store_meta)
    seen = lax.fori_loop(0, n_far, lambda s, c: mask_step(s, c, False), jnp.zeros((T, 1), F32))
    lax.fori_loop(n_far, n_steps, lambda s, c: mask_step(s, c, True), seen)
    mrep_scr[...] = jnp.broadcast_to(jnp.max(mx_scr[...], axis=1, keepdims=True), mrep_scr.shape)
    sum_pass(cmaug_scr[...], 1, lambda grp, u: lgm_scr[grp, :])
    lax.fori_loop(0, n_steps, sum_step, 0)

    for hd in range(H):
        rs = slice(hd * T, (hd + 1) * T)
        o = acc_scr[rs, 0:KV_RANK] / acc_scr[rs, KV_RANK:KV_RANK + 1]
        out_ref[:, hd * A_HEAD_DIM:(hd + 1) * A_HEAD_DIM] = jnp.dot(
            o.astype(BF16), wuv_ref[hd], preferred_element_type=F32).astype(out_ref.dtype)


def _dsa(qa, qi, sm, c, w_uk, w_uv, tables, batch, seq):
    n_real = batch * seq
    nq = seq // LANES
    n_tiles = ((nq + SCORE_CHUNK - 1) // SCORE_CHUNK) * SCORE_CHUNK
    top_k = min(TOPK_MAX, seq // 4)
    meta_blk = n_real // N_META
    near, metab, mmb = tables
    wuk_t = jnp.swapaxes(w_uk, 1, 2).astype(BF16)
    wuv = w_uv.astype(BF16)
    full = lambda a: pl.BlockSpec(a.shape, lambda b, i: (0,) * a.ndim)
    assert seq % (SCORE_CHUNK * LANES) == 0 and nq % KEY_SUB == 0 and SCORE_CHUNK % KEY_SUB == 0
    out_real, out_meta = pl.pallas_call(
        functools.partial(_dsa_kernel, top_k),
        grid=(batch, nq),
        in_specs=[pl.BlockSpec((LANES, A_WIDTH), lambda b, i: (b * nq + i, 0)),
                  pl.BlockSpec((LANES, IDX_HEADS * IDX_DIM), lambda b, i: (b * nq + i, 0)),
                  pl.BlockSpec((LANES, LANES), lambda b, i: (b * nq + i, 0)),
                  pl.BlockSpec((seq, KV_RANK), lambda b, i: (b, 0)),
                  pl.BlockSpec((seq, LANES), lambda b, i: (b, 0)),
                  pl.BlockSpec((N_META, KV_RANK), lambda b, i: (meta_blk + b, 0)),
                  pl.BlockSpec((N_META, A_WIDTH), lambda b, i: (meta_blk + b, 0)),
                  full(wuk_t), full(wuv), full(near), full(metab), full(mmb)],
        out_specs=[pl.BlockSpec((LANES, A_WIDTH), lambda b, i: (b * nq + i, 0)),
                   pl.BlockSpec((N_META, A_WIDTH), lambda b, i: (b, 0))],
        out_shape=[jax.ShapeDtypeStruct((n_real, A_WIDTH), BF16),
                   jax.ShapeDtypeStruct((batch * N_META, A_WIDTH), BF16)],
        scratch_shapes=[pltpu.VMEM((n_tiles, LANES, LANES), jnp.int32),
                        pltpu.VMEM((nq // KEY_SUB, A_HEADS * LANES, KEY_SUB * LANES), F32),
                        pltpu.VMEM((A_HEADS * LANES, LANES), F32),
                        pltpu.VMEM((seq, KV_RANK + LANES), BF16),
                        pltpu.VMEM((seq, IDX_DIM), BF16),
                        pltpu.VMEM((LANES, KV_RANK + LANES), BF16),
                        pltpu.VMEM((A_HEADS * LANES, KV_RANK), BF16),
                        pltpu.VMEM((A_HEADS * LANES, LANES), F32),
                        pltpu.VMEM((A_HEADS * LANES, LANES), F32),
                        pltpu.VMEM((A_HEADS * LANES, KV_RANK + LANES), F32)],
        compiler_params=_params(("arbitrary", "arbitrary")),
        name="dsa",
    )(qa, qi, sm, c, sm, c, qa, wuk_t, wuv, near, metab, mmb)
    return out_real, out_meta


def _merge_kernel(alpha, h_ref, hm_ref, ha_ref, g_ref, wbm_ref, wba_ref, wo_ref, lg_ref, lb_ref,
                  wr_ref, br_ref, h1_ref, comb_ref):
    d = h_ref.shape[1]
    gm = _sigmoid(g_ref[:, 0:d].astype(F32))
    ga = _sigmoid(g_ref[:, d:2 * d].astype(F32))
    y = gm * jnp.dot(hm_ref[...], wbm_ref[...], preferred_element_type=F32) + \
        ga * jnp.dot(ha_ref[...], wba_ref[...], preferred_element_type=F32)
    z = alpha * h_ref[...] + jnp.dot(y.astype(BF16), wo_ref[...], preferred_element_type=F32)
    h1 = _layer_norm_rows(z, lg_ref[...], lb_ref[...], 1e-5)
    h1_ref[...] = h1

    tm = h1.shape[0]
    logits_t = lax.dot_general(wr_ref[...], h1.astype(BF16), (((1,), (1,)), ((), ())),
                               preferred_element_type=F32)
    scores = _sigmoid(logits_t[0:N_EXPERTS, :])
    sel = scores + br_ref[0:N_EXPERTS, :]
    best = None
    for gidx in range(N_GROUPS):
        r0, r1, r2, r3 = (sel[gidx * GROUP_SIZE + u:gidx * GROUP_SIZE + u + 1, :] for u in range(4))
        a, b = jnp.maximum(r0, r1), jnp.minimum(r0, r1)
        c, dd = jnp.maximum(r2, r3), jnp.minimum(r2, r3)
        gs = jnp.maximum(a, c) + jnp.maximum(jnp.minimum(a, c), jnp.maximum(b, dd))
        if best is None:
            best, bg = gs, jnp.zeros((1, tm), jnp.int32)
        else:
            upd = gs > best
            bg = jnp.where(upd, gidx, bg)
            best = jnp.where(upd, gs, best)
    eidx = lax.broadcasted_iota(jnp.int32, (N_EXPERTS, tm), 0)
    masked = jnp.where((eidx // GROUP_SIZE) == bg, sel, -jnp.inf)
    v1 = jnp.max(masked, axis=0, keepdims=True)
    i1 = jnp.min(jnp.where(masked == v1, eidx, N_EXPERTS), axis=0, keepdims=True)
    masked2 = jnp.where(eidx == i1, -jnp.inf, masked)
    v2 = jnp.max(masked2, axis=0, keepdims=True)
    i2 = jnp.min(jnp.where(masked2 == v2, eidx, N_EXPERTS), axis=0, keepdims=True)
    s1 = jnp.sum(jnp.where(eidx == i1, scores, 0.0), axis=0, keepdims=True)
    s2 = jnp.sum(jnp.where(eidx == i2, scores, 0.0), axis=0, keepdims=True)
    tot = s1 + s2
    comb_t = jnp.where(eidx == i1, s1 / tot, 0.0) + jnp.where(eidx == i2, s2 / tot, 0.0)
    comb_pad = jnp.concatenate([comb_t, jnp.zeros((LANES - N_EXPERTS, tm), F32)], axis=0)
    comb_ref[...] = comb_pad.T


def _merge(h, hm, ha, g, w_bm, w_ba, w_o, ln_g, ln_b, w_router, b_router, alpha):
    n, d = h.shape
    wr_t = jnp.zeros((LANES, d), F32).at[0:N_EXPERTS].set(w_router.T).astype(BF16)
    br = jnp.zeros((LANES, 1), F32).at[0:N_EXPERTS, 0].set(b_router)
    row = lambda width: pl.BlockSpec((ROW_TILE, width), lambda r: (r, 0))
    full = lambda a: pl.BlockSpec(a.shape, lambda r: (0,) * a.ndim)
    args = (h, hm, ha, g, w_bm.astype(BF16), w_ba.astype(BF16), w_o.astype(BF16),
            ln_g.reshape(1, d), ln_b.reshape(1, d), wr_t, br)
    return pl.pallas_call(
        functools.partial(_merge_kernel, alpha),
        grid=(n // ROW_TILE,),
        in_specs=[row(d), row(M_WIDTH), row(A_WIDTH), row(2 * d)] + [full(a) for a in args[4:]],
        out_specs=[row(d), row(LANES)],
        out_shape=[jax.ShapeDtypeStruct((n, d), F32), jax.ShapeDtypeStruct((n, LANES), F32)],
        compiler_params=_params(("parallel",)),
        name="merge",
    )(*args)


def _moe_kernel(alpha, h_ref, comb_ref, wg_ref, wu_ref, wd_ref, lg_ref, lb_ref, out_ref,
                xb_scr, acc_scr):
    e = pl.program_id(1)

    @pl.when(e == 0)
    def _():
        xb_scr[...] = h_ref[...].astype(BF16)
        acc_scr[...] = jnp.zeros_like(acc_scr)

    xb = xb_scr[...]
    gate = jnp.dot(xb, wg_ref[...], preferred_element_type=F32)
    up = jnp.dot(xb, wu_ref[...], preferred_element_type=F32)
    he = gate * _sigmoid(gate) * up
    o = jnp.dot(he.astype(BF16), wd_ref[...], preferred_element_type=F32)
    comb = comb_ref[...]
    lane = lax.broadcasted_iota(jnp.int32, comb.shape, 1)
    ce = jnp.sum(jnp.where(lane == e, comb, 0.0), axis=1, keepdims=True)
    acc_scr[...] += ce * o

    @pl.when(e == pl.num_programs(1) - 1)
    def _():
        z = alpha * h_ref[...] + acc_scr[...]
        out_ref[...] = _layer_norm_rows(z, lg_ref[...], lb_ref[...], 1e-5)


def _moe(h, comb, w_gate, w_up, w_down, ln_g, ln_b, alpha):
    n, d = h.shape
    de = w_gate.shape[-1]
    tm = MOE_ROW_TILE if n % MOE_ROW_TILE == 0 else ROW_TILE
    return pl.pallas_call(
        functools.partial(_moe_kernel, alpha),
        grid=(n // tm, N_EXPERTS),
        in_specs=[pl.BlockSpec((tm, d), lambda r, e: (r, 0)),
                  pl.BlockSpec((tm, LANES), lambda r, e: (r, 0)),
                  pl.BlockSpec((None, d, de), lambda r, e: (e, 0, 0)),
                  pl.BlockSpec((None, d, de), lambda r, e: (e, 0, 0)),
                  pl.BlockSpec((None, de, d), lambda r, e: (e, 0, 0)),
                  pl.BlockSpec((1, d), lambda r, e: (0, 0)),
                  pl.BlockSpec((1, d), lambda r, e: (0, 0))],
        out_specs=pl.BlockSpec((tm, d), lambda r, e: (r, 0)),
        out_shape=jax.ShapeDtypeStruct((n, d), F32),
        scratch_shapes=[pltpu.VMEM((tm, d), BF16), pltpu.VMEM((tm, d), F32)],
        compiler_params=_params(("parallel", "arbitrary")),
        name="moe",
    )(h, comb, w_gate.astype(BF16), w_up.astype(BF16), w_down.astype(BF16),
      ln_g.reshape(1, d), ln_b.reshape(1, d))


def _with_meta(real, meta, n_pad):
    pad = n_pad - real.shape[0] - meta.shape[0]
    return jnp.concatenate([real, meta, jnp.zeros((pad, real.shape[1]), real.dtype)], axis=0)


def kernel(x, meta_tokens, ln_in_g, ln_in_b, w_in, conv_w, b_if, mnorm_g, kv_norm_g, w_uk, w_uv,
           w_branch_m, w_branch_a, w_out, ln1_g, ln1_b, w_router, b_router, w_gate, w_up, w_down,
           ln2_g, ln2_b, rel_bias):
    batch, seq, d = x.shape
    depth = w_in.shape[0]
    alpha = (2 * depth) ** 0.25
    n_real = batch * seq
    n_meta = batch * N_META
    tile = math.lcm(ROW_TILE, MOE_ROW_TILE)
    n_pad = -(-(n_real + n_meta) // tile) * tile
    assert seq % LANES == 0 and n_real % N_META == 0

    h = _with_meta(x.reshape(n_real, d), jnp.tile(meta_tokens.astype(x.dtype), (batch, 1)), n_pad)
    h = _input_ln(h, ln_in_g, ln_in_b)
    tables = _bias_tables(rel_bias)
    for l in range(depth):
        pm, qa, qi, c, sm, g = _project(h, _pack_w_in(w_in[l]), kv_norm_g[l], b_if[l])
        hm_real, hm_meta = _mlstm(pm, sm, conv_w[l], mnorm_g[l], batch, seq)
        ha_real, ha_meta = _dsa(qa, qi, sm, c, w_uk[l], w_uv[l], tables, batch, seq)
        hm = _with_meta(hm_real, hm_meta, n_pad)
        ha = _with_meta(ha_real, ha_meta, n_pad)
        h1, comb = _merge(h, hm, ha, g, w_branch_m[l], w_branch_a[l], w_out[l], ln1_g[l], ln1_b[l],
                          w_router, b_router, alpha)
        h = _moe(h1, comb, w_gate[l], w_up[l], w_down[l], ln2_g[l], ln2_b[l], alpha)
    return h[:n_real].reshape(batch, seq, d)
```

```python
import functools
import math

import numpy as np
import jax
import jax.numpy as jnp
from jax import lax
from jax.experimental import pallas as pl
from jax.experimental.pallas import tpu as pltpu

F32 = jnp.float32
BF16 = jnp.bfloat16

N_META = 16
M_HEADS = 4
M_HEAD_DIM = 128
M_WIDTH = M_HEADS * M_HEAD_DIM
CONV_WIDTH = 4
A_HEADS = 8
A_HEAD_DIM = 64
A_WIDTH = A_HEADS * A_HEAD_DIM
KV_RANK = 128
IDX_HEADS = 4
IDX_DIM = 64
IDX_SCALE = (IDX_HEADS * IDX_DIM) ** -0.5
TOPK_MAX = 256
REL_BUCKETS = 32
REL_MAX_EXACT = 16
REL_MAX_DIST = 128
N_EXPERTS = 16
N_GROUPS = 4
GROUP_SIZE = N_EXPERTS // N_GROUPS

LANES = 128
ROW_TILE = 256
MOE_ROW_TILE = 768
M_CHUNK = 256
SCORE_CHUNK = 4
KEY_SUB = 4
ATT_GROUPS = 2
VMEM_LIMIT = 56 * 1024 * 1024
NEG_BIG = -1e30
INT_MIN = -2 ** 31
HALF_BIAS = 2 ** 15
LOG2E = math.log2(math.e)

PK_PM = 0
PK_QA = 2048
PK_QI = 2560
PK_CKV = 2816
PK_SM = 2944
PK_G = 3072
PK_TOTAL = 5120
SM_KIDX = 0
SM_WIDX = 64
SM_IPRE = 68
SM_FPRE = 72


def _params(sem):
    return pltpu.CompilerParams(dimension_semantics=sem, vmem_limit_bytes=VMEM_LIMIT)


def _sigmoid(x):
    return 1.0 / (1.0 + jnp.exp(-x))


def _layer_norm_rows(x, g, b, eps):
    mu = jnp.mean(x, axis=-1, keepdims=True)
    xc = x - mu
    var = jnp.mean(xc * xc, axis=-1, keepdims=True)
    return xc * lax.rsqrt(var + eps) * g + b


def _ln_kernel(x_ref, g_ref, b_ref, o_ref):
    o_ref[...] = _layer_norm_rows(x_ref[...], g_ref[...], b_ref[...], 1e-5)


def _input_ln(x, g, b):
    n, d = x.shape
    return pl.pallas_call(
        _ln_kernel,
        grid=(n // ROW_TILE,),
        in_specs=[pl.BlockSpec((ROW_TILE, d), lambda r: (r, 0)),
                  pl.BlockSpec((1, d), lambda r: (0, 0)),
                  pl.BlockSpec((1, d), lambda r: (0, 0))],
        out_specs=pl.BlockSpec((ROW_TILE, d), lambda r: (r, 0)),
        out_shape=jax.ShapeDtypeStruct((n, d), F32),
        compiler_params=_params(("parallel",)),
        name="input_ln",
    )(x, g.reshape(1, d), b.reshape(1, d))


def _proj_kernel(h_ref, w_ref, kvg_ref, brow_ref, pm_ref, qa_ref, qi_ref, c_ref, sm_ref, g_ref):
    x = h_ref[...].astype(BF16)

    def mm(lo, width):
        return jnp.dot(x, w_ref[:, lo:lo + width], preferred_element_type=F32)

    pm_ref[...] = mm(PK_PM, 2048).astype(BF16)
    qa_ref[...] = mm(PK_QA, 512).astype(BF16)
    qi_ref[...] = mm(PK_QI, 256).astype(BF16)
    ckv = mm(PK_CKV, 128)
    c_ref[...] = ckv * lax.rsqrt(jnp.mean(ckv * ckv, axis=-1, keepdims=True) + 1e-6) * kvg_ref[...]
    sm_ref[...] = mm(PK_SM, 128) + brow_ref[...]
    g_ref[...] = mm(PK_G, 2048).astype(BF16)


def _pack_w_in(w):
    d = w.shape[0]
    cols = [w[:, 0:2048], w[:, 2056:2568], w[:, 2696:2952], w[:, 2568:2696],
            w[:, 2952:3016], w[:, 3016:3020], w[:, 2048:2056],
            jnp.zeros((d, LANES - 76), w.dtype), w[:, 3020:5068]]
    return jnp.concatenate(cols, axis=1).astype(BF16)


def _project(h, w_packed, kv_g, b_if):
    n, d = h.shape
    brow = jnp.concatenate([jnp.zeros((SM_IPRE,), F32), b_if.astype(F32),
                            jnp.zeros((LANES - SM_IPRE - 2 * M_HEADS,), F32)]).reshape(1, LANES)
    row = lambda width: pl.BlockSpec((ROW_TILE, width), lambda r: (r, 0))
    shp = lambda width, dt: jax.ShapeDtypeStruct((n, width), dt)
    return pl.pallas_call(
        _proj_kernel,
        grid=(n // ROW_TILE,),
        in_specs=[row(d),
                  pl.BlockSpec((d, PK_TOTAL), lambda r: (0, 0)),
                  pl.BlockSpec((1, KV_RANK), lambda r: (0, 0)),
                  pl.BlockSpec((1, LANES), lambda r: (0, 0))],
        out_specs=[row(2048), row(512), row(256), row(128), row(128), row(2048)],
        out_shape=[shp(2048, BF16), shp(512, BF16), shp(256, BF16), shp(128, F32), shp(128, F32),
                   shp(2048, BF16)],
        compiler_params=_params(("parallel",)),
        name="in_proj",
    )(h, w_packed, kv_g.reshape(1, KV_RANK), brow)


def _mlstm_chunk_math(q, k, v, li, lf, ct, n, m):
    L = q.shape[0]
    r = lax.broadcasted_iota(jnp.int32, (L, L), 0)
    c = lax.broadcasted_iota(jnp.int32, (L, L), 1)
    eye = r == c
    tril = c <= r
    lf_b = jnp.broadcast_to(lf, (L, L))
    li_b = jnp.broadcast_to(li, (L, L))
    lf_row = jnp.sum(jnp.where(eye, lf_b, 0.0), axis=0, keepdims=True)
    li_row = jnp.sum(jnp.where(eye, li_b, 0.0), axis=0, keepdims=True)
    b_col = jnp.sum(jnp.where(tril, jnp.broadcast_to(lf_row, (L, L)), 0.0), axis=1, keepdims=True)
    b_row = jnp.sum(jnp.where(r <= c, lf_b, 0.0), axis=0, keepdims=True)
    d = jnp.where(tril, b_col - b_row + li_row, -jnp.inf)
    inter = b_col + m
    m_t = jnp.maximum(inter, jnp.max(d, axis=1, keepdims=True))
    a = jnp.exp(inter - m_t)
    qb, kb, vb = q.astype(BF16), k.astype(BF16), v.astype(BF16)
    qk = lax.dot_general(qb, kb, (((1,), (1,)), ((), ())), preferred_element_type=F32)
    w = jnp.exp(d - m_t) * qk
    num = a * jnp.dot(qb, ct.astype(BF16), preferred_element_type=F32) + \
        jnp.dot(w.astype(BF16), vb, preferred_element_type=F32)
    den = a * jnp.sum(q * n, axis=1, keepdims=True) + jnp.sum(w, axis=1, keepdims=True)
    h = num / jnp.maximum(jnp.abs(den), jnp.exp(-m_t))
    b_last = b_col[L - 1:L, :]
    g = b_last - b_col + li
    m_new = jnp.maximum(b_last + m, jnp.max(g, axis=0, keepdims=True))
    decay = jnp.exp(b_last + m - m_new)
    kw = k * jnp.exp(g - m_new)
    ct_new = decay * ct + lax.dot_general(kw.astype(BF16), vb, (((0,), (0,)), ((), ())),
                                          preferred_element_type=F32)
    n_new = decay * n + jnp.sum(kw, axis=0, keepdims=True)
    return h, ct_new, n_new, m_new


def _mlstm_kernel(pm_ref, sm_ref, pmm_ref, smm_ref, cw_ref, mg_ref, out_ref, outm_ref,
                  ct_scr, n_scr, m_scr, x_scr):
    ci = pl.program_id(1)
    tail = 8

    def run_chunk(p_ref, s_ref, o_ref):
        L = p_ref.shape[0]
        x_scr[tail:tail + L, :] = p_ref[:, 0:2 * M_WIDTH].astype(F32)
        conv = cw_ref[0:1, :] * x_scr[tail - 3:tail - 3 + L, :]
        for j in range(1, CONV_WIDTH):
            conv = conv + cw_ref[j:j + 1, :] * x_scr[tail - 3 + j:tail - 3 + j + L, :]
        x_scr[0:tail, :] = x_scr[L:L + tail, :]
        qk = conv * _sigmoid(conv)
        for hd in range(M_HEADS):
            lo = hd * M_HEAD_DIM
            q = qk[:, lo:lo + M_HEAD_DIM] * (M_HEAD_DIM ** -0.5)
            k = qk[:, M_WIDTH + lo:M_WIDTH + lo + M_HEAD_DIM]
            v = p_ref[:, 2 * M_WIDTH + lo:2 * M_WIDTH + lo + M_HEAD_DIM].astype(F32)
            li = s_ref[:, SM_IPRE + hd:SM_IPRE + hd + 1]
            f = s_ref[:, SM_FPRE + hd:SM_FPRE + hd + 1]
            lf = jnp.minimum(f, 0.0) - jnp.log1p(jnp.exp(-jnp.abs(f)))
            h, ct_new, n_new, m_new = _mlstm_chunk_math(
                q, k, v, li, lf, ct_scr[hd], n_scr[hd:hd + 1, :], m_scr[hd:hd + 1, 0:1])
            ct_scr[hd] = ct_new
            n_scr[hd:hd + 1, :] = n_new
            m_scr[hd:hd + 1, :] = jnp.broadcast_to(m_new, (1, LANES))
            mu = jnp.mean(h, axis=-1, keepdims=True)
            hc = h - mu
            var = jnp.mean(hc * hc, axis=-1, keepdims=True)
            o_gate = _sigmoid(p_ref[:, 3 * M_WIDTH + lo:3 * M_WIDTH + lo + M_HEAD_DIM].astype(F32))
            o_ref[:, lo:lo + M_HEAD_DIM] = (hc * lax.rsqrt(var + 1e-5) * mg_ref[:, lo:lo + M_HEAD_DIM]
                                            * o_gate).astype(o_ref.dtype)

    @pl.when(ci == 0)
    def _():
        ct_scr[...] = jnp.zeros_like(ct_scr)
        n_scr[...] = jnp.zeros_like(n_scr)
        m_scr[...] = jnp.zeros_like(m_scr)
        x_scr[...] = jnp.zeros_like(x_scr)
        run_chunk(pmm_ref, smm_ref, outm_ref)

    run_chunk(pm_ref, sm_ref, out_ref)


def _mlstm(pm, sm, conv_w, mnorm_g, batch, seq):
    n_real = batch * seq
    chunk = min(M_CHUNK, seq)
    nc = seq // chunk
    meta_blk = n_real // N_META
    out_real, out_meta = pl.pallas_call(
        _mlstm_kernel,
        grid=(batch, nc),
        in_specs=[pl.BlockSpec((chunk, 4 * M_WIDTH), lambda b, c: (b * nc + c, 0)),
                  pl.BlockSpec((chunk, LANES), lambda b, c: (b * nc + c, 0)),
                  pl.BlockSpec((N_META, 4 * M_WIDTH), lambda b, c: (meta_blk + b, 0)),
                  pl.BlockSpec((N_META, LANES), lambda b, c: (meta_blk + b, 0)),
                  pl.BlockSpec((CONV_WIDTH, 2 * M_WIDTH), lambda b, c: (0, 0)),
                  pl.BlockSpec((1, M_WIDTH), lambda b, c: (0, 0))],
        out_specs=[pl.BlockSpec((chunk, M_WIDTH), lambda b, c: (b * nc + c, 0)),
                   pl.BlockSpec((N_META, M_WIDTH), lambda b, c: (b, 0))],
        out_shape=[jax.ShapeDtypeStruct((n_real, M_WIDTH), BF16),
                   jax.ShapeDtypeStruct((batch * N_META, M_WIDTH), BF16)],
        scratch_shapes=[pltpu.VMEM((M_HEADS, M_HEAD_DIM, M_HEAD_DIM), F32),
                        pltpu.VMEM((8, LANES), F32),
                        pltpu.VMEM((8, LANES), F32),
                        pltpu.VMEM((chunk + 8, 2 * M_WIDTH), F32)],
        compiler_params=_params(("arbitrary", "arbitrary")),
        name="mlstm",
    )(pm, sm, pm, sm, conv_w.astype(F32), mnorm_g.reshape(1, M_WIDTH).astype(F32))
    return out_real, out_meta


def _rel_bucket_np(dist):
    n = np.maximum(dist, 0)
    nf = np.maximum(n, REL_MAX_EXACT).astype(np.float32)
    large = REL_MAX_EXACT + (np.log(nf / np.float32(REL_MAX_EXACT)) /
                             np.float32(math.log(REL_MAX_DIST / REL_MAX_EXACT))
                             * np.float32(REL_BUCKETS - REL_MAX_EXACT)).astype(np.int32)
    large = np.minimum(large, REL_BUCKETS - 1)
    return np.where(n < REL_MAX_EXACT, n, large).astype(np.int32)


def _bias_tables(rel_bias):
    q = np.arange(LANES)[:, None]
    k = np.arange(LANES)[None, :]
    far = 4 * LANES
    assert (_rel_bucket_np(np.arange(LANES + 1, far)) == REL_BUCKETS - 1).all()
    far_idx = np.full((LANES, LANES), REL_BUCKETS - 1, np.int32)
    near_idx = np.stack([_rel_bucket_np(q - k), _rel_bucket_np(LANES + q - k), far_idx])
    meta_idx = np.stack([_rel_bucket_np(q + N_META - np.minimum(k, N_META - 1)), far_idx])
    mq = np.arange(N_META)[:, None]
    mm_idx = _rel_bucket_np(mq - np.minimum(k, N_META - 1))
    rb = rel_bias.astype(F32)
    gather = lambda idx: jnp.moveaxis(rb[idx], -1, -3)
    rel = lambda idx: (gather(idx) - rb[REL_BUCKETS - 1][:, None, None]) * LOG2E
    return rel(near_idx), rel(meta_idx), gather(mm_idx)


def _dsa_kernel(top_k, qa_ref, qi_ref, sm_ref, cb_ref, smb_ref, cm_ref, qam_ref,
                wuk_ref, wuv_ref, near_ref, metab_ref, mmb_ref,
                out_ref, outm_ref,
                keys_scr, hi_scr, lo_scr, lg_scr, lgm_scr, caug_scr, kbf_scr, cmaug_scr, qs_scr, mx_scr, mrep_scr, acc_scr):
    i = pl.program_id(1)
    T = LANES
    H = A_HEADS
    col = lax.broadcasted_iota(jnp.int32, (T, T), 1)
    row = lax.broadcasted_iota(jnp.int32, (T, T), 0)
    nt = (((1,), (1,)), ((), ()))

    def q_latent(qa, hd, scale):
        ql = jnp.dot(qa[:, hd * A_HEAD_DIM:(hd + 1) * A_HEAD_DIM], wuk_ref[hd],
                     preferred_element_type=F32)
        return (ql * scale).astype(BF16)

    def ones_column(n):
        return jnp.where(lax.broadcasted_iota(jnp.int32, (n, T), 1) == 0, 1.0, 0.0).astype(BF16)

    @pl.when(i == 0)
    def _():
        caug_scr[:, 0:KV_RANK] = cb_ref[...].astype(BF16)
        caug_scr[:, KV_RANK:KV_RANK + T] = ones_column(caug_scr.shape[0])
        kbf_scr[...] = smb_ref[:, SM_KIDX:SM_KIDX + IDX_DIM].astype(BF16)
        cm_pad = jnp.concatenate([cm_ref[...], jnp.zeros((T - N_META, KV_RANK), F32)], axis=0)
        cmaug_scr[:, 0:KV_RANK] = cm_pad.astype(BF16)
        cmaug_scr[:, KV_RANK:KV_RANK + T] = ones_column(T)
        cmk = cmaug_scr[:, 0:KV_RANK]
        qam = qam_ref[...]
        mrow = lax.broadcasted_iota(jnp.int32, (N_META, T), 0)
        mcol = lax.broadcasted_iota(jnp.int32, (N_META, T), 1)
        for hd in range(H):
            lg = lax.dot_general(q_latent(qam, hd, A_HEAD_DIM ** -0.5), cmk, nt,
                                 preferred_element_type=F32) + mmb_ref[hd]
            lg = jnp.where(mcol <= mrow, lg, NEG_BIG)
            p = jnp.exp(lg - jnp.max(lg, axis=1, keepdims=True))
            p = p / jnp.sum(p, axis=1, keepdims=True)
            o = jnp.dot(p.astype(BF16), cmk, preferred_element_type=F32)
            outm_ref[:, hd * A_HEAD_DIM:(hd + 1) * A_HEAD_DIM] = jnp.dot(
                o.astype(BF16), wuv_ref[hd], preferred_element_type=F32).astype(outm_ref.dtype)

    qa = qa_ref[...]
    for hd in range(H):
        qs_scr[hd * T:(hd + 1) * T, :] = q_latent(qa, hd, A_HEAD_DIM ** -0.5 * LOG2E)
    qi = qi_ref[...]
    wv = sm_ref[:, SM_WIDX:SM_WIDX + IDX_HEADS] * IDX_SCALE
    t_col = i * T + lax.broadcasted_iota(jnp.int32, (T, 1), 0)
    n_chunks = (i + SCORE_CHUNK) // SCORE_CHUNK
    CW = SCORE_CHUNK * T

    def score_body(cix, carry):
        kc = kbf_scr[pl.ds(pl.multiple_of(cix * CW, CW), CW), :]
        acc = jnp.zeros((T, CW), F32)
        for hh in range(IDX_HEADS):
            s = lax.dot_general(qi[:, hh * IDX_DIM:(hh + 1) * IDX_DIM], kc, nt,
                                preferred_element_type=F32)
            acc = acc + wv[:, hh:hh + 1] * jnp.maximum(s, 0.0)
        acc = jnp.where(acc == 0.0, 0.0, acc)
        bits = lax.bitcast_convert_type(acc, jnp.int32)
        key = jnp.where(bits < 0, bits ^ jnp.int32(0x7FFFFFFF), bits)
        s_idx = cix * CW + lax.broadcasted_iota(jnp.int32, (T, CW), 1)
        key = jnp.where(s_idx <= t_col, key, jnp.int32(INT_MIN))
        for u in range(SCORE_CHUNK):
            tile = key[:, u * T:(u + 1) * T]
            keys_scr[cix * SCORE_CHUNK + u] = tile
            tile_t = tile.T
            hi_scr[cix * SCORE_CHUNK + u] = lax.shift_right_arithmetic(tile_t, 16).astype(jnp.int16)
            lo_scr[cix * SCORE_CHUNK + u] = ((tile_t & 0xFFFF) - HALF_BIAS).astype(jnp.int16)
        return carry

    lax.fori_loop(0, n_chunks, score_body, 0)

    def rep16(row_i32):
        return jnp.broadcast_to(row_i32, (T, T)).astype(jnp.int16)

    def count16(src_scr, pred_fn):
        def body(cix, cnt):
            for u in range(SCORE_CHUNK):
                hit = pred_fn(src_scr[cix * SCORE_CHUNK + u])
                cnt = cnt + jnp.where(hit, jnp.int16(1), jnp.int16(0))
            return cnt
        cnt = lax.fori_loop(0, n_chunks, body, jnp.zeros((T, T), jnp.int16))
        return jnp.sum(cnt.astype(F32), axis=0, keepdims=True)

    def search16(src_scr, k_row):
        def bit_body(bi, ans):
            cand_u = ans | lax.shift_left(jnp.int32(1), jnp.int32(15) - bi)
            cand = rep16(cand_u - HALF_BIAS)
            total = count16(src_scr, lambda x: x >= cand)
            return jnp.where(total >= k_row, cand_u, ans)
        return lax.fori_loop(0, 16, bit_body, jnp.zeros((1, T), jnp.int32))

    k_row = jnp.full((1, T), float(top_k), F32)
    hi_s = search16(hi_scr, k_row) - HALF_BIAS
    hi_rep = rep16(hi_s)
    k_low = k_row - count16(hi_scr, lambda x: x > hi_rep)

    def band_body(j, carry):
        lo_scr[j] = jnp.where(hi_scr[j] == hi_rep, lo_scr[j], jnp.int16(-HALF_BIAS))
        return carry

    lax.fori_loop(0, n_chunks * SCORE_CHUNK, band_body, 0)
    lo_u = search16(lo_scr, k_low)
    lo_rep = rep16(lo_u - HALF_BIAS)
    need_row = k_low - count16(lo_scr, lambda x: x > lo_rep)
    thr_row = lax.shift_left(hi_s, 16) | lo_u
    thr = jnp.broadcast_to(thr_row, (T, T)).T
    need = jnp.broadcast_to(need_row, (T, T)).T

    mx_scr[...] = jnp.full_like(mx_scr, NEG_BIG)
    acc_scr[...] = jnp.zeros_like(acc_scr)
    hg = H // ATT_GROUPS
    groups = [slice(g * hg * T, (g + 1) * hg * T) for g in range(ATT_GROUPS)]
    meta_sel = jnp.minimum(i, 1)

    def max_pass(c_aug, madds, bias_fns, store):
        ck = c_aug[:, 0:KV_RANK]
        lgs = [lax.dot_general(qs_scr[rs, :], ck, nt, preferred_element_type=F32) for rs in groups]
        for hd in range(H):
            rs = slice(hd * T, (hd + 1) * T)
            lo = (hd % hg) * T
            mx = mx_scr[rs, :]
            for u in range(len(madds)):
                x = lgs[hd // hg][lo:lo + T, u * T:(u + 1) * T] + madds[u]
                if bias_fns[u] is not None:
                    x = x + bias_fns[u](hd)
                store(rs, u, x)
                mx = jnp.maximum(mx, x)
            mx_scr[rs, :] = mx

    def sum_pass(c_aug, n_sub, load):
        for grp in groups:
            m_rep = mrep_scr[grp, :]
            ph = [jnp.exp2(load(grp, u) - m_rep).astype(BF16) for u in range(n_sub)]
            p = ph[0] if n_sub == 1 else jnp.concatenate(ph, axis=1)
            acc_scr[grp, :] += jnp.dot(p, c_aug, preferred_element_type=F32)

    def key_rows(step):
        return pl.ds(pl.multiple_of(step * KEY_SUB * T, KEY_SUB * T), KEY_SUB * T)

    upper = (row < col).astype(BF16)

    def mask_step(step, seen, near):
        madds, bias_fns = [], []
        for u in range(KEY_SUB):
            j = KEY_SUB * step + u
            kk = keys_scr[j]
            eq = kk == thr
            eqf = jnp.where(eq, 1.0, 0.0)
            before = jnp.dot(eqf.astype(BF16), upper, preferred_element_type=F32) + seen
            sel = (kk > thr) | (eq & (before < need))
            if near:
                sel = sel & ((j * T + col) <= (i * T + row))
                dsel = jnp.clip(i - j, 0, 2)
                bias_fns.append(lambda hd, dsel=dsel: near_ref[dsel, hd])
            else:
                bias_fns.append(None)
            seen = seen + jnp.sum(eqf, axis=1, keepdims=True)
            madds.append(jnp.where(sel, 0.0, NEG_BIG))

        def store(rs, u, x):
            lg_scr[step, rs, u * T:(u + 1) * T] = x

        max_pass(caug_scr[key_rows(step), :], madds, bias_fns, store)
        return seen

    def sum_step(step, carry):
        sum_pass(caug_scr[key_rows(step), :], KEY_SUB,
                 lambda grp, u: lg_scr[step, grp, u * T:(u + 1) * T])
        return carry

    def store_meta(rs, u, x):
        lgm_scr[rs, :] = x

    n_far = jnp.maximum(i - 1, 0) // KEY_SUB
    n_steps = (i + KEY_SUB) // KEY_SUB
    max_pass(cmaug_scr[...], [jnp.where(col < N_META, 0.0, NEG_BIG)],
             [lambda hd: metab_ref[meta_sel, hd]], store_meta)
    seen = lax.fori_loop(0, n_far, lambda s, c: mask_step(s, c, False), jnp.zeros((T, 1), F32))
    lax.fori_loop(n_far, n_steps, lambda s, c: mask_step(s, c, True), seen)
    mrep_scr[...] = jnp.broadcast_to(jnp.max(mx_scr[...], axis=1, keepdims=True), mrep_scr.shape)
    sum_pass(cmaug_scr[...], 1, lambda grp, u: lgm_scr[grp, :])
    lax.fori_loop(0, n_steps, sum_step, 0)

    for hd in range(H):
        rs = slice(hd * T, (hd + 1) * T)
        o = acc_scr[rs, 0:KV_RANK] / acc_scr[rs, KV_RANK:KV_RANK + 1]
        out_ref[:, hd * A_HEAD_DIM:(hd + 1) * A_HEAD_DIM] = jnp.dot(
            o.astype(BF16), wuv_ref[hd], preferred_element_type=F32).astype(out_ref.dtype)


def _dsa(qa, qi, sm, c, w_uk, w_uv, tables, batch, seq):
    n_real = batch * seq
    nq = seq // LANES
    n_tiles = ((nq + SCORE_CHUNK - 1) // SCORE_CHUNK) * SCORE_CHUNK
    top_k = min(TOPK_MAX, seq // 4)
    meta_blk = n_real // N_META
    near, metab, mmb = tables
    wuk_t = jnp.swapaxes(w_uk, 1, 2).astype(BF16)
    wuv = w_uv.astype(BF16)
    full = lambda a: pl.BlockSpec(a.shape, lambda b, i: (0,) * a.ndim)
    assert seq % (SCORE_CHUNK * LANES) == 0 and nq % KEY_SUB == 0 and SCORE_CHUNK % KEY_SUB == 0
    out_real, out_meta = pl.pallas_call(
        functools.partial(_dsa_kernel, top_k),
        grid=(batch, nq),
        in_specs=[pl.BlockSpec((LANES, A_WIDTH), lambda b, i: (b * nq + i, 0)),
                  pl.BlockSpec((LANES, IDX_HEADS * IDX_DIM), lambda b, i: (b * nq + i, 0)),
                  pl.BlockSpec((LANES, LANES), lambda b, i: (b * nq + i, 0)),
                  pl.BlockSpec((seq, KV_RANK), lambda b, i: (b, 0)),
                  pl.BlockSpec((seq, LANES), lambda b, i: (b, 0)),
                  pl.BlockSpec((N_META, KV_RANK), lambda b, i: (meta_blk + b, 0)),
                  pl.BlockSpec((N_META, A_WIDTH), lambda b, i: (meta_blk + b, 0)),
                  full(wuk_t), full(wuv), full(near), full(metab), full(mmb)],
        out_specs=[pl.BlockSpec((LANES, A_WIDTH), lambda b, i: (b * nq + i, 0)),
                   pl.BlockSpec((N_META, A_WIDTH), lambda b, i: (b, 0))],
        out_shape=[jax.ShapeDtypeStruct((n_real, A_WIDTH), BF16),
                   jax.ShapeDtypeStruct((batch * N_META, A_WIDTH), BF16)],
        scratch_shapes=[pltpu.VMEM((n_tiles, LANES, LANES), jnp.int32),
                        pltpu.VMEM((n_tiles, LANES, LANES), jnp.int16),
                        pltpu.VMEM((n_tiles, LANES, LANES), jnp.int16),
                        pltpu.VMEM((nq // KEY_SUB, A_HEADS * LANES, KEY_SUB * LANES), F32),
                        pltpu.VMEM((A_HEADS * LANES, LANES), F32),
                        pltpu.VMEM((seq, KV_RANK + LANES), BF16),
                        pltpu.VMEM((seq, IDX_DIM), BF16),
                        pltpu.VMEM((LANES, KV_RANK + LANES), BF16),
                        pltpu.VMEM((A_HEADS * LANES, KV_RANK), BF16),
                        pltpu.VMEM((A_HEADS * LANES, LANES), F32),
                        pltpu.VMEM((A_HEADS * LANES, LANES), F32),
                        pltpu.VMEM((A_HEADS * LANES, KV_RANK + LANES), F32)],
        compiler_params=_params(("arbitrary", "arbitrary")),
        name="dsa",
    )(qa, qi, sm, c, sm, c, qa, wuk_t, wuv, near, metab, mmb)
    return out_real, out_meta


def _merge_kernel(alpha, h_ref, hm_ref, ha_ref, g_ref, wbm_ref, wba_ref, wo_ref, lg_ref, lb_ref,
                  wr_ref, br_ref, h1_ref, comb_ref):
    d = h_ref.shape[1]
    gm = _sigmoid(g_ref[:, 0:d].astype(F32))
    ga = _sigmoid(g_ref[:, d:2 * d].astype(F32))
    y = gm * jnp.dot(hm_ref[...], wbm_ref[...], preferred_element_type=F32) + \
        ga * jnp.dot(ha_ref[...], wba_ref[...], preferred_element_type=F32)
    z = alpha * h_ref[...] + jnp.dot(y.astype(BF16), wo_ref[...], preferred_element_type=F32)
    h1 = _layer_norm_rows(z, lg_ref[...], lb_ref[...], 1e-5)
    h1_ref[...] = h1

    tm = h1.shape[0]
    logits_t = lax.dot_general(wr_ref[...], h1.astype(BF16), (((1,), (1,)), ((), ())),
                               preferred_element_type=F32)
    scores = _sigmoid(logits_t[0:N_EXPERTS, :])
    sel = scores + br_ref[0:N_EXPERTS, :]
    best = None
    for gidx in range(N_GROUPS):
        r0, r1, r2, r3 = (sel[gidx * GROUP_SIZE + u:gidx * GROUP_SIZE + u + 1, :] for u in range(4))
        a, b = jnp.maximum(r0, r1), jnp.minimum(r0, r1)
        c, dd = jnp.maximum(r2, r3), jnp.minimum(r2, r3)
        gs = jnp.maximum(a, c) + jnp.maximum(jnp.minimum(a, c), jnp.maximum(b, dd))
        if best is None:
            best, bg = gs, jnp.zeros((1, tm), jnp.int32)
        else:
            upd = gs > best
            bg = jnp.where(upd, gidx, bg)
            best = jnp.where(upd, gs, best)
    eidx = lax.broadcasted_iota(jnp.int32, (N_EXPERTS, tm), 0)
    masked = jnp.where((eidx // GROUP_SIZE) == bg, sel, -jnp.inf)
    v1 = jnp.max(masked, axis=0, keepdims=True)
    i1 = jnp.min(jnp.where(masked == v1, eidx, N_EXPERTS), axis=0, keepdims=True)
    masked2 = jnp.where(eidx == i1, -jnp.inf, masked)
    v2 = jnp.max(masked2, axis=0, keepdims=True)
    i2 = jnp.min(jnp.where(masked2 == v2, eidx, N_EXPERTS), axis=0, keepdims=True)
    s1 = jnp.sum(jnp.where(eidx == i1, scores, 0.0), axis=0, keepdims=True)
    s2 = jnp.sum(jnp.where(eidx == i2, scores, 0.0), axis=0, keepdims=True)
    tot = s1 + s2
    comb_t = jnp.where(eidx == i1, s1 / tot, 0.0) + jnp.where(eidx == i2, s2 / tot, 0.0)
    comb_pad = jnp.concatenate([comb_t, jnp.zeros((LANES - N_EXPERTS, tm), F32)], axis=0)
    comb_ref[...] = comb_pad.T


def _merge(h, hm, ha, g, w_bm, w_ba, w_o, ln_g, ln_b, w_router, b_router, alpha):
    n, d = h.shape
    wr_t = jnp.zeros((LANES, d), F32).at[0:N_EXPERTS].set(w_router.T).astype(BF16)
    br = jnp.zeros((LANES, 1), F32).at[0:N_EXPERTS, 0].set(b_router)
    row = lambda width: pl.BlockSpec((ROW_TILE, width), lambda r: (r, 0))
    full = lambda a: pl.BlockSpec(a.shape, lambda r: (0,) * a.ndim)
    args = (h, hm, ha, g, w_bm.astype(BF16), w_ba.astype(BF16), w_o.astype(BF16),
            ln_g.reshape(1, d), ln_b.reshape(1, d), wr_t, br)
    return pl.pallas_call(
        functools.partial(_merge_kernel, alpha),
        grid=(n // ROW_TILE,),
        in_specs=[row(d), row(M_WIDTH), row(A_WIDTH), row(2 * d)] + [full(a) for a in args[4:]],
        out_specs=[row(d), row(LANES)],
        out_shape=[jax.ShapeDtypeStruct((n, d), F32), jax.ShapeDtypeStruct((n, LANES), F32)],
        compiler_params=_params(("parallel",)),
        name="merge",
    )(*args)


def _moe_kernel(alpha, h_ref, comb_ref, wg_ref, wu_ref, wd_ref, lg_ref, lb_ref, out_ref,
                xb_scr, acc_scr):
    e = pl.program_id(1)

    @pl.when(e == 0)
    def _():
        xb_scr[...] = h_ref[...].astype(BF16)
        acc_scr[...] = jnp.zeros_like(acc_scr)

    xb = xb_scr[...]
    gate = jnp.dot(xb, wg_ref[...], preferred_element_type=F32)
    up = jnp.dot(xb, wu_ref[...], preferred_element_type=F32)
    he = gate * _sigmoid(gate) * up
    o = jnp.dot(he.astype(BF16), wd_ref[...], preferred_element_type=F32)
    comb = comb_ref[...]
    lane = lax.broadcasted_iota(jnp.int32, comb.shape, 1)
    ce = jnp.sum(jnp.where(lane == e, comb, 0.0), axis=1, keepdims=True)
    acc_scr[...] += ce * o

    @pl.when(e == pl.num_programs(1) - 1)
    def _():
        z = alpha * h_ref[...] + acc_scr[...]
        out_ref[...] = _layer_norm_rows(z, lg_ref[...], lb_ref[...], 1e-5)


def _moe(h, comb, w_gate, w_up, w_down, ln_g, ln_b, alpha):
    n, d = h.shape
    de = w_gate.shape[-1]
    tm = MOE_ROW_TILE if n % MOE_ROW_TILE == 0 else ROW_TILE
    return pl.pallas_call(
        functools.partial(_moe_kernel, alpha),
        grid=(n // tm, N_EXPERTS),
        in_specs=[pl.BlockSpec((tm, d), lambda r, e: (r, 0)),
                  pl.BlockSpec((tm, LANES), lambda r, e: (r, 0)),
                  pl.BlockSpec((None, d, de), lambda r, e: (e, 0, 0)),
                  pl.BlockSpec((None, d, de), lambda r, e: (e, 0, 0)),
                  pl.BlockSpec((None, de, d), lambda r, e: (e, 0, 0)),
                  pl.BlockSpec((1, d), lambda r, e: (0, 0)),
                  pl.BlockSpec((1, d), lambda r, e: (0, 0))],
        out_specs=pl.BlockSpec((tm, d), lambda r, e: (r, 0)),
        out_shape=jax.ShapeDtypeStruct((n, d), F32),
        scratch_shapes=[pltpu.VMEM((tm, d), BF16), pltpu.VMEM((tm, d), F32)],
        compiler_params=_params(("parallel", "arbitrary")),
        name="moe",
    )(h, comb, w_gate.astype(BF16), w_up.astype(BF16), w_down.astype(BF16),
      ln_g.reshape(1, d), ln_b.reshape(1, d))


def _with_meta(real, meta, n_pad):
    pad = n_pad - real.shape[0] - meta.shape[0]
    return jnp.concatenate([real, meta, jnp.zeros((pad, real.shape[1]), real.dtype)], axis=0)


def kernel(x, meta_tokens, ln_in_g, ln_in_b, w_in, conv_w, b_if, mnorm_g, kv_norm_g, w_uk, w_uv,
           w_branch_m, w_branch_a, w_out, ln1_g, ln1_b, w_router, b_router, w_gate, w_up, w_down,
           ln2_g, ln2_b, rel_bias):
    batch, seq, d = x.shape
    depth = w_in.shape[0]
    alpha = (2 * depth) ** 0.25
    n_real = batch * seq
    n_meta = batch * N_META
    tile = math.lcm(ROW_TILE, MOE_ROW_TILE)
    n_pad = -(-(n_real + n_meta) // tile) * tile
    assert seq % LANES == 0 and n_real % N_META == 0

    h = _with_meta(x.reshape(n_real, d), jnp.tile(meta_tokens.astype(x.dtype), (batch, 1)), n_pad)
    h = _input_ln(h, ln_in_g, ln_in_b)
    tables = _bias_tables(rel_bias)
    for l in range(depth):
        pm, qa, qi, c, sm, g = _project(h, _pack_w_in(w_in[l]), kv_norm_g[l], b_if[l])
        hm_real, hm_meta = _mlstm(pm, sm, conv_w[l], mnorm_g[l], batch, seq)
        ha_real, ha_meta = _dsa(qa, qi, sm, c, w_uk[l], w_uv[l], tables, batch, seq)
        hm = _with_meta(hm_real, hm_meta, n_pad)
        ha = _with_meta(ha_real, ha_meta, n_pad)
        h1, comb = _merge(h, hm, ha, g, w_branch_m[l], w_branch_a[l], w_out[l], ln1_g[l], ln1_b[l],
                          w_router, b_router, alpha)
        h = _moe(h1, comb, w_gate[l], w_up[l], w_down[l], ln2_g[l], ln2_b[l], alpha)
    return h[:n_real].reshape(batch, seq, d)
```

```python
import functools
import math

import numpy as np
import jax
import jax.numpy as jnp
from jax import lax
from jax.experimental import pallas as pl
from jax.experimental.pallas import tpu as pltpu

F32 = jnp.float32
BF16 = jnp.bfloat16

N_META = 16
M_HEADS = 4
M_HEAD_DIM = 128
M_WIDTH = M_HEADS * M_HEAD_DIM
CONV_WIDTH = 4
A_HEADS = 8
A_HEAD_DIM = 64
A_WIDTH = A_HEADS * A_HEAD_DIM
KV_RANK = 128
IDX_HEADS = 4
IDX_DIM = 64
IDX_SCALE = (IDX_HEADS * IDX_DIM) ** -0.5
TOPK_MAX = 256
REL_BUCKETS = 32
REL_MAX_EXACT = 16
REL_MAX_DIST = 128
N_EXPERTS = 16
N_GROUPS = 4
GROUP_SIZE = N_EXPERTS // N_GROUPS

LANES = 128
ROW_TILE = 256
MOE_ROW_TILE = 768
MOE_CAP = 256
M_CHUNK = 256
SCORE_CHUNK = 4
KEY_SUB = 4
ATT_GROUPS = 2
VMEM_LIMIT = 56 * 1024 * 1024
NEG_BIG = -1e30
INT_MIN = -2 ** 31
HALF_BIAS = 2 ** 15
LOG2E = math.log2(math.e)

PK_PM = 0
PK_QA = 2048
PK_QI = 2560
PK_CKV = 2816
PK_SM = 2944
PK_G = 3072
PK_TOTAL = 5120
SM_KIDX = 0
SM_WIDX = 64
SM_IPRE = 68
SM_FPRE = 72


def _params(sem):
    return pltpu.CompilerParams(dimension_semantics=sem, vmem_limit_bytes=VMEM_LIMIT)


def _sigmoid(x):
    return 1.0 / (1.0 + jnp.exp(-x))


def _layer_norm_rows(x, g, b, eps):
    mu = jnp.mean(x, axis=-1, keepdims=True)
    xc = x - mu
    var = jnp.mean(xc * xc, axis=-1, keepdims=True)
    return xc * lax.rsqrt(var + eps) * g + b


def _ln_kernel(x_ref, g_ref, b_ref, o_ref):
    o_ref[...] = _layer_norm_rows(x_ref[...], g_ref[...], b_ref[...], 1e-5)


def _input_ln(x, g, b):
    n, d = x.shape
    return pl.pallas_call(
        _ln_kernel,
        grid=(n // ROW_TILE,),
        in_specs=[pl.BlockSpec((ROW_TILE, d), lambda r: (r, 0)),
                  pl.BlockSpec((1, d), lambda r: (0, 0)),
                  pl.BlockSpec((1, d), lambda r: (0, 0))],
        out_specs=pl.BlockSpec((ROW_TILE, d), lambda r: (r, 0)),
        out_shape=jax.ShapeDtypeStruct((n, d), F32),
        compiler_params=_params(("parallel",)),
        name="input_ln",
    )(x, g.reshape(1, d), b.reshape(1, d))


def _proj_kernel(h_ref, w_ref, kvg_ref, brow_ref, pm_ref, qa_ref, qi_ref, c_ref, sm_ref, g_ref):
    x = h_ref[...].astype(BF16)

    def mm(lo, width):
        return jnp.dot(x, w_ref[:, lo:lo + width], preferred_element_type=F32)

    pm_ref[...] = mm(PK_PM, 2048).astype(BF16)
    qa_ref[...] = mm(PK_QA, 512).astype(BF16)
    qi_ref[...] = mm(PK_QI, 256).astype(BF16)
    ckv = mm(PK_CKV, 128)
    c_ref[...] = ckv * lax.rsqrt(jnp.mean(ckv * ckv, axis=-1, keepdims=True) + 1e-6) * kvg_ref[...]
    sm_ref[...] = mm(PK_SM, 128) + brow_ref[...]
    g_ref[...] = mm(PK_G, 2048).astype(BF16)


def _pack_w_in(w):
    d = w.shape[0]
    cols = [w[:, 0:2048], w[:, 2056:2568], w[:, 2696:2952], w[:, 2568:2696],
            w[:, 2952:3016], w[:, 3016:3020], w[:, 2048:2056],
            jnp.zeros((d, LANES - 76), w.dtype), w[:, 3020:5068]]
    return jnp.concatenate(cols, axis=1).astype(BF16)


def _project(h, w_packed, kv_g, b_if):
    n, d = h.shape
    brow = jnp.concatenate([jnp.zeros((SM_IPRE,), F32), b_if.astype(F32),
                            jnp.zeros((LANES - SM_IPRE - 2 * M_HEADS,), F32)]).reshape(1, LANES)
    row = lambda width: pl.BlockSpec((ROW_TILE, width), lambda r: (r, 0))
    shp = lambda width, dt: jax.ShapeDtypeStruct((n, width), dt)
    return pl.pallas_call(
        _proj_kernel,
        grid=(n // ROW_TILE,),
        in_specs=[row(d),
                  pl.BlockSpec((d, PK_TOTAL), lambda r: (0, 0)),
                  pl.BlockSpec((1, KV_RANK), lambda r: (0, 0)),
                  pl.BlockSpec((1, LANES), lambda r: (0, 0))],
        out_specs=[row(2048), row(512), row(256), row(128), row(128), row(2048)],
        out_shape=[shp(2048, BF16), shp(512, BF16), shp(256, BF16), shp(128, F32), shp(128, F32),
                   shp(2048, BF16)],
        compiler_params=_params(("parallel",)),
        name="in_proj",
    )(h, w_packed, kv_g.reshape(1, KV_RANK), brow)


def _mlstm_chunk_math(q, k, v, li, lf, ct, n, m):
    L = q.shape[0]
    r = lax.broadcasted_iota(jnp.int32, (L, L), 0)
    c = lax.broadcasted_iota(jnp.int32, (L, L), 1)
    eye = r == c
    tril = c <= r
    lf_b = jnp.broadcast_to(lf, (L, L))
    li_b = jnp.broadcast_to(li, (L, L))
    lf_row = jnp.sum(jnp.where(eye, lf_b, 0.0), axis=0, keepdims=True)
    li_row = jnp.sum(jnp.where(eye, li_b, 0.0), axis=0, keepdims=True)
    b_col = jnp.sum(jnp.where(tril, jnp.broadcast_to(lf_row, (L, L)), 0.0), axis=1, keepdims=True)
    b_row = jnp.sum(jnp.where(r <= c, lf_b, 0.0), axis=0, keepdims=True)
    d = jnp.where(tril, b_col - b_row + li_row, -jnp.inf)
    inter = b_col + m
    m_t = jnp.maximum(inter, jnp.max(d, axis=1, keepdims=True))
    a = jnp.exp(inter - m_t)
    qb, kb, vb = q.astype(BF16), k.astype(BF16), v.astype(BF16)
    qk = lax.dot_general(qb, kb, (((1,), (1,)), ((), ())), preferred_element_type=F32)
    w = jnp.exp(d - m_t) * qk
    num = a * jnp.dot(qb, ct.astype(BF16), preferred_element_type=F32) + \
        jnp.dot(w.astype(BF16), vb, preferred_element_type=F32)
    den = a * jnp.sum(q * n, axis=1, keepdims=True) + jnp.sum(w, axis=1, keepdims=True)
    h = num / jnp.maximum(jnp.abs(den), jnp.exp(-m_t))
    b_last = b_col[L - 1:L, :]
    g = b_last - b_col + li
    m_new = jnp.maximum(b_last + m, jnp.max(g, axis=0, keepdims=True))
    decay = jnp.exp(b_last + m - m_new)
    kw = k * jnp.exp(g - m_new)
    ct_new = decay * ct + lax.dot_general(kw.astype(BF16), vb, (((0,), (0,)), ((), ())),
                                          preferred_element_type=F32)
    n_new = decay * n + jnp.sum(kw, axis=0, keepdims=True)
    return h, ct_new, n_new, m_new


def _mlstm_kernel(pm_ref, sm_ref, pmm_ref, smm_ref, cw_ref, mg_ref, out_ref, outm_ref,
                  ct_scr, n_scr, m_scr, x_scr):
    ci = pl.program_id(1)
    tail = 8

    def run_chunk(p_ref, s_ref, o_ref):
        L = p_ref.shape[0]
        x_scr[tail:tail + L, :] = p_ref[:, 0:2 * M_WIDTH].astype(F32)
        conv = cw_ref[0:1, :] * x_scr[tail - 3:tail - 3 + L, :]
        for j in range(1, CONV_WIDTH):
            conv = conv + cw_ref[j:j + 1, :] * x_scr[tail - 3 + j:tail - 3 + j + L, :]
        x_scr[0:tail, :] = x_scr[L:L + tail, :]
        qk = conv * _sigmoid(conv)
        for hd in range(M_HEADS):
            lo = hd * M_HEAD_DIM
            q = qk[:, lo:lo + M_HEAD_DIM] * (M_HEAD_DIM ** -0.5)
            k = qk[:, M_WIDTH + lo:M_WIDTH + lo + M_HEAD_DIM]
            v = p_ref[:, 2 * M_WIDTH + lo:2 * M_WIDTH + lo + M_HEAD_DIM].astype(F32)
            li = s_ref[:, SM_IPRE + hd:SM_IPRE + hd + 1]
            f = s_ref[:, SM_FPRE + hd:SM_FPRE + hd + 1]
            lf = jnp.minimum(f, 0.0) - jnp.log1p(jnp.exp(-jnp.abs(f)))
            h, ct_new, n_new, m_new = _mlstm_chunk_math(
                q, k, v, li, lf, ct_scr[hd], n_scr[hd:hd + 1, :], m_scr[hd:hd + 1, 0:1])
            ct_scr[hd] = ct_new
            n_scr[hd:hd + 1, :] = n_new
            m_scr[hd:hd + 1, :] = jnp.broadcast_to(m_new, (1, LANES))
            mu = jnp.mean(h, axis=-1, keepdims=True)
            hc = h - mu
            var = jnp.mean(hc * hc, axis=-1, keepdims=True)
            o_gate = _sigmoid(p_ref[:, 3 * M_WIDTH + lo:3 * M_WIDTH + lo + M_HEAD_DIM].astype(F32))
            o_ref[:, lo:lo + M_HEAD_DIM] = (hc * lax.rsqrt(var + 1e-5) * mg_ref[:, lo:lo + M_HEAD_DIM]
                                            * o_gate).astype(o_ref.dtype)

    @pl.when(ci == 0)
    def _():
        ct_scr[...] = jnp.zeros_like(ct_scr)
        n_scr[...] = jnp.zeros_like(n_scr)
        m_scr[...] = jnp.zeros_like(m_scr)
        x_scr[...] = jnp.zeros_like(x_scr)
        run_chunk(pmm_ref, smm_ref, outm_ref)

    run_chunk(pm_ref, sm_ref, out_ref)


def _mlstm(pm, sm, conv_w, mnorm_g, batch, seq):
    n_real = batch * seq
    chunk = min(M_CHUNK, seq)
    nc = seq // chunk
    meta_blk = n_real // N_META
    out_real, out_meta = pl.pallas_call(
        _mlstm_kernel,
        grid=(batch, nc),
        in_specs=[pl.BlockSpec((chunk, 4 * M_WIDTH), lambda b, c: (b * nc + c, 0)),
                  pl.BlockSpec((chunk, LANES), lambda b, c: (b * nc + c, 0)),
                  pl.BlockSpec((N_META, 4 * M_WIDTH), lambda b, c: (meta_blk + b, 0)),
                  pl.BlockSpec((N_META, LANES), lambda b, c: (meta_blk + b, 0)),
                  pl.BlockSpec((CONV_WIDTH, 2 * M_WIDTH), lambda b, c: (0, 0)),
                  pl.BlockSpec((1, M_WIDTH), lambda b, c: (0, 0))],
        out_specs=[pl.BlockSpec((chunk, M_WIDTH), lambda b, c: (b * nc + c, 0)),
                   pl.BlockSpec((N_META, M_WIDTH), lambda b, c: (b, 0))],
        out_shape=[jax.ShapeDtypeStruct((n_real, M_WIDTH), BF16),
                   jax.ShapeDtypeStruct((batch * N_META, M_WIDTH), BF16)],
        scratch_shapes=[pltpu.VMEM((M_HEADS, M_HEAD_DIM, M_HEAD_DIM), F32),
                        pltpu.VMEM((8, LANES), F32),
                        pltpu.VMEM((8, LANES), F32),
                        pltpu.VMEM((chunk + 8, 2 * M_WIDTH), F32)],
        compiler_params=_params(("arbitrary", "arbitrary")),
        name="mlstm",
    )(pm, sm, pm, sm, conv_w.astype(F32), mnorm_g.reshape(1, M_WIDTH).astype(F32))
    return out_real, out_meta


def _rel_bucket_np(dist):
    n = np.maximum(dist, 0)
    nf = np.maximum(n, REL_MAX_EXACT).astype(np.float32)
    large = REL_MAX_EXACT + (np.log(nf / np.float32(REL_MAX_EXACT)) /
                             np.float32(math.log(REL_MAX_DIST / REL_MAX_EXACT))
                             * np.float32(REL_BUCKETS - REL_MAX_EXACT)).astype(np.int32)
    large = np.minimum(large, REL_BUCKETS - 1)
    return np.where(n < REL_MAX_EXACT, n, large).astype(np.int32)


def _bias_tables(rel_bias):
    q = np.arange(LANES)[:, None]
    k = np.arange(LANES)[None, :]
    far = 4 * LANES
    assert (_rel_bucket_np(np.arange(LANES + 1, far)) == REL_BUCKETS - 1).all()
    far_idx = np.full((LANES, LANES), REL_BUCKETS - 1, np.int32)
    near_idx = np.stack([_rel_bucket_np(q - k), _rel_bucket_np(LANES + q - k), far_idx])
    meta_idx = np.stack([_rel_bucket_np(q + N_META - np.minimum(k, N_META - 1)), far_idx])
    mq = np.arange(N_META)[:, None]
    mm_idx = _rel_bucket_np(mq - np.minimum(k, N_META - 1))
    rb = rel_bias.astype(F32)
    gather = lambda idx: jnp.moveaxis(rb[idx], -1, -3)
    rel = lambda idx: (gather(idx) - rb[REL_BUCKETS - 1][:, None, None]) * LOG2E
    return rel(near_idx), rel(meta_idx), gather(mm_idx)


def _dsa_kernel(top_k, qa_ref, qi_ref, sm_ref, cb_ref, smb_ref, cm_ref, qam_ref,
                wuk_ref, wuv_ref, near_ref, metab_ref, mmb_ref,
                out_ref, outm_ref,
                keys_scr, hi_scr, lo_scr, lg_scr, lgm_scr, caug_scr, kbf_scr, cmaug_scr, qs_scr, mx_scr, mrep_scr, acc_scr):
    i = pl.program_id(1)
    T = LANES
    H = A_HEADS
    col = lax.broadcasted_iota(jnp.int32, (T, T), 1)
    row = lax.broadcasted_iota(jnp.int32, (T, T), 0)
    nt = (((1,), (1,)), ((), ()))

    def q_latent(qa, hd, scale):
        ql = jnp.dot(qa[:, hd * A_HEAD_DIM:(hd + 1) * A_HEAD_DIM], wuk_ref[hd],
                     preferred_element_type=F32)
        return (ql * scale).astype(BF16)

    def ones_column(n):
        return jnp.where(lax.broadcasted_iota(jnp.int32, (n, T), 1) == 0, 1.0, 0.0).astype(BF16)

    @pl.when(i == 0)
    def _():
        caug_scr[:, 0:KV_RANK] = cb_ref[...].astype(BF16)
        caug_scr[:, KV_RANK:KV_RANK + T] = ones_column(caug_scr.shape[0])
        kbf_scr[...] = smb_ref[:, SM_KIDX:SM_KIDX + IDX_DIM].astype(BF16)
        cm_pad = jnp.concatenate([cm_ref[...], jnp.zeros((T - N_META, KV_RANK), F32)], axis=0)
        cmaug_scr[:, 0:KV_RANK] = cm_pad.astype(BF16)
        cmaug_scr[:, KV_RANK:KV_RANK + T] = ones_column(T)
        cmk = cmaug_scr[:, 0:KV_RANK]
        qam = qam_ref[...]
        mrow = lax.broadcasted_iota(jnp.int32, (N_META, T), 0)
        mcol = lax.broadcasted_iota(jnp.int32, (N_META, T), 1)
        for hd in range(H):
            lg = lax.dot_general(q_latent(qam, hd, A_HEAD_DIM ** -0.5), cmk, nt,
                                 preferred_element_type=F32) + mmb_ref[hd]
            lg = jnp.where(mcol <= mrow, lg, NEG_BIG)
            p = jnp.exp(lg - jnp.max(lg, axis=1, keepdims=True))
            p = p / jnp.sum(p, axis=1, keepdims=True)
            o = jnp.dot(p.astype(BF16), cmk, preferred_element_type=F32)
            outm_ref[:, hd * A_HEAD_DIM:(hd + 1) * A_HEAD_DIM] = jnp.dot(
                o.astype(BF16), wuv_ref[hd], preferred_element_type=F32).astype(outm_ref.dtype)

    qa = qa_ref[...]
    for hd in range(H):
        qs_scr[hd * T:(hd + 1) * T, :] = q_latent(qa, hd, A_HEAD_DIM ** -0.5 * LOG2E)
    qi = qi_ref[...]
    wv = sm_ref[:, SM_WIDX:SM_WIDX + IDX_HEADS] * IDX_SCALE
    t_col = i * T + lax.broadcasted_iota(jnp.int32, (T, 1), 0)
    n_chunks = (i + SCORE_CHUNK) // SCORE_CHUNK
    CW = SCORE_CHUNK * T

    def score_body(cix, carry):
        kc = kbf_scr[pl.ds(pl.multiple_of(cix * CW, CW), CW), :]
        acc = jnp.zeros((T, CW), F32)
        for hh in range(IDX_HEADS):
            s = lax.dot_general(qi[:, hh * IDX_DIM:(hh + 1) * IDX_DIM], kc, nt,
                                preferred_element_type=F32)
            acc = acc + wv[:, hh:hh + 1] * jnp.maximum(s, 0.0)
        acc = jnp.where(acc == 0.0, 0.0, acc)
        bits = lax.bitcast_convert_type(acc, jnp.int32)
        key = jnp.where(bits < 0, bits ^ jnp.int32(0x7FFFFFFF), bits)
        s_idx = cix * CW + lax.broadcasted_iota(jnp.int32, (T, CW), 1)
        key = jnp.where(s_idx <= t_col, key, jnp.int32(INT_MIN))
        for u in range(SCORE_CHUNK):
            tile = key[:, u * T:(u + 1) * T]
            keys_scr[cix * SCORE_CHUNK + u] = tile
            tile_t = tile.T
            hi_scr[cix * SCORE_CHUNK + u] = lax.shift_right_arithmetic(tile_t, 16).astype(jnp.int16)
            lo_scr[cix * SCORE_CHUNK + u] = ((tile_t & 0xFFFF) - HALF_BIAS).astype(jnp.int16)
        return carry

    lax.fori_loop(0, n_chunks, score_body, 0)

    def rep16(row_i32):
        return jnp.broadcast_to(row_i32, (T, T)).astype(jnp.int16)

    def count16(src_scr, pred_fn):
        def body(cix, cnt):
            for u in range(SCORE_CHUNK):
                hit = pred_fn(src_scr[cix * SCORE_CHUNK + u])
                cnt = cnt + jnp.where(hit, jnp.int16(1), jnp.int16(0))
            return cnt
        cnt = lax.fori_loop(0, n_chunks, body, jnp.zeros((T, T), jnp.int16))
        return jnp.sum(cnt.astype(F32), axis=0, keepdims=True)

    def search16(src_scr, k_row):
        def bit_body(bi, ans):
            cand_u = ans | lax.shift_left(jnp.int32(1), jnp.int32(15) - bi)
            cand = rep16(cand_u - HALF_BIAS)
            total = count16(src_scr, lambda x: x >= cand)
            return jnp.where(total >= k_row, cand_u, ans)
        return lax.fori_loop(0, 16, bit_body, jnp.zeros((1, T), jnp.int32))

    k_row = jnp.full((1, T), float(top_k), F32)
    hi_s = search16(hi_scr, k_row) - HALF_BIAS
    hi_rep = rep16(hi_s)
    k_low = k_row - count16(hi_scr, lambda x: x > hi_rep)

    def band_body(j, carry):
        lo_scr[j] = jnp.where(hi_scr[j] == hi_rep, lo_scr[j], jnp.int16(-HALF_BIAS))
        return carry

    lax.fori_loop(0, n_chunks * SCORE_CHUNK, band_body, 0)
    lo_u = search16(lo_scr, k_low)
    lo_rep = rep16(lo_u - HALF_BIAS)
    need_row = k_low - count16(lo_scr, lambda x: x > lo_rep)
    thr_row = lax.shift_left(hi_s, 16) | lo_u
    thr = jnp.broadcast_to(thr_row, (T, T)).T
    need = jnp.broadcast_to(need_row, (T, T)).T

    mx_scr[...] = jnp.full_like(mx_scr, NEG_BIG)
    acc_scr[...] = jnp.zeros_like(acc_scr)
    hg = H // ATT_GROUPS
    groups = [slice(g * hg * T, (g + 1) * hg * T) for g in range(ATT_GROUPS)]
    meta_sel = jnp.minimum(i, 1)

    def max_pass(c_aug, madds, bias_fns, store):
        ck = c_aug[:, 0:KV_RANK]
        lgs = [lax.dot_general(qs_scr[rs, :], ck, nt, preferred_element_type=F32) for rs in groups]
        for hd in range(H):
            rs = slice(hd * T, (hd + 1) * T)
            lo = (hd % hg) * T
            mx = mx_scr[rs, :]
            for u in range(len(madds)):
                x = lgs[hd // hg][lo:lo + T, u * T:(u + 1) * T] + madds[u]
                if bias_fns[u] is not None:
                    x = x + bias_fns[u](hd)
                store(rs, u, x)
                mx = jnp.maximum(mx, x)
            mx_scr[rs, :] = mx

    def sum_pass(c_aug, n_sub, load):
        for grp in groups:
            m_rep = mrep_scr[grp, :]
            ph = [jnp.exp2(load(grp, u) - m_rep).astype(BF16) for u in range(n_sub)]
            p = ph[0] if n_sub == 1 else jnp.concatenate(ph, axis=1)
            acc_scr[grp, :] += jnp.dot(p, c_aug, preferred_element_type=F32)

    def key_rows(step):
        return pl.ds(pl.multiple_of(step * KEY_SUB * T, KEY_SUB * T), KEY_SUB * T)

    upper = (row < col).astype(BF16)

    def mask_step(step, seen, near):
        madds, bias_fns = [], []
        for u in range(KEY_SUB):
            j = KEY_SUB * step + u
            kk = keys_scr[j]
            eq = kk == thr
            eqf = jnp.where(eq, 1.0, 0.0)
            before = jnp.dot(eqf.astype(BF16), upper, preferred_element_type=F32) + seen
            sel = (kk > thr) | (eq & (before < need))
            if near:
                sel = sel & ((j * T + col) <= (i * T + row))
                dsel = jnp.clip(i - j, 0, 2)
                bias_fns.append(lambda hd, dsel=dsel: near_ref[dsel, hd])
            else:
                bias_fns.append(None)
            seen = seen + jnp.sum(eqf, axis=1, keepdims=True)
            madds.append(jnp.where(sel, 0.0, NEG_BIG))

        def store(rs, u, x):
            lg_scr[step, rs, u * T:(u + 1) * T] = x

        max_pass(caug_scr[key_rows(step), :], madds, bias_fns, store)
        return seen

    def sum_step(step, carry):
        sum_pass(caug_scr[key_rows(step), :], KEY_SUB,
                 lambda grp, u: lg_scr[step, grp, u * T:(u + 1) * T])
        return carry

    def store_meta(rs, u, x):
        lgm_scr[rs, :] = x

    n_far = jnp.maximum(i - 1, 0) // KEY_SUB
    n_steps = (i + KEY_SUB) // KEY_SUB
    max_pass(cmaug_scr[...], [jnp.where(col < N_META, 0.0, NEG_BIG)],
             [lambda hd: metab_ref[meta_sel, hd]], store_meta)
    seen = lax.fori_loop(0, n_far, lambda s, c: mask_step(s, c, False), jnp.zeros((T, 1), F32))
    lax.fori_loop(n_far, n_steps, lambda s, c: mask_step(s, c, True), seen)
    mrep_scr[...] = jnp.broadcast_to(jnp.max(mx_scr[...], axis=1, keepdims=True), mrep_scr.shape)
    sum_pass(cmaug_scr[...], 1, lambda grp, u: lgm_scr[grp, :])
    lax.fori_loop(0, n_steps, sum_step, 0)

    for hd in range(H):
        rs = slice(hd * T, (hd + 1) * T)
        o = acc_scr[rs, 0:KV_RANK] / acc_scr[rs, KV_RANK:KV_RANK + 1]
        out_ref[:, hd * A_HEAD_DIM:(hd + 1) * A_HEAD_DIM] = jnp.dot(
            o.astype(BF16), wuv_ref[hd], preferred_element_type=F32).astype(out_ref.dtype)


def _dsa(qa, qi, sm, c, w_uk, w_uv, tables, batch, seq):
    n_real = batch * seq
    nq = seq // LANES
    n_tiles = ((nq + SCORE_CHUNK - 1) // SCORE_CHUNK) * SCORE_CHUNK
    top_k = min(TOPK_MAX, seq // 4)
    meta_blk = n_real // N_META
    near, metab, mmb = tables
    wuk_t = jnp.swapaxes(w_uk, 1, 2).astype(BF16)
    wuv = w_uv.astype(BF16)
    full = lambda a: pl.BlockSpec(a.shape, lambda b, i: (0,) * a.ndim)
    assert seq % (SCORE_CHUNK * LANES) == 0 and nq % KEY_SUB == 0 and SCORE_CHUNK % KEY_SUB == 0
    out_real, out_meta = pl.pallas_call(
        functools.partial(_dsa_kernel, top_k),
        grid=(batch, nq),
        in_specs=[pl.BlockSpec((LANES, A_WIDTH), lambda b, i: (b * nq + i, 0)),
                  pl.BlockSpec((LANES, IDX_HEADS * IDX_DIM), lambda b, i: (b * nq + i, 0)),
                  pl.BlockSpec((LANES, LANES), lambda b, i: (b * nq + i, 0)),
                  pl.BlockSpec((seq, KV_RANK), lambda b, i: (b, 0)),
                  pl.BlockSpec((seq, LANES), lambda b, i: (b, 0)),
                  pl.BlockSpec((N_META, KV_RANK), lambda b, i: (meta_blk + b, 0)),
                  pl.BlockSpec((N_META, A_WIDTH), lambda b, i: (meta_blk + b, 0)),
                  full(wuk_t), full(wuv), full(near), full(metab), full(mmb)],
        out_specs=[pl.BlockSpec((LANES, A_WIDTH), lambda b, i: (b * nq + i, 0)),
                   pl.BlockSpec((N_META, A_WIDTH), lambda b, i: (b, 0))],
        out_shape=[jax.ShapeDtypeStruct((n_real, A_WIDTH), BF16),
                   jax.ShapeDtypeStruct((batch * N_META, A_WIDTH), BF16)],
        scratch_shapes=[pltpu.VMEM((n_tiles, LANES, LANES), jnp.int32),
                        pltpu.VMEM((n_tiles, LANES, LANES), jnp.int16),
                        pltpu.VMEM((n_tiles, LANES, LANES), jnp.int16),
                        pltpu.VMEM((nq // KEY_SUB, A_HEADS * LANES, KEY_SUB * LANES), F32),
                        pltpu.VMEM((A_HEADS * LANES, LANES), F32),
                        pltpu.VMEM((seq, KV_RANK + LANES), BF16),
                        pltpu.VMEM((seq, IDX_DIM), BF16),
                        pltpu.VMEM((LANES, KV_RANK + LANES), BF16),
                        pltpu.VMEM((A_HEADS * LANES, KV_RANK), BF16),
                        pltpu.VMEM((A_HEADS * LANES, LANES), F32),
                        pltpu.VMEM((A_HEADS * LANES, LANES), F32),
                        pltpu.VMEM((A_HEADS * LANES, KV_RANK + LANES), F32)],
        compiler_params=_params(("arbitrary", "arbitrary")),
        name="dsa",
    )(qa, qi, sm, c, sm, c, qa, wuk_t, wuv, near, metab, mmb)
    return out_real, out_meta


def _merge_kernel(alpha, h_ref, hm_ref, ha_ref, g_ref, wbm_ref, wba_ref, wo_ref, lg_ref, lb_ref,
                  wr_ref, br_ref, h1_ref, comb_ref, bgt_ref, cnt_ref):
    d = h_ref.shape[1]
    gm = _sigmoid(g_ref[:, 0:d].astype(F32))
    ga = _sigmoid(g_ref[:, d:2 * d].astype(F32))
    y = gm * jnp.dot(hm_ref[...], wbm_ref[...], preferred_element_type=F32) + \
        ga * jnp.dot(ha_ref[...], wba_ref[...], preferred_element_type=F32)
    z = alpha * h_ref[...] + jnp.dot(y.astype(BF16), wo_ref[...], preferred_element_type=F32)
    h1 = _layer_norm_rows(z, lg_ref[...], lb_ref[...], 1e-5)
    h1_ref[...] = h1

    tm = h1.shape[0]
    logits_t = lax.dot_general(wr_ref[...], h1.astype(BF16), (((1,), (1,)), ((), ())),
                               preferred_element_type=F32)
    scores = _sigmoid(logits_t[0:N_EXPERTS, :])
    sel = scores + br_ref[0:N_EXPERTS, :]
    best = None
    for gidx in range(N_GROUPS):
        r0, r1, r2, r3 = (sel[gidx * GROUP_SIZE + u:gidx * GROUP_SIZE + u + 1, :] for u in range(4))
        a, b = jnp.maximum(r0, r1), jnp.minimum(r0, r1)
        c, dd = jnp.maximum(r2, r3), jnp.minimum(r2, r3)
        gs = jnp.maximum(a, c) + jnp.maximum(jnp.minimum(a, c), jnp.maximum(b, dd))
        if best is None:
            best, bg = gs, jnp.zeros((1, tm), jnp.int32)
        else:
            upd = gs > best
            bg = jnp.where(upd, gidx, bg)
            best = jnp.where(upd, gs, best)
    eidx = lax.broadcasted_iota(jnp.int32, (N_EXPERTS, tm), 0)
    masked = jnp.where((eidx // GROUP_SIZE) == bg, sel, -jnp.inf)
    v1 = jnp.max(masked, axis=0, keepdims=True)
    i1 = jnp.min(jnp.where(masked == v1, eidx, N_EXPERTS), axis=0, keepdims=True)
    masked2 = jnp.where(eidx == i1, -jnp.inf, masked)
    v2 = jnp.max(masked2, axis=0, keepdims=True)
    i2 = jnp.min(jnp.where(masked2 == v2, eidx, N_EXPERTS), axis=0, keepdims=True)
    s1 = jnp.sum(jnp.where(eidx == i1, scores, 0.0), axis=0, keepdims=True)
    s2 = jnp.sum(jnp.where(eidx == i2, scores, 0.0), axis=0, keepdims=True)
    tot = s1 + s2
    comb_t = jnp.where(eidx == i1, s1 / tot, 0.0) + jnp.where(eidx == i2, s2 / tot, 0.0)
    comb_pad = jnp.concatenate([comb_t, jnp.zeros((LANES - N_EXPERTS, tm), F32)], axis=0)
    comb_ref[...] = comb_pad.T
    bgt_ref[...] = jnp.broadcast_to(bg, (8, tm))
    gidx8 = lax.broadcasted_iota(jnp.int32, (8, tm), 0)
    counts = jnp.sum(jnp.where(gidx8 == bg, 1.0, 0.0), axis=1, keepdims=True)
    cnt_ref[0] = jnp.broadcast_to(counts, (8, LANES)).astype(jnp.int32)


def _merge(h, hm, ha, g, w_bm, w_ba, w_o, ln_g, ln_b, w_router, b_router, alpha):
    n, d = h.shape
    tm = MOE_ROW_TILE
    wr_t = jnp.zeros((LANES, d), F32).at[0:N_EXPERTS].set(w_router.T).astype(BF16)
    br = jnp.zeros((LANES, 1), F32).at[0:N_EXPERTS, 0].set(b_router)
    row = lambda width: pl.BlockSpec((tm, width), lambda r: (r, 0))
    full = lambda a: pl.BlockSpec(a.shape, lambda r: (0,) * a.ndim)
    args = (h, hm, ha, g, w_bm.astype(BF16), w_ba.astype(BF16), w_o.astype(BF16),
            ln_g.reshape(1, d), ln_b.reshape(1, d), wr_t, br)
    return pl.pallas_call(
        functools.partial(_merge_kernel, alpha),
        grid=(n // tm,),
        in_specs=[row(d), row(M_WIDTH), row(A_WIDTH), row(2 * d)] + [full(a) for a in args[4:]],
        out_specs=[row(d), row(LANES),
                   pl.BlockSpec((8, tm), lambda r: (0, r)),
                   pl.BlockSpec((1, 8, LANES), lambda r: (r, 0, 0))],
        out_shape=[jax.ShapeDtypeStruct((n, d), F32), jax.ShapeDtypeStruct((n, LANES), F32),
                   jax.ShapeDtypeStruct((8, n), jnp.int32),
                   jax.ShapeDtypeStruct((n // tm, 8, LANES), jnp.int32)],
        compiler_params=_params(("parallel",)),
        name="merge",
    )(*args)


def _moe_kernel(alpha, cap, cnt_ref, h_ref, comb_ref, bgt_ref, wg_ref, wu_ref, wd_ref, lg_ref, lb_ref,
                out_ref, xb_scr, cs_scr, yt_scr, tri_scr):
    r = pl.program_id(0)
    g = pl.program_id(1)
    rows = h_ref.shape[0]

    @pl.when((r == 0) & (g == 0))
    def _():
        t0 = lax.broadcasted_iota(jnp.int32, (rows, rows), 0)
        t1 = lax.broadcasted_iota(jnp.int32, (rows, rows), 1)
        tri_scr[...] = (t0 < t1).astype(BF16)

    @pl.when(g == 0)
    def _():
        xb_scr[...] = h_ref[...].astype(BF16)
        yt_scr[...] = jnp.zeros_like(yt_scr)
        c = comb_ref[...]
        for part in range(3):
            cb = c.astype(BF16)
            cs_scr[part] = cb
            c = c - cb.astype(F32)

    member = bgt_ref[0:1, :] == g
    mem8 = jnp.broadcast_to(jnp.where(member, 1.0, 0.0), (8, rows)).astype(BF16)
    rank = jnp.dot(mem8, tri_scr[...], preferred_element_type=F32)[0:1, :].astype(jnp.int32)
    n_blocks = (cnt_ref[r * N_GROUPS + g] + cap - 1) // cap
    lane = lax.broadcasted_iota(jnp.int32, (cap, LANES), 1)
    tn = (((0,), (0,)), ((), ()))

    def block(b, carry):
        slot = lax.broadcasted_iota(jnp.int32, (cap, rows), 0) + b * cap
        onehot = jnp.where(member & (rank == slot), 1.0, 0.0).astype(BF16)
        xg = jnp.dot(onehot, xb_scr[...], preferred_element_type=F32).astype(BF16)
        cw = jnp.dot(onehot, cs_scr[0], preferred_element_type=F32)
        for part in range(1, 3):
            cw = cw + jnp.dot(onehot, cs_scr[part], preferred_element_type=F32)
        y = jnp.zeros((cap, out_ref.shape[1]), F32)
        for e in range(GROUP_SIZE):
            gate = jnp.dot(xg, wg_ref[e], preferred_element_type=F32)
            up = jnp.dot(xg, wu_ref[e], preferred_element_type=F32)
            he = gate * _sigmoid(gate) * up
            o = jnp.dot(he.astype(BF16), wd_ref[e], preferred_element_type=F32)
            ce = jnp.sum(jnp.where(lane == g * GROUP_SIZE + e, cw, 0.0), axis=1, keepdims=True)
            y = y + ce * o
        y_hi = y.astype(BF16)
        y_lo = (y - y_hi.astype(F32)).astype(BF16)
        yt_scr[...] += lax.dot_general(onehot, y_hi, tn, preferred_element_type=F32) + \
            lax.dot_general(onehot, y_lo, tn, preferred_element_type=F32)
        return carry

    lax.fori_loop(0, n_blocks, block, 0)

    @pl.when(g == pl.num_programs(1) - 1)
    def _():
        z = alpha * h_ref[...] + yt_scr[...]
        out_ref[...] = _layer_norm_rows(z, lg_ref[...], lb_ref[...], 1e-5)


def _moe(h, comb, bgt, counts, w_gate, w_up, w_down, ln_g, ln_b, alpha):
    n, d = h.shape
    de = w_gate.shape[-1]
    tm = MOE_ROW_TILE
    cnt = counts[:, 0:N_GROUPS, 0].reshape(-1)
    grid_spec = pltpu.PrefetchScalarGridSpec(
        num_scalar_prefetch=1,
        grid=(n // tm, N_GROUPS),
        in_specs=[pl.BlockSpec((tm, d), lambda r, g, c: (r, 0)),
                  pl.BlockSpec((tm, LANES), lambda r, g, c: (r, 0)),
                  pl.BlockSpec((8, tm), lambda r, g, c: (0, r)),
                  pl.BlockSpec((GROUP_SIZE, d, de), lambda r, g, c: (g, 0, 0)),
                  pl.BlockSpec((GROUP_SIZE, d, de), lambda r, g, c: (g, 0, 0)),
                  pl.BlockSpec((GROUP_SIZE, de, d), lambda r, g, c: (g, 0, 0)),
                  pl.BlockSpec((1, d), lambda r, g, c: (0, 0)),
                  pl.BlockSpec((1, d), lambda r, g, c: (0, 0))],
        out_specs=pl.BlockSpec((tm, d), lambda r, g, c: (r, 0)),
        scratch_shapes=[pltpu.VMEM((tm, d), BF16), pltpu.VMEM((3, tm, LANES), BF16),
                        pltpu.VMEM((tm, d), F32), pltpu.VMEM((tm, tm), BF16)])
    return pl.pallas_call(
        functools.partial(_moe_kernel, alpha, MOE_CAP),
        grid_spec=grid_spec,
        out_shape=jax.ShapeDtypeStruct((n, d), F32),
        compiler_params=_params(("arbitrary", "arbitrary")),
        name="moe",
    )(cnt, h, comb, bgt, w_gate.astype(BF16), w_up.astype(BF16), w_down.astype(BF16),
      ln_g.reshape(1, d), ln_b.reshape(1, d))


def _with_meta(real, meta, n_pad):
    pad = n_pad - real.shape[0] - meta.shape[0]
    return jnp.concatenate([real, meta, jnp.zeros((pad, real.shape[1]), real.dtype)], axis=0)


def kernel(x, meta_tokens, ln_in_g, ln_in_b, w_in, conv_w, b_if, mnorm_g, kv_norm_g, w_uk, w_uv,
           w_branch_m, w_branch_a, w_out, ln1_g, ln1_b, w_router, b_router, w_gate, w_up, w_down,
           ln2_g, ln2_b, rel_bias):
    batch, seq, d = x.shape
    depth = w_in.shape[0]
    alpha = (2 * depth) ** 0.25
    n_real = batch * seq
    n_meta = batch * N_META
    tile = math.lcm(ROW_TILE, MOE_ROW_TILE)
    n_pad = -(-(n_real + n_meta) // tile) * tile
    assert seq % LANES == 0 and n_real % N_META == 0

    h = _with_meta(x.reshape(n_real, d), jnp.tile(meta_tokens.astype(x.dtype), (batch, 1)), n_pad)
    h = _input_ln(h, ln_in_g, ln_in_b)
    tables = _bias_tables(rel_bias)
    for l in range(depth):
        pm, qa, qi, c, sm, g = _project(h, _pack_w_in(w_in[l]), kv_norm_g[l], b_if[l])
        hm_real, hm_meta = _mlstm(pm, sm, conv_w[l], mnorm_g[l], batch, seq)
        ha_real, ha_meta = _dsa(qa, qi, sm, c, w_uk[l], w_uv[l], tables, batch, seq)
        hm = _with_meta(hm_real, hm_meta, n_pad)
        ha = _with_meta(ha_real, ha_meta, n_pad)
        h1, comb, bgt, counts = _merge(h, hm, ha, g, w_branch_m[l], w_branch_a[l], w_out[l], ln1_g[l], ln1_b[l],
                          w_router, b_router, alpha)
        h = _moe(h1, comb, bgt, counts, w_gate[l], w_up[l], w_down[l], ln2_g[l], ln2_b[l], alpha)
    return h[:n_real].reshape(batch, seq, d)
```

```python
import functools
import math

import numpy as np
import jax
import jax.numpy as jnp
from jax import lax
from jax.experimental import pallas as pl
from jax.experimental.pallas import tpu as pltpu

F32 = jnp.float32
BF16 = jnp.bfloat16

N_META = 16
M_HEADS = 4
M_HEAD_DIM = 128
M_WIDTH = M_HEADS * M_HEAD_DIM
CONV_WIDTH = 4
A_HEADS = 8
A_HEAD_DIM = 64
A_WIDTH = A_HEADS * A_HEAD_DIM
KV_RANK = 128
IDX_HEADS = 4
IDX_DIM = 64
IDX_SCALE = (IDX_HEADS * IDX_DIM) ** -0.5
TOPK_MAX = 256
REL_BUCKETS = 32
REL_MAX_EXACT = 16
REL_MAX_DIST = 128
N_EXPERTS = 16
N_GROUPS = 4
GROUP_SIZE = N_EXPERTS // N_GROUPS

LANES = 128
ROW_TILE = 256
MOE_ROW_TILE = 768
MOE_CAP = 256
M_CHUNK = 256
SCORE_CHUNK = 4
KEY_SUB = 4
ATT_GROUPS = 2
VMEM_LIMIT = 56 * 1024 * 1024
NEG_BIG = -1e30
INT_MIN = -2 ** 31
HALF_BIAS = 2 ** 15
LOG2E = math.log2(math.e)

PK_PM = 0
PK_QA = 2048
PK_QI = 2560
PK_CKV = 2816
PK_SM = 2944
PK_G = 3072
PK_TOTAL = 5120
SM_KIDX = 0
SM_WIDX = 64
SM_IPRE = 68
SM_FPRE = 72


def _params(sem):
    return pltpu.CompilerParams(dimension_semantics=sem, vmem_limit_bytes=VMEM_LIMIT)


def _sigmoid(x):
    return 1.0 / (1.0 + jnp.exp(-x))


def _layer_norm_rows(x, g, b, eps):
    mu = jnp.mean(x, axis=-1, keepdims=True)
    xc = x - mu
    var = jnp.mean(xc * xc, axis=-1, keepdims=True)
    return xc * lax.rsqrt(var + eps) * g + b


def _ln_kernel(x_ref, g_ref, b_ref, o_ref):
    o_ref[...] = _layer_norm_rows(x_ref[...], g_ref[...], b_ref[...], 1e-5)


def _input_ln(x, g, b):
    n, d = x.shape
    return pl.pallas_call(
        _ln_kernel,
        grid=(n // ROW_TILE,),
        in_specs=[pl.BlockSpec((ROW_TILE, d), lambda r: (r, 0)),
                  pl.BlockSpec((1, d), lambda r: (0, 0)),
                  pl.BlockSpec((1, d), lambda r: (0, 0))],
        out_specs=pl.BlockSpec((ROW_TILE, d), lambda r: (r, 0)),
        out_shape=jax.ShapeDtypeStruct((n, d), F32),
        compiler_params=_params(("parallel",)),
        name="input_ln",
    )(x, g.reshape(1, d), b.reshape(1, d))


def _proj_kernel(h_ref, w_ref, kvg_ref, brow_ref, pm_ref, qa_ref, qi_ref, c_ref, sm_ref, g_ref):
    x = h_ref[...].astype(BF16)

    def mm(lo, width):
        return jnp.dot(x, w_ref[:, lo:lo + width], preferred_element_type=F32)

    pm_ref[...] = mm(PK_PM, 2048).astype(BF16)
    qa_ref[...] = mm(PK_QA, 512).astype(BF16)
    qi_ref[...] = mm(PK_QI, 256).astype(BF16)
    ckv = mm(PK_CKV, 128)
    c_ref[...] = ckv * lax.rsqrt(jnp.mean(ckv * ckv, axis=-1, keepdims=True) + 1e-6) * kvg_ref[...]
    sm_ref[...] = mm(PK_SM, 128) + brow_ref[...]
    g_ref[...] = mm(PK_G, 2048).astype(BF16)


def _pack_w_in(w):
    cols = [w[..., 0:2048], w[..., 2056:2568], w[..., 2696:2952], w[..., 2568:2696],
            w[..., 2952:3016], w[..., 3016:3020], w[..., 2048:2056],
            jnp.zeros(w.shape[:-1] + (LANES - 76,), w.dtype), w[..., 3020:5068]]
    return jnp.concatenate(cols, axis=-1).astype(BF16)


def _pack_b_if(b_if):
    depth = b_if.shape[0]
    return jnp.concatenate([jnp.zeros((depth, SM_IPRE), F32), b_if.astype(F32),
                            jnp.zeros((depth, LANES - SM_IPRE - 2 * M_HEADS), F32)], axis=1)[:, None, :]


def _layer_block(arr, l):
    zeros = (0,) * (arr.ndim - 1)
    return pl.BlockSpec((None,) + arr.shape[1:], lambda *idx: (l,) + zeros)


def _project(h, w_packed, kv_g, brow, l):
    n, d = h.shape
    row = lambda width: pl.BlockSpec((ROW_TILE, width), lambda r: (r, 0))
    shp = lambda width, dt: jax.ShapeDtypeStruct((n, width), dt)
    return pl.pallas_call(
        _proj_kernel,
        grid=(n // ROW_TILE,),
        in_specs=[row(d), _layer_block(w_packed, l), _layer_block(kv_g, l), _layer_block(brow, l)],
        out_specs=[row(2048), row(512), row(256), row(128), row(128), row(2048)],
        out_shape=[shp(2048, BF16), shp(512, BF16), shp(256, BF16), shp(128, F32), shp(128, F32),
                   shp(2048, BF16)],
        compiler_params=_params(("parallel",)),
        name="in_proj",
    )(h, w_packed, kv_g, brow)


def _mlstm_chunk_math(q, k, v, li, lf, ct, n, m):
    L = q.shape[0]
    r = lax.broadcasted_iota(jnp.int32, (L, L), 0)
    c = lax.broadcasted_iota(jnp.int32, (L, L), 1)
    eye = r == c
    tril = c <= r
    lf_b = jnp.broadcast_to(lf, (L, L))
    li_b = jnp.broadcast_to(li, (L, L))
    lf_row = jnp.sum(jnp.where(eye, lf_b, 0.0), axis=0, keepdims=True)
    li_row = jnp.sum(jnp.where(eye, li_b, 0.0), axis=0, keepdims=True)
    b_col = jnp.sum(jnp.where(tril, jnp.broadcast_to(lf_row, (L, L)), 0.0), axis=1, keepdims=True)
    b_row = jnp.sum(jnp.where(r <= c, lf_b, 0.0), axis=0, keepdims=True)
    d = jnp.where(tril, b_col - b_row + li_row, -jnp.inf)
    inter = b_col + m
    m_t = jnp.maximum(inter, jnp.max(d, axis=1, keepdims=True))
    a = jnp.exp(inter - m_t)
    qb, kb, vb = q.astype(BF16), k.astype(BF16), v.astype(BF16)
    qk = lax.dot_general(qb, kb, (((1,), (1,)), ((), ())), preferred_element_type=F32)
    w = jnp.exp(d - m_t) * qk
    num = a * jnp.dot(qb, ct.astype(BF16), preferred_element_type=F32) + \
        jnp.dot(w.astype(BF16), vb, preferred_element_type=F32)
    den = a * jnp.sum(q * n, axis=1, keepdims=True) + jnp.sum(w, axis=1, keepdims=True)
    h = num / jnp.maximum(jnp.abs(den), jnp.exp(-m_t))
    b_last = b_col[L - 1:L, :]
    g = b_last - b_col + li
    m_new = jnp.maximum(b_last + m, jnp.max(g, axis=0, keepdims=True))
    decay = jnp.exp(b_last + m - m_new)
    kw = k * jnp.exp(g - m_new)
    ct_new = decay * ct + lax.dot_general(kw.astype(BF16), vb, (((0,), (0,)), ((), ())),
                                          preferred_element_type=F32)
    n_new = decay * n + jnp.sum(kw, axis=0, keepdims=True)
    return h, ct_new, n_new, m_new


def _mlstm_kernel(pm_ref, sm_ref, pmm_ref, smm_ref, cw_ref, mg_ref, out_ref, outm_ref,
                  ct_scr, n_scr, m_scr, x_scr):
    ci = pl.program_id(1)
    tail = 8

    def run_chunk(p_ref, s_ref, o_ref):
        L = p_ref.shape[0]
        x_scr[tail:tail + L, :] = p_ref[:, 0:2 * M_WIDTH].astype(F32)
        conv = cw_ref[0:1, :] * x_scr[tail - 3:tail - 3 + L, :]
        for j in range(1, CONV_WIDTH):
            conv = conv + cw_ref[j:j + 1, :] * x_scr[tail - 3 + j:tail - 3 + j + L, :]
        x_scr[0:tail, :] = x_scr[L:L + tail, :]
        qk = conv * _sigmoid(conv)
        for hd in range(M_HEADS):
            lo = hd * M_HEAD_DIM
            q = qk[:, lo:lo + M_HEAD_DIM] * (M_HEAD_DIM ** -0.5)
            k = qk[:, M_WIDTH + lo:M_WIDTH + lo + M_HEAD_DIM]
            v = p_ref[:, 2 * M_WIDTH + lo:2 * M_WIDTH + lo + M_HEAD_DIM].astype(F32)
            li = s_ref[:, SM_IPRE + hd:SM_IPRE + hd + 1]
            f = s_ref[:, SM_FPRE + hd:SM_FPRE + hd + 1]
            lf = jnp.minimum(f, 0.0) - jnp.log1p(jnp.exp(-jnp.abs(f)))
            h, ct_new, n_new, m_new = _mlstm_chunk_math(
                q, k, v, li, lf, ct_scr[hd], n_scr[hd:hd + 1, :], m_scr[hd:hd + 1, 0:1])
            ct_scr[hd] = ct_new
            n_scr[hd:hd + 1, :] = n_new
            m_scr[hd:hd + 1, :] = jnp.broadcast_to(m_new, (1, LANES))
            mu = jnp.mean(h, axis=-1, keepdims=True)
            hc = h - mu
            var = jnp.mean(hc * hc, axis=-1, keepdims=True)
            o_gate = _sigmoid(p_ref[:, 3 * M_WIDTH + lo:3 * M_WIDTH + lo + M_HEAD_DIM].astype(F32))
            o_ref[:, lo:lo + M_HEAD_DIM] = (hc * lax.rsqrt(var + 1e-5) * mg_ref[:, lo:lo + M_HEAD_DIM]
                                            * o_gate).astype(o_ref.dtype)

    @pl.when(ci == 0)
    def _():
        ct_scr[...] = jnp.zeros_like(ct_scr)
        n_scr[...] = jnp.zeros_like(n_scr)
        m_scr[...] = jnp.zeros_like(m_scr)
        x_scr[...] = jnp.zeros_like(x_scr)
        run_chunk(pmm_ref, smm_ref, outm_ref)

    run_chunk(pm_ref, sm_ref, out_ref)


def _mlstm(pm, sm, conv_w, mnorm_g, l, batch, seq):
    n_real = batch * seq
    chunk = min(M_CHUNK, seq)
    nc = seq // chunk
    meta_blk = n_real // N_META
    out_real, out_meta = pl.pallas_call(
        _mlstm_kernel,
        grid=(batch, nc),
        in_specs=[pl.BlockSpec((chunk, 4 * M_WIDTH), lambda b, c: (b * nc + c, 0)),
                  pl.BlockSpec((chunk, LANES), lambda b, c: (b * nc + c, 0)),
                  pl.BlockSpec((N_META, 4 * M_WIDTH), lambda b, c: (meta_blk + b, 0)),
                  pl.BlockSpec((N_META, LANES), lambda b, c: (meta_blk + b, 0)),
                  _layer_block(conv_w, l), _layer_block(mnorm_g, l)],
        out_specs=[pl.BlockSpec((chunk, M_WIDTH), lambda b, c: (b * nc + c, 0)),
                   pl.BlockSpec((N_META, M_WIDTH), lambda b, c: (b, 0))],
        out_shape=[jax.ShapeDtypeStruct((n_real, M_WIDTH), BF16),
                   jax.ShapeDtypeStruct((batch * N_META, M_WIDTH), BF16)],
        scratch_shapes=[pltpu.VMEM((M_HEADS, M_HEAD_DIM, M_HEAD_DIM), F32),
                        pltpu.VMEM((8, LANES), F32),
                        pltpu.VMEM((8, LANES), F32),
                        pltpu.VMEM((chunk + 8, 2 * M_WIDTH), F32)],
        compiler_params=_params(("arbitrary", "arbitrary")),
        name="mlstm",
    )(pm, sm, pm, sm, conv_w, mnorm_g)
    return out_real, out_meta


def _rel_bucket_np(dist):
    n = np.maximum(dist, 0)
    nf = np.maximum(n, REL_MAX_EXACT).astype(np.float32)
    large = REL_MAX_EXACT + (np.log(nf / np.float32(REL_MAX_EXACT)) /
                             np.float32(math.log(REL_MAX_DIST / REL_MAX_EXACT))
                             * np.float32(REL_BUCKETS - REL_MAX_EXACT)).astype(np.int32)
    large = np.minimum(large, REL_BUCKETS - 1)
    return np.where(n < REL_MAX_EXACT, n, large).astype(np.int32)


def _bias_tables(rel_bias):
    q = np.arange(LANES)[:, None]
    k = np.arange(LANES)[None, :]
    far = 4 * LANES
    assert (_rel_bucket_np(np.arange(LANES + 1, far)) == REL_BUCKETS - 1).all()
    far_idx = np.full((LANES, LANES), REL_BUCKETS - 1, np.int32)
    near_idx = np.stack([_rel_bucket_np(q - k), _rel_bucket_np(LANES + q - k), far_idx])
    meta_idx = np.stack([_rel_bucket_np(q + N_META - np.minimum(k, N_META - 1)), far_idx])
    mq = np.arange(N_META)[:, None]
    mm_idx = _rel_bucket_np(mq - np.minimum(k, N_META - 1))
    rb = rel_bias.astype(F32)
    gather = lambda idx: jnp.moveaxis(rb[idx], -1, -3)
    rel = lambda idx: (gather(idx) - rb[REL_BUCKETS - 1][:, None, None]) * LOG2E
    return rel(near_idx), rel(meta_idx), gather(mm_idx)


def _dsa_kernel(top_k, qa_ref, qi_ref, sm_ref, cb_ref, smb_ref, cm_ref, qam_ref,
                wuk_ref, wuv_ref, near_ref, metab_ref, mmb_ref,
                out_ref, outm_ref,
                keys_scr, hi_scr, lo_scr, lg_scr, lgm_scr, caug_scr, kbf_scr, cmaug_scr, qs_scr, mx_scr, mrep_scr, acc_scr):
    i = pl.program_id(1)
    T = LANES
    H = A_HEADS
    col = lax.broadcasted_iota(jnp.int32, (T, T), 1)
    row = lax.broadcasted_iota(jnp.int32, (T, T), 0)
    nt = (((1,), (1,)), ((), ()))

    def q_latent(qa, hd, scale):
        ql = jnp.dot(qa[:, hd * A_HEAD_DIM:(hd + 1) * A_HEAD_DIM], wuk_ref[hd],
                     preferred_element_type=F32)
        return (ql * scale).astype(BF16)

    def ones_column(n):
        return jnp.where(lax.broadcasted_iota(jnp.int32, (n, T), 1) == 0, 1.0, 0.0).astype(BF16)

    @pl.when(i == 0)
    def _():
        caug_scr[:, 0:KV_RANK] = cb_ref[...].astype(BF16)
        caug_scr[:, KV_RANK:KV_RANK + T] = ones_column(caug_scr.shape[0])
        kbf_scr[...] = smb_ref[:, SM_KIDX:SM_KIDX + IDX_DIM].astype(BF16)
        cm_pad = jnp.concatenate([cm_ref[...], jnp.zeros((T - N_META, KV_RANK), F32)], axis=0)
        cmaug_scr[:, 0:KV_RANK] = cm_pad.astype(BF16)
        cmaug_scr[:, KV_RANK:KV_RANK + T] = ones_column(T)
        cmk = cmaug_scr[:, 0:KV_RANK]
        qam = qam_ref[...]
        mrow = lax.broadcasted_iota(jnp.int32, (N_META, T), 0)
        mcol = lax.broadcasted_iota(jnp.int32, (N_META, T), 1)
        for hd in range(H):
            lg = lax.dot_general(q_latent(qam, hd, A_HEAD_DIM ** -0.5), cmk, nt,
                                 preferred_element_type=F32) + mmb_ref[hd]
            lg = jnp.where(mcol <= mrow, lg, NEG_BIG)
            p = jnp.exp(lg - jnp.max(lg, axis=1, keepdims=True))
            p = p / jnp.sum(p, axis=1, keepdims=True)
            o = jnp.dot(p.astype(BF16), cmk, preferred_element_type=F32)
            outm_ref[:, hd * A_HEAD_DIM:(hd + 1) * A_HEAD_DIM] = jnp.dot(
                o.astype(BF16), wuv_ref[hd], preferred_element_type=F32).astype(outm_ref.dtype)

    qa = qa_ref[...]
    for hd in range(H):
        qs_scr[hd * T:(hd + 1) * T, :] = q_latent(qa, hd, A_HEAD_DIM ** -0.5 * LOG2E)
    qi = qi_ref[...]
    wv = sm_ref[:, SM_WIDX:SM_WIDX + IDX_HEADS] * IDX_SCALE
    t_col = i * T + lax.broadcasted_iota(jnp.int32, (T, 1), 0)
    n_chunks = (i + SCORE_CHUNK) // SCORE_CHUNK
    CW = SCORE_CHUNK * T

    def score_body(cix, carry):
        kc = kbf_scr[pl.ds(pl.multiple_of(cix * CW, CW), CW), :]
        acc = jnp.zeros((T, CW), F32)
        for hh in range(IDX_HEADS):
            s = lax.dot_general(qi[:, hh * IDX_DIM:(hh + 1) * IDX_DIM], kc, nt,
                                preferred_element_type=F32)
            acc = acc + wv[:, hh:hh + 1] * jnp.maximum(s, 0.0)
        acc = jnp.where(acc == 0.0, 0.0, acc)
        bits = lax.bitcast_convert_type(acc, jnp.int32)
        key = jnp.where(bits < 0, bits ^ jnp.int32(0x7FFFFFFF), bits)
        s_idx = cix * CW + lax.broadcasted_iota(jnp.int32, (T, CW), 1)
        key = jnp.where(s_idx <= t_col, key, jnp.int32(INT_MIN))
        for u in range(SCORE_CHUNK):
            tile = key[:, u * T:(u + 1) * T]
            keys_scr[cix * SCORE_CHUNK + u] = tile
            tile_t = tile.T
            hi_scr[cix * SCORE_CHUNK + u] = lax.shift_right_arithmetic(tile_t, 16).astype(jnp.int16)
            lo_scr[cix * SCORE_CHUNK + u] = ((tile_t & 0xFFFF) - HALF_BIAS).astype(jnp.int16)
        return carry

    lax.fori_loop(0, n_chunks, score_body, 0)

    def rep16(row_i32):
        return jnp.broadcast_to(row_i32, (T, T)).astype(jnp.int16)

    def count16(src_scr, pred_fn):
        def body(cix, cnt):
            for u in range(SCORE_CHUNK):
                hit = pred_fn(src_scr[cix * SCORE_CHUNK + u])
                cnt = cnt + jnp.where(hit, jnp.int16(1), jnp.int16(0))
            return cnt
        cnt = lax.fori_loop(0, n_chunks, body, jnp.zeros((T, T), jnp.int16))
        return jnp.sum(cnt.astype(F32), axis=0, keepdims=True)

    def search16(src_scr, k_row):
        def bit_body(bi, ans):
            cand_u = ans | lax.shift_left(jnp.int32(1), jnp.int32(15) - bi)
            cand = rep16(cand_u - HALF_BIAS)
            total = count16(src_scr, lambda x: x >= cand)
            return jnp.where(total >= k_row, cand_u, ans)
        return lax.fori_loop(0, 16, bit_body, jnp.zeros((1, T), jnp.int32))

    k_row = jnp.full((1, T), float(top_k), F32)
    hi_s = search16(hi_scr, k_row) - HALF_BIAS
    hi_rep = rep16(hi_s)
    k_low = k_row - count16(hi_scr, lambda x: x > hi_rep)

    def band_body(j, carry):
        lo_scr[j] = jnp.where(hi_scr[j] == hi_rep, lo_scr[j], jnp.int16(-HALF_BIAS))
        return carry

    lax.fori_loop(0, n_chunks * SCORE_CHUNK, band_body, 0)
    lo_u = search16(lo_scr, k_low)
    lo_rep = rep16(lo_u - HALF_BIAS)
    need_row = k_low - count16(lo_scr, lambda x: x > lo_rep)
    thr_row = lax.shift_left(hi_s, 16) | lo_u
    thr = jnp.broadcast_to(thr_row, (T, T)).T
    need = jnp.broadcast_to(need_row, (T, T)).T

    mx_scr[...] = jnp.full_like(mx_scr, NEG_BIG)
    acc_scr[...] = jnp.zeros_like(acc_scr)
    hg = H // ATT_GROUPS
    groups = [slice(g * hg * T, (g + 1) * hg * T) for g in range(ATT_GROUPS)]
    meta_sel = jnp.minimum(i, 1)

    def max_pass(c_aug, madds, bias_fns, store):
        ck = c_aug[:, 0:KV_RANK]
        lgs = [lax.dot_general(qs_scr[rs, :], ck, nt, preferred_element_type=F32) for rs in groups]
        for hd in range(H):
            rs = slice(hd * T, (hd + 1) * T)
            lo = (hd % hg) * T
            mx = mx_scr[rs, :]
            for u in range(len(madds)):
                x = lgs[hd // hg][lo:lo + T, u * T:(u + 1) * T] + madds[u]
                if bias_fns[u] is not None:
                    x = x + bias_fns[u](hd)
                store(rs, u, x)
                mx = jnp.maximum(mx, x)
            mx_scr[rs, :] = mx

    def sum_pass(c_aug, n_sub, load):
        for grp in groups:
            m_rep = mrep_scr[grp, :]
            ph = [jnp.exp2(load(grp, u) - m_rep).astype(BF16) for u in range(n_sub)]
            p = ph[0] if n_sub == 1 else jnp.concatenate(ph, axis=1)
            acc_scr[grp, :] += jnp.dot(p, c_aug, preferred_element_type=F32)

    def key_rows(step):
        return pl.ds(pl.multiple_of(step * KEY_SUB * T, KEY_SUB * T), KEY_SUB * T)

    upper = (row < col).astype(BF16)

    def mask_step(step, seen, near):
        madds, bias_fns = [], []
        for u in range(KEY_SUB):
            j = KEY_SUB * step + u
            kk = keys_scr[j]
            eq = kk == thr
            eqf = jnp.where(eq, 1.0, 0.0)
            before = jnp.dot(eqf.astype(BF16), upper, preferred_element_type=F32) + seen
            sel = (kk > thr) | (eq & (before < need))
            if near:
                sel = sel & ((j * T + col) <= (i * T + row))
                dsel = jnp.clip(i - j, 0, 2)
                bias_fns.append(lambda hd, dsel=dsel: near_ref[dsel, hd])
            else:
                bias_fns.append(None)
            seen = seen + jnp.sum(eqf, axis=1, keepdims=True)
            madds.append(jnp.where(sel, 0.0, NEG_BIG))

        def store(rs, u, x):
            lg_scr[step, rs, u * T:(u + 1) * T] = x

        max_pass(caug_scr[key_rows(step), :], madds, bias_fns, store)
        return seen

    def sum_step(step, carry):
        sum_pass(caug_scr[key_rows(step), :], KEY_SUB,
                 lambda grp, u: lg_scr[step, grp, u * T:(u + 1) * T])
        return carry

    def store_meta(rs, u, x):
        lgm_scr[rs, :] = x

    n_far = jnp.maximum(i - 1, 0) // KEY_SUB
    n_steps = (i + KEY_SUB) // KEY_SUB
    max_pass(cmaug_scr[...], [jnp.where(col < N_META, 0.0, NEG_BIG)],
             [lambda hd: metab_ref[meta_sel, hd]], store_meta)
    seen = lax.fori_loop(0, n_far, lambda s, c: mask_step(s, c, False), jnp.zeros((T, 1), F32))
    lax.fori_loop(n_far, n_steps, lambda s, c: mask_step(s, c, True), seen)
    mrep_scr[...] = jnp.broadcast_to(jnp.max(mx_scr[...], axis=1, keepdims=True), mrep_scr.shape)
    sum_pass(cmaug_scr[...], 1, lambda grp, u: lgm_scr[grp, :])
    lax.fori_loop(0, n_steps, sum_step, 0)

    for hd in range(H):
        rs = slice(hd * T, (hd + 1) * T)
        o = acc_scr[rs, 0:KV_RANK] / acc_scr[rs, KV_RANK:KV_RANK + 1]
        out_ref[:, hd * A_HEAD_DIM:(hd + 1) * A_HEAD_DIM] = jnp.dot(
            o.astype(BF16), wuv_ref[hd], preferred_element_type=F32).astype(out_ref.dtype)


def _dsa(qa, qi, sm, c, wuk_t, wuv, tables, l, batch, seq):
    n_real = batch * seq
    nq = seq // LANES
    n_tiles = ((nq + SCORE_CHUNK - 1) // SCORE_CHUNK) * SCORE_CHUNK
    top_k = min(TOPK_MAX, seq // 4)
    meta_blk = n_real // N_META
    near, metab, mmb = tables
    full = lambda a: pl.BlockSpec(a.shape, lambda b, i: (0,) * a.ndim)
    assert seq % (SCORE_CHUNK * LANES) == 0 and nq % KEY_SUB == 0 and SCORE_CHUNK % KEY_SUB == 0
    out_real, out_meta = pl.pallas_call(
        functools.partial(_dsa_kernel, top_k),
        grid=(batch, nq),
        in_specs=[pl.BlockSpec((LANES, A_WIDTH), lambda b, i: (b * nq + i, 0)),
                  pl.BlockSpec((LANES, IDX_HEADS * IDX_DIM), lambda b, i: (b * nq + i, 0)),
                  pl.BlockSpec((LANES, LANES), lambda b, i: (b * nq + i, 0)),
                  pl.BlockSpec((seq, KV_RANK), lambda b, i: (b, 0)),
                  pl.BlockSpec((seq, LANES), lambda b, i: (b, 0)),
                  pl.BlockSpec((N_META, KV_RANK), lambda b, i: (meta_blk + b, 0)),
                  pl.BlockSpec((N_META, A_WIDTH), lambda b, i: (meta_blk + b, 0)),
                  _layer_block(wuk_t, l), _layer_block(wuv, l), full(near), full(metab), full(mmb)],
        out_specs=[pl.BlockSpec((LANES, A_WIDTH), lambda b, i: (b * nq + i, 0)),
                   pl.BlockSpec((N_META, A_WIDTH), lambda b, i: (b, 0))],
        out_shape=[jax.ShapeDtypeStruct((n_real, A_WIDTH), BF16),
                   jax.ShapeDtypeStruct((batch * N_META, A_WIDTH), BF16)],
        scratch_shapes=[pltpu.VMEM((n_tiles, LANES, LANES), jnp.int32),
                        pltpu.VMEM((n_tiles, LANES, LANES), jnp.int16),
                        pltpu.VMEM((n_tiles, LANES, LANES), jnp.int16),
                        pltpu.VMEM((nq // KEY_SUB, A_HEADS * LANES, KEY_SUB * LANES), F32),
                        pltpu.VMEM((A_HEADS * LANES, LANES), F32),
                        pltpu.VMEM((seq, KV_RANK + LANES), BF16),
                        pltpu.VMEM((seq, IDX_DIM), BF16),
                        pltpu.VMEM((LANES, KV_RANK + LANES), BF16),
                        pltpu.VMEM((A_HEADS * LANES, KV_RANK), BF16),
                        pltpu.VMEM((A_HEADS * LANES, LANES), F32),
                        pltpu.VMEM((A_HEADS * LANES, LANES), F32),
                        pltpu.VMEM((A_HEADS * LANES, KV_RANK + LANES), F32)],
        compiler_params=_params(("arbitrary", "arbitrary")),
        name="dsa",
    )(qa, qi, sm, c, sm, c, qa, wuk_t, wuv, near, metab, mmb)
    return out_real, out_meta


def _merge_kernel(alpha, h_ref, hm_ref, ha_ref, g_ref, wbm_ref, wba_ref, wo_ref, lg_ref, lb_ref,
                  wr_ref, br_ref, h1_ref, comb_ref, bgt_ref, cnt_ref):
    d = h_ref.shape[1]
    gm = _sigmoid(g_ref[:, 0:d].astype(F32))
    ga = _sigmoid(g_ref[:, d:2 * d].astype(F32))
    y = gm * jnp.dot(hm_ref[...], wbm_ref[...], preferred_element_type=F32) + \
        ga * jnp.dot(ha_ref[...], wba_ref[...], preferred_element_type=F32)
    z = alpha * h_ref[...] + jnp.dot(y.astype(BF16), wo_ref[...], preferred_element_type=F32)
    h1 = _layer_norm_rows(z, lg_ref[...], lb_ref[...], 1e-5)
    h1_ref[...] = h1

    tm = h1.shape[0]
    logits_t = lax.dot_general(wr_ref[...], h1.astype(BF16), (((1,), (1,)), ((), ())),
                               preferred_element_type=F32)
    scores = _sigmoid(logits_t[0:N_EXPERTS, :])
    sel = scores + br_ref[0:N_EXPERTS, :]
    best = None
    for gidx in range(N_GROUPS):
        r0, r1, r2, r3 = (sel[gidx * GROUP_SIZE + u:gidx * GROUP_SIZE + u + 1, :] for u in range(4))
        a, b = jnp.maximum(r0, r1), jnp.minimum(r0, r1)
        c, dd = jnp.maximum(r2, r3), jnp.minimum(r2, r3)
        gs = jnp.maximum(a, c) + jnp.maximum(jnp.minimum(a, c), jnp.maximum(b, dd))
        if best is None:
            best, bg = gs, jnp.zeros((1, tm), jnp.int32)
        else:
            upd = gs > best
            bg = jnp.where(upd, gidx, bg)
            best = jnp.where(upd, gs, best)
    eidx = lax.broadcasted_iota(jnp.int32, (N_EXPERTS, tm), 0)
    masked = jnp.where((eidx // GROUP_SIZE) == bg, sel, -jnp.inf)
    v1 = jnp.max(masked, axis=0, keepdims=True)
    i1 = jnp.min(jnp.where(masked == v1, eidx, N_EXPERTS), axis=0, keepdims=True)
    masked2 = jnp.where(eidx == i1, -jnp.inf, masked)
    v2 = jnp.max(masked2, axis=0, keepdims=True)
    i2 = jnp.min(jnp.where(masked2 == v2, eidx, N_EXPERTS), axis=0, keepdims=True)
    s1 = jnp.sum(jnp.where(eidx == i1, scores, 0.0), axis=0, keepdims=True)
    s2 = jnp.sum(jnp.where(eidx == i2, scores, 0.0), axis=0, keepdims=True)
    tot = s1 + s2
    comb_t = jnp.where(eidx == i1, s1 / tot, 0.0) + jnp.where(eidx == i2, s2 / tot, 0.0)
    comb_pad = jnp.concatenate([comb_t, jnp.zeros((LANES - N_EXPERTS, tm), F32)], axis=0)
    comb_ref[...] = comb_pad.T
    bgt_ref[...] = jnp.broadcast_to(bg, (8, tm))
    gidx8 = lax.broadcasted_iota(jnp.int32, (8, tm), 0)
    counts = jnp.sum(jnp.where(gidx8 == bg, 1.0, 0.0), axis=1, keepdims=True)
    cnt_ref[0] = jnp.broadcast_to(counts, (8, LANES)).astype(jnp.int32)


def _merge(h, hm, ha, g, w_bm, w_ba, w_o, ln_g, ln_b, wr_t, br, l, alpha):
    n, d = h.shape
    tm = MOE_ROW_TILE
    row = lambda width: pl.BlockSpec((tm, width), lambda r: (r, 0))
    full = lambda a: pl.BlockSpec(a.shape, lambda r: (0,) * a.ndim)
    args = (h, hm, ha, g, w_bm, w_ba, w_o, ln_g, ln_b, wr_t, br)
    return pl.pallas_call(
        functools.partial(_merge_kernel, alpha),
        grid=(n // tm,),
        in_specs=[row(d), row(M_WIDTH), row(A_WIDTH), row(2 * d)] +
                 [_layer_block(a, l) for a in args[4:9]] + [full(wr_t), full(br)],
        out_specs=[row(d), row(LANES),
                   pl.BlockSpec((8, tm), lambda r: (0, r)),
                   pl.BlockSpec((1, 8, LANES), lambda r: (r, 0, 0))],
        out_shape=[jax.ShapeDtypeStruct((n, d), F32), jax.ShapeDtypeStruct((n, LANES), F32),
                   jax.ShapeDtypeStruct((8, n), jnp.int32),
                   jax.ShapeDtypeStruct((n // tm, 8, LANES), jnp.int32)],
        compiler_params=_params(("parallel",)),
        name="merge",
    )(*args)


def _moe_kernel(alpha, cap, cnt_ref, h_ref, comb_ref, bgt_ref, wg_ref, wu_ref, wd_ref, lg_ref, lb_ref,
                out_ref, xb_scr, cs_scr, yt_scr, tri_scr):
    r = pl.program_id(0)
    g = pl.program_id(1)
    rows = h_ref.shape[0]

    @pl.when((r == 0) & (g == 0))
    def _():
        t0 = lax.broadcasted_iota(jnp.int32, (rows, rows), 0)
        t1 = lax.broadcasted_iota(jnp.int32, (rows, rows), 1)
        tri_scr[...] = (t0 < t1).astype(BF16)

    @pl.when(g == 0)
    def _():
        xb_scr[...] = h_ref[...].astype(BF16)
        yt_scr[...] = jnp.zeros_like(yt_scr)
        c = comb_ref[...]
        for part in range(2):
            cb = c.astype(BF16)
            cs_scr[part] = cb
            c = c - cb.astype(F32)

    member = bgt_ref[0:1, :] == g
    mem8 = jnp.broadcast_to(jnp.where(member, 1.0, 0.0), (8, rows)).astype(BF16)
    rank = jnp.dot(mem8, tri_scr[...], preferred_element_type=F32)[0:1, :].astype(jnp.int32)
    n_blocks = (cnt_ref[r * N_GROUPS + g] + cap - 1) // cap
    lane = lax.broadcasted_iota(jnp.int32, (cap, LANES), 1)
    tn = (((0,), (0,)), ((), ()))

    def block(b, carry):
        slot = lax.broadcasted_iota(jnp.int32, (cap, rows), 0) + b * cap
        onehot = jnp.where(member & (rank == slot), 1.0, 0.0).astype(BF16)
        xg = jnp.dot(onehot, xb_scr[...], preferred_element_type=F32).astype(BF16)
        cw = jnp.dot(onehot, cs_scr[0], preferred_element_type=F32)
        cw = cw + jnp.dot(onehot, cs_scr[1], preferred_element_type=F32)
        y = jnp.zeros((cap, out_ref.shape[1]), F32)
        for e in range(GROUP_SIZE):
            gate = jnp.dot(xg, wg_ref[e], preferred_element_type=F32)
            up = jnp.dot(xg, wu_ref[e], preferred_element_type=F32)
            he = gate * _sigmoid(gate) * up
            o = jnp.dot(he.astype(BF16), wd_ref[e], preferred_element_type=F32)
            ce = jnp.sum(jnp.where(lane == g * GROUP_SIZE + e, cw, 0.0), axis=1, keepdims=True)
            y = y + ce * o
        yt_scr[...] += lax.dot_general(onehot, y.astype(BF16), tn, preferred_element_type=F32)
        return carry

    lax.fori_loop(0, n_blocks, block, 0)

    @pl.when(g == pl.num_programs(1) - 1)
    def _():
        z = alpha * h_ref[...] + yt_scr[...]
        out_ref[...] = _layer_norm_rows(z, lg_ref[...], lb_ref[...], 1e-5)


def _moe(h, comb, bgt, counts, w_gate, w_up, w_down, ln_g, ln_b, l, alpha):
    n, d = h.shape
    de = w_gate.shape[-1]
    tm = MOE_ROW_TILE
    cnt = counts[:, 0:N_GROUPS, 0].reshape(-1)
    grid_spec = pltpu.PrefetchScalarGridSpec(
        num_scalar_prefetch=1,
        grid=(n // tm, N_GROUPS),
        in_specs=[pl.BlockSpec((tm, d), lambda r, g, c: (r, 0)),
                  pl.BlockSpec((tm, LANES), lambda r, g, c: (r, 0)),
                  pl.BlockSpec((8, tm), lambda r, g, c: (0, r)),
                  pl.BlockSpec((None, GROUP_SIZE, d, de), lambda r, g, c: (l, g, 0, 0)),
                  pl.BlockSpec((None, GROUP_SIZE, d, de), lambda r, g, c: (l, g, 0, 0)),
                  pl.BlockSpec((None, GROUP_SIZE, de, d), lambda r, g, c: (l, g, 0, 0)),
                  _layer_block(ln_g, l), _layer_block(ln_b, l)],
        out_specs=pl.BlockSpec((tm, d), lambda r, g, c: (r, 0)),
        scratch_shapes=[pltpu.VMEM((tm, d), BF16), pltpu.VMEM((2, tm, LANES), BF16),
                        pltpu.VMEM((tm, d), F32), pltpu.VMEM((tm, tm), BF16)])
    return pl.pallas_call(
        functools.partial(_moe_kernel, alpha, MOE_CAP),
        grid_spec=grid_spec,
        out_shape=jax.ShapeDtypeStruct((n, d), F32),
        compiler_params=_params(("arbitrary", "arbitrary")),
        name="moe",
    )(cnt, h, comb, bgt, w_gate, w_up, w_down, ln_g, ln_b)


def _with_meta(real, meta, n_pad):
    pad = n_pad - real.shape[0] - meta.shape[0]
    return jnp.concatenate([real, meta, jnp.zeros((pad, real.shape[1]), real.dtype)], axis=0)


def kernel(x, meta_tokens, ln_in_g, ln_in_b, w_in, conv_w, b_if, mnorm_g, kv_norm_g, w_uk, w_uv,
           w_branch_m, w_branch_a, w_out, ln1_g, ln1_b, w_router, b_router, w_gate, w_up, w_down,
           ln2_g, ln2_b, rel_bias):
    batch, seq, d = x.shape
    depth = w_in.shape[0]
    alpha = (2 * depth) ** 0.25
    n_real = batch * seq
    n_meta = batch * N_META
    tile = math.lcm(ROW_TILE, MOE_ROW_TILE)
    n_pad = -(-(n_real + n_meta) // tile) * tile
    assert seq % LANES == 0 and n_real % N_META == 0

    h = _with_meta(x.reshape(n_real, d), jnp.tile(meta_tokens.astype(x.dtype), (batch, 1)), n_pad)
    h = _input_ln(h, ln_in_g, ln_in_b)
    tables = _bias_tables(rel_bias)
    row3 = lambda a: a.astype(F32)[:, None, :]
    w_packed, brow, kv_g = _pack_w_in(w_in), _pack_b_if(b_if), row3(kv_norm_g)
    conv_f, mnorm = conv_w.astype(F32), row3(mnorm_g)
    wuk_t = jnp.swapaxes(w_uk, 2, 3).astype(BF16)
    wuv = w_uv.astype(BF16)
    w_bm, w_ba, w_o = w_branch_m.astype(BF16), w_branch_a.astype(BF16), w_out.astype(BF16)
    wr_t = jnp.zeros((LANES, d), F32).at[0:N_EXPERTS].set(w_router.T).astype(BF16)
    br = jnp.zeros((LANES, 1), F32).at[0:N_EXPERTS, 0].set(b_router)
    wg, wu, wd = w_gate.astype(BF16), w_up.astype(BF16), w_down.astype(BF16)
    g1, b1, g2, b2 = row3(ln1_g), row3(ln1_b), row3(ln2_g), row3(ln2_b)
    for l in range(depth):
        pm, qa, qi, c, sm, g = _project(h, w_packed, kv_g, brow, l)
        hm_real, hm_meta = _mlstm(pm, sm, conv_f, mnorm, l, batch, seq)
        ha_real, ha_meta = _dsa(qa, qi, sm, c, wuk_t, wuv, tables, l, batch, seq)
        hm = _with_meta(hm_real, hm_meta, n_pad)
        ha = _with_meta(ha_real, ha_meta, n_pad)
        h1, comb, bgt, counts = _merge(h, hm, ha, g, w_bm, w_ba, w_o, g1, b1, wr_t, br, l, alpha)
        h = _moe(h1, comb, bgt, counts, wg, wu, wd, g2, b2, l, alpha)
    return h[:n_real].reshape(batch, seq, d)
```

```python
import functools
import math

import numpy as np
import jax
import jax.numpy as jnp
from jax import lax
from jax.experimental import pallas as pl
from jax.experimental.pallas import tpu as pltpu

F32 = jnp.float32
BF16 = jnp.bfloat16

N_META = 16
M_HEADS = 4
M_HEAD_DIM = 128
M_WIDTH = M_HEADS * M_HEAD_DIM
CONV_WIDTH = 4
A_HEADS = 8
A_HEAD_DIM = 64
A_WIDTH = A_HEADS * A_HEAD_DIM
KV_RANK = 128
IDX_HEADS = 4
IDX_DIM = 64
IDX_SCALE = (IDX_HEADS * IDX_DIM) ** -0.5
TOPK_MAX = 256
REL_BUCKETS = 32
REL_MAX_EXACT = 16
REL_MAX_DIST = 128
N_EXPERTS = 16
N_GROUPS = 4
GROUP_SIZE = N_EXPERTS // N_GROUPS

LANES = 128
ROW_TILE = 256
MOE_ROW_TILE = 768
MOE_CAP = 256
M_CHUNK = 256
SCORE_CHUNK = 4
KEY_SUB = 4
ATT_GROUPS = 2
VMEM_LIMIT = 56 * 1024 * 1024
NEG_BIG = -1e30
INT_MIN = -2 ** 31
HALF_BIAS = 2 ** 15
LOG2E = math.log2(math.e)

PK_PM = 0
PK_QA = 2048
PK_QI = 2560
PK_CKV = 2816
PK_SM = 2944
PK_G = 3072
PK_TOTAL = 5120
SM_KIDX = 0
SM_WIDX = 64
SM_IPRE = 68
SM_FPRE = 72


def _params(sem):
    return pltpu.CompilerParams(dimension_semantics=sem, vmem_limit_bytes=VMEM_LIMIT)


def _sigmoid(x):
    return 1.0 / (1.0 + jnp.exp(-x))


def _layer_norm_rows(x, g, b, eps):
    mu = jnp.mean(x, axis=-1, keepdims=True)
    xc = x - mu
    var = jnp.mean(xc * xc, axis=-1, keepdims=True)
    return xc * lax.rsqrt(var + eps) * g + b


def _ln_kernel(x_ref, g_ref, b_ref, o_ref):
    o_ref[...] = _layer_norm_rows(x_ref[...], g_ref[...], b_ref[...], 1e-5)


def _input_ln(x, g, b):
    n, d = x.shape
    return pl.pallas_call(
        _ln_kernel,
        grid=(n // ROW_TILE,),
        in_specs=[pl.BlockSpec((ROW_TILE, d), lambda r: (r, 0)),
                  pl.BlockSpec((1, d), lambda r: (0, 0)),
                  pl.BlockSpec((1, d), lambda r: (0, 0))],
        out_specs=pl.BlockSpec((ROW_TILE, d), lambda r: (r, 0)),
        out_shape=jax.ShapeDtypeStruct((n, d), F32),
        compiler_params=_params(("parallel",)),
        name="input_ln",
    )(x, g.reshape(1, d), b.reshape(1, d))


def _proj_kernel(h_ref, w_ref, kvg_ref, brow_ref, pm_ref, qa_ref, qi_ref, c_ref, sm_ref, g_ref):
    x = h_ref[...].astype(BF16)

    def mm(lo, width):
        return jnp.dot(x, w_ref[:, lo:lo + width], preferred_element_type=F32)

    pm_ref[...] = mm(PK_PM, 2048).astype(BF16)
    qa_ref[...] = mm(PK_QA, 512).astype(BF16)
    qi_ref[...] = mm(PK_QI, 256).astype(BF16)
    ckv = mm(PK_CKV, 128)
    c_ref[...] = ckv * lax.rsqrt(jnp.mean(ckv * ckv, axis=-1, keepdims=True) + 1e-6) * kvg_ref[...]
    sm_ref[...] = mm(PK_SM, 128) + brow_ref[...]
    g_ref[...] = mm(PK_G, 2048).astype(BF16)


def _pack_w_in(w):
    cols = [w[..., 0:2048], w[..., 2056:2568], w[..., 2696:2952], w[..., 2568:2696],
            w[..., 2952:3016], w[..., 3016:3020], w[..., 2048:2056],
            jnp.zeros(w.shape[:-1] + (LANES - 76,), w.dtype), w[..., 3020:5068]]
    return jnp.concatenate(cols, axis=-1).astype(BF16)


def _pack_b_if(b_if):
    depth = b_if.shape[0]
    return jnp.concatenate([jnp.zeros((depth, SM_IPRE), F32), b_if.astype(F32),
                            jnp.zeros((depth, LANES - SM_IPRE - 2 * M_HEADS), F32)], axis=1)[:, None, :]


def _layer_block(arr, l):
    zeros = (0,) * (arr.ndim - 1)
    return pl.BlockSpec((None,) + arr.shape[1:], lambda *idx: (l,) + zeros)


def _project(h, w_packed, kv_g, brow, l):
    n, d = h.shape
    row = lambda width: pl.BlockSpec((ROW_TILE, width), lambda r: (r, 0))
    shp = lambda width, dt: jax.ShapeDtypeStruct((n, width), dt)
    return pl.pallas_call(
        _proj_kernel,
        grid=(n // ROW_TILE,),
        in_specs=[row(d), _layer_block(w_packed, l), _layer_block(kv_g, l), _layer_block(brow, l)],
        out_specs=[row(2048), row(512), row(256), row(128), row(128), row(2048)],
        out_shape=[shp(2048, BF16), shp(512, BF16), shp(256, BF16), shp(128, F32), shp(128, F32),
                   shp(2048, BF16)],
        compiler_params=_params(("parallel",)),
        name="in_proj",
    )(h, w_packed, kv_g, brow)


def _mlstm_chunk_math(q, k, v, li, lf, ct, n, m):
    L = q.shape[0]
    r = lax.broadcasted_iota(jnp.int32, (L, L), 0)
    c = lax.broadcasted_iota(jnp.int32, (L, L), 1)
    eye = r == c
    tril = c <= r
    lf_b = jnp.broadcast_to(lf, (L, L))
    li_b = jnp.broadcast_to(li, (L, L))
    lf_row = jnp.sum(jnp.where(eye, lf_b, 0.0), axis=0, keepdims=True)
    li_row = jnp.sum(jnp.where(eye, li_b, 0.0), axis=0, keepdims=True)
    b_col = jnp.sum(jnp.where(tril, jnp.broadcast_to(lf_row, (L, L)), 0.0), axis=1, keepdims=True)
    b_row = jnp.sum(jnp.where(r <= c, lf_b, 0.0), axis=0, keepdims=True)
    d = jnp.where(tril, b_col - b_row + li_row, -jnp.inf)
    inter = b_col + m
    m_t = jnp.maximum(inter, jnp.max(d, axis=1, keepdims=True))
    a = jnp.exp(inter - m_t)
    qb, kb, vb = q.astype(BF16), k.astype(BF16), v.astype(BF16)
    qk = lax.dot_general(qb, kb, (((1,), (1,)), ((), ())), preferred_element_type=F32)
    w = jnp.exp(d - m_t) * qk
    num = a * jnp.dot(qb, ct.astype(BF16), preferred_element_type=F32) + \
        jnp.dot(w.astype(BF16), vb, preferred_element_type=F32)
    den = a * jnp.sum(q * n, axis=1, keepdims=True) + jnp.sum(w, axis=1, keepdims=True)
    h = num / jnp.maximum(jnp.abs(den), jnp.exp(-m_t))
    b_last = b_col[L - 1:L, :]
    g = b_last - b_col + li
    m_new = jnp.maximum(b_last + m, jnp.max(g, axis=0, keepdims=True))
    decay = jnp.exp(b_last + m - m_new)
    kw = k * jnp.exp(g - m_new)
    ct_new = decay * ct + lax.dot_general(kw.astype(BF16), vb, (((0,), (0,)), ((), ())),
                                          preferred_element_type=F32)
    n_new = decay * n + jnp.sum(kw, axis=0, keepdims=True)
    return h, ct_new, n_new, m_new


def _mlstm_kernel(pm_ref, sm_ref, pmm_ref, smm_ref, cw_ref, mg_ref, out_ref, outm_ref,
                  ct_scr, n_scr, m_scr, x_scr):
    ci = pl.program_id(1)
    tail = 8

    def run_chunk(p_ref, s_ref, o_ref):
        L = p_ref.shape[0]
        x_scr[tail:tail + L, :] = p_ref[:, 0:2 * M_WIDTH].astype(F32)
        conv = cw_ref[0:1, :] * x_scr[tail - 3:tail - 3 + L, :]
        for j in range(1, CONV_WIDTH):
            conv = conv + cw_ref[j:j + 1, :] * x_scr[tail - 3 + j:tail - 3 + j + L, :]
        x_scr[0:tail, :] = x_scr[L:L + tail, :]
        qk = conv * _sigmoid(conv)
        for hd in range(M_HEADS):
            lo = hd * M_HEAD_DIM
            q = qk[:, lo:lo + M_HEAD_DIM] * (M_HEAD_DIM ** -0.5)
            k = qk[:, M_WIDTH + lo:M_WIDTH + lo + M_HEAD_DIM]
            v = p_ref[:, 2 * M_WIDTH + lo:2 * M_WIDTH + lo + M_HEAD_DIM].astype(F32)
            li = s_ref[:, SM_IPRE + hd:SM_IPRE + hd + 1]
            f = s_ref[:, SM_FPRE + hd:SM_FPRE + hd + 1]
            lf = jnp.minimum(f, 0.0) - jnp.log1p(jnp.exp(-jnp.abs(f)))
            h, ct_new, n_new, m_new = _mlstm_chunk_math(
                q, k, v, li, lf, ct_scr[hd], n_scr[hd:hd + 1, :], m_scr[hd:hd + 1, 0:1])
            ct_scr[hd] = ct_new
            n_scr[hd:hd + 1, :] = n_new
            m_scr[hd:hd + 1, :] = jnp.broadcast_to(m_new, (1, LANES))
            mu = jnp.mean(h, axis=-1, keepdims=True)
            hc = h - mu
            var = jnp.mean(hc * hc, axis=-1, keepdims=True)
            o_gate = _sigmoid(p_ref[:, 3 * M_WIDTH + lo:3 * M_WIDTH + lo + M_HEAD_DIM].astype(F32))
            o_ref[:, lo:lo + M_HEAD_DIM] = (hc * lax.rsqrt(var + 1e-5) * mg_ref[:, lo:lo + M_HEAD_DIM]
                                            * o_gate).astype(o_ref.dtype)

    @pl.when(ci == 0)
    def _():
        ct_scr[...] = jnp.zeros_like(ct_scr)
        n_scr[...] = jnp.zeros_like(n_scr)
        m_scr[...] = jnp.zeros_like(m_scr)
        x_scr[...] = jnp.zeros_like(x_scr)
        run_chunk(pmm_ref, smm_ref, outm_ref)

    run_chunk(pm_ref, sm_ref, out_ref)


def _mlstm(pm, sm, conv_w, mnorm_g, l, batch, seq):
    n_real = batch * seq
    chunk = min(M_CHUNK, seq)
    nc = seq // chunk
    meta_blk = n_real // N_META
    out_real, out_meta = pl.pallas_call(
        _mlstm_kernel,
        grid=(batch, nc),
        in_specs=[pl.BlockSpec((chunk, 4 * M_WIDTH), lambda b, c: (b * nc + c, 0)),
                  pl.BlockSpec((chunk, LANES), lambda b, c: (b * nc + c, 0)),
                  pl.BlockSpec((N_META, 4 * M_WIDTH), lambda b, c: (meta_blk + b, 0)),
                  pl.BlockSpec((N_META, LANES), lambda b, c: (meta_blk + b, 0)),
                  _layer_block(conv_w, l), _layer_block(mnorm_g, l)],
        out_specs=[pl.BlockSpec((chunk, M_WIDTH), lambda b, c: (b * nc + c, 0)),
                   pl.BlockSpec((N_META, M_WIDTH), lambda b, c: (b, 0))],
        out_shape=[jax.ShapeDtypeStruct((n_real, M_WIDTH), BF16),
                   jax.ShapeDtypeStruct((batch * N_META, M_WIDTH), BF16)],
        scratch_shapes=[pltpu.VMEM((M_HEADS, M_HEAD_DIM, M_HEAD_DIM), F32),
                        pltpu.VMEM((8, LANES), F32),
                        pltpu.VMEM((8, LANES), F32),
                        pltpu.VMEM((chunk + 8, 2 * M_WIDTH), F32)],
        compiler_params=_params(("arbitrary", "arbitrary")),
        name="mlstm",
    )(pm, sm, pm, sm, conv_w, mnorm_g)
    return out_real, out_meta


def _rel_bucket_np(dist):
    n = np.maximum(dist, 0)
    nf = np.maximum(n, REL_MAX_EXACT).astype(np.float32)
    large = REL_MAX_EXACT + (np.log(nf / np.float32(REL_MAX_EXACT)) /
                             np.float32(math.log(REL_MAX_DIST / REL_MAX_EXACT))
                             * np.float32(REL_BUCKETS - REL_MAX_EXACT)).astype(np.int32)
    large = np.minimum(large, REL_BUCKETS - 1)
    return np.where(n < REL_MAX_EXACT, n, large).astype(np.int32)


def _bias_tables(rel_bias):
    q = np.arange(LANES)[:, None]
    k = np.arange(LANES)[None, :]
    far = 4 * LANES
    assert (_rel_bucket_np(np.arange(LANES + 1, far)) == REL_BUCKETS - 1).all()
    far_idx = np.full((LANES, LANES), REL_BUCKETS - 1, np.int32)
    near_idx = np.stack([_rel_bucket_np(q - k), _rel_bucket_np(LANES + q - k), far_idx])
    meta_idx = np.stack([_rel_bucket_np(q + N_META - np.minimum(k, N_META - 1)), far_idx])
    mq = np.arange(N_META)[:, None]
    mm_idx = _rel_bucket_np(mq - np.minimum(k, N_META - 1))
    rb = rel_bias.astype(F32)
    gather = lambda idx: jnp.moveaxis(rb[idx], -1, -3)
    rel = lambda idx: (gather(idx) - rb[REL_BUCKETS - 1][:, None, None]) * LOG2E
    return rel(near_idx), rel(meta_idx), gather(mm_idx)


def _dsa_kernel(top_k, qa_ref, qi_ref, sm_ref, cb_ref, smb_ref, cm_ref, qam_ref,
                wuk_ref, wuv_ref, near_ref, metab_ref, mmb_ref,
                out_ref, outm_ref,
                keys_scr, hi_scr, lo_scr, lg_scr, lgm_scr, rawa_scr, rawb_scr, pa_scr, pb_scr,
                caug_scr, kbf_scr, cmaug_scr, qs_scr, mx_scr, mrep_scr, acc_scr):
    i = pl.program_id(1)
    T = LANES
    H = A_HEADS
    col = lax.broadcasted_iota(jnp.int32, (T, T), 1)
    row = lax.broadcasted_iota(jnp.int32, (T, T), 0)
    nt = (((1,), (1,)), ((), ()))

    def q_latent(qa, hd, scale):
        ql = jnp.dot(qa[:, hd * A_HEAD_DIM:(hd + 1) * A_HEAD_DIM], wuk_ref[hd],
                     preferred_element_type=F32)
        return (ql * scale).astype(BF16)

    def ones_column(n):
        return jnp.where(lax.broadcasted_iota(jnp.int32, (n, T), 1) == 0, 1.0, 0.0).astype(BF16)

    @pl.when(i == 0)
    def _():
        caug_scr[:, 0:KV_RANK] = cb_ref[...].astype(BF16)
        caug_scr[:, KV_RANK:KV_RANK + T] = ones_column(caug_scr.shape[0])
        kbf_scr[...] = smb_ref[:, SM_KIDX:SM_KIDX + IDX_DIM].astype(BF16)
        cm_pad = jnp.concatenate([cm_ref[...], jnp.zeros((T - N_META, KV_RANK), F32)], axis=0)
        cmaug_scr[:, 0:KV_RANK] = cm_pad.astype(BF16)
        cmaug_scr[:, KV_RANK:KV_RANK + T] = ones_column(T)
        cmk = cmaug_scr[:, 0:KV_RANK]
        qam = qam_ref[...]
        mrow = lax.broadcasted_iota(jnp.int32, (N_META, T), 0)
        mcol = lax.broadcasted_iota(jnp.int32, (N_META, T), 1)
        for hd in range(H):
            lg = lax.dot_general(q_latent(qam, hd, A_HEAD_DIM ** -0.5), cmk, nt,
                                 preferred_element_type=F32) + mmb_ref[hd]
            lg = jnp.where(mcol <= mrow, lg, NEG_BIG)
            p = jnp.exp(lg - jnp.max(lg, axis=1, keepdims=True))
            p = p / jnp.sum(p, axis=1, keepdims=True)
            o = jnp.dot(p.astype(BF16), cmk, preferred_element_type=F32)
            outm_ref[:, hd * A_HEAD_DIM:(hd + 1) * A_HEAD_DIM] = jnp.dot(
                o.astype(BF16), wuv_ref[hd], preferred_element_type=F32).astype(outm_ref.dtype)

    qa = qa_ref[...]
    for hd in range(H):
        qs_scr[hd * T:(hd + 1) * T, :] = q_latent(qa, hd, A_HEAD_DIM ** -0.5 * LOG2E)
    qi = qi_ref[...]
    wv = sm_ref[:, SM_WIDX:SM_WIDX + IDX_HEADS] * IDX_SCALE
    t_col = i * T + lax.broadcasted_iota(jnp.int32, (T, 1), 0)
    n_chunks = (i + SCORE_CHUNK) // SCORE_CHUNK
    CW = SCORE_CHUNK * T

    def run_pipelined(n, produce, consume, buf_a, buf_b):
        last = n - 1
        produce(0, buf_a)

        def body(t, carry):
            s = 2 * t
            produce(jnp.minimum(s + 1, last), buf_b)
            consume(s, buf_a)
            produce(jnp.minimum(s + 2, last), buf_a)
            consume(s + 1, buf_b)
            return carry

        lax.fori_loop(0, n // 2, body, 0)

        @pl.when(n % 2 == 1)
        def _():
            consume(last, buf_a)

    def score_matmul(cix, buf):
        kc = kbf_scr[pl.ds(pl.multiple_of(cix * CW, CW), CW), :]
        for hh in range(IDX_HEADS):
            buf[hh * T:(hh + 1) * T, :] = lax.dot_general(
                qi[:, hh * IDX_DIM:(hh + 1) * IDX_DIM], kc, nt, preferred_element_type=F32)

    def score_keys(cix, buf):
        acc = jnp.zeros((T, CW), F32)
        for hh in range(IDX_HEADS):
            acc = acc + wv[:, hh:hh + 1] * jnp.maximum(buf[hh * T:(hh + 1) * T, :], 0.0)
        acc = jnp.where(acc == 0.0, 0.0, acc)
        bits = lax.bitcast_convert_type(acc, jnp.int32)
        key = jnp.where(bits < 0, bits ^ jnp.int32(0x7FFFFFFF), bits)
        s_idx = cix * CW + lax.broadcasted_iota(jnp.int32, (T, CW), 1)
        key = jnp.where(s_idx <= t_col, key, jnp.int32(INT_MIN))
        for u in range(SCORE_CHUNK):
            tile = key[:, u * T:(u + 1) * T]
            keys_scr[cix * SCORE_CHUNK + u] = tile
            tile_t = tile.T
            hi_scr[cix * SCORE_CHUNK + u] = lax.shift_right_arithmetic(tile_t, 16).astype(jnp.int16)
            lo_scr[cix * SCORE_CHUNK + u] = ((tile_t & 0xFFFF) - HALF_BIAS).astype(jnp.int16)

    run_pipelined(n_chunks, score_matmul, score_keys, rawa_scr, rawb_scr)

    def rep16(row_i32):
        return jnp.broadcast_to(row_i32, (T, T)).astype(jnp.int16)

    def count16(src_scr, pred_fn):
        def body(cix, cnt):
            for u in range(SCORE_CHUNK):
                hit = pred_fn(src_scr[cix * SCORE_CHUNK + u])
                cnt = cnt + jnp.where(hit, jnp.int16(1), jnp.int16(0))
            return cnt
        cnt = lax.fori_loop(0, n_chunks, body, jnp.zeros((T, T), jnp.int16))
        return jnp.sum(cnt.astype(F32), axis=0, keepdims=True)

    def search16(src_scr, k_row):
        def bit_body(bi, ans):
            cand_u = ans | lax.shift_left(jnp.int32(1), jnp.int32(15) - bi)
            cand = rep16(cand_u - HALF_BIAS)
            total = count16(src_scr, lambda x: x >= cand)
            return jnp.where(total >= k_row, cand_u, ans)
        return lax.fori_loop(0, 16, bit_body, jnp.zeros((1, T), jnp.int32))

    k_row = jnp.full((1, T), float(top_k), F32)
    hi_s = search16(hi_scr, k_row) - HALF_BIAS
    hi_rep = rep16(hi_s)
    k_low = k_row - count16(hi_scr, lambda x: x > hi_rep)

    def band_body(j, carry):
        lo_scr[j] = jnp.where(hi_scr[j] == hi_rep, lo_scr[j], jnp.int16(-HALF_BIAS))
        return carry

    lax.fori_loop(0, n_chunks * SCORE_CHUNK, band_body, 0)
    lo_u = search16(lo_scr, k_low)
    lo_rep = rep16(lo_u - HALF_BIAS)
    need_row = k_low - count16(lo_scr, lambda x: x > lo_rep)
    thr_row = lax.shift_left(hi_s, 16) | lo_u
    thr = jnp.broadcast_to(thr_row, (T, T)).T
    need = jnp.broadcast_to(need_row, (T, T)).T

    mx_scr[...] = jnp.full_like(mx_scr, NEG_BIG)
    acc_scr[...] = jnp.zeros_like(acc_scr)
    hg = H // ATT_GROUPS
    groups = [slice(g * hg * T, (g + 1) * hg * T) for g in range(ATT_GROUPS)]
    meta_sel = jnp.minimum(i, 1)

    def max_pass(c_aug, madds, bias_fns, store):
        ck = c_aug[:, 0:KV_RANK]
        lgs = [lax.dot_general(qs_scr[rs, :], ck, nt, preferred_element_type=F32) for rs in groups]
        for hd in range(H):
            rs = slice(hd * T, (hd + 1) * T)
            lo = (hd % hg) * T
            mx = mx_scr[rs, :]
            for u in range(len(madds)):
                x = lgs[hd // hg][lo:lo + T, u * T:(u + 1) * T] + madds[u]
                if bias_fns[u] is not None:
                    x = x + bias_fns[u](hd)
                store(rs, u, x)
                mx = jnp.maximum(mx, x)
            mx_scr[rs, :] = mx

    def sum_pass(c_aug, n_sub, load):
        for grp in groups:
            m_rep = mrep_scr[grp, :]
            ph = [jnp.exp2(load(grp, u) - m_rep).astype(BF16) for u in range(n_sub)]
            p = ph[0] if n_sub == 1 else jnp.concatenate(ph, axis=1)
            acc_scr[grp, :] += jnp.dot(p, c_aug, preferred_element_type=F32)

    def key_rows(step):
        return pl.ds(pl.multiple_of(step * KEY_SUB * T, KEY_SUB * T), KEY_SUB * T)

    upper = (row < col).astype(BF16)

    def mask_step(step, seen, near):
        madds, bias_fns = [], []
        for u in range(KEY_SUB):
            j = KEY_SUB * step + u
            kk = keys_scr[j]
            eq = kk == thr
            eqf = jnp.where(eq, 1.0, 0.0)
            before = jnp.dot(eqf.astype(BF16), upper, preferred_element_type=F32) + seen
            sel = (kk > thr) | (eq & (before < need))
            if near:
                sel = sel & ((j * T + col) <= (i * T + row))
                dsel = jnp.clip(i - j, 0, 2)
                bias_fns.append(lambda hd, dsel=dsel: near_ref[dsel, hd])
            else:
                bias_fns.append(None)
            seen = seen + jnp.sum(eqf, axis=1, keepdims=True)
            madds.append(jnp.where(sel, 0.0, NEG_BIG))

        def store(rs, u, x):
            lg_scr[step, rs, u * T:(u + 1) * T] = x

        max_pass(caug_scr[key_rows(step), :], madds, bias_fns, store)
        return seen

    def weights_step(step, buf):
        for grp in groups:
            m_rep = mrep_scr[grp, :]
            for u in range(KEY_SUB):
                buf[grp, u * T:(u + 1) * T] = jnp.exp2(
                    lg_scr[step, grp, u * T:(u + 1) * T] - m_rep).astype(BF16)

    def accumulate_step(step, buf):
        c_aug = caug_scr[key_rows(step), :]
        for grp in groups:
            acc_scr[grp, :] += jnp.dot(buf[grp, :], c_aug, preferred_element_type=F32)

    def store_meta(rs, u, x):
        lgm_scr[rs, :] = x

    n_far = jnp.maximum(i - 1, 0) // KEY_SUB
    n_steps = (i + KEY_SUB) // KEY_SUB
    max_pass(cmaug_scr[...], [jnp.where(col < N_META, 0.0, NEG_BIG)],
             [lambda hd: metab_ref[meta_sel, hd]], store_meta)
    seen = lax.fori_loop(0, n_far, lambda s, c: mask_step(s, c, False), jnp.zeros((T, 1), F32))
    lax.fori_loop(n_far, n_steps, lambda s, c: mask_step(s, c, True), seen)
    mrep_scr[...] = jnp.broadcast_to(jnp.max(mx_scr[...], axis=1, keepdims=True), mrep_scr.shape)
    sum_pass(cmaug_scr[...], 1, lambda grp, u: lgm_scr[grp, :])
    def sum_step(step, carry):
        weights_step(step, pa_scr)
        accumulate_step(step, pa_scr)
        return carry

    lax.fori_loop(0, n_steps, sum_step, 0)

    for hd in range(H):
        rs = slice(hd * T, (hd + 1) * T)
        o = acc_scr[rs, 0:KV_RANK] / acc_scr[rs, KV_RANK:KV_RANK + 1]
        out_ref[:, hd * A_HEAD_DIM:(hd + 1) * A_HEAD_DIM] = jnp.dot(
            o.astype(BF16), wuv_ref[hd], preferred_element_type=F32).astype(out_ref.dtype)


def _dsa(qa, qi, sm, c, wuk_t, wuv, tables, l, batch, seq):
    n_real = batch * seq
    nq = seq // LANES
    n_tiles = ((nq + SCORE_CHUNK - 1) // SCORE_CHUNK) * SCORE_CHUNK
    top_k = min(TOPK_MAX, seq // 4)
    meta_blk = n_real // N_META
    near, metab, mmb = tables
    full = lambda a: pl.BlockSpec(a.shape, lambda b, i: (0,) * a.ndim)
    assert seq % (SCORE_CHUNK * LANES) == 0 and nq % KEY_SUB == 0 and SCORE_CHUNK % KEY_SUB == 0
    out_real, out_meta = pl.pallas_call(
        functools.partial(_dsa_kernel, top_k),
        grid=(batch, nq),
        in_specs=[pl.BlockSpec((LANES, A_WIDTH), lambda b, i: (b * nq + i, 0)),
                  pl.BlockSpec((LANES, IDX_HEADS * IDX_DIM), lambda b, i: (b * nq + i, 0)),
                  pl.BlockSpec((LANES, LANES), lambda b, i: (b * nq + i, 0)),
                  pl.BlockSpec((seq, KV_RANK), lambda b, i: (b, 0)),
                  pl.BlockSpec((seq, LANES), lambda b, i: (b, 0)),
                  pl.BlockSpec((N_META, KV_RANK), lambda b, i: (meta_blk + b, 0)),
                  pl.BlockSpec((N_META, A_WIDTH), lambda b, i: (meta_blk + b, 0)),
                  _layer_block(wuk_t, l), _layer_block(wuv, l), full(near), full(metab), full(mmb)],
        out_specs=[pl.BlockSpec((LANES, A_WIDTH), lambda b, i: (b * nq + i, 0)),
                   pl.BlockSpec((N_META, A_WIDTH), lambda b, i: (b, 0))],
        out_shape=[jax.ShapeDtypeStruct((n_real, A_WIDTH), BF16),
                   jax.ShapeDtypeStruct((batch * N_META, A_WIDTH), BF16)],
        scratch_shapes=[pltpu.VMEM((n_tiles, LANES, LANES), jnp.int32),
                        pltpu.VMEM((n_tiles, LANES, LANES), jnp.int16),
                        pltpu.VMEM((n_tiles, LANES, LANES), jnp.int16),
                        pltpu.VMEM((nq // KEY_SUB, A_HEADS * LANES, KEY_SUB * LANES), F32),
                        pltpu.VMEM((A_HEADS * LANES, LANES), F32),
                        pltpu.VMEM((IDX_HEADS * LANES, SCORE_CHUNK * LANES), F32),
                        pltpu.VMEM((IDX_HEADS * LANES, SCORE_CHUNK * LANES), F32),
                        pltpu.VMEM((A_HEADS * LANES, KEY_SUB * LANES), BF16),
                        pltpu.VMEM((A_HEADS * LANES, KEY_SUB * LANES), BF16),
                        pltpu.VMEM((seq, KV_RANK + LANES), BF16),
                        pltpu.VMEM((seq, IDX_DIM), BF16),
                        pltpu.VMEM((LANES, KV_RANK + LANES), BF16),
                        pltpu.VMEM((A_HEADS * LANES, KV_RANK), BF16),
                        pltpu.VMEM((A_HEADS * LANES, LANES), F32),
                        pltpu.VMEM((A_HEADS * LANES, LANES), F32),
                        pltpu.VMEM((A_HEADS * LANES, KV_RANK + LANES), F32)],
        compiler_params=_params(("arbitrary", "arbitrary")),
        name="dsa",
    )(qa, qi, sm, c, sm, c, qa, wuk_t, wuv, near, metab, mmb)
    return out_real, out_meta


def _merge_kernel(alpha, h_ref, hm_ref, ha_ref, g_ref, wbm_ref, wba_ref, wo_ref, lg_ref, lb_ref,
                  wr_ref, br_ref, h1_ref, comb_ref, bgt_ref, cnt_ref):
    d = h_ref.shape[1]
    gm = _sigmoid(g_ref[:, 0:d].astype(F32))
    ga = _sigmoid(g_ref[:, d:2 * d].astype(F32))
    y = gm * jnp.dot(hm_ref[...], wbm_ref[...], preferred_element_type=F32) + \
        ga * jnp.dot(ha_ref[...], wba_ref[...], preferred_element_type=F32)
    z = alpha * h_ref[...] + jnp.dot(y.astype(BF16), wo_ref[...], preferred_element_type=F32)
    h1 = _layer_norm_rows(z, lg_ref[...], lb_ref[...], 1e-5)
    h1_ref[...] = h1

    tm = h1.shape[0]
    logits_t = lax.dot_general(wr_ref[...], h1.astype(BF16), (((1,), (1,)), ((), ())),
                               preferred_element_type=F32)
    scores = _sigmoid(logits_t[0:N_EXPERTS, :])
    sel = scores + br_ref[0:N_EXPERTS, :]
    best = None
    for gidx in range(N_GROUPS):
        r0, r1, r2, r3 = (sel[gidx * GROUP_SIZE + u:gidx * GROUP_SIZE + u + 1, :] for u in range(4))
        a, b = jnp.maximum(r0, r1), jnp.minimum(r0, r1)
        c, dd = jnp.maximum(r2, r3), jnp.minimum(r2, r3)
        gs = jnp.maximum(a, c) + jnp.maximum(jnp.minimum(a, c), jnp.maximum(b, dd))
        if best is None:
            best, bg = gs, jnp.zeros((1, tm), jnp.int32)
        else:
            upd = gs > best
            bg = jnp.where(upd, gidx, bg)
            best = jnp.where(upd, gs, best)
    eidx = lax.broadcasted_iota(jnp.int32, (N_EXPERTS, tm), 0)
    masked = jnp.where((eidx // GROUP_SIZE) == bg, sel, -jnp.inf)
    v1 = jnp.max(masked, axis=0, keepdims=True)
    i1 = jnp.min(jnp.where(masked == v1, eidx, N_EXPERTS), axis=0, keepdims=True)
    masked2 = jnp.where(eidx == i1, -jnp.inf, masked)
    v2 = jnp.max(masked2, axis=0, keepdims=True)
    i2 = jnp.min(jnp.where(masked2 == v2, eidx, N_EXPERTS), axis=0, keepdims=True)
    s1 = jnp.sum(jnp.where(eidx == i1, scores, 0.0), axis=0, keepdims=True)
    s2 = jnp.sum(jnp.where(eidx == i2, scores, 0.0), axis=0, keepdims=True)
    tot = s1 + s2
    comb_t = jnp.where(eidx == i1, s1 / tot, 0.0) + jnp.where(eidx == i2, s2 / tot, 0.0)
    comb_pad = jnp.concatenate([comb_t, jnp.zeros((LANES - N_EXPERTS, tm), F32)], axis=0)
    comb_ref[...] = comb_pad.T
    bgt_ref[...] = jnp.broadcast_to(bg, (8, tm))
    gidx8 = lax.broadcasted_iota(jnp.int32, (8, tm), 0)
    counts = jnp.sum(jnp.where(gidx8 == bg, 1.0, 0.0), axis=1, keepdims=True)
    cnt_ref[0] = jnp.broadcast_to(counts, (8, LANES)).astype(jnp.int32)


def _merge(h, hm, ha, g, w_bm, w_ba, w_o, ln_g, ln_b, wr_t, br, l, alpha):
    n, d = h.shape
    tm = MOE_ROW_TILE
    row = lambda width: pl.BlockSpec((tm, width), lambda r: (r, 0))
    full = lambda a: pl.BlockSpec(a.shape, lambda r: (0,) * a.ndim)
    args = (h, hm, ha, g, w_bm, w_ba, w_o, ln_g, ln_b, wr_t, br)
    return pl.pallas_call(
        functools.partial(_merge_kernel, alpha),
        grid=(n // tm,),
        in_specs=[row(d), row(M_WIDTH), row(A_WIDTH), row(2 * d)] +
                 [_layer_block(a, l) for a in args[4:9]] + [full(wr_t), full(br)],
        out_specs=[row(d), row(LANES),
                   pl.BlockSpec((8, tm), lambda r: (0, r)),
                   pl.BlockSpec((1, 8, LANES), lambda r: (r, 0, 0))],
        out_shape=[jax.ShapeDtypeStruct((n, d), F32), jax.ShapeDtypeStruct((n, LANES), F32),
                   jax.ShapeDtypeStruct((8, n), jnp.int32),
                   jax.ShapeDtypeStruct((n // tm, 8, LANES), jnp.int32)],
        compiler_params=_params(("parallel",)),
        name="merge",
    )(*args)


def _moe_kernel(alpha, cap, cnt_ref, h_ref, comb_ref, bgt_ref, wg_ref, wu_ref, wd_ref, lg_ref, lb_ref,
                out_ref, xb_scr, cs_scr, yt_scr, tri_scr):
    r = pl.program_id(0)
    g = pl.program_id(1)
    rows = h_ref.shape[0]

    @pl.when((r == 0) & (g == 0))
    def _():
        t0 = lax.broadcasted_iota(jnp.int32, (rows, rows), 0)
        t1 = lax.broadcasted_iota(jnp.int32, (rows, rows), 1)
        tri_scr[...] = (t0 < t1).astype(BF16)

    @pl.when(g == 0)
    def _():
        xb_scr[...] = h_ref[...].astype(BF16)
        yt_scr[...] = jnp.zeros_like(yt_scr)
        c = comb_ref[...]
        for part in range(2):
            cb = c.astype(BF16)
            cs_scr[part] = cb
            c = c - cb.astype(F32)

    member = bgt_ref[0:1, :] == g
    mem8 = jnp.broadcast_to(jnp.where(member, 1.0, 0.0), (8, rows)).astype(BF16)
    rank = jnp.dot(mem8, tri_scr[...], preferred_element_type=F32)[0:1, :].astype(jnp.int32)
    n_blocks = (cnt_ref[r * N_GROUPS + g] + cap - 1) // cap
    lane = lax.broadcasted_iota(jnp.int32, (cap, LANES), 1)
    tn = (((0,), (0,)), ((), ()))

    def block(b, carry):
        slot = lax.broadcasted_iota(jnp.int32, (cap, rows), 0) + b * cap
        onehot = jnp.where(member & (rank == slot), 1.0, 0.0).astype(BF16)
        xg = jnp.dot(onehot, xb_scr[...], preferred_element_type=F32).astype(BF16)
        cw = jnp.dot(onehot, cs_scr[0], preferred_element_type=F32)
        cw = cw + jnp.dot(onehot, cs_scr[1], preferred_element_type=F32)
        y = jnp.zeros((cap, out_ref.shape[1]), F32)
        for e in range(GROUP_SIZE):
            gate = jnp.dot(xg, wg_ref[e], preferred_element_type=F32)
            up = jnp.dot(xg, wu_ref[e], preferred_element_type=F32)
            he = gate * _sigmoid(gate) * up
            o = jnp.dot(he.astype(BF16), wd_ref[e], preferred_element_type=F32)
            ce = jnp.sum(jnp.where(lane == g * GROUP_SIZE + e, cw, 0.0), axis=1, keepdims=True)
            y = y + ce * o
        yt_scr[...] += lax.dot_general(onehot, y.astype(BF16), tn, preferred_element_type=F32)
        return carry

    lax.fori_loop(0, n_blocks, block, 0)

    @pl.when(g == pl.num_programs(1) - 1)
    def _():
        z = alpha * h_ref[...] + yt_scr[...]
        out_ref[...] = _layer_norm_rows(z, lg_ref[...], lb_ref[...], 1e-5)


def _moe(h, comb, bgt, counts, w_gate, w_up, w_down, ln_g, ln_b, l, alpha):
    n, d = h.shape
    de = w_gate.shape[-1]
    tm = MOE_ROW_TILE
    cnt = counts[:, 0:N_GROUPS, 0].reshape(-1)
    grid_spec = pltpu.PrefetchScalarGridSpec(
        num_scalar_prefetch=1,
        grid=(n // tm, N_GROUPS),
        in_specs=[pl.BlockSpec((tm, d), lambda r, g, c: (r, 0)),
                  pl.BlockSpec((tm, LANES), lambda r, g, c: (r, 0)),
                  pl.BlockSpec((8, tm), lambda r, g, c: (0, r)),
                  pl.BlockSpec((None, GROUP_SIZE, d, de), lambda r, g, c: (l, g, 0, 0)),
                  pl.BlockSpec((None, GROUP_SIZE, d, de), lambda r, g, c: (l, g, 0, 0)),
                  pl.BlockSpec((None, GROUP_SIZE, de, d), lambda r, g, c: (l, g, 0, 0)),
                  _layer_block(ln_g, l), _layer_block(ln_b, l)],
        out_specs=pl.BlockSpec((tm, d), lambda r, g, c: (r, 0)),
        scratch_shapes=[pltpu.VMEM((tm, d), BF16), pltpu.VMEM((2, tm, LANES), BF16),
                        pltpu.VMEM((tm, d), F32), pltpu.VMEM((tm, tm), BF16)])
    return pl.pallas_call(
        functools.partial(_moe_kernel, alpha, MOE_CAP),
        grid_spec=grid_spec,
        out_shape=jax.ShapeDtypeStruct((n, d), F32),
        compiler_params=_params(("arbitrary", "arbitrary")),
        name="moe",
    )(cnt, h, comb, bgt, w_gate, w_up, w_down, ln_g, ln_b)


def _with_meta(real, meta, n_pad):
    pad = n_pad - real.shape[0] - meta.shape[0]
    return jnp.concatenate([real, meta, jnp.zeros((pad, real.shape[1]), real.dtype)], axis=0)


def kernel(x, meta_tokens, ln_in_g, ln_in_b, w_in, conv_w, b_if, mnorm_g, kv_norm_g, w_uk, w_uv,
           w_branch_m, w_branch_a, w_out, ln1_g, ln1_b, w_router, b_router, w_gate, w_up, w_down,
           ln2_g, ln2_b, rel_bias):
    batch, seq, d = x.shape
    depth = w_in.shape[0]
    alpha = (2 * depth) ** 0.25
    n_real = batch * seq
    n_meta = batch * N_META
    tile = math.lcm(ROW_TILE, MOE_ROW_TILE)
    n_pad = -(-(n_real + n_meta) // tile) * tile
    assert seq % LANES == 0 and n_real % N_META == 0

    h = _with_meta(x.reshape(n_real, d), jnp.tile(meta_tokens.astype(x.dtype), (batch, 1)), n_pad)
    h = _input_ln(h, ln_in_g, ln_in_b)
    tables = _bias_tables(rel_bias)
    row3 = lambda a: a.astype(F32)[:, None, :]
    w_packed, brow, kv_g = _pack_w_in(w_in), _pack_b_if(b_if), row3(kv_norm_g)
    conv_f, mnorm = conv_w.astype(F32), row3(mnorm_g)
    wuk_t = jnp.swapaxes(w_uk, 2, 3).astype(BF16)
    wuv = w_uv.astype(BF16)
    w_bm, w_ba, w_o = w_branch_m.astype(BF16), w_branch_a.astype(BF16), w_out.astype(BF16)
    wr_t = jnp.zeros((LANES, d), F32).at[0:N_EXPERTS].set(w_router.T).astype(BF16)
    br = jnp.zeros((LANES, 1), F32).at[0:N_EXPERTS, 0].set(b_router)
    wg, wu, wd = w_gate.astype(BF16), w_up.astype(BF16), w_down.astype(BF16)
    g1, b1, g2, b2 = row3(ln1_g), row3(ln1_b), row3(ln2_g), row3(ln2_b)
    for l in range(depth):
        pm, qa, qi, c, sm, g = _project(h, w_packed, kv_g, brow, l)
        hm_real, hm_meta = _mlstm(pm, sm, conv_f, mnorm, l, batch, seq)
        ha_real, ha_meta = _dsa(qa, qi, sm, c, wuk_t, wuv, tables, l, batch, seq)
        hm = _with_meta(hm_real, hm_meta, n_pad)
        ha = _with_meta(ha_real, ha_meta, n_pad)
        h1, comb, bgt, counts = _merge(h, hm, ha, g, w_bm, w_ba, w_o, g1, b1, wr_t, br, l, alpha)
        h = _moe(h1, comb, bgt, counts, wg, wu, wd, g2, b2, l, alpha)
    return h[:n_real].reshape(batch, seq, d)
```

```python
import functools
import math

import numpy as np
import jax
import jax.numpy as jnp
from jax import lax
from jax.experimental import pallas as pl
from jax.experimental.pallas import tpu as pltpu

F32 = jnp.float32
BF16 = jnp.bfloat16

N_META = 16
M_HEADS = 4
M_HEAD_DIM = 128
M_WIDTH = M_HEADS * M_HEAD_DIM
CONV_WIDTH = 4
A_HEADS = 8
A_HEAD_DIM = 64
A_WIDTH = A_HEADS * A_HEAD_DIM
KV_RANK = 128
IDX_HEADS = 4
IDX_DIM = 64
IDX_SCALE = (IDX_HEADS * IDX_DIM) ** -0.5
TOPK_MAX = 256
REL_BUCKETS = 32
REL_MAX_EXACT = 16
REL_MAX_DIST = 128
N_EXPERTS = 16
N_GROUPS = 4
GROUP_SIZE = N_EXPERTS // N_GROUPS

LANES = 128
ROW_TILE = 256
MOE_ROW_TILE = 768
MOE_CAP = 256
M_CHUNK = 256
SCORE_CHUNK = 4
KEY_SUB = 4
ATT_GROUPS = 2
VMEM_LIMIT = 56 * 1024 * 1024
NEG_BIG = -1e30
INT_MIN = -2 ** 31
HALF_BIAS = 2 ** 15
LOG2E = math.log2(math.e)

PK_PM = 0
PK_QA = 2048
PK_QI = 2560
PK_CKV = 2816
PK_SM = 2944
PK_G = 3072
PK_TOTAL = 5120
SM_KIDX = 0
SM_WIDX = 64
SM_IPRE = 68
SM_FPRE = 72


def _params(sem):
    return pltpu.CompilerParams(dimension_semantics=sem, vmem_limit_bytes=VMEM_LIMIT)


def _sigmoid(x):
    return 1.0 / (1.0 + jnp.exp(-x))


def _layer_norm_rows(x, g, b, eps):
    mu = jnp.mean(x, axis=-1, keepdims=True)
    xc = x - mu
    var = jnp.mean(xc * xc, axis=-1, keepdims=True)
    return xc * lax.rsqrt(var + eps) * g + b


def _ln_kernel(x_ref, g_ref, b_ref, o_ref):
    o_ref[...] = _layer_norm_rows(x_ref[...], g_ref[...], b_ref[...], 1e-5)


def _input_ln(x, g, b):
    n, d = x.shape
    return pl.pallas_call(
        _ln_kernel,
        grid=(n // ROW_TILE,),
        in_specs=[pl.BlockSpec((ROW_TILE, d), lambda r: (r, 0)),
                  pl.BlockSpec((1, d), lambda r: (0, 0)),
                  pl.BlockSpec((1, d), lambda r: (0, 0))],
        out_specs=pl.BlockSpec((ROW_TILE, d), lambda r: (r, 0)),
        out_shape=jax.ShapeDtypeStruct((n, d), F32),
        compiler_params=_params(("parallel",)),
        name="input_ln",
    )(x, g.reshape(1, d), b.reshape(1, d))


def _proj_kernel(h_ref, w_ref, kvg_ref, brow_ref, pm_ref, qa_ref, qi_ref, c_ref, sm_ref, g_ref):
    x = h_ref[...].astype(BF16)

    def mm(lo, width):
        return jnp.dot(x, w_ref[:, lo:lo + width], preferred_element_type=F32)

    pm_ref[...] = mm(PK_PM, 2048).astype(BF16)
    qa_ref[...] = mm(PK_QA, 512).astype(BF16)
    qi_ref[...] = mm(PK_QI, 256).astype(BF16)
    ckv = mm(PK_CKV, 128)
    c_ref[...] = ckv * lax.rsqrt(jnp.mean(ckv * ckv, axis=-1, keepdims=True) + 1e-6) * kvg_ref[...]
    sm_ref[...] = mm(PK_SM, 128) + brow_ref[...]
    g_ref[...] = mm(PK_G, 2048).astype(BF16)


def _pack_w_in(w):
    cols = [w[..., 0:2048], w[..., 2056:2568], w[..., 2696:2952], w[..., 2568:2696],
            w[..., 2952:3016], w[..., 3016:3020], w[..., 2048:2056],
            jnp.zeros(w.shape[:-1] + (LANES - 76,), w.dtype), w[..., 3020:5068]]
    return jnp.concatenate(cols, axis=-1).astype(BF16)


def _pack_b_if(b_if):
    depth = b_if.shape[0]
    return jnp.concatenate([jnp.zeros((depth, SM_IPRE), F32), b_if.astype(F32),
                            jnp.zeros((depth, LANES - SM_IPRE - 2 * M_HEADS), F32)], axis=1)[:, None, :]


def _layer_block(arr, l):
    zeros = (0,) * (arr.ndim - 1)
    return pl.BlockSpec((None,) + arr.shape[1:], lambda *idx: (l,) + zeros)


def _project(h, w_packed, kv_g, brow, l):
    n, d = h.shape
    row = lambda width: pl.BlockSpec((ROW_TILE, width), lambda r: (r, 0))
    shp = lambda width, dt: jax.ShapeDtypeStruct((n, width), dt)
    return pl.pallas_call(
        _proj_kernel,
        grid=(n // ROW_TILE,),
        in_specs=[row(d), _layer_block(w_packed, l), _layer_block(kv_g, l), _layer_block(brow, l)],
        out_specs=[row(2048), row(512), row(256), row(128), row(128), row(2048)],
        out_shape=[shp(2048, BF16), shp(512, BF16), shp(256, BF16), shp(128, F32), shp(128, F32),
                   shp(2048, BF16)],
        compiler_params=_params(("parallel",)),
        name="in_proj",
    )(h, w_packed, kv_g, brow)


def _log_sigmoid(f):
    return jnp.minimum(f, 0.0) - jnp.log1p(jnp.exp(-jnp.abs(f)))


def _row_from_col(col):
    L = col.shape[0]
    eye = lax.broadcasted_iota(jnp.int32, (L, L), 0) == lax.broadcasted_iota(jnp.int32, (L, L), 1)
    return jnp.sum(jnp.where(eye, jnp.broadcast_to(col, (L, L)), 0.0), axis=0, keepdims=True)


def _mlstm_chunk_math(q, k, v, li, li_row, lf_row, ct, n, m):
    L = q.shape[0]
    r = lax.broadcasted_iota(jnp.int32, (L, L), 0)
    c = lax.broadcasted_iota(jnp.int32, (L, L), 1)
    tril = c <= r
    b_col = jnp.sum(jnp.where(tril, jnp.broadcast_to(lf_row, (L, L)), 0.0), axis=1, keepdims=True)
    b_row = _row_from_col(b_col)
    d = jnp.where(tril, b_col - b_row + li_row, -jnp.inf)
    inter = b_col + m
    m_t = jnp.maximum(inter, jnp.max(d, axis=1, keepdims=True))
    a = jnp.exp(inter - m_t)
    qb, kb, vb = q.astype(BF16), k.astype(BF16), v.astype(BF16)
    qk = lax.dot_general(qb, kb, (((1,), (1,)), ((), ())), preferred_element_type=F32)
    w = jnp.exp(d - m_t) * qk
    num = a * jnp.dot(qb, ct.astype(BF16), preferred_element_type=F32) + \
        jnp.dot(w.astype(BF16), vb, preferred_element_type=F32)
    den = a * jnp.sum(q * n, axis=1, keepdims=True) + jnp.sum(w, axis=1, keepdims=True)
    h = num / jnp.maximum(jnp.abs(den), jnp.exp(-m_t))
    b_last = b_col[L - 1:L, :]
    g = b_last - b_col + li
    m_new = jnp.maximum(b_last + m, jnp.max(g, axis=0, keepdims=True))
    decay = jnp.exp(b_last + m - m_new)
    kw = k * jnp.exp(g - m_new)
    ct_new = decay * ct + lax.dot_general(kw.astype(BF16), vb, (((0,), (0,)), ((), ())),
                                          preferred_element_type=F32)
    n_new = decay * n + jnp.sum(kw, axis=0, keepdims=True)
    return h, ct_new, n_new, m_new


def _mlstm_kernel(pm_ref, sm_ref, pmm_ref, smm_ref, cw_ref, mg_ref, out_ref, outm_ref,
                  ct_scr, n_scr, m_scr, x_scr):
    ci = pl.program_id(1)
    tail = 8

    def run_chunk(p_ref, s_ref, o_ref):
        L = p_ref.shape[0]
        x_scr[tail:tail + L, :] = p_ref[:, 0:2 * M_WIDTH].astype(F32)
        conv = cw_ref[0:1, :] * x_scr[tail - 3:tail - 3 + L, :]
        for j in range(1, CONV_WIDTH):
            conv = conv + cw_ref[j:j + 1, :] * x_scr[tail - 3 + j:tail - 3 + j + L, :]
        x_scr[0:tail, :] = x_scr[L:L + tail, :]
        qk = conv * _sigmoid(conv)
        if L % LANES == 0:
            gates_t = s_ref[...].T
            li_rows = gates_t[SM_IPRE:SM_IPRE + M_HEADS, :]
            lf_rows = _log_sigmoid(gates_t[SM_FPRE:SM_FPRE + M_HEADS, :])
        for hd in range(M_HEADS):
            lo = hd * M_HEAD_DIM
            q = qk[:, lo:lo + M_HEAD_DIM] * (M_HEAD_DIM ** -0.5)
            k = qk[:, M_WIDTH + lo:M_WIDTH + lo + M_HEAD_DIM]
            v = p_ref[:, 2 * M_WIDTH + lo:2 * M_WIDTH + lo + M_HEAD_DIM].astype(F32)
            li = s_ref[:, SM_IPRE + hd:SM_IPRE + hd + 1]
            if L % LANES == 0:
                li_row, lf_row = li_rows[hd:hd + 1, :], lf_rows[hd:hd + 1, :]
            else:
                li_row = _row_from_col(li)
                lf_row = _row_from_col(_log_sigmoid(s_ref[:, SM_FPRE + hd:SM_FPRE + hd + 1]))
            h, ct_new, n_new, m_new = _mlstm_chunk_math(
                q, k, v, li, li_row, lf_row, ct_scr[hd], n_scr[hd:hd + 1, :], m_scr[hd:hd + 1, 0:1])
            ct_scr[hd] = ct_new
            n_scr[hd:hd + 1, :] = n_new
            m_scr[hd:hd + 1, :] = jnp.broadcast_to(m_new, (1, LANES))
            mu = jnp.mean(h, axis=-1, keepdims=True)
            hc = h - mu
            var = jnp.mean(hc * hc, axis=-1, keepdims=True)
            o_gate = _sigmoid(p_ref[:, 3 * M_WIDTH + lo:3 * M_WIDTH + lo + M_HEAD_DIM].astype(F32))
            o_ref[:, lo:lo + M_HEAD_DIM] = (hc * lax.rsqrt(var + 1e-5) * mg_ref[:, lo:lo + M_HEAD_DIM]
                                            * o_gate).astype(o_ref.dtype)

    @pl.when(ci == 0)
    def _():
        ct_scr[...] = jnp.zeros_like(ct_scr)
        n_scr[...] = jnp.zeros_like(n_scr)
        m_scr[...] = jnp.zeros_like(m_scr)
        x_scr[...] = jnp.zeros_like(x_scr)
        run_chunk(pmm_ref, smm_ref, outm_ref)

    run_chunk(pm_ref, sm_ref, out_ref)


def _mlstm(pm, sm, conv_w, mnorm_g, l, batch, seq):
    n_real = batch * seq
    chunk = min(M_CHUNK, seq)
    nc = seq // chunk
    meta_blk = n_real // N_META
    out_real, out_meta = pl.pallas_call(
        _mlstm_kernel,
        grid=(batch, nc),
        in_specs=[pl.BlockSpec((chunk, 4 * M_WIDTH), lambda b, c: (b * nc + c, 0)),
                  pl.BlockSpec((chunk, LANES), lambda b, c: (b * nc + c, 0)),
                  pl.BlockSpec((N_META, 4 * M_WIDTH), lambda b, c: (meta_blk + b, 0)),
                  pl.BlockSpec((N_META, LANES), lambda b, c: (meta_blk + b, 0)),
                  _layer_block(conv_w, l), _layer_block(mnorm_g, l)],
        out_specs=[pl.BlockSpec((chunk, M_WIDTH), lambda b, c: (b * nc + c, 0)),
                   pl.BlockSpec((N_META, M_WIDTH), lambda b, c: (b, 0))],
        out_shape=[jax.ShapeDtypeStruct((n_real, M_WIDTH), BF16),
                   jax.ShapeDtypeStruct((batch * N_META, M_WIDTH), BF16)],
        scratch_shapes=[pltpu.VMEM((M_HEADS, M_HEAD_DIM, M_HEAD_DIM), F32),
                        pltpu.VMEM((8, LANES), F32),
                        pltpu.VMEM((8, LANES), F32),
                        pltpu.VMEM((chunk + 8, 2 * M_WIDTH), F32)],
        compiler_params=_params(("arbitrary", "arbitrary")),
        name="mlstm",
    )(pm, sm, pm, sm, conv_w, mnorm_g)
    return out_real, out_meta


def _rel_bucket_np(dist):
    n = np.maximum(dist, 0)
    nf = np.maximum(n, REL_MAX_EXACT).astype(np.float32)
    large = REL_MAX_EXACT + (np.log(nf / np.float32(REL_MAX_EXACT)) /
                             np.float32(math.log(REL_MAX_DIST / REL_MAX_EXACT))
                             * np.float32(REL_BUCKETS - REL_MAX_EXACT)).astype(np.int32)
    large = np.minimum(large, REL_BUCKETS - 1)
    return np.where(n < REL_MAX_EXACT, n, large).astype(np.int32)


def _bias_tables(rel_bias):
    q = np.arange(LANES)[:, None]
    k = np.arange(LANES)[None, :]
    far = 4 * LANES
    assert (_rel_bucket_np(np.arange(LANES + 1, far)) == REL_BUCKETS - 1).all()
    far_idx = np.full((LANES, LANES), REL_BUCKETS - 1, np.int32)
    near_idx = np.stack([_rel_bucket_np(q - k), _rel_bucket_np(LANES + q - k), far_idx])
    meta_idx = np.stack([_rel_bucket_np(q + N_META - np.minimum(k, N_META - 1)), far_idx])
    mq = np.arange(N_META)[:, None]
    mm_idx = _rel_bucket_np(mq - np.minimum(k, N_META - 1))
    rb = rel_bias.astype(F32)

    def lookup(idx, values):
        onehot = jnp.asarray(idx[..., None] == np.arange(REL_BUCKETS), F32)
        out = jnp.einsum('...b,bh->...h', onehot, values, precision=lax.Precision.HIGHEST)
        return jnp.moveaxis(out, -1, -3)

    rel_values = (rb - rb[REL_BUCKETS - 1][None, :]) * LOG2E
    return lookup(near_idx, rel_values), lookup(meta_idx, rel_values), lookup(mm_idx, rb)


def _dsa_kernel(top_k, qa_ref, qi_ref, sm_ref, cb_ref, smb_ref, cm_ref, qam_ref,
                wuk_ref, wuv_ref, near_ref, metab_ref, mmb_ref,
                out_ref, outm_ref,
                keys_scr, hi_scr, lo_scr, lg_scr, lgm_scr, rawa_scr, rawb_scr, pa_scr, pb_scr,
                caug_scr, kbf_scr, cmaug_scr, qs_scr, mx_scr, mrep_scr, acc_scr):
    i = pl.program_id(1)
    T = LANES
    H = A_HEADS
    col = lax.broadcasted_iota(jnp.int32, (T, T), 1)
    row = lax.broadcasted_iota(jnp.int32, (T, T), 0)
    nt = (((1,), (1,)), ((), ()))

    def q_latent(qa, hd, scale):
        ql = jnp.dot(qa[:, hd * A_HEAD_DIM:(hd + 1) * A_HEAD_DIM], wuk_ref[hd],
                     preferred_element_type=F32)
        return (ql * scale).astype(BF16)

    def ones_column(n):
        return jnp.where(lax.broadcasted_iota(jnp.int32, (n, T), 1) == 0, 1.0, 0.0).astype(BF16)

    @pl.when(i == 0)
    def _():
        caug_scr[:, 0:KV_RANK] = cb_ref[...].astype(BF16)
        caug_scr[:, KV_RANK:KV_RANK + T] = ones_column(caug_scr.shape[0])
        kbf_scr[...] = smb_ref[:, SM_KIDX:SM_KIDX + IDX_DIM].astype(BF16)
        cm_pad = jnp.concatenate([cm_ref[...], jnp.zeros((T - N_META, KV_RANK), F32)], axis=0)
        cmaug_scr[:, 0:KV_RANK] = cm_pad.astype(BF16)
        cmaug_scr[:, KV_RANK:KV_RANK + T] = ones_column(T)
        cmk = cmaug_scr[:, 0:KV_RANK]
        qam = qam_ref[...]
        mrow = lax.broadcasted_iota(jnp.int32, (N_META, T), 0)
        mcol = lax.broadcasted_iota(jnp.int32, (N_META, T), 1)
        for hd in range(H):
            lg = lax.dot_general(q_latent(qam, hd, A_HEAD_DIM ** -0.5), cmk, nt,
                                 preferred_element_type=F32) + mmb_ref[hd]
            lg = jnp.where(mcol <= mrow, lg, NEG_BIG)
            p = jnp.exp(lg - jnp.max(lg, axis=1, keepdims=True))
            p = p / jnp.sum(p, axis=1, keepdims=True)
            o = jnp.dot(p.astype(BF16), cmk, preferred_element_type=F32)
            outm_ref[:, hd * A_HEAD_DIM:(hd + 1) * A_HEAD_DIM] = jnp.dot(
                o.astype(BF16), wuv_ref[hd], preferred_element_type=F32).astype(outm_ref.dtype)

    qa = qa_ref[...]
    for hd in range(H):
        qs_scr[hd * T:(hd + 1) * T, :] = q_latent(qa, hd, A_HEAD_DIM ** -0.5 * LOG2E)
    qi = qi_ref[...]
    wv = sm_ref[:, SM_WIDX:SM_WIDX + IDX_HEADS] * IDX_SCALE
    t_col = i * T + lax.broadcasted_iota(jnp.int32, (T, 1), 0)
    n_chunks = (i + SCORE_CHUNK) // SCORE_CHUNK
    CW = SCORE_CHUNK * T

    def run_pipelined(n, produce, consume, buf_a, buf_b):
        last = n - 1
        produce(0, buf_a)

        def body(t, carry):
            s = 2 * t
            produce(jnp.minimum(s + 1, last), buf_b)
            consume(s, buf_a)
            produce(jnp.minimum(s + 2, last), buf_a)
            consume(s + 1, buf_b)
            return carry

        lax.fori_loop(0, n // 2, body, 0)

        @pl.when(n % 2 == 1)
        def _():
            consume(last, buf_a)

    def score_matmul(cix, buf):
        kc = kbf_scr[pl.ds(pl.multiple_of(cix * CW, CW), CW), :]
        for hh in range(IDX_HEADS):
            buf[hh * T:(hh + 1) * T, :] = lax.dot_general(
                qi[:, hh * IDX_DIM:(hh + 1) * IDX_DIM], kc, nt, preferred_element_type=F32)

    def score_keys(cix, buf):
        acc = jnp.zeros((T, CW), F32)
        for hh in range(IDX_HEADS):
            acc = acc + wv[:, hh:hh + 1] * jnp.maximum(buf[hh * T:(hh + 1) * T, :], 0.0)
        acc = jnp.where(acc == 0.0, 0.0, acc)
        bits = lax.bitcast_convert_type(acc, jnp.int32)
        key = jnp.where(bits < 0, bits ^ jnp.int32(0x7FFFFFFF), bits)
        s_idx = cix * CW + lax.broadcasted_iota(jnp.int32, (T, CW), 1)
        key = jnp.where(s_idx <= t_col, key, jnp.int32(INT_MIN))
        for u in range(SCORE_CHUNK):
            tile = key[:, u * T:(u + 1) * T]
            keys_scr[cix * SCORE_CHUNK + u] = tile
            tile_t = tile.T
            hi_scr[cix * SCORE_CHUNK + u] = lax.shift_right_arithmetic(tile_t, 16).astype(jnp.int16)
            lo_scr[cix * SCORE_CHUNK + u] = ((tile_t & 0xFFFF) - HALF_BIAS).astype(jnp.int16)

    run_pipelined(n_chunks, score_matmul, score_keys, rawa_scr, rawb_scr)

    def rep16(row_i32):
        return jnp.broadcast_to(row_i32, (T, T)).astype(jnp.int16)

    def count16(src_scr, pred_fn):
        def body(cix, cnt):
            for u in range(SCORE_CHUNK):
                hit = pred_fn(src_scr[cix * SCORE_CHUNK + u])
                cnt = cnt + jnp.where(hit, jnp.int16(1), jnp.int16(0))
            return cnt
        cnt = lax.fori_loop(0, n_chunks, body, jnp.zeros((T, T), jnp.int16))
        return jnp.sum(cnt.astype(F32), axis=0, keepdims=True)

    def search16(src_scr, k_row):
        def bit_body(bi, ans):
            cand_u = ans | lax.shift_left(jnp.int32(1), jnp.int32(15) - bi)
            cand = rep16(cand_u - HALF_BIAS)
            total = count16(src_scr, lambda x: x >= cand)
            return jnp.where(total >= k_row, cand_u, ans)
        return lax.fori_loop(0, 16, bit_body, jnp.zeros((1, T), jnp.int32))

    k_row = jnp.full((1, T), float(top_k), F32)
    hi_s = search16(hi_scr, k_row) - HALF_BIAS
    hi_rep = rep16(hi_s)
    k_low = k_row - count16(hi_scr, lambda x: x > hi_rep)

    def band_body(j, carry):
        lo_scr[j] = jnp.where(hi_scr[j] == hi_rep, lo_scr[j], jnp.int16(-HALF_BIAS))
        return carry

    lax.fori_loop(0, n_chunks * SCORE_CHUNK, band_body, 0)
    lo_u = search16(lo_scr, k_low)
    lo_rep = rep16(lo_u - HALF_BIAS)
    need_row = k_low - count16(lo_scr, lambda x: x > lo_rep)
    thr_row = lax.shift_left(hi_s, 16) | lo_u
    thr = jnp.broadcast_to(thr_row, (T, T)).T
    need = jnp.broadcast_to(need_row, (T, T)).T

    mx_scr[...] = jnp.full_like(mx_scr, NEG_BIG)
    acc_scr[...] = jnp.zeros_like(acc_scr)
    hg = H // ATT_GROUPS
    groups = [slice(g * hg * T, (g + 1) * hg * T) for g in range(ATT_GROUPS)]
    meta_sel = jnp.minimum(i, 1)

    def max_pass(c_aug, madds, bias_fns, store):
        ck = c_aug[:, 0:KV_RANK]
        lgs = [lax.dot_general(qs_scr[rs, :], ck, nt, preferred_element_type=F32) for rs in groups]
        for hd in range(H):
            rs = slice(hd * T, (hd + 1) * T)
            lo = (hd % hg) * T
            mx = mx_scr[rs, :]
            for u in range(len(madds)):
                x = lgs[hd // hg][lo:lo + T, u * T:(u + 1) * T] + madds[u]
                if bias_fns[u] is not None:
                    x = x + bias_fns[u](hd)
                store(rs, u, x)
                mx = jnp.maximum(mx, x)
            mx_scr[rs, :] = mx

    def sum_pass(c_aug, n_sub, load):
        for grp in groups:
            m_rep = mrep_scr[grp, :]
            ph = [jnp.exp2(load(grp, u) - m_rep).astype(BF16) for u in range(n_sub)]
            p = ph[0] if n_sub == 1 else jnp.concatenate(ph, axis=1)
            acc_scr[grp, :] += jnp.dot(p, c_aug, preferred_element_type=F32)

    def key_rows(step):
        return pl.ds(pl.multiple_of(step * KEY_SUB * T, KEY_SUB * T), KEY_SUB * T)

    upper = (row < col).astype(BF16)

    def mask_step(step, seen, near):
        madds, bias_fns = [], []
        for u in range(KEY_SUB):
            j = KEY_SUB * step + u
            kk = keys_scr[j]
            eq = kk == thr
            eqf = jnp.where(eq, 1.0, 0.0)
            before = jnp.dot(eqf.astype(BF16), upper, preferred_element_type=F32) + seen
            sel = (kk > thr) | (eq & (before < need))
            if near:
                sel = sel & ((j * T + col) <= (i * T + row))
                dsel = jnp.clip(i - j, 0, 2)
                bias_fns.append(lambda hd, dsel=dsel: near_ref[dsel, hd])
            else:
                bias_fns.append(None)
            seen = seen + jnp.sum(eqf, axis=1, keepdims=True)
            madds.append(jnp.where(sel, 0.0, NEG_BIG))

        def store(rs, u, x):
            lg_scr[step, rs, u * T:(u + 1) * T] = x

        max_pass(caug_scr[key_rows(step), :], madds, bias_fns, store)
        return seen

    def weights_step(step, buf):
        for grp in groups:
            m_rep = mrep_scr[grp, :]
            for u in range(KEY_SUB):
                buf[grp, u * T:(u + 1) * T] = jnp.exp2(
                    lg_scr[step, grp, u * T:(u + 1) * T] - m_rep).astype(BF16)

    def accumulate_step(step, buf):
        c_aug = caug_scr[key_rows(step), :]
        for grp in groups:
            acc_scr[grp, :] += jnp.dot(buf[grp, :], c_aug, preferred_element_type=F32)

    def store_meta(rs, u, x):
        lgm_scr[rs, :] = x

    n_far = jnp.maximum(i - 1, 0) // KEY_SUB
    n_steps = (i + KEY_SUB) // KEY_SUB
    max_pass(cmaug_scr[...], [jnp.where(col < N_META, 0.0, NEG_BIG)],
             [lambda hd: metab_ref[meta_sel, hd]], store_meta)
    seen = lax.fori_loop(0, n_far, lambda s, c: mask_step(s, c, False), jnp.zeros((T, 1), F32))
    lax.fori_loop(n_far, n_steps, lambda s, c: mask_step(s, c, True), seen)
    mrep_scr[...] = jnp.broadcast_to(jnp.max(mx_scr[...], axis=1, keepdims=True), mrep_scr.shape)
    sum_pass(cmaug_scr[...], 1, lambda grp, u: lgm_scr[grp, :])
    def sum_step(step, carry):
        weights_step(step, pa_scr)
        accumulate_step(step, pa_scr)
        return carry

    lax.fori_loop(0, n_steps, sum_step, 0)

    for hd in range(H):
        rs = slice(hd * T, (hd + 1) * T)
        o = acc_scr[rs, 0:KV_RANK] / acc_scr[rs, KV_RANK:KV_RANK + 1]
        out_ref[:, hd * A_HEAD_DIM:(hd + 1) * A_HEAD_DIM] = jnp.dot(
            o.astype(BF16), wuv_ref[hd], preferred_element_type=F32).astype(out_ref.dtype)


def _dsa(qa, qi, sm, c, wuk_t, wuv, tables, l, batch, seq):
    n_real = batch * seq
    nq = seq // LANES
    n_tiles = ((nq + SCORE_CHUNK - 1) // SCORE_CHUNK) * SCORE_CHUNK
    top_k = min(TOPK_MAX, seq // 4)
    meta_blk = n_real // N_META
    near, metab, mmb = tables
    full = lambda a: pl.BlockSpec(a.shape, lambda b, i: (0,) * a.ndim)
    assert seq % (SCORE_CHUNK * LANES) == 0 and nq % KEY_SUB == 0 and SCORE_CHUNK % KEY_SUB == 0
    out_real, out_meta = pl.pallas_call(
        functools.partial(_dsa_kernel, top_k),
        grid=(batch, nq),
        in_specs=[pl.BlockSpec((LANES, A_WIDTH), lambda b, i: (b * nq + i, 0)),
                  pl.BlockSpec((LANES, IDX_HEADS * IDX_DIM), lambda b, i: (b * nq + i, 0)),
                  pl.BlockSpec((LANES, LANES), lambda b, i: (b * nq + i, 0)),
                  pl.BlockSpec((seq, KV_RANK), lambda b, i: (b, 0)),
                  pl.BlockSpec((seq, LANES), lambda b, i: (b, 0)),
                  pl.BlockSpec((N_META, KV_RANK), lambda b, i: (meta_blk + b, 0)),
                  pl.BlockSpec((N_META, A_WIDTH), lambda b, i: (meta_blk + b, 0)),
                  _layer_block(wuk_t, l), _layer_block(wuv, l), full(near), full(metab), full(mmb)],
        out_specs=[pl.BlockSpec((LANES, A_WIDTH), lambda b, i: (b * nq + i, 0)),
                   pl.BlockSpec((N_META, A_WIDTH), lambda b, i: (b, 0))],
        out_shape=[jax.ShapeDtypeStruct((n_real, A_WIDTH), BF16),
                   jax.ShapeDtypeStruct((batch * N_META, A_WIDTH), BF16)],
        scratch_shapes=[pltpu.VMEM((n_tiles, LANES, LANES), jnp.int32),
                        pltpu.VMEM((n_tiles, LANES, LANES), jnp.int16),
                        pltpu.VMEM((n_tiles, LANES, LANES), jnp.int16),
                        pltpu.VMEM((nq // KEY_SUB, A_HEADS * LANES, KEY_SUB * LANES), F32),
                        pltpu.VMEM((A_HEADS * LANES, LANES), F32),
                        pltpu.VMEM((IDX_HEADS * LANES, SCORE_CHUNK * LANES), F32),
                        pltpu.VMEM((IDX_HEADS * LANES, SCORE_CHUNK * LANES), F32),
                        pltpu.VMEM((A_HEADS * LANES, KEY_SUB * LANES), BF16),
                        pltpu.VMEM((A_HEADS * LANES, KEY_SUB * LANES), BF16),
                        pltpu.VMEM((seq, KV_RANK + LANES), BF16),
                        pltpu.VMEM((seq, IDX_DIM), BF16),
                        pltpu.VMEM((LANES, KV_RANK + LANES), BF16),
                        pltpu.VMEM((A_HEADS * LANES, KV_RANK), BF16),
                        pltpu.VMEM((A_HEADS * LANES, LANES), F32),
                        pltpu.VMEM((A_HEADS * LANES, LANES), F32),
                        pltpu.VMEM((A_HEADS * LANES, KV_RANK + LANES), F32)],
        compiler_params=_params(("arbitrary", "arbitrary")),
        name="dsa",
    )(qa, qi, sm, c, sm, c, qa, wuk_t, wuv, near, metab, mmb)
    return out_real, out_meta


def _merge_kernel(alpha, h_ref, hm_ref, ha_ref, g_ref, wbm_ref, wba_ref, wo_ref, lg_ref, lb_ref,
                  wr_ref, br_ref, h1_ref, comb_ref, bgt_ref, cnt_ref):
    d = h_ref.shape[1]
    gm = _sigmoid(g_ref[:, 0:d].astype(F32))
    ga = _sigmoid(g_ref[:, d:2 * d].astype(F32))
    y = gm * jnp.dot(hm_ref[...], wbm_ref[...], preferred_element_type=F32) + \
        ga * jnp.dot(ha_ref[...], wba_ref[...], preferred_element_type=F32)
    z = alpha * h_ref[...] + jnp.dot(y.astype(BF16), wo_ref[...], preferred_element_type=F32)
    h1 = _layer_norm_rows(z, lg_ref[...], lb_ref[...], 1e-5)
    h1_ref[...] = h1

    tm = h1.shape[0]
    logits_t = lax.dot_general(wr_ref[...], h1.astype(BF16), (((1,), (1,)), ((), ())),
                               preferred_element_type=F32)
    scores = _sigmoid(logits_t[0:N_EXPERTS, :])
    sel = scores + br_ref[0:N_EXPERTS, :]
    best = None
    for gidx in range(N_GROUPS):
        r0, r1, r2, r3 = (sel[gidx * GROUP_SIZE + u:gidx * GROUP_SIZE + u + 1, :] for u in range(4))
        a, b = jnp.maximum(r0, r1), jnp.minimum(r0, r1)
        c, dd = jnp.maximum(r2, r3), jnp.minimum(r2, r3)
        gs = jnp.maximum(a, c) + jnp.maximum(jnp.minimum(a, c), jnp.maximum(b, dd))
        if best is None:
            best, bg = gs, jnp.zeros((1, tm), jnp.int32)
        else:
            upd = gs > best
            bg = jnp.where(upd, gidx, bg)
            best = jnp.where(upd, gs, best)
    eidx = lax.broadcasted_iota(jnp.int32, (N_EXPERTS, tm), 0)
    masked = jnp.where((eidx // GROUP_SIZE) == bg, sel, -jnp.inf)
    v1 = jnp.max(masked, axis=0, keepdims=True)
    i1 = jnp.min(jnp.where(masked == v1, eidx, N_EXPERTS), axis=0, keepdims=True)
    masked2 = jnp.where(eidx == i1, -jnp.inf, masked)
    v2 = jnp.max(masked2, axis=0, keepdims=True)
    i2 = jnp.min(jnp.where(masked2 == v2, eidx, N_EXPERTS), axis=0, keepdims=True)
    s1 = jnp.sum(jnp.where(eidx == i1, scores, 0.0), axis=0, keepdims=True)
    s2 = jnp.sum(jnp.where(eidx == i2, scores, 0.0), axis=0, keepdims=True)
    tot = s1 + s2
    comb_t = jnp.where(eidx == i1, s1 / tot, 0.0) + jnp.where(eidx == i2, s2 / tot, 0.0)
    comb_pad = jnp.concatenate([comb_t, jnp.zeros((LANES - N_EXPERTS, tm), F32)], axis=0)
    comb_ref[...] = comb_pad.T
    bgt_ref[...] = jnp.broadcast_to(bg, (8, tm))
    gidx8 = lax.broadcasted_iota(jnp.int32, (8, tm), 0)
    counts = jnp.sum(jnp.where(gidx8 == bg, 1.0, 0.0), axis=1, keepdims=True)
    cnt_ref[0] = jnp.broadcast_to(counts, (8, LANES)).astype(jnp.int32)


def _merge(h, hm, ha, g, w_bm, w_ba, w_o, ln_g, ln_b, wr_t, br, l, alpha):
    n, d = h.shape
    tm = MOE_ROW_TILE
    row = lambda width: pl.BlockSpec((tm, width), lambda r: (r, 0))
    full = lambda a: pl.BlockSpec(a.shape, lambda r: (0,) * a.ndim)
    args = (h, hm, ha, g, w_bm, w_ba, w_o, ln_g, ln_b, wr_t, br)
    return pl.pallas_call(
        functools.partial(_merge_kernel, alpha),
        grid=(n // tm,),
        in_specs=[row(d), row(M_WIDTH), row(A_WIDTH), row(2 * d)] +
                 [_layer_block(a, l) for a in args[4:9]] + [full(wr_t), full(br)],
        out_specs=[row(d), row(LANES),
                   pl.BlockSpec((8, tm), lambda r: (0, r)),
                   pl.BlockSpec((1, 8, LANES), lambda r: (r, 0, 0))],
        out_shape=[jax.ShapeDtypeStruct((n, d), F32), jax.ShapeDtypeStruct((n, LANES), F32),
                   jax.ShapeDtypeStruct((8, n), jnp.int32),
                   jax.ShapeDtypeStruct((n // tm, 8, LANES), jnp.int32)],
        compiler_params=_params(("parallel",)),
        name="merge",
    )(*args)


def _moe_kernel(alpha, cap, cnt_ref, h_ref, comb_ref, bgt_ref, wg_ref, wu_ref, wd_ref, lg_ref, lb_ref,
                out_ref, xb_scr, cs_scr, yt_scr, tri_scr):
    r = pl.program_id(0)
    g = pl.program_id(1)
    rows = h_ref.shape[0]

    @pl.when((r == 0) & (g == 0))
    def _():
        t0 = lax.broadcasted_iota(jnp.int32, (rows, rows), 0)
        t1 = lax.broadcasted_iota(jnp.int32, (rows, rows), 1)
        tri_scr[...] = (t0 < t1).astype(BF16)

    @pl.when(g == 0)
    def _():
        xb_scr[...] = h_ref[...].astype(BF16)
        yt_scr[...] = jnp.zeros_like(yt_scr)
        c = comb_ref[...]
        for part in range(2):
            cb = c.astype(BF16)
            cs_scr[part] = cb
            c = c - cb.astype(F32)

    member = bgt_ref[0:1, :] == g
    mem8 = jnp.broadcast_to(jnp.where(member, 1.0, 0.0), (8, rows)).astype(BF16)
    rank = jnp.dot(mem8, tri_scr[...], preferred_element_type=F32)[0:1, :].astype(jnp.int32)
    n_blocks = (cnt_ref[r * N_GROUPS + g] + cap - 1) // cap
    lane = lax.broadcasted_iota(jnp.int32, (cap, LANES), 1)
    tn = (((0,), (0,)), ((), ()))

    def block(b, carry):
        slot = lax.broadcasted_iota(jnp.int32, (cap, rows), 0) + b * cap
        onehot = jnp.where(member & (rank == slot), 1.0, 0.0).astype(BF16)
        xg = jnp.dot(onehot, xb_scr[...], preferred_element_type=F32).astype(BF16)
        cw = jnp.dot(onehot, cs_scr[0], preferred_element_type=F32)
        cw = cw + jnp.dot(onehot, cs_scr[1], preferred_element_type=F32)
        y = jnp.zeros((cap, out_ref.shape[1]), F32)
        for e in range(GROUP_SIZE):
            gate = jnp.dot(xg, wg_ref[e], preferred_element_type=F32)
            up = jnp.dot(xg, wu_ref[e], preferred_element_type=F32)
            he = gate * _sigmoid(gate) * up
            o = jnp.dot(he.astype(BF16), wd_ref[e], preferred_element_type=F32)
            ce = jnp.sum(jnp.where(lane == g * GROUP_SIZE + e, cw, 0.0), axis=1, keepdims=True)
            y = y + ce * o
        yt_scr[...] += lax.dot_general(onehot, y.astype(BF16), tn, preferred_element_type=F32)
        return carry

    lax.fori_loop(0, n_blocks, block, 0)

    @pl.when(g == pl.num_programs(1) - 1)
    def _():
        z = alpha * h_ref[...] + yt_scr[...]
        out_ref[...] = _layer_norm_rows(z, lg_ref[...], lb_ref[...], 1e-5)


def _moe(h, comb, bgt, counts, w_gate, w_up, w_down, ln_g, ln_b, l, alpha):
    n, d = h.shape
    de = w_gate.shape[-1]
    tm = MOE_ROW_TILE
    cnt = counts[:, 0:N_GROUPS, 0].reshape(-1)
    grid_spec = pltpu.PrefetchScalarGridSpec(
        num_scalar_prefetch=1,
        grid=(n // tm, N_GROUPS),
        in_specs=[pl.BlockSpec((tm, d), lambda r, g, c: (r, 0)),
                  pl.BlockSpec((tm, LANES), lambda r, g, c: (r, 0)),
                  pl.BlockSpec((8, tm), lambda r, g, c: (0, r)),
                  pl.BlockSpec((None, GROUP_SIZE, d, de), lambda r, g, c: (l, g, 0, 0)),
                  pl.BlockSpec((None, GROUP_SIZE, d, de), lambda r, g, c: (l, g, 0, 0)),
                  pl.BlockSpec((None, GROUP_SIZE, de, d), lambda r, g, c: (l, g, 0, 0)),
                  _layer_block(ln_g, l), _layer_block(ln_b, l)],
        out_specs=pl.BlockSpec((tm, d), lambda r, g, c: (r, 0)),
        scratch_shapes=[pltpu.VMEM((tm, d), BF16), pltpu.VMEM((2, tm, LANES), BF16),
                        pltpu.VMEM((tm, d), F32), pltpu.VMEM((tm, tm), BF16)])
    return pl.pallas_call(
        functools.partial(_moe_kernel, alpha, MOE_CAP),
        grid_spec=grid_spec,
        out_shape=jax.ShapeDtypeStruct((n, d), F32),
        compiler_params=_params(("arbitrary", "arbitrary")),
        name="moe",
    )(cnt, h, comb, bgt, w_gate, w_up, w_down, ln_g, ln_b)


def _with_meta(real, meta, n_pad):
    pad = n_pad - real.shape[0] - meta.shape[0]
    return jnp.concatenate([real, meta, jnp.zeros((pad, real.shape[1]), real.dtype)], axis=0)


def kernel(x, meta_tokens, ln_in_g, ln_in_b, w_in, conv_w, b_if, mnorm_g, kv_norm_g, w_uk, w_uv,
           w_branch_m, w_branch_a, w_out, ln1_g, ln1_b, w_router, b_router, w_gate, w_up, w_down,
           ln2_g, ln2_b, rel_bias):
    batch, seq, d = x.shape
    depth = w_in.shape[0]
    alpha = (2 * depth) ** 0.25
    n_real = batch * seq
    n_meta = batch * N_META
    tile = math.lcm(ROW_TILE, MOE_ROW_TILE)
    n_pad = -(-(n_real + n_meta) // tile) * tile
    assert seq % LANES == 0 and n_real % N_META == 0

    h = _with_meta(x.reshape(n_real, d), jnp.tile(meta_tokens.astype(x.dtype), (batch, 1)), n_pad)
    h = _input_ln(h, ln_in_g, ln_in_b)
    tables = _bias_tables(rel_bias)
    row3 = lambda a: a.astype(F32)[:, None, :]
    w_packed, brow, kv_g = _pack_w_in(w_in), _pack_b_if(b_if), row3(kv_norm_g)
    conv_f, mnorm = conv_w.astype(F32), row3(mnorm_g)
    wuk_t = jnp.swapaxes(w_uk, 2, 3).astype(BF16)
    wuv = w_uv.astype(BF16)
    w_bm, w_ba, w_o = w_branch_m.astype(BF16), w_branch_a.astype(BF16), w_out.astype(BF16)
    wr_t = jnp.zeros((LANES, d), F32).at[0:N_EXPERTS].set(w_router.T).astype(BF16)
    br = jnp.zeros((LANES, 1), F32).at[0:N_EXPERTS, 0].set(b_router)
    wg, wu, wd = w_gate.astype(BF16), w_up.astype(BF16), w_down.astype(BF16)
    g1, b1, g2, b2 = row3(ln1_g), row3(ln1_b), row3(ln2_g), row3(ln2_b)
    for l in range(depth):
        pm, qa, qi, c, sm, g = _project(h, w_packed, kv_g, brow, l)
        hm_real, hm_meta = _mlstm(pm, sm, conv_f, mnorm, l, batch, seq)
        ha_real, ha_meta = _dsa(qa, qi, sm, c, wuk_t, wuv, tables, l, batch, seq)
        hm = _with_meta(hm_real, hm_meta, n_pad)
        ha = _with_meta(ha_real, ha_meta, n_pad)
        h1, comb, bgt, counts = _merge(h, hm, ha, g, w_bm, w_ba, w_o, g1, b1, wr_t, br, l, alpha)
        h = _moe(h1, comb, bgt, counts, wg, wu, wd, g2, b2, l, alpha)
    return h[:n_real].reshape(batch, seq, d)
```

```python
import functools
import math

import numpy as np
import jax
import jax.numpy as jnp
from jax import lax
from jax.experimental import pallas as pl
from jax.experimental.pallas import tpu as pltpu

F32 = jnp.float32
BF16 = jnp.bfloat16

N_META = 16
M_HEADS = 4
M_HEAD_DIM = 128
M_WIDTH = M_HEADS * M_HEAD_DIM
CONV_WIDTH = 4
A_HEADS = 8
A_HEAD_DIM = 64
A_WIDTH = A_HEADS * A_HEAD_DIM
KV_RANK = 128
IDX_HEADS = 4
IDX_DIM = 64
IDX_SCALE = (IDX_HEADS * IDX_DIM) ** -0.5
TOPK_MAX = 256
REL_BUCKETS = 32
REL_MAX_EXACT = 16
REL_MAX_DIST = 128
N_EXPERTS = 16
N_GROUPS = 4
GROUP_SIZE = N_EXPERTS // N_GROUPS

LANES = 128
ROW_TILE = 256
MOE_ROW_TILE = 768
MOE_CAP = 256
M_CHUNK = 256
SCORE_CHUNK = 4
KEY_SUB = 4
ATT_GROUPS = 2
VMEM_LIMIT = 56 * 1024 * 1024
NEG_BIG = -1e30
INT_MIN = -2 ** 31
HALF_BIAS = 2 ** 15
LOG2E = math.log2(math.e)

PK_PM = 0
PK_QA = 2048
PK_QI = 2560
PK_CKV = 2816
PK_SM = 2944
PK_G = 3072
PK_TOTAL = 5120
SM_KIDX = 0
SM_WIDX = 64
SM_IPRE = 68
SM_FPRE = 72


def _params(sem):
    return pltpu.CompilerParams(dimension_semantics=sem, vmem_limit_bytes=VMEM_LIMIT)


def _sigmoid(x):
    return 1.0 / (1.0 + jnp.exp(-x))


def _layer_norm_rows(x, g, b, eps):
    mu = jnp.mean(x, axis=-1, keepdims=True)
    xc = x - mu
    var = jnp.mean(xc * xc, axis=-1, keepdims=True)
    return xc * lax.rsqrt(var + eps) * g + b


def _ln_kernel(x_ref, g_ref, b_ref, o_ref):
    o_ref[...] = _layer_norm_rows(x_ref[...], g_ref[...], b_ref[...], 1e-5)


def _input_ln(x, g, b):
    n, d = x.shape
    return pl.pallas_call(
        _ln_kernel,
        grid=(n // ROW_TILE,),
        in_specs=[pl.BlockSpec((ROW_TILE, d), lambda r: (r, 0)),
                  pl.BlockSpec((1, d), lambda r: (0, 0)),
                  pl.BlockSpec((1, d), lambda r: (0, 0))],
        out_specs=pl.BlockSpec((ROW_TILE, d), lambda r: (r, 0)),
        out_shape=jax.ShapeDtypeStruct((n, d), F32),
        compiler_params=_params(("parallel",)),
        name="input_ln",
    )(x, g.reshape(1, d), b.reshape(1, d))


def _proj_kernel(h_ref, w_ref, kvg_ref, brow_ref, pm_ref, qa_ref, qi_ref, c_ref, sm_ref, g_ref):
    x = h_ref[...].astype(BF16)

    def mm(lo, width):
        return jnp.dot(x, w_ref[:, lo:lo + width], preferred_element_type=F32)

    pm_ref[...] = mm(PK_PM, 2048).astype(BF16)
    qa_ref[...] = mm(PK_QA, 512).astype(BF16)
    qi_ref[...] = mm(PK_QI, 256).astype(BF16)
    ckv = mm(PK_CKV, 128)
    c_ref[...] = ckv * lax.rsqrt(jnp.mean(ckv * ckv, axis=-1, keepdims=True) + 1e-6) * kvg_ref[...]
    sm_ref[...] = mm(PK_SM, 128) + brow_ref[...]
    g_ref[...] = mm(PK_G, 2048).astype(BF16)


def _pack_w_in(w):
    cols = [w[..., 0:2048], w[..., 2056:2568], w[..., 2696:2952], w[..., 2568:2696],
            w[..., 2952:3016], w[..., 3016:3020], w[..., 2048:2056],
            jnp.zeros(w.shape[:-1] + (LANES - 76,), w.dtype), w[..., 3020:5068]]
    return jnp.concatenate(cols, axis=-1).astype(BF16)


def _pack_b_if(b_if):
    depth = b_if.shape[0]
    return jnp.concatenate([jnp.zeros((depth, SM_IPRE), F32), b_if.astype(F32),
                            jnp.zeros((depth, LANES - SM_IPRE - 2 * M_HEADS), F32)], axis=1)[:, None, :]


def _layer_block(arr, l):
    zeros = (0,) * (arr.ndim - 1)
    return pl.BlockSpec((None,) + arr.shape[1:], lambda *idx: (l,) + zeros)


def _project(h, w_packed, kv_g, brow, l):
    n, d = h.shape
    row = lambda width: pl.BlockSpec((ROW_TILE, width), lambda r: (r, 0))
    shp = lambda width, dt: jax.ShapeDtypeStruct((n, width), dt)
    return pl.pallas_call(
        _proj_kernel,
        grid=(n // ROW_TILE,),
        in_specs=[row(d), _layer_block(w_packed, l), _layer_block(kv_g, l), _layer_block(brow, l)],
        out_specs=[row(2048), row(512), row(256), row(128), row(128), row(2048)],
        out_shape=[shp(2048, BF16), shp(512, BF16), shp(256, BF16), shp(128, F32), shp(128, F32),
                   shp(2048, BF16)],
        compiler_params=_params(("parallel",)),
        name="in_proj",
    )(h, w_packed, kv_g, brow)


def _log_sigmoid(f):
    return jnp.minimum(f, 0.0) - jnp.log1p(jnp.exp(-jnp.abs(f)))


def _row_from_col(col):
    L = col.shape[0]
    eye = lax.broadcasted_iota(jnp.int32, (L, L), 0) == lax.broadcasted_iota(jnp.int32, (L, L), 1)
    return jnp.sum(jnp.where(eye, jnp.broadcast_to(col, (L, L)), 0.0), axis=0, keepdims=True)


def _mlstm_chunk_math(q, k, v, li, li_row, lf_row, ct, n, m):
    L = q.shape[0]
    r = lax.broadcasted_iota(jnp.int32, (L, L), 0)
    c = lax.broadcasted_iota(jnp.int32, (L, L), 1)
    tril = c <= r
    b_col = jnp.sum(jnp.where(tril, jnp.broadcast_to(lf_row, (L, L)), 0.0), axis=1, keepdims=True)
    b_row = _row_from_col(b_col)
    d = jnp.where(tril, b_col - b_row + li_row, -jnp.inf)
    inter = b_col + m
    m_t = jnp.maximum(inter, jnp.max(d, axis=1, keepdims=True))
    a = jnp.exp(inter - m_t)
    qb, kb, vb = q.astype(BF16), k.astype(BF16), v.astype(BF16)
    qk = lax.dot_general(qb, kb, (((1,), (1,)), ((), ())), preferred_element_type=F32)
    w = jnp.exp(d - m_t) * qk
    num = a * jnp.dot(qb, ct.astype(BF16), preferred_element_type=F32) + \
        jnp.dot(w.astype(BF16), vb, preferred_element_type=F32)
    den = a * jnp.sum(q * n, axis=1, keepdims=True) + jnp.sum(w, axis=1, keepdims=True)
    h = num / jnp.maximum(jnp.abs(den), jnp.exp(-m_t))
    b_last = b_col[L - 1:L, :]
    g = b_last - b_col + li
    m_new = jnp.maximum(b_last + m, jnp.max(g, axis=0, keepdims=True))
    decay = jnp.exp(b_last + m - m_new)
    kw = k * jnp.exp(g - m_new)
    ct_new = decay * ct + lax.dot_general(kw.astype(BF16), vb, (((0,), (0,)), ((), ())),
                                          preferred_element_type=F32)
    n_new = decay * n + jnp.sum(kw, axis=0, keepdims=True)
    return h, ct_new, n_new, m_new


def _mlstm_kernel(pm_ref, sm_ref, pmm_ref, smm_ref, cw_ref, mg_ref, out_ref, outm_ref,
                  ct_scr, n_scr, m_scr, x_scr):
    ci = pl.program_id(1)
    tail = 8

    def run_chunk(p_ref, s_ref, o_ref):
        L = p_ref.shape[0]
        x_scr[tail:tail + L, :] = p_ref[:, 0:2 * M_WIDTH].astype(F32)
        conv = cw_ref[0:1, :] * x_scr[tail - 3:tail - 3 + L, :]
        for j in range(1, CONV_WIDTH):
            conv = conv + cw_ref[j:j + 1, :] * x_scr[tail - 3 + j:tail - 3 + j + L, :]
        x_scr[0:tail, :] = x_scr[L:L + tail, :]
        qk = conv * _sigmoid(conv)
        if L % LANES == 0:
            gates_t = s_ref[...].T
            li_rows = gates_t[SM_IPRE:SM_IPRE + M_HEADS, :]
            lf_rows = _log_sigmoid(gates_t[SM_FPRE:SM_FPRE + M_HEADS, :])
        for hd in range(M_HEADS):
            lo = hd * M_HEAD_DIM
            q = qk[:, lo:lo + M_HEAD_DIM] * (M_HEAD_DIM ** -0.5)
            k = qk[:, M_WIDTH + lo:M_WIDTH + lo + M_HEAD_DIM]
            v = p_ref[:, 2 * M_WIDTH + lo:2 * M_WIDTH + lo + M_HEAD_DIM].astype(F32)
            li = s_ref[:, SM_IPRE + hd:SM_IPRE + hd + 1]
            if L % LANES == 0:
                li_row, lf_row = li_rows[hd:hd + 1, :], lf_rows[hd:hd + 1, :]
            else:
                li_row = _row_from_col(li)
                lf_row = _row_from_col(_log_sigmoid(s_ref[:, SM_FPRE + hd:SM_FPRE + hd + 1]))
            h, ct_new, n_new, m_new = _mlstm_chunk_math(
                q, k, v, li, li_row, lf_row, ct_scr[hd], n_scr[hd:hd + 1, :], m_scr[hd:hd + 1, 0:1])
            ct_scr[hd] = ct_new
            n_scr[hd:hd + 1, :] = n_new
            m_scr[hd:hd + 1, :] = jnp.broadcast_to(m_new, (1, LANES))
            mu = jnp.mean(h, axis=-1, keepdims=True)
            hc = h - mu
            var = jnp.mean(hc * hc, axis=-1, keepdims=True)
            o_gate = _sigmoid(p_ref[:, 3 * M_WIDTH + lo:3 * M_WIDTH + lo + M_HEAD_DIM].astype(F32))
            o_ref[:, lo:lo + M_HEAD_DIM] = (hc * lax.rsqrt(var + 1e-5) * mg_ref[:, lo:lo + M_HEAD_DIM]
                                            * o_gate).astype(o_ref.dtype)

    @pl.when(ci == 0)
    def _():
        ct_scr[...] = jnp.zeros_like(ct_scr)
        n_scr[...] = jnp.zeros_like(n_scr)
        m_scr[...] = jnp.zeros_like(m_scr)
        x_scr[...] = jnp.zeros_like(x_scr)
        run_chunk(pmm_ref, smm_ref, outm_ref)

    run_chunk(pm_ref, sm_ref, out_ref)


def _mlstm(pm, sm, conv_w, mnorm_g, l, batch, seq):
    n_real = batch * seq
    chunk = min(M_CHUNK, seq)
    nc = seq // chunk
    meta_blk = n_real // N_META
    out_real, out_meta = pl.pallas_call(
        _mlstm_kernel,
        grid=(batch, nc),
        in_specs=[pl.BlockSpec((chunk, 4 * M_WIDTH), lambda b, c: (b * nc + c, 0)),
                  pl.BlockSpec((chunk, LANES), lambda b, c: (b * nc + c, 0)),
                  pl.BlockSpec((N_META, 4 * M_WIDTH), lambda b, c: (meta_blk + b, 0)),
                  pl.BlockSpec((N_META, LANES), lambda b, c: (meta_blk + b, 0)),
                  _layer_block(conv_w, l), _layer_block(mnorm_g, l)],
        out_specs=[pl.BlockSpec((chunk, M_WIDTH), lambda b, c: (b * nc + c, 0)),
                   pl.BlockSpec((N_META, M_WIDTH), lambda b, c: (b, 0))],
        out_shape=[jax.ShapeDtypeStruct((n_real, M_WIDTH), BF16),
                   jax.ShapeDtypeStruct((batch * N_META, M_WIDTH), BF16)],
        scratch_shapes=[pltpu.VMEM((M_HEADS, M_HEAD_DIM, M_HEAD_DIM), F32),
                        pltpu.VMEM((8, LANES), F32),
                        pltpu.VMEM((8, LANES), F32),
                        pltpu.VMEM((chunk + 8, 2 * M_WIDTH), F32)],
        compiler_params=_params(("arbitrary", "arbitrary")),
        name="mlstm",
    )(pm, sm, pm, sm, conv_w, mnorm_g)
    return out_real, out_meta


def _rel_bucket_np(dist):
    n = np.maximum(dist, 0)
    nf = np.maximum(n, REL_MAX_EXACT).astype(np.float32)
    large = REL_MAX_EXACT + (np.log(nf / np.float32(REL_MAX_EXACT)) /
                             np.float32(math.log(REL_MAX_DIST / REL_MAX_EXACT))
                             * np.float32(REL_BUCKETS - REL_MAX_EXACT)).astype(np.int32)
    large = np.minimum(large, REL_BUCKETS - 1)
    return np.where(n < REL_MAX_EXACT, n, large).astype(np.int32)


def _bias_tables(rel_bias):
    q = np.arange(LANES)[:, None]
    k = np.arange(LANES)[None, :]
    far = 4 * LANES
    assert (_rel_bucket_np(np.arange(LANES + 1, far)) == REL_BUCKETS - 1).all()
    far_idx = np.full((LANES, LANES), REL_BUCKETS - 1, np.int32)
    near_idx = np.stack([_rel_bucket_np(q - k), _rel_bucket_np(LANES + q - k), far_idx])
    meta_idx = np.stack([_rel_bucket_np(q + N_META - np.minimum(k, N_META - 1)), far_idx])
    mq = np.arange(N_META)[:, None]
    mm_idx = _rel_bucket_np(mq - np.minimum(k, N_META - 1))
    rb = rel_bias.astype(F32)

    def lookup(idx, values):
        onehot = jnp.asarray(idx[..., None] == np.arange(REL_BUCKETS), F32)
        out = jnp.einsum('...b,bh->...h', onehot, values, precision=lax.Precision.HIGHEST)
        return jnp.moveaxis(out, -1, -3)

    rel_values = (rb - rb[REL_BUCKETS - 1][None, :]) * LOG2E
    return lookup(near_idx, rel_values), lookup(meta_idx, rel_values), lookup(mm_idx, rb)


def _dsa_kernel(top_k, qa_ref, qi_ref, sm_ref, cb_ref, smb_ref, cm_ref, qam_ref,
                wuk_ref, wuv_ref, near_ref, metab_ref, mmb_ref,
                out_ref, outm_ref,
                keys_scr, hi_scr, lo_scr, lg_scr, lgm_scr, rawa_scr, rawb_scr, pa_scr, pb_scr,
                caug_scr, ct_scr, kt_scr, cmaug_scr, cmt_scr, qs_scr, mx_scr, mrep_scr, acc_scr):
    i = pl.program_id(1)
    T = LANES
    H = A_HEADS
    col = lax.broadcasted_iota(jnp.int32, (T, T), 1)
    row = lax.broadcasted_iota(jnp.int32, (T, T), 0)
    nt = (((1,), (1,)), ((), ()))

    def q_latent(qa, hd, scale):
        ql = jnp.dot(qa[:, hd * A_HEAD_DIM:(hd + 1) * A_HEAD_DIM], wuk_ref[hd],
                     preferred_element_type=F32)
        return (ql * scale).astype(BF16)

    def ones_column(n):
        return jnp.where(lax.broadcasted_iota(jnp.int32, (n, T), 1) == 0, 1.0, 0.0).astype(BF16)

    @pl.when(i == 0)
    def _():
        caug_scr[:, 0:KV_RANK] = cb_ref[...].astype(BF16)
        caug_scr[:, KV_RANK:KV_RANK + T] = ones_column(caug_scr.shape[0])

        def transpose_keys(blk, carry):
            rows = pl.ds(pl.multiple_of(blk * KEY_SUB * T, KEY_SUB * T), KEY_SUB * T)
            ct_scr[blk] = cb_ref[rows, :].T.astype(BF16)
            kt_scr[blk] = smb_ref[rows, :].T[SM_KIDX:SM_KIDX + IDX_DIM, :].astype(BF16)
            return carry

        lax.fori_loop(0, ct_scr.shape[0], transpose_keys, 0)
        cm_pad = jnp.concatenate([cm_ref[...], jnp.zeros((T - N_META, KV_RANK), F32)], axis=0)
        cmaug_scr[:, 0:KV_RANK] = cm_pad.astype(BF16)
        cmt_scr[...] = cm_pad.T.astype(BF16)
        cmaug_scr[:, KV_RANK:KV_RANK + T] = ones_column(T)
        cmk = cmaug_scr[:, 0:KV_RANK]
        qam = qam_ref[...]
        mrow = lax.broadcasted_iota(jnp.int32, (N_META, T), 0)
        mcol = lax.broadcasted_iota(jnp.int32, (N_META, T), 1)
        for hd in range(H):
            lg = lax.dot_general(q_latent(qam, hd, A_HEAD_DIM ** -0.5), cmk, nt,
                                 preferred_element_type=F32) + mmb_ref[hd]
            lg = jnp.where(mcol <= mrow, lg, NEG_BIG)
            p = jnp.exp(lg - jnp.max(lg, axis=1, keepdims=True))
            p = p / jnp.sum(p, axis=1, keepdims=True)
            o = jnp.dot(p.astype(BF16), cmk, preferred_element_type=F32)
            outm_ref[:, hd * A_HEAD_DIM:(hd + 1) * A_HEAD_DIM] = jnp.dot(
                o.astype(BF16), wuv_ref[hd], preferred_element_type=F32).astype(outm_ref.dtype)

    qa = qa_ref[...]
    for hd in range(H):
        qs_scr[hd * T:(hd + 1) * T, :] = q_latent(qa, hd, A_HEAD_DIM ** -0.5 * LOG2E)
    qi = qi_ref[...]
    wv = sm_ref[:, SM_WIDX:SM_WIDX + IDX_HEADS] * IDX_SCALE
    t_col = i * T + lax.broadcasted_iota(jnp.int32, (T, 1), 0)
    n_chunks = (i + SCORE_CHUNK) // SCORE_CHUNK
    CW = SCORE_CHUNK * T

    def run_pipelined(n, produce, consume, buf_a, buf_b):
        last = n - 1
        produce(0, buf_a)

        def body(t, carry):
            s = 2 * t
            produce(jnp.minimum(s + 1, last), buf_b)
            consume(s, buf_a)
            produce(jnp.minimum(s + 2, last), buf_a)
            consume(s + 1, buf_b)
            return carry

        lax.fori_loop(0, n // 2, body, 0)

        @pl.when(n % 2 == 1)
        def _():
            consume(last, buf_a)

    def score_matmul(cix, buf):
        kc_t = kt_scr[cix]
        for hh in range(IDX_HEADS):
            buf[hh * T:(hh + 1) * T, :] = jnp.dot(
                qi[:, hh * IDX_DIM:(hh + 1) * IDX_DIM], kc_t, preferred_element_type=F32)

    def score_keys(cix, buf):
        acc = jnp.zeros((T, CW), F32)
        for hh in range(IDX_HEADS):
            acc = acc + wv[:, hh:hh + 1] * jnp.maximum(buf[hh * T:(hh + 1) * T, :], 0.0)
        acc = jnp.where(acc == 0.0, 0.0, acc)
        bits = lax.bitcast_convert_type(acc, jnp.int32)
        key = jnp.where(bits < 0, bits ^ jnp.int32(0x7FFFFFFF), bits)
        s_idx = cix * CW + lax.broadcasted_iota(jnp.int32, (T, CW), 1)
        key = jnp.where(s_idx <= t_col, key, jnp.int32(INT_MIN))
        for u in range(SCORE_CHUNK):
            tile = key[:, u * T:(u + 1) * T]
            keys_scr[cix * SCORE_CHUNK + u] = tile
            tile_t = tile.T
            hi_scr[cix * SCORE_CHUNK + u] = lax.shift_right_arithmetic(tile_t, 16).astype(jnp.int16)
            lo_scr[cix * SCORE_CHUNK + u] = ((tile_t & 0xFFFF) - HALF_BIAS).astype(jnp.int16)

    run_pipelined(n_chunks, score_matmul, score_keys, rawa_scr, rawb_scr)

    def rep16(row_i32):
        return jnp.broadcast_to(row_i32, (T, T)).astype(jnp.int16)

    def count16(src_scr, pred_fn):
        def body(cix, cnt):
            for u in range(SCORE_CHUNK):
                hit = pred_fn(src_scr[cix * SCORE_CHUNK + u])
                cnt = cnt + jnp.where(hit, jnp.int16(1), jnp.int16(0))
            return cnt
        cnt = lax.fori_loop(0, n_chunks, body, jnp.zeros((T, T), jnp.int16))
        return jnp.sum(cnt.astype(F32), axis=0, keepdims=True)

    def search16(src_scr, k_row):
        def bit_body(bi, ans):
            cand_u = ans | lax.shift_left(jnp.int32(1), jnp.int32(15) - bi)
            cand = rep16(cand_u - HALF_BIAS)
            total = count16(src_scr, lambda x: x >= cand)
            return jnp.where(total >= k_row, cand_u, ans)
        return lax.fori_loop(0, 16, bit_body, jnp.zeros((1, T), jnp.int32))

    k_row = jnp.full((1, T), float(top_k), F32)
    hi_s = search16(hi_scr, k_row) - HALF_BIAS
    hi_rep = rep16(hi_s)
    k_low = k_row - count16(hi_scr, lambda x: x > hi_rep)

    def band_body(j, carry):
        lo_scr[j] = jnp.where(hi_scr[j] == hi_rep, lo_scr[j], jnp.int16(-HALF_BIAS))
        return carry

    lax.fori_loop(0, n_chunks * SCORE_CHUNK, band_body, 0)
    lo_u = search16(lo_scr, k_low)
    lo_rep = rep16(lo_u - HALF_BIAS)
    need_row = k_low - count16(lo_scr, lambda x: x > lo_rep)
    thr_row = lax.shift_left(hi_s, 16) | lo_u
    thr = jnp.broadcast_to(thr_row, (T, T)).T
    need = jnp.broadcast_to(need_row, (T, T)).T

    mx_scr[...] = jnp.full_like(mx_scr, NEG_BIG)
    acc_scr[...] = jnp.zeros_like(acc_scr)
    hg = H // ATT_GROUPS
    groups = [slice(g * hg * T, (g + 1) * hg * T) for g in range(ATT_GROUPS)]
    meta_sel = jnp.minimum(i, 1)

    def max_pass(ck_t, madds, bias_fns, store):
        lgs = [jnp.dot(qs_scr[rs, :], ck_t, preferred_element_type=F32) for rs in groups]
        for hd in range(H):
            rs = slice(hd * T, (hd + 1) * T)
            lo = (hd % hg) * T
            mx = mx_scr[rs, :]
            for u in range(len(madds)):
                x = lgs[hd // hg][lo:lo + T, u * T:(u + 1) * T] + madds[u]
                if bias_fns[u] is not None:
                    x = x + bias_fns[u](hd)
                store(rs, u, x)
                mx = jnp.maximum(mx, x)
            mx_scr[rs, :] = mx

    def sum_pass(c_aug, n_sub, load):
        for grp in groups:
            m_rep = mrep_scr[grp, :]
            ph = [jnp.exp2(load(grp, u) - m_rep).astype(BF16) for u in range(n_sub)]
            p = ph[0] if n_sub == 1 else jnp.concatenate(ph, axis=1)
            acc_scr[grp, :] += jnp.dot(p, c_aug, preferred_element_type=F32)

    def key_rows(step):
        return pl.ds(pl.multiple_of(step * KEY_SUB * T, KEY_SUB * T), KEY_SUB * T)

    upper = (row < col).astype(BF16)

    def mask_step(step, seen, near):
        madds, bias_fns = [], []
        for u in range(KEY_SUB):
            j = KEY_SUB * step + u
            kk = keys_scr[j]
            eq = kk == thr
            eqf = jnp.where(eq, 1.0, 0.0)
            before = jnp.dot(eqf.astype(BF16), upper, preferred_element_type=F32) + seen
            sel = (kk > thr) | (eq & (before < need))
            if near:
                sel = sel & ((j * T + col) <= (i * T + row))
                dsel = jnp.clip(i - j, 0, 2)
                bias_fns.append(lambda hd, dsel=dsel: near_ref[dsel, hd])
            else:
                bias_fns.append(None)
            seen = seen + jnp.sum(eqf, axis=1, keepdims=True)
            madds.append(jnp.where(sel, 0.0, NEG_BIG))

        def store(rs, u, x):
            lg_scr[step, rs, u * T:(u + 1) * T] = x

        max_pass(ct_scr[step], madds, bias_fns, store)
        return seen

    def weights_step(step, buf):
        for grp in groups:
            m_rep = mrep_scr[grp, :]
            for u in range(KEY_SUB):
                buf[grp, u * T:(u + 1) * T] = jnp.exp2(
                    lg_scr[step, grp, u * T:(u + 1) * T] - m_rep).astype(BF16)

    def accumulate_step(step, buf):
        c_aug = caug_scr[key_rows(step), :]
        for grp in groups:
            acc_scr[grp, :] += jnp.dot(buf[grp, :], c_aug, preferred_element_type=F32)

    def store_meta(rs, u, x):
        lgm_scr[rs, :] = x

    n_far = jnp.maximum(i - 1, 0) // KEY_SUB
    n_steps = (i + KEY_SUB) // KEY_SUB
    max_pass(cmt_scr[...], [jnp.where(col < N_META, 0.0, NEG_BIG)],
             [lambda hd: metab_ref[meta_sel, hd]], store_meta)
    seen = lax.fori_loop(0, n_far, lambda s, c: mask_step(s, c, False), jnp.zeros((T, 1), F32))
    lax.fori_loop(n_far, n_steps, lambda s, c: mask_step(s, c, True), seen)
    mrep_scr[...] = jnp.broadcast_to(jnp.max(mx_scr[...], axis=1, keepdims=True), mrep_scr.shape)
    sum_pass(cmaug_scr[...], 1, lambda grp, u: lgm_scr[grp, :])
    def sum_step(step, carry):
        weights_step(step, pa_scr)
        accumulate_step(step, pa_scr)
        return carry

    lax.fori_loop(0, n_steps, sum_step, 0)

    for hd in range(H):
        rs = slice(hd * T, (hd + 1) * T)
        o = acc_scr[rs, 0:KV_RANK] / acc_scr[rs, KV_RANK:KV_RANK + 1]
        out_ref[:, hd * A_HEAD_DIM:(hd + 1) * A_HEAD_DIM] = jnp.dot(
            o.astype(BF16), wuv_ref[hd], preferred_element_type=F32).astype(out_ref.dtype)


def _dsa(qa, qi, sm, c, wuk_t, wuv, tables, l, batch, seq):
    n_real = batch * seq
    nq = seq // LANES
    n_tiles = ((nq + SCORE_CHUNK - 1) // SCORE_CHUNK) * SCORE_CHUNK
    top_k = min(TOPK_MAX, seq // 4)
    meta_blk = n_real // N_META
    near, metab, mmb = tables
    full = lambda a: pl.BlockSpec(a.shape, lambda b, i: (0,) * a.ndim)
    assert seq % (SCORE_CHUNK * LANES) == 0 and nq % KEY_SUB == 0 and SCORE_CHUNK == KEY_SUB
    out_real, out_meta = pl.pallas_call(
        functools.partial(_dsa_kernel, top_k),
        grid=(batch, nq),
        in_specs=[pl.BlockSpec((LANES, A_WIDTH), lambda b, i: (b * nq + i, 0)),
                  pl.BlockSpec((LANES, IDX_HEADS * IDX_DIM), lambda b, i: (b * nq + i, 0)),
                  pl.BlockSpec((LANES, LANES), lambda b, i: (b * nq + i, 0)),
                  pl.BlockSpec((seq, KV_RANK), lambda b, i: (b, 0)),
                  pl.BlockSpec((seq, LANES), lambda b, i: (b, 0)),
                  pl.BlockSpec((N_META, KV_RANK), lambda b, i: (meta_blk + b, 0)),
                  pl.BlockSpec((N_META, A_WIDTH), lambda b, i: (meta_blk + b, 0)),
                  _layer_block(wuk_t, l), _layer_block(wuv, l), full(near), full(metab), full(mmb)],
        out_specs=[pl.BlockSpec((LANES, A_WIDTH), lambda b, i: (b * nq + i, 0)),
                   pl.BlockSpec((N_META, A_WIDTH), lambda b, i: (b, 0))],
        out_shape=[jax.ShapeDtypeStruct((n_real, A_WIDTH), BF16),
                   jax.ShapeDtypeStruct((batch * N_META, A_WIDTH), BF16)],
        scratch_shapes=[pltpu.VMEM((n_tiles, LANES, LANES), jnp.int32),
                        pltpu.VMEM((n_tiles, LANES, LANES), jnp.int16),
                        pltpu.VMEM((n_tiles, LANES, LANES), jnp.int16),
                        pltpu.VMEM((nq // KEY_SUB, A_HEADS * LANES, KEY_SUB * LANES), F32),
                        pltpu.VMEM((A_HEADS * LANES, LANES), F32),
                        pltpu.VMEM((IDX_HEADS * LANES, SCORE_CHUNK * LANES), F32),
                        pltpu.VMEM((IDX_HEADS * LANES, SCORE_CHUNK * LANES), F32),
                        pltpu.VMEM((A_HEADS * LANES, KEY_SUB * LANES), BF16),
                        pltpu.VMEM((A_HEADS * LANES, KEY_SUB * LANES), BF16),
                        pltpu.VMEM((seq, KV_RANK + LANES), BF16),
                        pltpu.VMEM((nq // KEY_SUB, KV_RANK, KEY_SUB * LANES), BF16),
                        pltpu.VMEM((nq // KEY_SUB, IDX_DIM, KEY_SUB * LANES), BF16),
                        pltpu.VMEM((LANES, KV_RANK + LANES), BF16),
                        pltpu.VMEM((KV_RANK, LANES), BF16),
                        pltpu.VMEM((A_HEADS * LANES, KV_RANK), BF16),
                        pltpu.VMEM((A_HEADS * LANES, LANES), F32),
                        pltpu.VMEM((A_HEADS * LANES, LANES), F32),
                        pltpu.VMEM((A_HEADS * LANES, KV_RANK + LANES), F32)],
        compiler_params=_params(("arbitrary", "arbitrary")),
        name="dsa",
    )(qa, qi, sm, c, sm, c, qa, wuk_t, wuv, near, metab, mmb)
    return out_real, out_meta


def _merge_kernel(alpha, h_ref, hm_ref, ha_ref, g_ref, wbm_ref, wba_ref, wo_ref, lg_ref, lb_ref,
                  wr_ref, br_ref, h1_ref, comb_ref, bgt_ref, cnt_ref):
    d = h_ref.shape[1]
    gm = _sigmoid(g_ref[:, 0:d].astype(F32))
    ga = _sigmoid(g_ref[:, d:2 * d].astype(F32))
    y = gm * jnp.dot(hm_ref[...], wbm_ref[...], preferred_element_type=F32) + \
        ga * jnp.dot(ha_ref[...], wba_ref[...], preferred_element_type=F32)
    z = alpha * h_ref[...] + jnp.dot(y.astype(BF16), wo_ref[...], preferred_element_type=F32)
    h1 = _layer_norm_rows(z, lg_ref[...], lb_ref[...], 1e-5)
    h1_ref[...] = h1

    tm = h1.shape[0]
    logits_t = lax.dot_general(wr_ref[...], h1.astype(BF16), (((1,), (1,)), ((), ())),
                               preferred_element_type=F32)
    scores = _sigmoid(logits_t[0:N_EXPERTS, :])
    sel = scores + br_ref[0:N_EXPERTS, :]
    best = None
    for gidx in range(N_GROUPS):
        r0, r1, r2, r3 = (sel[gidx * GROUP_SIZE + u:gidx * GROUP_SIZE + u + 1, :] for u in range(4))
        a, b = jnp.maximum(r0, r1), jnp.minimum(r0, r1)
        c, dd = jnp.maximum(r2, r3), jnp.minimum(r2, r3)
        gs = jnp.maximum(a, c) + jnp.maximum(jnp.minimum(a, c), jnp.maximum(b, dd))
        if best is None:
            best, bg = gs, jnp.zeros((1, tm), jnp.int32)
        else:
            upd = gs > best
            bg = jnp.where(upd, gidx, bg)
            best = jnp.where(upd, gs, best)
    eidx = lax.broadcasted_iota(jnp.int32, (N_EXPERTS, tm), 0)
    masked = jnp.where((eidx // GROUP_SIZE) == bg, sel, -jnp.inf)
    v1 = jnp.max(masked, axis=0, keepdims=True)
    i1 = jnp.min(jnp.where(masked == v1, eidx, N_EXPERTS), axis=0, keepdims=True)
    masked2 = jnp.where(eidx == i1, -jnp.inf, masked)
    v2 = jnp.max(masked2, axis=0, keepdims=True)
    i2 = jnp.min(jnp.where(masked2 == v2, eidx, N_EXPERTS), axis=0, keepdims=True)
    s1 = jnp.sum(jnp.where(eidx == i1, scores, 0.0), axis=0, keepdims=True)
    s2 = jnp.sum(jnp.where(eidx == i2, scores, 0.0), axis=0, keepdims=True)
    tot = s1 + s2
    comb_t = jnp.where(eidx == i1, s1 / tot, 0.0) + jnp.where(eidx == i2, s2 / tot, 0.0)
    comb_pad = jnp.concatenate([comb_t, jnp.zeros((LANES - N_EXPERTS, tm), F32)], axis=0)
    comb_ref[...] = comb_pad.T
    bgt_ref[...] = jnp.broadcast_to(bg, (8, tm))
    gidx8 = lax.broadcasted_iota(jnp.int32, (8, tm), 0)
    counts = jnp.sum(jnp.where(gidx8 == bg, 1.0, 0.0), axis=1, keepdims=True)
    cnt_ref[0] = jnp.broadcast_to(counts, (8, LANES)).astype(jnp.int32)


def _merge(h, hm, ha, g, w_bm, w_ba, w_o, ln_g, ln_b, wr_t, br, l, alpha):
    n, d = h.shape
    tm = MOE_ROW_TILE
    row = lambda width: pl.BlockSpec((tm, width), lambda r: (r, 0))
    full = lambda a: pl.BlockSpec(a.shape, lambda r: (0,) * a.ndim)
    args = (h, hm, ha, g, w_bm, w_ba, w_o, ln_g, ln_b, wr_t, br)
    return pl.pallas_call(
        functools.partial(_merge_kernel, alpha),
        grid=(n // tm,),
        in_specs=[row(d), row(M_WIDTH), row(A_WIDTH), row(2 * d)] +
                 [_layer_block(a, l) for a in args[4:9]] + [full(wr_t), full(br)],
        out_specs=[row(d), row(LANES),
                   pl.BlockSpec((8, tm), lambda r: (0, r)),
                   pl.BlockSpec((1, 8, LANES), lambda r: (r, 0, 0))],
        out_shape=[jax.ShapeDtypeStruct((n, d), F32), jax.ShapeDtypeStruct((n, LANES), F32),
                   jax.ShapeDtypeStruct((8, n), jnp.int32),
                   jax.ShapeDtypeStruct((n // tm, 8, LANES), jnp.int32)],
        compiler_params=_params(("parallel",)),
        name="merge",
    )(*args)


def _moe_kernel(alpha, cap, cnt_ref, h_ref, comb_ref, bgt_ref, wg_ref, wu_ref, wd_ref, lg_ref, lb_ref,
                out_ref, xb_scr, cs_scr, yt_scr, tri_scr):
    r = pl.program_id(0)
    g = pl.program_id(1)
    rows = h_ref.shape[0]

    @pl.when((r == 0) & (g == 0))
    def _():
        t0 = lax.broadcasted_iota(jnp.int32, (rows, rows), 0)
        t1 = lax.broadcasted_iota(jnp.int32, (rows, rows), 1)
        tri_scr[...] = (t0 < t1).astype(BF16)

    @pl.when(g == 0)
    def _():
        xb_scr[...] = h_ref[...].astype(BF16)
        yt_scr[...] = jnp.zeros_like(yt_scr)
        c = comb_ref[...]
        for part in range(2):
            cb = c.astype(BF16)
            cs_scr[part] = cb
            c = c - cb.astype(F32)

    member = bgt_ref[0:1, :] == g
    mem8 = jnp.broadcast_to(jnp.where(member, 1.0, 0.0), (8, rows)).astype(BF16)
    rank = jnp.dot(mem8, tri_scr[...], preferred_element_type=F32)[0:1, :].astype(jnp.int32)
    n_blocks = (cnt_ref[r * N_GROUPS + g] + cap - 1) // cap
    lane = lax.broadcasted_iota(jnp.int32, (cap, LANES), 1)
    tn = (((0,), (0,)), ((), ()))

    def block(b, carry):
        slot = lax.broadcasted_iota(jnp.int32, (cap, rows), 0) + b * cap
        onehot = jnp.where(member & (rank == slot), 1.0, 0.0).astype(BF16)
        xg = jnp.dot(onehot, xb_scr[...], preferred_element_type=F32).astype(BF16)
        cw = jnp.dot(onehot, cs_scr[0], preferred_element_type=F32)
        cw = cw + jnp.dot(onehot, cs_scr[1], preferred_element_type=F32)
        y = jnp.zeros((cap, out_ref.shape[1]), F32)
        for e in range(GROUP_SIZE):
            gate = jnp.dot(xg, wg_ref[e], preferred_element_type=F32)
            up = jnp.dot(xg, wu_ref[e], preferred_element_type=F32)
            he = gate * _sigmoid(gate) * up
            o = jnp.dot(he.astype(BF16), wd_ref[e], preferred_element_type=F32)
            ce = jnp.sum(jnp.where(lane == g * GROUP_SIZE + e, cw, 0.0), axis=1, keepdims=True)
            y = y + ce * o
        yt_scr[...] += lax.dot_general(onehot, y.astype(BF16), tn, preferred_element_type=F32)
        return carry

    lax.fori_loop(0, n_blocks, block, 0)

    @pl.when(g == pl.num_programs(1) - 1)
    def _():
        z = alpha * h_ref[...] + yt_scr[...]
        out_ref[...] = _layer_norm_rows(z, lg_ref[...], lb_ref[...], 1e-5)


def _moe(h, comb, bgt, counts, w_gate, w_up, w_down, ln_g, ln_b, l, alpha):
    n, d = h.shape
    de = w_gate.shape[-1]
    tm = MOE_ROW_TILE
    cnt = counts[:, 0:N_GROUPS, 0].reshape(-1)
    grid_spec = pltpu.PrefetchScalarGridSpec(
        num_scalar_prefetch=1,
        grid=(n // tm, N_GROUPS),
        in_specs=[pl.BlockSpec((tm, d), lambda r, g, c: (r, 0)),
                  pl.BlockSpec((tm, LANES), lambda r, g, c: (r, 0)),
                  pl.BlockSpec((8, tm), lambda r, g, c: (0, r)),
                  pl.BlockSpec((None, GROUP_SIZE, d, de), lambda r, g, c: (l, g, 0, 0)),
                  pl.BlockSpec((None, GROUP_SIZE, d, de), lambda r, g, c: (l, g, 0, 0)),
                  pl.BlockSpec((None, GROUP_SIZE, de, d), lambda r, g, c: (l, g, 0, 0)),
                  _layer_block(ln_g, l), _layer_block(ln_b, l)],
        out_specs=pl.BlockSpec((tm, d), lambda r, g, c: (r, 0)),
        scratch_shapes=[pltpu.VMEM((tm, d), BF16), pltpu.VMEM((2, tm, LANES), BF16),
                        pltpu.VMEM((tm, d), F32), pltpu.VMEM((tm, tm), BF16)])
    return pl.pallas_call(
        functools.partial(_moe_kernel, alpha, MOE_CAP),
        grid_spec=grid_spec,
        out_shape=jax.ShapeDtypeStruct((n, d), F32),
        compiler_params=_params(("arbitrary", "arbitrary")),
        name="moe",
    )(cnt, h, comb, bgt, w_gate, w_up, w_down, ln_g, ln_b)


def _with_meta(real, meta, n_pad):
    pad = n_pad - real.shape[0] - meta.shape[0]
    return jnp.concatenate([real, meta, jnp.zeros((pad, real.shape[1]), real.dtype)], axis=0)


def kernel(x, meta_tokens, ln_in_g, ln_in_b, w_in, conv_w, b_if, mnorm_g, kv_norm_g, w_uk, w_uv,
           w_branch_m, w_branch_a, w_out, ln1_g, ln1_b, w_router, b_router, w_gate, w_up, w_down,
           ln2_g, ln2_b, rel_bias):
    batch, seq, d = x.shape
    depth = w_in.shape[0]
    alpha = (2 * depth) ** 0.25
    n_real = batch * seq
    n_meta = batch * N_META
    tile = math.lcm(ROW_TILE, MOE_ROW_TILE)
    n_pad = -(-(n_real + n_meta) // tile) * tile
    assert seq % LANES == 0 and n_real % N_META == 0

    h = _with_meta(x.reshape(n_real, d), jnp.tile(meta_tokens.astype(x.dtype), (batch, 1)), n_pad)
    h = _input_ln(h, ln_in_g, ln_in_b)
    tables = _bias_tables(rel_bias)
    row3 = lambda a: a.astype(F32)[:, None, :]
    w_packed, brow, kv_g = _pack_w_in(w_in), _pack_b_if(b_if), row3(kv_norm_g)
    conv_f, mnorm = conv_w.astype(F32), row3(mnorm_g)
    wuk_t = jnp.swapaxes(w_uk, 2, 3).astype(BF16)
    wuv = w_uv.astype(BF16)
    w_bm, w_ba, w_o = w_branch_m.astype(BF16), w_branch_a.astype(BF16), w_out.astype(BF16)
    wr_t = jnp.zeros((LANES, d), F32).at[0:N_EXPERTS].set(w_router.T).astype(BF16)
    br = jnp.zeros((LANES, 1), F32).at[0:N_EXPERTS, 0].set(b_router)
    wg, wu, wd = w_gate.astype(BF16), w_up.astype(BF16), w_down.astype(BF16)
    g1, b1, g2, b2 = row3(ln1_g), row3(ln1_b), row3(ln2_g), row3(ln2_b)
    for l in range(depth):
        pm, qa, qi, c, sm, g = _project(h, w_packed, kv_g, brow, l)
        hm_real, hm_meta = _mlstm(pm, sm, conv_f, mnorm, l, batch, seq)
        ha_real, ha_meta = _dsa(qa, qi, sm, c, wuk_t, wuv, tables, l, batch, seq)
        hm = _with_meta(hm_real, hm_meta, n_pad)
        ha = _with_meta(ha_real, ha_meta, n_pad)
        h1, comb, bgt, counts = _merge(h, hm, ha, g, w_bm, w_ba, w_o, g1, b1, wr_t, br, l, alpha)
        h = _moe(h1, comb, bgt, counts, wg, wu, wd, g2, b2, l, alpha)
    return h[:n_real].reshape(batch, seq, d)
```

```python
import functools
import math

import numpy as np
import jax
import jax.numpy as jnp
from jax import lax
from jax.experimental import pallas as pl
from jax.experimental.pallas import tpu as pltpu

F32 = jnp.float32
BF16 = jnp.bfloat16

N_META = 16
M_HEADS = 4
M_HEAD_DIM = 128
M_WIDTH = M_HEADS * M_HEAD_DIM
CONV_WIDTH = 4
A_HEADS = 8
A_HEAD_DIM = 64
A_WIDTH = A_HEADS * A_HEAD_DIM
KV_RANK = 128
IDX_HEADS = 4
IDX_DIM = 64
IDX_SCALE = (IDX_HEADS * IDX_DIM) ** -0.5
TOPK_MAX = 256
REL_BUCKETS = 32
REL_MAX_EXACT = 16
REL_MAX_DIST = 128
N_EXPERTS = 16
N_GROUPS = 4
GROUP_SIZE = N_EXPERTS // N_GROUPS

LANES = 128
ROW_TILE = 256
MOE_ROW_TILE = 768
MOE_CAP = 224
M_CHUNK = 256
SCORE_CHUNK = 4
KEY_SUB = 4
ATT_GROUPS = 2
VMEM_LIMIT = 56 * 1024 * 1024
NEG_BIG = -1e30
INT_MIN = -2 ** 31
HALF_BIAS = 2 ** 15
LOG2E = math.log2(math.e)

PK_PM = 0
PK_QA = 2048
PK_QI = 2560
PK_CKV = 2816
PK_SM = 2944
PK_G = 3072
PK_TOTAL = 5120
SM_KIDX = 0
SM_WIDX = 64
SM_IPRE = 68
SM_FPRE = 72


def _params(sem):
    return pltpu.CompilerParams(dimension_semantics=sem, vmem_limit_bytes=VMEM_LIMIT)


def _sigmoid(x):
    return 1.0 / (1.0 + jnp.exp(-x))


def _layer_norm_rows(x, g, b, eps):
    mu = jnp.mean(x, axis=-1, keepdims=True)
    xc = x - mu
    var = jnp.mean(xc * xc, axis=-1, keepdims=True)
    return xc * lax.rsqrt(var + eps) * g + b


def _ln_kernel(real_tiles, x_ref, m_ref, g_ref, b_ref, o_ref):
    r = pl.program_id(0)

    @pl.when(r < real_tiles)
    def _():
        o_ref[...] = _layer_norm_rows(x_ref[...], g_ref[...], b_ref[...], 1e-5)

    @pl.when(r >= real_tiles)
    def _():
        o_ref[...] = _layer_norm_rows(m_ref[...], g_ref[...], b_ref[...], 1e-5)


def _input_ln(x, tail, g, b):
    n_real, d = x.shape
    real_tiles = n_real // ROW_TILE
    tiles = real_tiles + tail.shape[0] // ROW_TILE
    return pl.pallas_call(
        functools.partial(_ln_kernel, real_tiles),
        grid=(tiles,),
        in_specs=[pl.BlockSpec((ROW_TILE, d), lambda r: (jnp.minimum(r, real_tiles - 1), 0)),
                  pl.BlockSpec((ROW_TILE, d), lambda r: (jnp.maximum(r - real_tiles, 0), 0)),
                  pl.BlockSpec((1, d), lambda r: (0, 0)),
                  pl.BlockSpec((1, d), lambda r: (0, 0))],
        out_specs=pl.BlockSpec((ROW_TILE, d), lambda r: (r, 0)),
        out_shape=jax.ShapeDtypeStruct((tiles * ROW_TILE, d), F32),
        compiler_params=_params(("arbitrary",)),
        name="input_ln",
    )(x, tail, g.reshape(1, d), b.reshape(1, d))


def _proj_kernel(h_ref, w_ref, kvg_ref, brow_ref, pm_ref, qa_ref, qi_ref, c_ref, sm_ref, g_ref):
    x = h_ref[...].astype(BF16)

    def mm(lo, width):
        return jnp.dot(x, w_ref[:, lo:lo + width], preferred_element_type=F32)

    pm_ref[...] = mm(PK_PM, 2048).astype(BF16)
    qa_ref[...] = mm(PK_QA, 512).astype(BF16)
    qi_ref[...] = mm(PK_QI, 256).astype(BF16)
    ckv = mm(PK_CKV, 128)
    c_ref[...] = ckv * lax.rsqrt(jnp.mean(ckv * ckv, axis=-1, keepdims=True) + 1e-6) * kvg_ref[...]
    sm_ref[...] = mm(PK_SM, 128) + brow_ref[...]
    g_ref[...] = mm(PK_G, 2048).astype(BF16)


def _pack_w_in(w):
    cols = [w[..., 0:2048], w[..., 2056:2568], w[..., 2696:2952], w[..., 2568:2696],
            w[..., 2952:3016], w[..., 3016:3020], w[..., 2048:2056],
            jnp.zeros(w.shape[:-1] + (LANES - 76,), w.dtype), w[..., 3020:5068]]
    return jnp.concatenate(cols, axis=-1).astype(BF16)


def _pack_b_if(b_if):
    depth = b_if.shape[0]
    return jnp.concatenate([jnp.zeros((depth, SM_IPRE), F32), b_if.astype(F32),
                            jnp.zeros((depth, LANES - SM_IPRE - 2 * M_HEADS), F32)], axis=1)[:, None, :]


def _layer_block(arr, l):
    zeros = (0,) * (arr.ndim - 1)
    return pl.BlockSpec((None,) + arr.shape[1:], lambda *idx: (l,) + zeros)


def _project(h, w_packed, kv_g, brow, l):
    n, d = h.shape
    row = lambda width: pl.BlockSpec((ROW_TILE, width), lambda r: (r, 0))
    shp = lambda width, dt: jax.ShapeDtypeStruct((n, width), dt)
    return pl.pallas_call(
        _proj_kernel,
        grid=(n // ROW_TILE,),
        in_specs=[row(d), _layer_block(w_packed, l), _layer_block(kv_g, l), _layer_block(brow, l)],
        out_specs=[row(2048), row(512), row(256), row(128), row(128), row(2048)],
        out_shape=[shp(2048, BF16), shp(512, BF16), shp(256, BF16), shp(128, F32), shp(128, F32),
                   shp(2048, BF16)],
        compiler_params=_params(("parallel",)),
        name="in_proj",
    )(h, w_packed, kv_g, brow)


def _log_sigmoid(f):
    return jnp.minimum(f, 0.0) - jnp.log1p(jnp.exp(-jnp.abs(f)))


def _row_from_col(col):
    L = col.shape[0]
    eye = lax.broadcasted_iota(jnp.int32, (L, L), 0) == lax.broadcasted_iota(jnp.int32, (L, L), 1)
    return jnp.sum(jnp.where(eye, jnp.broadcast_to(col, (L, L)), 0.0), axis=0, keepdims=True)


def _mlstm_chunk_math(q, k, v, li, li_row, lf_row, ct, n, m):
    L = q.shape[0]
    r = lax.broadcasted_iota(jnp.int32, (L, L), 0)
    c = lax.broadcasted_iota(jnp.int32, (L, L), 1)
    tril = c <= r
    b_col = jnp.sum(jnp.where(tril, jnp.broadcast_to(lf_row, (L, L)), 0.0), axis=1, keepdims=True)
    b_row = _row_from_col(b_col)
    d = jnp.where(tril, b_col - b_row + li_row, -jnp.inf)
    inter = b_col + m
    m_t = jnp.maximum(inter, jnp.max(d, axis=1, keepdims=True))
    a = jnp.exp(inter - m_t)
    qb, kb, vb = q.astype(BF16), k.astype(BF16), v.astype(BF16)
    qk = lax.dot_general(qb, kb, (((1,), (1,)), ((), ())), preferred_element_type=F32)
    w = jnp.exp(d - m_t) * qk
    num = a * jnp.dot(qb, ct.astype(BF16), preferred_element_type=F32) + \
        jnp.dot(w.astype(BF16), vb, preferred_element_type=F32)
    den = a * jnp.sum(q * n, axis=1, keepdims=True) + jnp.sum(w, axis=1, keepdims=True)
    h = num / jnp.maximum(jnp.abs(den), jnp.exp(-m_t))
    b_last = b_col[L - 1:L, :]
    g = b_last - b_col + li
    m_new = jnp.maximum(b_last + m, jnp.max(g, axis=0, keepdims=True))
    decay = jnp.exp(b_last + m - m_new)
    kw = k * jnp.exp(g - m_new)
    ct_new = decay * ct + lax.dot_general(kw.astype(BF16), vb, (((0,), (0,)), ((), ())),
                                          preferred_element_type=F32)
    n_new = decay * n + jnp.sum(kw, axis=0, keepdims=True)
    return h, ct_new, n_new, m_new


def _mlstm_kernel(pm_ref, sm_ref, pmm_ref, smm_ref, cw_ref, mg_ref, out_ref, outm_ref,
                  ct_scr, n_scr, m_scr, x_scr):
    ci = pl.program_id(1)
    tail = 8

    def run_chunk(p_ref, s_ref, o_ref):
        L = p_ref.shape[0]
        x_scr[tail:tail + L, :] = p_ref[:, 0:2 * M_WIDTH].astype(F32)
        conv = cw_ref[0:1, :] * x_scr[tail - 3:tail - 3 + L, :]
        for j in range(1, CONV_WIDTH):
            conv = conv + cw_ref[j:j + 1, :] * x_scr[tail - 3 + j:tail - 3 + j + L, :]
        x_scr[0:tail, :] = x_scr[L:L + tail, :]
        qk = conv * _sigmoid(conv)
        if L % LANES == 0:
            gates_t = s_ref[...].T
            li_rows = gates_t[SM_IPRE:SM_IPRE + M_HEADS, :]
            lf_rows = _log_sigmoid(gates_t[SM_FPRE:SM_FPRE + M_HEADS, :])
        for hd in range(M_HEADS):
            lo = hd * M_HEAD_DIM
            q = qk[:, lo:lo + M_HEAD_DIM] * (M_HEAD_DIM ** -0.5)
            k = qk[:, M_WIDTH + lo:M_WIDTH + lo + M_HEAD_DIM]
            v = p_ref[:, 2 * M_WIDTH + lo:2 * M_WIDTH + lo + M_HEAD_DIM].astype(F32)
            li = s_ref[:, SM_IPRE + hd:SM_IPRE + hd + 1]
            if L % LANES == 0:
                li_row, lf_row = li_rows[hd:hd + 1, :], lf_rows[hd:hd + 1, :]
            else:
                li_row = _row_from_col(li)
                lf_row = _row_from_col(_log_sigmoid(s_ref[:, SM_FPRE + hd:SM_FPRE + hd + 1]))
            h, ct_new, n_new, m_new = _mlstm_chunk_math(
                q, k, v, li, li_row, lf_row, ct_scr[hd], n_scr[hd:hd + 1, :], m_scr[hd:hd + 1, 0:1])
            ct_scr[hd] = ct_new
            n_scr[hd:hd + 1, :] = n_new
            m_scr[hd:hd + 1, :] = jnp.broadcast_to(m_new, (1, LANES))
            mu = jnp.mean(h, axis=-1, keepdims=True)
            hc = h - mu
            var = jnp.mean(hc * hc, axis=-1, keepdims=True)
            o_gate = _sigmoid(p_ref[:, 3 * M_WIDTH + lo:3 * M_WIDTH + lo + M_HEAD_DIM].astype(F32))
            o_ref[:, lo:lo + M_HEAD_DIM] = (hc * lax.rsqrt(var + 1e-5) * mg_ref[:, lo:lo + M_HEAD_DIM]
                                            * o_gate).astype(o_ref.dtype)

    @pl.when(ci == 0)
    def _():
        ct_scr[...] = jnp.zeros_like(ct_scr)
        n_scr[...] = jnp.zeros_like(n_scr)
        m_scr[...] = jnp.zeros_like(m_scr)
        x_scr[...] = jnp.zeros_like(x_scr)
        run_chunk(pmm_ref, smm_ref, outm_ref)

    run_chunk(pm_ref, sm_ref, out_ref)


def _mlstm(pm, sm, conv_w, mnorm_g, l, batch, seq):
    n_real = batch * seq
    chunk = min(M_CHUNK, seq)
    nc = seq // chunk
    meta_blk = n_real // N_META
    out_real, out_meta = pl.pallas_call(
        _mlstm_kernel,
        grid=(batch, nc),
        in_specs=[pl.BlockSpec((chunk, 4 * M_WIDTH), lambda b, c: (b * nc + c, 0)),
                  pl.BlockSpec((chunk, LANES), lambda b, c: (b * nc + c, 0)),
                  pl.BlockSpec((N_META, 4 * M_WIDTH), lambda b, c: (meta_blk + b, 0)),
                  pl.BlockSpec((N_META, LANES), lambda b, c: (meta_blk + b, 0)),
                  _layer_block(conv_w, l), _layer_block(mnorm_g, l)],
        out_specs=[pl.BlockSpec((chunk, M_WIDTH), lambda b, c: (b * nc + c, 0)),
                   pl.BlockSpec((N_META, M_WIDTH), lambda b, c: (b, 0))],
        out_shape=[jax.ShapeDtypeStruct((n_real, M_WIDTH), BF16),
                   jax.ShapeDtypeStruct((batch * N_META, M_WIDTH), BF16)],
        scratch_shapes=[pltpu.VMEM((M_HEADS, M_HEAD_DIM, M_HEAD_DIM), F32),
                        pltpu.VMEM((8, LANES), F32),
                        pltpu.VMEM((8, LANES), F32),
                        pltpu.VMEM((chunk + 8, 2 * M_WIDTH), F32)],
        compiler_params=_params(("arbitrary", "arbitrary")),
        name="mlstm",
    )(pm, sm, pm, sm, conv_w, mnorm_g)
    return out_real, out_meta


def _rel_bucket_np(dist):
    n = np.maximum(dist, 0)
    nf = np.maximum(n, REL_MAX_EXACT).astype(np.float32)
    large = REL_MAX_EXACT + (np.log(nf / np.float32(REL_MAX_EXACT)) /
                             np.float32(math.log(REL_MAX_DIST / REL_MAX_EXACT))
                             * np.float32(REL_BUCKETS - REL_MAX_EXACT)).astype(np.int32)
    large = np.minimum(large, REL_BUCKETS - 1)
    return np.where(n < REL_MAX_EXACT, n, large).astype(np.int32)


def _bias_tables(rel_bias):
    q = np.arange(LANES)[:, None]
    k = np.arange(LANES)[None, :]
    far = 4 * LANES
    assert (_rel_bucket_np(np.arange(LANES + 1, far)) == REL_BUCKETS - 1).all()
    far_idx = np.full((LANES, LANES), REL_BUCKETS - 1, np.int32)
    near_idx = np.stack([_rel_bucket_np(q - k), _rel_bucket_np(LANES + q - k), far_idx])
    meta_idx = np.stack([_rel_bucket_np(q + N_META - np.minimum(k, N_META - 1)), far_idx])
    mq = np.arange(N_META)[:, None]
    mm_idx = _rel_bucket_np(mq - np.minimum(k, N_META - 1))
    rb = rel_bias.astype(F32)

    def lookup(idx, values):
        onehot = jnp.asarray(idx[..., None] == np.arange(REL_BUCKETS), F32)
        out = jnp.einsum('...b,bh->...h', onehot, values, precision=lax.Precision.HIGHEST)
        return jnp.moveaxis(out, -1, -3)

    rel_values = (rb - rb[REL_BUCKETS - 1][None, :]) * LOG2E
    return lookup(near_idx, rel_values), lookup(meta_idx, rel_values), lookup(mm_idx, rb)


def _dsa_kernel(top_k, qa_ref, qi_ref, sm_ref, cb_ref, smb_ref, cm_ref, qam_ref,
                wuk_ref, wuv_ref, near_ref, metab_ref, mmb_ref,
                out_ref, outm_ref,
                keys_scr, hi_scr, lo_scr, lg_scr, lgm_scr, rawa_scr, rawb_scr, pa_scr, pb_scr,
                caug_scr, ct_scr, kt_scr, cmaug_scr, cmt_scr, qs_scr, mx_scr, mrep_scr, acc_scr):
    i = pl.program_id(1)
    T = LANES
    H = A_HEADS
    col = lax.broadcasted_iota(jnp.int32, (T, T), 1)
    row = lax.broadcasted_iota(jnp.int32, (T, T), 0)
    nt = (((1,), (1,)), ((), ()))

    def q_latent(qa, hd, scale):
        ql = jnp.dot(qa[:, hd * A_HEAD_DIM:(hd + 1) * A_HEAD_DIM], wuk_ref[hd],
                     preferred_element_type=F32)
        return (ql * scale).astype(BF16)

    def ones_column(n):
        return jnp.where(lax.broadcasted_iota(jnp.int32, (n, T), 1) == 0, 1.0, 0.0).astype(BF16)

    @pl.when(i == 0)
    def _():
        caug_scr[:, 0:KV_RANK] = cb_ref[...].astype(BF16)
        caug_scr[:, KV_RANK:KV_RANK + T] = ones_column(caug_scr.shape[0])

        def transpose_keys(blk, carry):
            rows = pl.ds(pl.multiple_of(blk * KEY_SUB * T, KEY_SUB * T), KEY_SUB * T)
            ct_scr[blk] = cb_ref[rows, :].T.astype(BF16)
            kt_scr[blk] = smb_ref[rows, :].T[SM_KIDX:SM_KIDX + IDX_DIM, :].astype(BF16)
            return carry

        lax.fori_loop(0, ct_scr.shape[0], transpose_keys, 0)
        cm_pad = jnp.concatenate([cm_ref[...], jnp.zeros((T - N_META, KV_RANK), F32)], axis=0)
        cmaug_scr[:, 0:KV_RANK] = cm_pad.astype(BF16)
        cmt_scr[...] = cm_pad.T.astype(BF16)
        cmaug_scr[:, KV_RANK:KV_RANK + T] = ones_column(T)
        cmk = cmaug_scr[:, 0:KV_RANK]
        qam = qam_ref[...]
        mrow = lax.broadcasted_iota(jnp.int32, (N_META, T), 0)
        mcol = lax.broadcasted_iota(jnp.int32, (N_META, T), 1)
        for hd in range(H):
            lg = lax.dot_general(q_latent(qam, hd, A_HEAD_DIM ** -0.5), cmk, nt,
                                 preferred_element_type=F32) + mmb_ref[hd]
            lg = jnp.where(mcol <= mrow, lg, NEG_BIG)
            p = jnp.exp(lg - jnp.max(lg, axis=1, keepdims=True))
            p = p / jnp.sum(p, axis=1, keepdims=True)
            o = jnp.dot(p.astype(BF16), cmk, preferred_element_type=F32)
            outm_ref[:, hd * A_HEAD_DIM:(hd + 1) * A_HEAD_DIM] = jnp.dot(
                o.astype(BF16), wuv_ref[hd], preferred_element_type=F32).astype(outm_ref.dtype)

    qa = qa_ref[...]
    for hd in range(H):
        qs_scr[hd * T:(hd + 1) * T, :] = q_latent(qa, hd, A_HEAD_DIM ** -0.5 * LOG2E)
    qi = qi_ref[...]
    wv = sm_ref[:, SM_WIDX:SM_WIDX + IDX_HEADS] * IDX_SCALE
    t_col = i * T + lax.broadcasted_iota(jnp.int32, (T, 1), 0)
    n_chunks = (i + SCORE_CHUNK) // SCORE_CHUNK
    CW = SCORE_CHUNK * T

    def run_pipelined(n, produce, consume, buf_a, buf_b):
        last = n - 1
        produce(0, buf_a)

        def body(t, carry):
            s = 2 * t
            produce(jnp.minimum(s + 1, last), buf_b)
            consume(s, buf_a)
            produce(jnp.minimum(s + 2, last), buf_a)
            consume(s + 1, buf_b)
            return carry

        lax.fori_loop(0, n // 2, body, 0)

        @pl.when(n % 2 == 1)
        def _():
            consume(last, buf_a)

    def score_matmul(cix, buf):
        kc_t = kt_scr[cix]
        for hh in range(IDX_HEADS):
            buf[hh * T:(hh + 1) * T, :] = jnp.dot(
                qi[:, hh * IDX_DIM:(hh + 1) * IDX_DIM], kc_t, preferred_element_type=F32)

    def score_keys(cix, buf):
        acc = jnp.zeros((T, CW), F32)
        for hh in range(IDX_HEADS):
            acc = acc + wv[:, hh:hh + 1] * jnp.maximum(buf[hh * T:(hh + 1) * T, :], 0.0)
        acc = jnp.where(acc == 0.0, 0.0, acc)
        bits = lax.bitcast_convert_type(acc, jnp.int32)
        key = jnp.where(bits < 0, bits ^ jnp.int32(0x7FFFFFFF), bits)
        s_idx = cix * CW + lax.broadcasted_iota(jnp.int32, (T, CW), 1)
        key = jnp.where(s_idx <= t_col, key, jnp.int32(INT_MIN))
        for u in range(SCORE_CHUNK):
            tile = key[:, u * T:(u + 1) * T]
            keys_scr[cix * SCORE_CHUNK + u] = tile
            tile_t = tile.T
            hi_scr[cix * SCORE_CHUNK + u] = lax.shift_right_arithmetic(tile_t, 16).astype(jnp.int16)
            lo_scr[cix * SCORE_CHUNK + u] = ((tile_t & 0xFFFF) - HALF_BIAS).astype(jnp.int16)

    run_pipelined(n_chunks, score_matmul, score_keys, rawa_scr, rawb_scr)

    def rep16(row_i32):
        return jnp.broadcast_to(row_i32, (T, T)).astype(jnp.int16)

    def count16(src_scr, pred_fn):
        def body(cix, cnt):
            for u in range(SCORE_CHUNK):
                hit = pred_fn(src_scr[cix * SCORE_CHUNK + u])
                cnt = cnt + jnp.where(hit, jnp.int16(1), jnp.int16(0))
            return cnt
        cnt = lax.fori_loop(0, n_chunks, body, jnp.zeros((T, T), jnp.int16))
        return jnp.sum(cnt.astype(F32), axis=0, keepdims=True)

    def search16(src_scr, k_row):
        def bit_body(bi, carry):
            ans, above = carry
            cand_u = ans | lax.shift_left(jnp.int32(1), jnp.int32(15) - bi)
            cand = rep16(cand_u - HALF_BIAS)
            total = count16(src_scr, lambda x: x >= cand)
            take = total >= k_row
            return jnp.where(take, cand_u, ans), jnp.where(take, above, total)
        return lax.fori_loop(0, 16, bit_body, (jnp.zeros((1, T), jnp.int32), jnp.zeros((1, T), F32)))

    k_row = jnp.full((1, T), float(top_k), F32)
    hi_u, hi_above = search16(hi_scr, k_row)
    hi_s = hi_u - HALF_BIAS
    hi_rep = rep16(hi_s)
    k_low = k_row - hi_above

    def band_body(cix, carry):
        for u in range(SCORE_CHUNK):
            j = cix * SCORE_CHUNK + u
            lo_scr[j] = jnp.where(hi_scr[j] == hi_rep, lo_scr[j], jnp.int16(-HALF_BIAS))
        return carry

    lax.fori_loop(0, n_chunks, band_body, 0)
    lo_u, lo_above = search16(lo_scr, k_low)
    need_row = k_low - lo_above
    thr_row = lax.shift_left(hi_s, 16) | lo_u
    thr = jnp.broadcast_to(thr_row, (T, T)).T
    need = jnp.broadcast_to(need_row, (T, T)).T

    mx_scr[...] = jnp.full_like(mx_scr, NEG_BIG)
    acc_scr[...] = jnp.zeros_like(acc_scr)
    hg = H // ATT_GROUPS
    groups = [slice(g * hg * T, (g + 1) * hg * T) for g in range(ATT_GROUPS)]
    meta_sel = jnp.minimum(i, 1)

    def max_pass(ck_t, madds, bias_fns, store):
        lgs = [jnp.dot(qs_scr[rs, :], ck_t, preferred_element_type=F32) for rs in groups]
        for hd in range(H):
            rs = slice(hd * T, (hd + 1) * T)
            lo = (hd % hg) * T
            mx = mx_scr[rs, :]
            for u in range(len(madds)):
                x = lgs[hd // hg][lo:lo + T, u * T:(u + 1) * T] + madds[u]
                if bias_fns[u] is not None:
                    x = x + bias_fns[u](hd)
                store(rs, u, x)
                mx = jnp.maximum(mx, x)
            mx_scr[rs, :] = mx

    def sum_pass(c_aug, n_sub, load):
        for grp in groups:
            m_rep = mrep_scr[grp, :]
            ph = [jnp.exp2(load(grp, u) - m_rep).astype(BF16) for u in range(n_sub)]
            p = ph[0] if n_sub == 1 else jnp.concatenate(ph, axis=1)
            acc_scr[grp, :] += jnp.dot(p, c_aug, preferred_element_type=F32)

    def key_rows(step):
        return pl.ds(pl.multiple_of(step * KEY_SUB * T, KEY_SUB * T), KEY_SUB * T)

    upper = (row < col).astype(BF16)

    def mask_step(step, seen, near):
        madds, bias_fns = [], []
        for u in range(KEY_SUB):
            j = KEY_SUB * step + u
            kk = keys_scr[j]
            eq = kk == thr
            eqf = jnp.where(eq, 1.0, 0.0)
            before = jnp.dot(eqf.astype(BF16), upper, preferred_element_type=F32) + seen
            sel = (kk > thr) | (eq & (before < need))
            if near:
                sel = sel & ((j * T + col) <= (i * T + row))
                dsel = jnp.clip(i - j, 0, 2)
                bias_fns.append(lambda hd, dsel=dsel: near_ref[dsel, hd])
            else:
                bias_fns.append(None)
            seen = seen + jnp.sum(eqf, axis=1, keepdims=True)
            madds.append(jnp.where(sel, 0.0, NEG_BIG))

        def store(rs, u, x):
            lg_scr[step, rs, u * T:(u + 1) * T] = x

        max_pass(ct_scr[step], madds, bias_fns, store)
        return seen

    def weights_step(step, buf):
        for grp in groups:
            m_rep = mrep_scr[grp, :]
            for u in range(KEY_SUB):
                buf[grp, u * T:(u + 1) * T] = jnp.exp2(
                    lg_scr[step, grp, u * T:(u + 1) * T] - m_rep).astype(BF16)

    def accumulate_step(step, buf):
        c_aug = caug_scr[key_rows(step), :]
        for grp in groups:
            acc_scr[grp, :] += jnp.dot(buf[grp, :], c_aug, preferred_element_type=F32)

    def store_meta(rs, u, x):
        lgm_scr[rs, :] = x

    n_far = jnp.maximum(i - 1, 0) // KEY_SUB
    n_steps = (i + KEY_SUB) // KEY_SUB
    max_pass(cmt_scr[...], [jnp.where(col < N_META, 0.0, NEG_BIG)],
             [lambda hd: metab_ref[meta_sel, hd]], store_meta)
    seen = lax.fori_loop(0, n_far, lambda s, c: mask_step(s, c, False), jnp.zeros((T, 1), F32))
    lax.fori_loop(n_far, n_steps, lambda s, c: mask_step(s, c, True), seen)
    mrep_scr[...] = jnp.broadcast_to(jnp.max(mx_scr[...], axis=1, keepdims=True), mrep_scr.shape)
    sum_pass(cmaug_scr[...], 1, lambda grp, u: lgm_scr[grp, :])
    def sum_step(step, carry):
        weights_step(step, pa_scr)
        accumulate_step(step, pa_scr)
        return carry

    lax.fori_loop(0, n_steps, sum_step, 0)

    for hd in range(H):
        rs = slice(hd * T, (hd + 1) * T)
        o = acc_scr[rs, 0:KV_RANK] / acc_scr[rs, KV_RANK:KV_RANK + 1]
        out_ref[:, hd * A_HEAD_DIM:(hd + 1) * A_HEAD_DIM] = jnp.dot(
            o.astype(BF16), wuv_ref[hd], preferred_element_type=F32).astype(out_ref.dtype)


def _dsa(qa, qi, sm, c, wuk_t, wuv, tables, l, batch, seq):
    n_real = batch * seq
    nq = seq // LANES
    n_tiles = ((nq + SCORE_CHUNK - 1) // SCORE_CHUNK) * SCORE_CHUNK
    top_k = min(TOPK_MAX, seq // 4)
    meta_blk = n_real // N_META
    near, metab, mmb = tables
    full = lambda a: pl.BlockSpec(a.shape, lambda b, i: (0,) * a.ndim)
    assert seq % (SCORE_CHUNK * LANES) == 0 and nq % KEY_SUB == 0 and SCORE_CHUNK == KEY_SUB
    out_real, out_meta = pl.pallas_call(
        functools.partial(_dsa_kernel, top_k),
        grid=(batch, nq),
        in_specs=[pl.BlockSpec((LANES, A_WIDTH), lambda b, i: (b * nq + i, 0)),
                  pl.BlockSpec((LANES, IDX_HEADS * IDX_DIM), lambda b, i: (b * nq + i, 0)),
                  pl.BlockSpec((LANES, LANES), lambda b, i: (b * nq + i, 0)),
                  pl.BlockSpec((seq, KV_RANK), lambda b, i: (b, 0)),
                  pl.BlockSpec((seq, LANES), lambda b, i: (b, 0)),
                  pl.BlockSpec((N_META, KV_RANK), lambda b, i: (meta_blk + b, 0)),
                  pl.BlockSpec((N_META, A_WIDTH), lambda b, i: (meta_blk + b, 0)),
                  _layer_block(wuk_t, l), _layer_block(wuv, l), full(near), full(metab), full(mmb)],
        out_specs=[pl.BlockSpec((LANES, A_WIDTH), lambda b, i: (b * nq + i, 0)),
                   pl.BlockSpec((N_META, A_WIDTH), lambda b, i: (b, 0))],
        out_shape=[jax.ShapeDtypeStruct((n_real, A_WIDTH), BF16),
                   jax.ShapeDtypeStruct((batch * N_META, A_WIDTH), BF16)],
        scratch_shapes=[pltpu.VMEM((n_tiles, LANES, LANES), jnp.int32),
                        pltpu.VMEM((n_tiles, LANES, LANES), jnp.int16),
                        pltpu.VMEM((n_tiles, LANES, LANES), jnp.int16),
                        pltpu.VMEM((nq // KEY_SUB, A_HEADS * LANES, KEY_SUB * LANES), F32),
                        pltpu.VMEM((A_HEADS * LANES, LANES), F32),
                        pltpu.VMEM((IDX_HEADS * LANES, SCORE_CHUNK * LANES), F32),
                        pltpu.VMEM((IDX_HEADS * LANES, SCORE_CHUNK * LANES), F32),
                        pltpu.VMEM((A_HEADS * LANES, KEY_SUB * LANES), BF16),
                        pltpu.VMEM((A_HEADS * LANES, KEY_SUB * LANES), BF16),
                        pltpu.VMEM((seq, KV_RANK + LANES), BF16),
                        pltpu.VMEM((nq // KEY_SUB, KV_RANK, KEY_SUB * LANES), BF16),
                        pltpu.VMEM((nq // KEY_SUB, IDX_DIM, KEY_SUB * LANES), BF16),
                        pltpu.VMEM((LANES, KV_RANK + LANES), BF16),
                        pltpu.VMEM((KV_RANK, LANES), BF16),
                        pltpu.VMEM((A_HEADS * LANES, KV_RANK), BF16),
                        pltpu.VMEM((A_HEADS * LANES, LANES), F32),
                        pltpu.VMEM((A_HEADS * LANES, LANES), F32),
                        pltpu.VMEM((A_HEADS * LANES, KV_RANK + LANES), F32)],
        compiler_params=_params(("arbitrary", "arbitrary")),
        name="dsa",
    )(qa, qi, sm, c, sm, c, qa, wuk_t, wuv, near, metab, mmb)
    return out_real, out_meta


def _merge_kernel(alpha, h_ref, hm_ref, ha_ref, g_ref, wbm_ref, wba_ref, wo_ref, lg_ref, lb_ref,
                  wr_ref, br_ref, h1_ref, comb_ref, bgt_ref, cnt_ref):
    d = h_ref.shape[1]
    gm = _sigmoid(g_ref[:, 0:d].astype(F32))
    ga = _sigmoid(g_ref[:, d:2 * d].astype(F32))
    y = gm * jnp.dot(hm_ref[...], wbm_ref[...], preferred_element_type=F32) + \
        ga * jnp.dot(ha_ref[...], wba_ref[...], preferred_element_type=F32)
    z = alpha * h_ref[...] + jnp.dot(y.astype(BF16), wo_ref[...], preferred_element_type=F32)
    h1 = _layer_norm_rows(z, lg_ref[...], lb_ref[...], 1e-5)
    h1_ref[...] = h1

    tm = h1.shape[0]
    logits_t = lax.dot_general(wr_ref[...], h1.astype(BF16), (((1,), (1,)), ((), ())),
                               preferred_element_type=F32)
    scores = _sigmoid(logits_t[0:N_EXPERTS, :])
    sel = scores + br_ref[0:N_EXPERTS, :]
    best = None
    for gidx in range(N_GROUPS):
        r0, r1, r2, r3 = (sel[gidx * GROUP_SIZE + u:gidx * GROUP_SIZE + u + 1, :] for u in range(4))
        a, b = jnp.maximum(r0, r1), jnp.minimum(r0, r1)
        c, dd = jnp.maximum(r2, r3), jnp.minimum(r2, r3)
        gs = jnp.maximum(a, c) + jnp.maximum(jnp.minimum(a, c), jnp.maximum(b, dd))
        if best is None:
            best, bg = gs, jnp.zeros((1, tm), jnp.int32)
        else:
            upd = gs > best
            bg = jnp.where(upd, gidx, bg)
            best = jnp.where(upd, gs, best)
    eidx = lax.broadcasted_iota(jnp.int32, (N_EXPERTS, tm), 0)
    masked = jnp.where((eidx // GROUP_SIZE) == bg, sel, -jnp.inf)
    v1 = jnp.max(masked, axis=0, keepdims=True)
    i1 = jnp.min(jnp.where(masked == v1, eidx, N_EXPERTS), axis=0, keepdims=True)
    masked2 = jnp.where(eidx == i1, -jnp.inf, masked)
    v2 = jnp.max(masked2, axis=0, keepdims=True)
    i2 = jnp.min(jnp.where(masked2 == v2, eidx, N_EXPERTS), axis=0, keepdims=True)
    s1 = jnp.sum(jnp.where(eidx == i1, scores, 0.0), axis=0, keepdims=True)
    s2 = jnp.sum(jnp.where(eidx == i2, scores, 0.0), axis=0, keepdims=True)
    tot = s1 + s2
    comb_t = jnp.where(eidx == i1, s1 / tot, 0.0) + jnp.where(eidx == i2, s2 / tot, 0.0)
    comb_pad = jnp.concatenate([comb_t, jnp.zeros((LANES - N_EXPERTS, tm), F32)], axis=0)
    comb_ref[...] = comb_pad.T
    bgt_ref[...] = jnp.broadcast_to(bg, (8, tm))
    gidx8 = lax.broadcasted_iota(jnp.int32, (8, tm), 0)
    counts = jnp.sum(jnp.where(gidx8 == bg, 1.0, 0.0), axis=1, keepdims=True)
    cnt_ref[0] = jnp.broadcast_to(counts, (8, LANES)).astype(jnp.int32)


def _merge(h, hm, ha, g, w_bm, w_ba, w_o, ln_g, ln_b, wr_t, br, l, alpha):
    n, d = h.shape
    tm = MOE_ROW_TILE
    row = lambda width: pl.BlockSpec((tm, width), lambda r: (r, 0))
    full = lambda a: pl.BlockSpec(a.shape, lambda r: (0,) * a.ndim)
    args = (h, hm, ha, g, w_bm, w_ba, w_o, ln_g, ln_b, wr_t, br)
    return pl.pallas_call(
        functools.partial(_merge_kernel, alpha),
        grid=(n // tm,),
        in_specs=[row(d), row(M_WIDTH), row(A_WIDTH), row(2 * d)] +
                 [_layer_block(a, l) for a in args[4:9]] + [full(wr_t), full(br)],
        out_specs=[row(d), row(LANES),
                   pl.BlockSpec((8, tm), lambda r: (0, r)),
                   pl.BlockSpec((1, 8, LANES), lambda r: (r, 0, 0))],
        out_shape=[jax.ShapeDtypeStruct((n, d), F32), jax.ShapeDtypeStruct((n, LANES), F32),
                   jax.ShapeDtypeStruct((8, n), jnp.int32),
                   jax.ShapeDtypeStruct((n // tm, 8, LANES), jnp.int32)],
        compiler_params=_params(("parallel",)),
        name="merge",
    )(*args)


def _moe_kernel(alpha, cap, cnt_ref, h_ref, comb_ref, bgt_ref, wg_ref, wu_ref, wd_ref, lg_ref, lb_ref,
                out_ref, xb_scr, cs_scr, yt_scr, tri_scr):
    r = pl.program_id(0)
    g = pl.program_id(1)
    rows = h_ref.shape[0]

    @pl.when((r == 0) & (g == 0))
    def _():
        t0 = lax.broadcasted_iota(jnp.int32, (rows, rows), 0)
        t1 = lax.broadcasted_iota(jnp.int32, (rows, rows), 1)
        tri_scr[...] = (t0 < t1).astype(BF16)

    @pl.when(g == 0)
    def _():
        xb_scr[...] = h_ref[...].astype(BF16)
        yt_scr[...] = jnp.zeros_like(yt_scr)
        c = comb_ref[...]
        for part in range(2):
            cb = c.astype(BF16)
            cs_scr[part] = cb
            c = c - cb.astype(F32)

    member = bgt_ref[0:1, :] == g
    mem8 = jnp.broadcast_to(jnp.where(member, 1.0, 0.0), (8, rows)).astype(BF16)
    rank = jnp.dot(mem8, tri_scr[...], preferred_element_type=F32)[0:1, :].astype(jnp.int32)
    n_blocks = (cnt_ref[r * N_GROUPS + g] + cap - 1) // cap
    lane = lax.broadcasted_iota(jnp.int32, (cap, LANES), 1)
    tn = (((0,), (0,)), ((), ()))

    def block(b, carry):
        slot = lax.broadcasted_iota(jnp.int32, (cap, rows), 0) + b * cap
        onehot = jnp.where(member & (rank == slot), 1.0, 0.0).astype(BF16)
        xg = jnp.dot(onehot, xb_scr[...], preferred_element_type=F32).astype(BF16)
        cw = jnp.dot(onehot, cs_scr[0], preferred_element_type=F32)
        cw = cw + jnp.dot(onehot, cs_scr[1], preferred_element_type=F32)
        y = jnp.zeros((cap, out_ref.shape[1]), F32)
        for e in range(GROUP_SIZE):
            gate = jnp.dot(xg, wg_ref[e], preferred_element_type=F32)
            up = jnp.dot(xg, wu_ref[e], preferred_element_type=F32)
            he = gate * _sigmoid(gate) * up
            o = jnp.dot(he.astype(BF16), wd_ref[e], preferred_element_type=F32)
            ce = jnp.sum(jnp.where(lane == g * GROUP_SIZE + e, cw, 0.0), axis=1, keepdims=True)
            y = y + ce * o
        yt_scr[...] += lax.dot_general(onehot, y.astype(BF16), tn, preferred_element_type=F32)
        return carry

    lax.fori_loop(0, n_blocks, block, 0)

    @pl.when(g == pl.num_programs(1) - 1)
    def _():
        z = alpha * h_ref[...] + yt_scr[...]
        out_ref[...] = _layer_norm_rows(z, lg_ref[...], lb_ref[...], 1e-5)


def _moe(h, comb, bgt, counts, w_gate, w_up, w_down, ln_g, ln_b, l, alpha):
    n, d = h.shape
    de = w_gate.shape[-1]
    tm = MOE_ROW_TILE
    cnt = counts[:, 0:N_GROUPS, 0].reshape(-1)
    grid_spec = pltpu.PrefetchScalarGridSpec(
        num_scalar_prefetch=1,
        grid=(n // tm, N_GROUPS),
        in_specs=[pl.BlockSpec((tm, d), lambda r, g, c: (r, 0)),
                  pl.BlockSpec((tm, LANES), lambda r, g, c: (r, 0)),
                  pl.BlockSpec((8, tm), lambda r, g, c: (0, r)),
                  pl.BlockSpec((None, GROUP_SIZE, d, de), lambda r, g, c: (l, g, 0, 0)),
                  pl.BlockSpec((None, GROUP_SIZE, d, de), lambda r, g, c: (l, g, 0, 0)),
                  pl.BlockSpec((None, GROUP_SIZE, de, d), lambda r, g, c: (l, g, 0, 0)),
                  _layer_block(ln_g, l), _layer_block(ln_b, l)],
        out_specs=pl.BlockSpec((tm, d), lambda r, g, c: (r, 0)),
        scratch_shapes=[pltpu.VMEM((tm, d), BF16), pltpu.VMEM((2, tm, LANES), BF16),
                        pltpu.VMEM((tm, d), F32), pltpu.VMEM((tm, tm), BF16)])
    return pl.pallas_call(
        functools.partial(_moe_kernel, alpha, MOE_CAP),
        grid_spec=grid_spec,
        out_shape=jax.ShapeDtypeStruct((n, d), F32),
        compiler_params=_params(("arbitrary", "arbitrary")),
        name="moe",
    )(cnt, h, comb, bgt, w_gate, w_up, w_down, ln_g, ln_b)


def _with_meta(real, meta, n_pad):
    pad = n_pad - real.shape[0] - meta.shape[0]
    return jnp.concatenate([real, meta, jnp.zeros((pad, real.shape[1]), real.dtype)], axis=0)


def kernel(x, meta_tokens, ln_in_g, ln_in_b, w_in, conv_w, b_if, mnorm_g, kv_norm_g, w_uk, w_uv,
           w_branch_m, w_branch_a, w_out, ln1_g, ln1_b, w_router, b_router, w_gate, w_up, w_down,
           ln2_g, ln2_b, rel_bias):
    batch, seq, d = x.shape
    depth = w_in.shape[0]
    alpha = (2 * depth) ** 0.25
    n_real = batch * seq
    n_meta = batch * N_META
    tile = math.lcm(ROW_TILE, MOE_ROW_TILE)
    n_pad = -(-(n_real + n_meta) // tile) * tile
    assert seq % LANES == 0 and n_real % N_META == 0

    assert n_real % ROW_TILE == 0
    tail = _with_meta(jnp.tile(meta_tokens.astype(x.dtype), (batch, 1)), jnp.zeros((0, d), x.dtype),
                      n_pad - n_real)
    h = _input_ln(x.reshape(n_real, d), tail, ln_in_g, ln_in_b)
    tables = _bias_tables(rel_bias)
    row3 = lambda a: a.astype(F32)[:, None, :]
    w_packed, brow, kv_g = _pack_w_in(w_in), _pack_b_if(b_if), row3(kv_norm_g)
    conv_f, mnorm = conv_w.astype(F32), row3(mnorm_g)
    wuk_t = jnp.swapaxes(w_uk, 2, 3).astype(BF16)
    wuv = w_uv.astype(BF16)
    w_bm, w_ba, w_o = w_branch_m.astype(BF16), w_branch_a.astype(BF16), w_out.astype(BF16)
    wr_t = jnp.zeros((LANES, d), F32).at[0:N_EXPERTS].set(w_router.T).astype(BF16)
    br = jnp.zeros((LANES, 1), F32).at[0:N_EXPERTS, 0].set(b_router)
    wg, wu, wd = w_gate.astype(BF16), w_up.astype(BF16), w_down.astype(BF16)
    g1, b1, g2, b2 = row3(ln1_g), row3(ln1_b), row3(ln2_g), row3(ln2_b)
    for l in range(depth):
        pm, qa, qi, c, sm, g = _project(h, w_packed, kv_g, brow, l)
        hm_real, hm_meta = _mlstm(pm, sm, conv_f, mnorm, l, batch, seq)
        ha_real, ha_meta = _dsa(qa, qi, sm, c, wuk_t, wuv, tables, l, batch, seq)
        hm = _with_meta(hm_real, hm_meta, n_pad)
        ha = _with_meta(ha_real, ha_meta, n_pad)
        h1, comb, bgt, counts = _merge(h, hm, ha, g, w_bm, w_ba, w_o, g1, b1, wr_t, br, l, alpha)
        h = _moe(h1, comb, bgt, counts, wg, wu, wd, g2, b2, l, alpha)
    return h[:n_real].reshape(batch, seq, d)
```

```python
import functools
import math

import numpy as np
import jax
import jax.numpy as jnp
from jax import lax
from jax.experimental import pallas as pl
from jax.experimental.pallas import tpu as pltpu

F32 = jnp.float32
BF16 = jnp.bfloat16

N_META = 16
M_HEADS = 4
M_HEAD_DIM = 128
M_WIDTH = M_HEADS * M_HEAD_DIM
CONV_WIDTH = 4
A_HEADS = 8
A_HEAD_DIM = 64
A_WIDTH = A_HEADS * A_HEAD_DIM
KV_RANK = 128
IDX_HEADS = 4
IDX_DIM = 64
IDX_SCALE = (IDX_HEADS * IDX_DIM) ** -0.5
TOPK_MAX = 256
REL_BUCKETS = 32
REL_MAX_EXACT = 16
REL_MAX_DIST = 128
N_EXPERTS = 16
N_GROUPS = 4
GROUP_SIZE = N_EXPERTS // N_GROUPS

LANES = 128
ROW_TILE = 256
MOE_ROW_TILE = 768
MOE_CAP = 224
M_CHUNK = 256
SCORE_CHUNK = 4
KEY_SUB = 4
ATT_GROUPS = 2
VMEM_LIMIT = 56 * 1024 * 1024
NEG_BIG = -1e30
INT_MIN = -2 ** 31
HALF_BIAS = 2 ** 15
LOG2E = math.log2(math.e)

PK_PM = 0
PK_QA = 2048
PK_QI = 2560
PK_CKV = 2816
PK_SM = 2944
PK_G = 3072
PK_TOTAL = 5120
SM_KIDX = 0
SM_WIDX = 64
SM_IPRE = 68
SM_FPRE = 72


def _params(sem):
    return pltpu.CompilerParams(dimension_semantics=sem, vmem_limit_bytes=VMEM_LIMIT)


def _sigmoid(x):
    return 1.0 / (1.0 + jnp.exp(-x))


def _layer_norm_rows(x, g, b, eps):
    mu = jnp.mean(x, axis=-1, keepdims=True)
    xc = x - mu
    var = jnp.mean(xc * xc, axis=-1, keepdims=True)
    return xc * lax.rsqrt(var + eps) * g + b


def _ln_kernel(real_tiles, x_ref, m_ref, g_ref, b_ref, o_ref):
    r = pl.program_id(0)

    @pl.when(r < real_tiles)
    def _():
        o_ref[...] = _layer_norm_rows(x_ref[...], g_ref[...], b_ref[...], 1e-5)

    @pl.when(r >= real_tiles)
    def _():
        o_ref[...] = _layer_norm_rows(m_ref[...], g_ref[...], b_ref[...], 1e-5)


def _input_ln(x, tail, g, b):
    n_real, d = x.shape
    real_tiles = n_real // ROW_TILE
    tiles = real_tiles + tail.shape[0] // ROW_TILE
    return pl.pallas_call(
        functools.partial(_ln_kernel, real_tiles),
        grid=(tiles,),
        in_specs=[pl.BlockSpec((ROW_TILE, d), lambda r: (jnp.minimum(r, real_tiles - 1), 0)),
                  pl.BlockSpec((ROW_TILE, d), lambda r: (jnp.maximum(r - real_tiles, 0), 0)),
                  pl.BlockSpec((1, d), lambda r: (0, 0)),
                  pl.BlockSpec((1, d), lambda r: (0, 0))],
        out_specs=pl.BlockSpec((ROW_TILE, d), lambda r: (r, 0)),
        out_shape=jax.ShapeDtypeStruct((tiles * ROW_TILE, d), F32),
        compiler_params=_params(("arbitrary",)),
        name="input_ln",
    )(x, tail, g.reshape(1, d), b.reshape(1, d))


def _proj_kernel(h_ref, w_ref, kvg_ref, brow_ref, pm_ref, qa_ref, qi_ref, c_ref, sm_ref, g_ref):
    x = h_ref[...].astype(BF16)

    def mm(lo, width):
        return jnp.dot(x, w_ref[:, lo:lo + width], preferred_element_type=F32)

    pm_ref[...] = mm(PK_PM, 2048).astype(BF16)
    qa_ref[...] = mm(PK_QA, 512).astype(BF16)
    qi_ref[...] = mm(PK_QI, 256).astype(BF16)
    ckv = mm(PK_CKV, 128)
    c_ref[...] = ckv * lax.rsqrt(jnp.mean(ckv * ckv, axis=-1, keepdims=True) + 1e-6) * kvg_ref[...]
    sm_ref[...] = mm(PK_SM, 128) + brow_ref[...]
    g_ref[...] = mm(PK_G, 2048).astype(BF16)


def _pack_w_in(w):
    cols = [w[..., 0:2048], w[..., 2056:2568], w[..., 2696:2952], w[..., 2568:2696],
            w[..., 2952:3016], w[..., 3016:3020], w[..., 2048:2056],
            jnp.zeros(w.shape[:-1] + (LANES - 76,), w.dtype), w[..., 3020:5068]]
    return jnp.concatenate(cols, axis=-1).astype(BF16)


def _pack_b_if(b_if):
    depth = b_if.shape[0]
    return jnp.concatenate([jnp.zeros((depth, SM_IPRE), F32), b_if.astype(F32),
                            jnp.zeros((depth, LANES - SM_IPRE - 2 * M_HEADS), F32)], axis=1)[:, None, :]


def _layer_block(arr, l):
    zeros = (0,) * (arr.ndim - 1)
    return pl.BlockSpec((None,) + arr.shape[1:], lambda *idx: (l,) + zeros)


def _project(h, w_packed, kv_g, brow, l):
    n, d = h.shape
    row = lambda width: pl.BlockSpec((ROW_TILE, width), lambda r: (r, 0))
    shp = lambda width, dt: jax.ShapeDtypeStruct((n, width), dt)
    return pl.pallas_call(
        _proj_kernel,
        grid=(n // ROW_TILE,),
        in_specs=[row(d), _layer_block(w_packed, l), _layer_block(kv_g, l), _layer_block(brow, l)],
        out_specs=[row(2048), row(512), row(256), row(128), row(128), row(2048)],
        out_shape=[shp(2048, BF16), shp(512, BF16), shp(256, BF16), shp(128, F32), shp(128, F32),
                   shp(2048, BF16)],
        compiler_params=_params(("parallel",)),
        name="in_proj",
    )(h, w_packed, kv_g, brow)


def _log_sigmoid(f):
    return jnp.minimum(f, 0.0) - jnp.log1p(jnp.exp(-jnp.abs(f)))


def _row_from_col(col):
    L = col.shape[0]
    eye = lax.broadcasted_iota(jnp.int32, (L, L), 0) == lax.broadcasted_iota(jnp.int32, (L, L), 1)
    return jnp.sum(jnp.where(eye, jnp.broadcast_to(col, (L, L)), 0.0), axis=0, keepdims=True)


def _mlstm_heads(qs, ks, vs, lis, li_rows, lf_rows, cts, ns, ms):
    L = qs[0].shape[0]
    heads = range(len(qs))
    r = lax.broadcasted_iota(jnp.int32, (L, L), 0)
    c = lax.broadcasted_iota(jnp.int32, (L, L), 1)
    tril = c <= r
    eye = c == r
    nt = (((1,), (1,)), ((), ()))
    tn = (((0,), (0,)), ((), ()))
    b_cols = [jnp.sum(jnp.where(tril, jnp.broadcast_to(lf_rows[h], (L, L)), 0.0), axis=1, keepdims=True)
              for h in heads]
    b_rows = [jnp.sum(jnp.where(eye, jnp.broadcast_to(b_cols[h], (L, L)), 0.0), axis=0, keepdims=True)
              for h in heads]
    qb = [qs[h].astype(BF16) for h in heads]
    kb = [ks[h].astype(BF16) for h in heads]
    vb = [vs[h].astype(BF16) for h in heads]
    qk = [lax.dot_general(qb[h], kb[h], nt, preferred_element_type=F32) for h in heads]
    qc = [jnp.dot(qb[h], cts[h].astype(BF16), preferred_element_type=F32) for h in heads]
    d = [jnp.where(tril, b_cols[h] - b_rows[h] + li_rows[h], -jnp.inf) for h in heads]
    inter = [b_cols[h] + ms[h] for h in heads]
    m_t = [jnp.maximum(inter[h], jnp.max(d[h], axis=1, keepdims=True)) for h in heads]
    a = [jnp.exp(inter[h] - m_t[h]) for h in heads]
    w = [jnp.exp(d[h] - m_t[h]) * qk[h] for h in heads]
    wv = [jnp.dot(w[h].astype(BF16), vb[h], preferred_element_type=F32) for h in heads]
    den = [a[h] * jnp.sum(qs[h] * ns[h], axis=1, keepdims=True) + jnp.sum(w[h], axis=1, keepdims=True)
           for h in heads]
    hs = [(a[h] * qc[h] + wv[h]) / jnp.maximum(jnp.abs(den[h]), jnp.exp(-m_t[h])) for h in heads]
    b_last = [b_cols[h][L - 1:L, :] for h in heads]
    g = [b_last[h] - b_cols[h] + lis[h] for h in heads]
    m_new = [jnp.maximum(b_last[h] + ms[h], jnp.max(g[h], axis=0, keepdims=True)) for h in heads]
    decay = [jnp.exp(b_last[h] + ms[h] - m_new[h]) for h in heads]
    kw = [ks[h] * jnp.exp(g[h] - m_new[h]) for h in heads]
    ct_new = [decay[h] * cts[h] + lax.dot_general(kw[h].astype(BF16), vb[h], tn, preferred_element_type=F32)
              for h in heads]
    n_new = [decay[h] * ns[h] + jnp.sum(kw[h], axis=0, keepdims=True) for h in heads]
    return hs, ct_new, n_new, m_new


def _mlstm_kernel(pm_ref, sm_ref, pmm_ref, smm_ref, cw_ref, mg_ref, out_ref, outm_ref,
                  ct_scr, n_scr, m_scr, x_scr):
    ci = pl.program_id(1)
    tail = 8

    def run_chunk(p_ref, s_ref, o_ref):
        L = p_ref.shape[0]
        heads = range(M_HEADS)
        x_scr[tail:tail + L, :] = p_ref[:, 0:2 * M_WIDTH].astype(F32)
        conv = cw_ref[0:1, :] * x_scr[tail - 3:tail - 3 + L, :]
        for j in range(1, CONV_WIDTH):
            conv = conv + cw_ref[j:j + 1, :] * x_scr[tail - 3 + j:tail - 3 + j + L, :]
        x_scr[0:tail, :] = x_scr[L:L + tail, :]
        qk = conv * _sigmoid(conv)
        cols = lambda base, h: slice(base + h * M_HEAD_DIM, base + (h + 1) * M_HEAD_DIM)
        lis = [s_ref[:, SM_IPRE + h:SM_IPRE + h + 1] for h in heads]
        if L % LANES == 0:
            gates_t = s_ref[...].T
            lf_all = _log_sigmoid(gates_t[SM_FPRE:SM_FPRE + M_HEADS, :])
            li_rows = [gates_t[SM_IPRE + h:SM_IPRE + h + 1, :] for h in heads]
            lf_rows = [lf_all[h:h + 1, :] for h in heads]
        else:
            li_rows = [_row_from_col(lis[h]) for h in heads]
            lf_rows = [_row_from_col(_log_sigmoid(s_ref[:, SM_FPRE + h:SM_FPRE + h + 1])) for h in heads]
        hs, ct_new, n_new, m_new = _mlstm_heads(
            [qk[:, cols(0, h)] * (M_HEAD_DIM ** -0.5) for h in heads],
            [qk[:, cols(M_WIDTH, h)] for h in heads],
            [p_ref[:, cols(2 * M_WIDTH, h)].astype(F32) for h in heads],
            lis, li_rows, lf_rows,
            [ct_scr[h] for h in heads], [n_scr[h:h + 1, :] for h in heads],
            [m_scr[h:h + 1, 0:1] for h in heads])
        for h in heads:
            ct_scr[h] = ct_new[h]
            n_scr[h:h + 1, :] = n_new[h]
            m_scr[h:h + 1, :] = jnp.broadcast_to(m_new[h], (1, LANES))
        mus = [jnp.mean(hs[h], axis=-1, keepdims=True) for h in heads]
        hcs = [hs[h] - mus[h] for h in heads]
        vars_ = [jnp.mean(hcs[h] * hcs[h], axis=-1, keepdims=True) for h in heads]
        for h in heads:
            o_gate = _sigmoid(p_ref[:, cols(3 * M_WIDTH, h)].astype(F32))
            o_ref[:, cols(0, h)] = (hcs[h] * lax.rsqrt(vars_[h] + 1e-5) * mg_ref[:, cols(0, h)]
                                    * o_gate).astype(o_ref.dtype)

    @pl.when(ci == 0)
    def _():
        ct_scr[...] = jnp.zeros_like(ct_scr)
        n_scr[...] = jnp.zeros_like(n_scr)
        m_scr[...] = jnp.zeros_like(m_scr)
        x_scr[...] = jnp.zeros_like(x_scr)
        run_chunk(pmm_ref, smm_ref, outm_ref)

    run_chunk(pm_ref, sm_ref, out_ref)


def _mlstm(pm, sm, conv_w, mnorm_g, l, batch, seq):
    n_real = batch * seq
    chunk = min(M_CHUNK, seq)
    nc = seq // chunk
    meta_blk = n_real // N_META
    out_real, out_meta = pl.pallas_call(
        _mlstm_kernel,
        grid=(batch, nc),
        in_specs=[pl.BlockSpec((chunk, 4 * M_WIDTH), lambda b, c: (b * nc + c, 0)),
                  pl.BlockSpec((chunk, LANES), lambda b, c: (b * nc + c, 0)),
                  pl.BlockSpec((N_META, 4 * M_WIDTH), lambda b, c: (meta_blk + b, 0)),
                  pl.BlockSpec((N_META, LANES), lambda b, c: (meta_blk + b, 0)),
                  _layer_block(conv_w, l), _layer_block(mnorm_g, l)],
        out_specs=[pl.BlockSpec((chunk, M_WIDTH), lambda b, c: (b * nc + c, 0)),
                   pl.BlockSpec((N_META, M_WIDTH), lambda b, c: (b, 0))],
        out_shape=[jax.ShapeDtypeStruct((n_real, M_WIDTH), BF16),
                   jax.ShapeDtypeStruct((batch * N_META, M_WIDTH), BF16)],
        scratch_shapes=[pltpu.VMEM((M_HEADS, M_HEAD_DIM, M_HEAD_DIM), F32),
                        pltpu.VMEM((8, LANES), F32),
                        pltpu.VMEM((8, LANES), F32),
                        pltpu.VMEM((chunk + 8, 2 * M_WIDTH), F32)],
        compiler_params=_params(("arbitrary", "arbitrary")),
        name="mlstm",
    )(pm, sm, pm, sm, conv_w, mnorm_g)
    return out_real, out_meta


def _rel_bucket_np(dist):
    n = np.maximum(dist, 0)
    nf = np.maximum(n, REL_MAX_EXACT).astype(np.float32)
    large = REL_MAX_EXACT + (np.log(nf / np.float32(REL_MAX_EXACT)) /
                             np.float32(math.log(REL_MAX_DIST / REL_MAX_EXACT))
                             * np.float32(REL_BUCKETS - REL_MAX_EXACT)).astype(np.int32)
    large = np.minimum(large, REL_BUCKETS - 1)
    return np.where(n < REL_MAX_EXACT, n, large).astype(np.int32)


def _bias_tables(rel_bias):
    q = np.arange(LANES)[:, None]
    k = np.arange(LANES)[None, :]
    far = 4 * LANES
    assert (_rel_bucket_np(np.arange(LANES + 1, far)) == REL_BUCKETS - 1).all()
    far_idx = np.full((LANES, LANES), REL_BUCKETS - 1, np.int32)
    near_idx = np.stack([_rel_bucket_np(q - k), _rel_bucket_np(LANES + q - k), far_idx])
    meta_idx = np.stack([_rel_bucket_np(q + N_META - np.minimum(k, N_META - 1)), far_idx])
    mq = np.arange(N_META)[:, None]
    mm_idx = _rel_bucket_np(mq - np.minimum(k, N_META - 1))
    rb = rel_bias.astype(F32)

    def lookup(idx, values):
        onehot = jnp.asarray(idx[..., None] == np.arange(REL_BUCKETS), F32)
        out = jnp.einsum('...b,bh->...h', onehot, values, precision=lax.Precision.HIGHEST)
        return jnp.moveaxis(out, -1, -3)

    rel_values = (rb - rb[REL_BUCKETS - 1][None, :]) * LOG2E
    return lookup(near_idx, rel_values), lookup(meta_idx, rel_values), lookup(mm_idx, rb)


def _dsa_kernel(top_k, qa_ref, qi_ref, sm_ref, cb_ref, smb_ref, cm_ref, qam_ref,
                wuk_ref, wuv_ref, near_ref, metab_ref, mmb_ref,
                out_ref, outm_ref,
                keys_scr, hi_scr, lo_scr, lg_scr, lgm_scr, rawa_scr, rawb_scr, pa_scr, pb_scr,
                caug_scr, ct_scr, kt_scr, cmaug_scr, cmt_scr, qs_scr, mx_scr, mrep_scr, acc_scr):
    i = pl.program_id(1)
    T = LANES
    H = A_HEADS
    col = lax.broadcasted_iota(jnp.int32, (T, T), 1)
    row = lax.broadcasted_iota(jnp.int32, (T, T), 0)
    nt = (((1,), (1,)), ((), ()))

    def q_latent(qa, hd, scale):
        ql = jnp.dot(qa[:, hd * A_HEAD_DIM:(hd + 1) * A_HEAD_DIM], wuk_ref[hd],
                     preferred_element_type=F32)
        return (ql * scale).astype(BF16)

    def ones_column(n):
        return jnp.where(lax.broadcasted_iota(jnp.int32, (n, T), 1) == 0, 1.0, 0.0).astype(BF16)

    @pl.when(i == 0)
    def _():
        caug_scr[:, 0:KV_RANK] = cb_ref[...].astype(BF16)
        caug_scr[:, KV_RANK:KV_RANK + T] = ones_column(caug_scr.shape[0])

        def transpose_keys(blk, carry):
            rows = pl.ds(pl.multiple_of(blk * KEY_SUB * T, KEY_SUB * T), KEY_SUB * T)
            ct_scr[blk] = cb_ref[rows, :].T.astype(BF16)
            kt_scr[blk] = smb_ref[rows, :].T[SM_KIDX:SM_KIDX + IDX_DIM, :].astype(BF16)
            return carry

        lax.fori_loop(0, ct_scr.shape[0], transpose_keys, 0)
        cm_pad = jnp.concatenate([cm_ref[...], jnp.zeros((T - N_META, KV_RANK), F32)], axis=0)
        cmaug_scr[:, 0:KV_RANK] = cm_pad.astype(BF16)
        cmt_scr[...] = cm_pad.T.astype(BF16)
        cmaug_scr[:, KV_RANK:KV_RANK + T] = ones_column(T)
        cmk = cmaug_scr[:, 0:KV_RANK]
        qam = qam_ref[...]
        mrow = lax.broadcasted_iota(jnp.int32, (N_META, T), 0)
        mcol = lax.broadcasted_iota(jnp.int32, (N_META, T), 1)
        for hd in range(H):
            lg = lax.dot_general(q_latent(qam, hd, A_HEAD_DIM ** -0.5), cmk, nt,
                                 preferred_element_type=F32) + mmb_ref[hd]
            lg = jnp.where(mcol <= mrow, lg, NEG_BIG)
            p = jnp.exp(lg - jnp.max(lg, axis=1, keepdims=True))
            p = p / jnp.sum(p, axis=1, keepdims=True)
            o = jnp.dot(p.astype(BF16), cmk, preferred_element_type=F32)
            outm_ref[:, hd * A_HEAD_DIM:(hd + 1) * A_HEAD_DIM] = jnp.dot(
                o.astype(BF16), wuv_ref[hd], preferred_element_type=F32).astype(outm_ref.dtype)

    qa = qa_ref[...]
    for hd in range(H):
        qs_scr[hd * T:(hd + 1) * T, :] = q_latent(qa, hd, A_HEAD_DIM ** -0.5 * LOG2E)
    qi = qi_ref[...]
    wv = sm_ref[:, SM_WIDX:SM_WIDX + IDX_HEADS] * IDX_SCALE
    t_col = i * T + lax.broadcasted_iota(jnp.int32, (T, 1), 0)
    n_chunks = (i + SCORE_CHUNK) // SCORE_CHUNK
    CW = SCORE_CHUNK * T

    def run_pipelined(n, produce, consume, buf_a, buf_b):
        last = n - 1
        produce(0, buf_a)

        def body(t, carry):
            s = 2 * t
            produce(jnp.minimum(s + 1, last), buf_b)
            consume(s, buf_a)
            produce(jnp.minimum(s + 2, last), buf_a)
            consume(s + 1, buf_b)
            return carry

        lax.fori_loop(0, n // 2, body, 0)

        @pl.when(n % 2 == 1)
        def _():
            consume(last, buf_a)

    def score_matmul(cix, buf):
        kc_t = kt_scr[cix]
        for hh in range(IDX_HEADS):
            buf[hh * T:(hh + 1) * T, :] = jnp.dot(
                qi[:, hh * IDX_DIM:(hh + 1) * IDX_DIM], kc_t, preferred_element_type=F32)

    def score_keys(cix, buf):
        acc = jnp.zeros((T, CW), F32)
        for hh in range(IDX_HEADS):
            acc = acc + wv[:, hh:hh + 1] * jnp.maximum(buf[hh * T:(hh + 1) * T, :], 0.0)
        acc = jnp.where(acc == 0.0, 0.0, acc)
        bits = lax.bitcast_convert_type(acc, jnp.int32)
        key = jnp.where(bits < 0, bits ^ jnp.int32(0x7FFFFFFF), bits)
        s_idx = cix * CW + lax.broadcasted_iota(jnp.int32, (T, CW), 1)
        key = jnp.where(s_idx <= t_col, key, jnp.int32(INT_MIN))
        for u in range(SCORE_CHUNK):
            tile = key[:, u * T:(u + 1) * T]
            keys_scr[cix * SCORE_CHUNK + u] = tile
            tile_t = tile.T
            hi_scr[cix * SCORE_CHUNK + u] = lax.shift_right_arithmetic(tile_t, 16).astype(jnp.int16)
            lo_scr[cix * SCORE_CHUNK + u] = ((tile_t & 0xFFFF) - HALF_BIAS).astype(jnp.int16)

    run_pipelined(n_chunks, score_matmul, score_keys, rawa_scr, rawb_scr)

    def rep16(row_i32):
        return jnp.broadcast_to(row_i32, (T, T)).astype(jnp.int16)

    def count16(src_scr, pred_fn):
        def body(cix, cnt):
            for u in range(SCORE_CHUNK):
                hit = pred_fn(src_scr[cix * SCORE_CHUNK + u])
                cnt = cnt + jnp.where(hit, jnp.int16(1), jnp.int16(0))
            return cnt
        cnt = lax.fori_loop(0, n_chunks, body, jnp.zeros((T, T), jnp.int16))
        return jnp.sum(cnt.astype(F32), axis=0, keepdims=True)

    def search16(src_scr, k_row):
        def bit_body(bi, carry):
            ans, above = carry
            cand_u = ans | lax.shift_left(jnp.int32(1), jnp.int32(15) - bi)
            cand = rep16(cand_u - HALF_BIAS)
            total = count16(src_scr, lambda x: x >= cand)
            take = total >= k_row
            return jnp.where(take, cand_u, ans), jnp.where(take, above, total)
        return lax.fori_loop(0, 16, bit_body, (jnp.zeros((1, T), jnp.int32), jnp.zeros((1, T), F32)))

    k_row = jnp.full((1, T), float(top_k), F32)
    hi_u, hi_above = search16(hi_scr, k_row)
    hi_s = hi_u - HALF_BIAS
    hi_rep = rep16(hi_s)
    k_low = k_row - hi_above

    def band_body(cix, carry):
        for u in range(SCORE_CHUNK):
            j = cix * SCORE_CHUNK + u
            lo_scr[j] = jnp.where(hi_scr[j] == hi_rep, lo_scr[j], jnp.int16(-HALF_BIAS))
        return carry

    lax.fori_loop(0, n_chunks, band_body, 0)
    lo_u, lo_above = search16(lo_scr, k_low)
    need_row = k_low - lo_above
    thr_row = lax.shift_left(hi_s, 16) | lo_u
    thr = jnp.broadcast_to(thr_row, (T, T)).T
    need = jnp.broadcast_to(need_row, (T, T)).T

    mx_scr[...] = jnp.full_like(mx_scr, NEG_BIG)
    acc_scr[...] = jnp.zeros_like(acc_scr)
    hg = H // ATT_GROUPS
    groups = [slice(g * hg * T, (g + 1) * hg * T) for g in range(ATT_GROUPS)]
    meta_sel = jnp.minimum(i, 1)

    def max_pass(ck_t, madds, bias_fns, store):
        lgs = [jnp.dot(qs_scr[rs, :], ck_t, preferred_element_type=F32) for rs in groups]
        for hd in range(H):
            rs = slice(hd * T, (hd + 1) * T)
            lo = (hd % hg) * T
            mx = mx_scr[rs, :]
            for u in range(len(madds)):
                x = lgs[hd // hg][lo:lo + T, u * T:(u + 1) * T] + madds[u]
                if bias_fns[u] is not None:
                    x = x + bias_fns[u](hd)
                store(rs, u, x)
                mx = jnp.maximum(mx, x)
            mx_scr[rs, :] = mx

    def sum_pass(c_aug, n_sub, load):
        for grp in groups:
            m_rep = mrep_scr[grp, :]
            ph = [jnp.exp2(load(grp, u) - m_rep).astype(BF16) for u in range(n_sub)]
            p = ph[0] if n_sub == 1 else jnp.concatenate(ph, axis=1)
            acc_scr[grp, :] += jnp.dot(p, c_aug, preferred_element_type=F32)

    def key_rows(step):
        return pl.ds(pl.multiple_of(step * KEY_SUB * T, KEY_SUB * T), KEY_SUB * T)

    upper = (row < col).astype(BF16)

    def mask_step(step, seen, near):
        madds, bias_fns = [], []
        for u in range(KEY_SUB):
            j = KEY_SUB * step + u
            kk = keys_scr[j]
            eq = kk == thr
            eqf = jnp.where(eq, 1.0, 0.0)
            before = jnp.dot(eqf.astype(BF16), upper, preferred_element_type=F32) + seen
            sel = (kk > thr) | (eq & (before < need))
            if near:
                sel = sel & ((j * T + col) <= (i * T + row))
                dsel = jnp.clip(i - j, 0, 2)
                bias_fns.append(lambda hd, dsel=dsel: near_ref[dsel, hd])
            else:
                bias_fns.append(None)
            seen = seen + jnp.sum(eqf, axis=1, keepdims=True)
            madds.append(jnp.where(sel, 0.0, NEG_BIG))

        def store(rs, u, x):
            lg_scr[step, rs, u * T:(u + 1) * T] = x

        max_pass(ct_scr[step], madds, bias_fns, store)
        return seen

    def weights_step(step, buf):
        for grp in groups:
            m_rep = mrep_scr[grp, :]
            for u in range(KEY_SUB):
                buf[grp, u * T:(u + 1) * T] = jnp.exp2(
                    lg_scr[step, grp, u * T:(u + 1) * T] - m_rep).astype(BF16)

    def accumulate_step(step, buf):
        c_aug = caug_scr[key_rows(step), :]
        for grp in groups:
            acc_scr[grp, :] += jnp.dot(buf[grp, :], c_aug, preferred_element_type=F32)

    def store_meta(rs, u, x):
        lgm_scr[rs, :] = x

    n_far = jnp.maximum(i - 1, 0) // KEY_SUB
    n_steps = (i + KEY_SUB) // KEY_SUB
    max_pass(cmt_scr[...], [jnp.where(col < N_META, 0.0, NEG_BIG)],
             [lambda hd: metab_ref[meta_sel, hd]], store_meta)
    seen = lax.fori_loop(0, n_far, lambda s, c: mask_step(s, c, False), jnp.zeros((T, 1), F32))
    lax.fori_loop(n_far, n_steps, lambda s, c: mask_step(s, c, True), seen)
    mrep_scr[...] = jnp.broadcast_to(jnp.max(mx_scr[...], axis=1, keepdims=True), mrep_scr.shape)
    sum_pass(cmaug_scr[...], 1, lambda grp, u: lgm_scr[grp, :])
    def sum_step(step, carry):
        weights_step(step, pa_scr)
        accumulate_step(step, pa_scr)
        return carry

    lax.fori_loop(0, n_steps, sum_step, 0)

    for hd in range(H):
        rs = slice(hd * T, (hd + 1) * T)
        o = acc_scr[rs, 0:KV_RANK] / acc_scr[rs, KV_RANK:KV_RANK + 1]
        out_ref[:, hd * A_HEAD_DIM:(hd + 1) * A_HEAD_DIM] = jnp.dot(
            o.astype(BF16), wuv_ref[hd], preferred_element_type=F32).astype(out_ref.dtype)


def _dsa(qa, qi, sm, c, wuk_t, wuv, tables, l, batch, seq):
    n_real = batch * seq
    nq = seq // LANES
    n_tiles = ((nq + SCORE_CHUNK - 1) // SCORE_CHUNK) * SCORE_CHUNK
    top_k = min(TOPK_MAX, seq // 4)
    meta_blk = n_real // N_META
    near, metab, mmb = tables
    full = lambda a: pl.BlockSpec(a.shape, lambda b, i: (0,) * a.ndim)
    assert seq % (SCORE_CHUNK * LANES) == 0 and nq % KEY_SUB == 0 and SCORE_CHUNK == KEY_SUB
    out_real, out_meta = pl.pallas_call(
        functools.partial(_dsa_kernel, top_k),
        grid=(batch, nq),
        in_specs=[pl.BlockSpec((LANES, A_WIDTH), lambda b, i: (b * nq + i, 0)),
                  pl.BlockSpec((LANES, IDX_HEADS * IDX_DIM), lambda b, i: (b * nq + i, 0)),
                  pl.BlockSpec((LANES, LANES), lambda b, i: (b * nq + i, 0)),
                  pl.BlockSpec((seq, KV_RANK), lambda b, i: (b, 0)),
                  pl.BlockSpec((seq, LANES), lambda b, i: (b, 0)),
                  pl.BlockSpec((N_META, KV_RANK), lambda b, i: (meta_blk + b, 0)),
                  pl.BlockSpec((N_META, A_WIDTH), lambda b, i: (meta_blk + b, 0)),
                  _layer_block(wuk_t, l), _layer_block(wuv, l), full(near), full(metab), full(mmb)],
        out_specs=[pl.BlockSpec((LANES, A_WIDTH), lambda b, i: (b * nq + i, 0)),
                   pl.BlockSpec((N_META, A_WIDTH), lambda b, i: (b, 0))],
        out_shape=[jax.ShapeDtypeStruct((n_real, A_WIDTH), BF16),
                   jax.ShapeDtypeStruct((batch * N_META, A_WIDTH), BF16)],
        scratch_shapes=[pltpu.VMEM((n_tiles, LANES, LANES), jnp.int32),
                        pltpu.VMEM((n_tiles, LANES, LANES), jnp.int16),
                        pltpu.VMEM((n_tiles, LANES, LANES), jnp.int16),
                        pltpu.VMEM((nq // KEY_SUB, A_HEADS * LANES, KEY_SUB * LANES), F32),
                        pltpu.VMEM((A_HEADS * LANES, LANES), F32),
                        pltpu.VMEM((IDX_HEADS * LANES, SCORE_CHUNK * LANES), F32),
                        pltpu.VMEM((IDX_HEADS * LANES, SCORE_CHUNK * LANES), F32),
                        pltpu.VMEM((A_HEADS * LANES, KEY_SUB * LANES), BF16),
                        pltpu.VMEM((A_HEADS * LANES, KEY_SUB * LANES), BF16),
                        pltpu.VMEM((seq, KV_RANK + LANES), BF16),
                        pltpu.VMEM((nq // KEY_SUB, KV_RANK, KEY_SUB * LANES), BF16),
                        pltpu.VMEM((nq // KEY_SUB, IDX_DIM, KEY_SUB * LANES), BF16),
                        pltpu.VMEM((LANES, KV_RANK + LANES), BF16),
                        pltpu.VMEM((KV_RANK, LANES), BF16),
                        pltpu.VMEM((A_HEADS * LANES, KV_RANK), BF16),
                        pltpu.VMEM((A_HEADS * LANES, LANES), F32),
                        pltpu.VMEM((A_HEADS * LANES, LANES), F32),
                        pltpu.VMEM((A_HEADS * LANES, KV_RANK + LANES), F32)],
        compiler_params=_params(("arbitrary", "arbitrary")),
        name="dsa",
    )(qa, qi, sm, c, sm, c, qa, wuk_t, wuv, near, metab, mmb)
    return out_real, out_meta


def _merge_kernel(alpha, h_ref, hm_ref, ha_ref, g_ref, wbm_ref, wba_ref, wo_ref, lg_ref, lb_ref,
                  wr_ref, br_ref, h1_ref, comb_ref, bgt_ref, cnt_ref):
    d = h_ref.shape[1]
    gm = _sigmoid(g_ref[:, 0:d].astype(F32))
    ga = _sigmoid(g_ref[:, d:2 * d].astype(F32))
    y = gm * jnp.dot(hm_ref[...], wbm_ref[...], preferred_element_type=F32) + \
        ga * jnp.dot(ha_ref[...], wba_ref[...], preferred_element_type=F32)
    z = alpha * h_ref[...] + jnp.dot(y.astype(BF16), wo_ref[...], preferred_element_type=F32)
    h1 = _layer_norm_rows(z, lg_ref[...], lb_ref[...], 1e-5)
    h1_ref[...] = h1

    tm = h1.shape[0]
    logits_t = lax.dot_general(wr_ref[...], h1.astype(BF16), (((1,), (1,)), ((), ())),
                               preferred_element_type=F32)
    scores = _sigmoid(logits_t[0:N_EXPERTS, :])
    sel = scores + br_ref[0:N_EXPERTS, :]
    best = None
    for gidx in range(N_GROUPS):
        r0, r1, r2, r3 = (sel[gidx * GROUP_SIZE + u:gidx * GROUP_SIZE + u + 1, :] for u in range(4))
        a, b = jnp.maximum(r0, r1), jnp.minimum(r0, r1)
        c, dd = jnp.maximum(r2, r3), jnp.minimum(r2, r3)
        gs = jnp.maximum(a, c) + jnp.maximum(jnp.minimum(a, c), jnp.maximum(b, dd))
        if best is None:
            best, bg = gs, jnp.zeros((1, tm), jnp.int32)
        else:
            upd = gs > best
            bg = jnp.where(upd, gidx, bg)
            best = jnp.where(upd, gs, best)
    eidx = lax.broadcasted_iota(jnp.int32, (N_EXPERTS, tm), 0)
    masked = jnp.where((eidx // GROUP_SIZE) == bg, sel, -jnp.inf)
    v1 = jnp.max(masked, axis=0, keepdims=True)
    i1 = jnp.min(jnp.where(masked == v1, eidx, N_EXPERTS), axis=0, keepdims=True)
    masked2 = jnp.where(eidx == i1, -jnp.inf, masked)
    v2 = jnp.max(masked2, axis=0, keepdims=True)
    i2 = jnp.min(jnp.where(masked2 == v2, eidx, N_EXPERTS), axis=0, keepdims=True)
    s1 = jnp.sum(jnp.where(eidx == i1, scores, 0.0), axis=0, keepdims=True)
    s2 = jnp.sum(jnp.where(eidx == i2, scores, 0.0), axis=0, keepdims=True)
    tot = s1 + s2
    comb_t = jnp.where(eidx == i1, s1 / tot, 0.0) + jnp.where(eidx == i2, s2 / tot, 0.0)
    comb_pad = jnp.concatenate([comb_t, jnp.zeros((LANES - N_EXPERTS, tm), F32)], axis=0)
    comb_ref[...] = comb_pad.T
    bgt_ref[...] = jnp.broadcast_to(bg, (8, tm))
    gidx8 = lax.broadcasted_iota(jnp.int32, (8, tm), 0)
    counts = jnp.sum(jnp.where(gidx8 == bg, 1.0, 0.0), axis=1, keepdims=True)
    cnt_ref[0] = jnp.broadcast_to(counts, (8, LANES)).astype(jnp.int32)


def _merge(h, hm, ha, g, w_bm, w_ba, w_o, ln_g, ln_b, wr_t, br, l, alpha):
    n, d = h.shape
    tm = MOE_ROW_TILE
    row = lambda width: pl.BlockSpec((tm, width), lambda r: (r, 0))
    full = lambda a: pl.BlockSpec(a.shape, lambda r: (0,) * a.ndim)
    args = (h, hm, ha, g, w_bm, w_ba, w_o, ln_g, ln_b, wr_t, br)
    return pl.pallas_call(
        functools.partial(_merge_kernel, alpha),
        grid=(n // tm,),
        in_specs=[row(d), row(M_WIDTH), row(A_WIDTH), row(2 * d)] +
                 [_layer_block(a, l) for a in args[4:9]] + [full(wr_t), full(br)],
        out_specs=[row(d), row(LANES),
                   pl.BlockSpec((8, tm), lambda r: (0, r)),
                   pl.BlockSpec((1, 8, LANES), lambda r: (r, 0, 0))],
        out_shape=[jax.ShapeDtypeStruct((n, d), F32), jax.ShapeDtypeStruct((n, LANES), F32),
                   jax.ShapeDtypeStruct((8, n), jnp.int32),
                   jax.ShapeDtypeStruct((n // tm, 8, LANES), jnp.int32)],
        compiler_params=_params(("parallel",)),
        name="merge",
    )(*args)


def _moe_kernel(alpha, cap, cnt_ref, h_ref, comb_ref, bgt_ref, wg_ref, wu_ref, wd_ref, lg_ref, lb_ref,
                out_ref, xb_scr, cs_scr, yt_scr, tri_scr):
    r = pl.program_id(0)
    g = pl.program_id(1)
    rows = h_ref.shape[0]

    @pl.when((r == 0) & (g == 0))
    def _():
        t0 = lax.broadcasted_iota(jnp.int32, (rows, rows), 0)
        t1 = lax.broadcasted_iota(jnp.int32, (rows, rows), 1)
        tri_scr[...] = (t0 < t1).astype(BF16)

    @pl.when(g == 0)
    def _():
        xb_scr[...] = h_ref[...].astype(BF16)
        yt_scr[...] = jnp.zeros_like(yt_scr)
        c = comb_ref[...]
        for part in range(2):
            cb = c.astype(BF16)
            cs_scr[part] = cb
            c = c - cb.astype(F32)

    member = bgt_ref[0:1, :] == g
    mem8 = jnp.broadcast_to(jnp.where(member, 1.0, 0.0), (8, rows)).astype(BF16)
    rank = jnp.dot(mem8, tri_scr[...], preferred_element_type=F32)[0:1, :].astype(jnp.int32)
    n_blocks = (cnt_ref[r * N_GROUPS + g] + cap - 1) // cap
    lane = lax.broadcasted_iota(jnp.int32, (cap, LANES), 1)
    tn = (((0,), (0,)), ((), ()))

    def block(b, carry):
        slot = lax.broadcasted_iota(jnp.int32, (cap, rows), 0) + b * cap
        onehot = jnp.where(member & (rank == slot), 1.0, 0.0).astype(BF16)
        xg = jnp.dot(onehot, xb_scr[...], preferred_element_type=F32).astype(BF16)
        cw = jnp.dot(onehot, cs_scr[0], preferred_element_type=F32)
        cw = cw + jnp.dot(onehot, cs_scr[1], preferred_element_type=F32)
        y = jnp.zeros((cap, out_ref.shape[1]), F32)
        for e in range(GROUP_SIZE):
            gate = jnp.dot(xg, wg_ref[e], preferred_element_type=F32)
            up = jnp.dot(xg, wu_ref[e], preferred_element_type=F32)
            he = gate * _sigmoid(gate) * up
            o = jnp.dot(he.astype(BF16), wd_ref[e], preferred_element_type=F32)
            ce = jnp.sum(jnp.where(lane == g * GROUP_SIZE + e, cw, 0.0), axis=1, keepdims=True)
            y = y + ce * o
        yt_scr[...] += lax.dot_general(onehot, y.astype(BF16), tn, preferred_element_type=F32)
        return carry

    lax.fori_loop(0, n_blocks, block, 0)

    @pl.when(g == pl.num_programs(1) - 1)
    def _():
        z = alpha * h_ref[...] + yt_scr[...]
        out_ref[...] = _layer_norm_rows(z, lg_ref[...], lb_ref[...], 1e-5)


def _moe(h, comb, bgt, counts, w_gate, w_up, w_down, ln_g, ln_b, l, alpha):
    n, d = h.shape
    de = w_gate.shape[-1]
    tm = MOE_ROW_TILE
    cnt = counts[:, 0:N_GROUPS, 0].reshape(-1)
    grid_spec = pltpu.PrefetchScalarGridSpec(
        num_scalar_prefetch=1,
        grid=(n // tm, N_GROUPS),
        in_specs=[pl.BlockSpec((tm, d), lambda r, g, c: (r, 0)),
                  pl.BlockSpec((tm, LANES), lambda r, g, c: (r, 0)),
                  pl.BlockSpec((8, tm), lambda r, g, c: (0, r)),
                  pl.BlockSpec((None, GROUP_SIZE, d, de), lambda r, g, c: (l, g, 0, 0)),
                  pl.BlockSpec((None, GROUP_SIZE, d, de), lambda r, g, c: (l, g, 0, 0)),
                  pl.BlockSpec((None, GROUP_SIZE, de, d), lambda r, g, c: (l, g, 0, 0)),
                  _layer_block(ln_g, l), _layer_block(ln_b, l)],
        out_specs=pl.BlockSpec((tm, d), lambda r, g, c: (r, 0)),
        scratch_shapes=[pltpu.VMEM((tm, d), BF16), pltpu.VMEM((2, tm, LANES), BF16),
                        pltpu.VMEM((tm, d), F32), pltpu.VMEM((tm, tm), BF16)])
    return pl.pallas_call(
        functools.partial(_moe_kernel, alpha, MOE_CAP),
        grid_spec=grid_spec,
        out_shape=jax.ShapeDtypeStruct((n, d), F32),
        compiler_params=_params(("arbitrary", "arbitrary")),
        name="moe",
    )(cnt, h, comb, bgt, w_gate, w_up, w_down, ln_g, ln_b)


def _with_meta(real, meta, n_pad):
    pad = n_pad - real.shape[0] - meta.shape[0]
    return jnp.concatenate([real, meta, jnp.zeros((pad, real.shape[1]), real.dtype)], axis=0)


def kernel(x, meta_tokens, ln_in_g, ln_in_b, w_in, conv_w, b_if, mnorm_g, kv_norm_g, w_uk, w_uv,
           w_branch_m, w_branch_a, w_out, ln1_g, ln1_b, w_router, b_router, w_gate, w_up, w_down,
           ln2_g, ln2_b, rel_bias):
    batch, seq, d = x.shape
    depth = w_in.shape[0]
    alpha = (2 * depth) ** 0.25
    n_real = batch * seq
    n_meta = batch * N_META
    tile = math.lcm(ROW_TILE, MOE_ROW_TILE)
    n_pad = -(-(n_real + n_meta) // tile) * tile
    assert seq % LANES == 0 and n_real % N_META == 0

    assert n_real % ROW_TILE == 0
    tail = _with_meta(jnp.tile(meta_tokens.astype(x.dtype), (batch, 1)), jnp.zeros((0, d), x.dtype),
                      n_pad - n_real)
    h = _input_ln(x.reshape(n_real, d), tail, ln_in_g, ln_in_b)
    tables = _bias_tables(rel_bias)
    row3 = lambda a: a.astype(F32)[:, None, :]
    w_packed, brow, kv_g = _pack_w_in(w_in), _pack_b_if(b_if), row3(kv_norm_g)
    conv_f, mnorm = conv_w.astype(F32), row3(mnorm_g)
    wuk_t = jnp.swapaxes(w_uk, 2, 3).astype(BF16)
    wuv = w_uv.astype(BF16)
    w_bm, w_ba, w_o = w_branch_m.astype(BF16), w_branch_a.astype(BF16), w_out.astype(BF16)
    wr_t = jnp.zeros((LANES, d), F32).at[0:N_EXPERTS].set(w_router.T).astype(BF16)
    br = jnp.zeros((LANES, 1), F32).at[0:N_EXPERTS, 0].set(b_router)
    wg, wu, wd = w_gate.astype(BF16), w_up.astype(BF16), w_down.astype(BF16)
    g1, b1, g2, b2 = row3(ln1_g), row3(ln1_b), row3(ln2_g), row3(ln2_b)
    for l in range(depth):
        pm, qa, qi, c, sm, g = _project(h, w_packed, kv_g, brow, l)
        hm_real, hm_meta = _mlstm(pm, sm, conv_f, mnorm, l, batch, seq)
        ha_real, ha_meta = _dsa(qa, qi, sm, c, wuk_t, wuv, tables, l, batch, seq)
        hm = _with_meta(hm_real, hm_meta, n_pad)
        ha = _with_meta(ha_real, ha_meta, n_pad)
        h1, comb, bgt, counts = _merge(h, hm, ha, g, w_bm, w_ba, w_o, g1, b1, wr_t, br, l, alpha)
        h = _moe(h1, comb, bgt, counts, wg, wu, wd, g2, b2, l, alpha)
    return h[:n_real].reshape(batch, seq, d)
```

```python
import functools
import math

import numpy as np
import jax
import jax.numpy as jnp
from jax import lax
from jax.experimental import pallas as pl
from jax.experimental.pallas import tpu as pltpu

F32 = jnp.float32
BF16 = jnp.bfloat16

N_META = 16
M_HEADS = 4
M_HEAD_DIM = 128
M_WIDTH = M_HEADS * M_HEAD_DIM
CONV_WIDTH = 4
A_HEADS = 8
A_HEAD_DIM = 64
A_WIDTH = A_HEADS * A_HEAD_DIM
KV_RANK = 128
IDX_HEADS = 4
IDX_DIM = 64
IDX_SCALE = (IDX_HEADS * IDX_DIM) ** -0.5
TOPK_MAX = 256
REL_BUCKETS = 32
REL_MAX_EXACT = 16
REL_MAX_DIST = 128
N_EXPERTS = 16
N_GROUPS = 4
GROUP_SIZE = N_EXPERTS // N_GROUPS

LANES = 128
ROW_TILE = 256
MOE_ROW_TILE = 768
MOE_CAP = 224
M_CHUNK = 256
SCORE_CHUNK = 4
KEY_SUB = 4
ATT_GROUPS = 2
VMEM_LIMIT = 56 * 1024 * 1024
NEG_BIG = -1e30
INT_MIN = -2 ** 31
HALF_BIAS = 2 ** 15
LOG2E = math.log2(math.e)

D_MODEL = 1024
IN_WIDTHS = (("q_m", M_WIDTH), ("k_m", M_WIDTH), ("v_m", M_WIDTH), ("o_m", M_WIDTH), ("i_pre", M_HEADS),
             ("f_pre", M_HEADS), ("q_a", A_WIDTH), ("c_kv", KV_RANK), ("q_idx", IDX_HEADS * IDX_DIM),
             ("k_idx", IDX_DIM), ("w_idx", IDX_HEADS), ("g_m", D_MODEL), ("g_a", D_MODEL))
W_PM = 4 * M_WIDTH
W_G = 2 * D_MODEL
PK_PM = 0
PK_QA = PK_PM + W_PM
PK_QI = PK_QA + A_WIDTH
PK_CKV = PK_QI + IDX_HEADS * IDX_DIM
PK_SM = PK_CKV + KV_RANK
PK_G = PK_SM + LANES
PK_TOTAL = PK_G + W_G
SM_KIDX = 0
SM_WIDX = SM_KIDX + IDX_DIM
SM_IPRE = SM_WIDX + IDX_HEADS
SM_FPRE = SM_IPRE + M_HEADS
SM_USED = SM_FPRE + M_HEADS


def _params(sem):
    return pltpu.CompilerParams(dimension_semantics=sem, vmem_limit_bytes=VMEM_LIMIT)


def _sigmoid(x):
    return 1.0 / (1.0 + jnp.exp(-x))


def _layer_norm_rows(x, g, b, eps):
    mu = jnp.mean(x, axis=-1, keepdims=True)
    xc = x - mu
    var = jnp.mean(xc * xc, axis=-1, keepdims=True)
    return xc * lax.rsqrt(var + eps) * g + b


def _ln_kernel(real_tiles, x_ref, m_ref, g_ref, b_ref, o_ref):
    r = pl.program_id(0)

    @pl.when(r < real_tiles)
    def _():
        o_ref[...] = _layer_norm_rows(x_ref[...], g_ref[...], b_ref[...], 1e-5)

    @pl.when(r >= real_tiles)
    def _():
        o_ref[...] = _layer_norm_rows(m_ref[...], g_ref[...], b_ref[...], 1e-5)


def _input_ln(x, tail, g, b):
    n_real, d = x.shape
    real_tiles = n_real // ROW_TILE
    tiles = real_tiles + tail.shape[0] // ROW_TILE
    return pl.pallas_call(
        functools.partial(_ln_kernel, real_tiles),
        grid=(tiles,),
        in_specs=[pl.BlockSpec((ROW_TILE, d), lambda r: (jnp.minimum(r, real_tiles - 1), 0)),
                  pl.BlockSpec((ROW_TILE, d), lambda r: (jnp.maximum(r - real_tiles, 0), 0)),
                  pl.BlockSpec((1, d), lambda r: (0, 0)),
                  pl.BlockSpec((1, d), lambda r: (0, 0))],
        out_specs=pl.BlockSpec((ROW_TILE, d), lambda r: (r, 0)),
        out_shape=jax.ShapeDtypeStruct((tiles * ROW_TILE, d), F32),
        compiler_params=_params(("arbitrary",)),
        name="input_ln",
    )(x, tail, g.reshape(1, d), b.reshape(1, d))


def _proj_kernel(h_ref, w_ref, kvg_ref, brow_ref, pm_ref, qa_ref, qi_ref, c_ref, sm_ref, g_ref):
    x = h_ref[...].astype(BF16)

    def mm(lo, width):
        return jnp.dot(x, w_ref[:, lo:lo + width], preferred_element_type=F32)

    pm_ref[...] = mm(PK_PM, W_PM).astype(BF16)
    qa_ref[...] = mm(PK_QA, A_WIDTH).astype(BF16)
    qi_ref[...] = mm(PK_QI, IDX_HEADS * IDX_DIM).astype(BF16)
    ckv = mm(PK_CKV, KV_RANK)
    c_ref[...] = ckv * lax.rsqrt(jnp.mean(ckv * ckv, axis=-1, keepdims=True) + 1e-6) * kvg_ref[...]
    sm_ref[...] = mm(PK_SM, LANES) + brow_ref[...]
    g_ref[...] = mm(PK_G, W_G).astype(BF16)


def _pack_w_in(w):
    part, start = {}, 0
    for name, width in IN_WIDTHS:
        part[name] = w[..., start:start + width]
        start += width
    assert start == w.shape[-1] and w.shape[-2] == D_MODEL
    order = ("q_m", "k_m", "v_m", "o_m", "q_a", "q_idx", "c_kv", "k_idx", "w_idx", "i_pre", "f_pre")
    cols = [part[name] for name in order]
    cols += [jnp.zeros(w.shape[:-1] + (LANES - SM_USED,), w.dtype), part["g_m"], part["g_a"]]
    return jnp.concatenate(cols, axis=-1).astype(BF16)


def _pack_b_if(b_if):
    depth = b_if.shape[0]
    return jnp.concatenate([jnp.zeros((depth, SM_IPRE), F32), b_if.astype(F32),
                            jnp.zeros((depth, LANES - SM_USED), F32)], axis=1)[:, None, :]


def _layer_block(arr, l):
    zeros = (0,) * (arr.ndim - 1)
    return pl.BlockSpec((None,) + arr.shape[1:], lambda *idx: (l,) + zeros)


def _project(h, w_packed, kv_g, brow, l):
    n, d = h.shape
    row = lambda width: pl.BlockSpec((MOE_ROW_TILE, width), lambda r: (r, 0))
    shp = lambda width, dt: jax.ShapeDtypeStruct((n, width), dt)
    outs = ((W_PM, BF16), (A_WIDTH, BF16), (IDX_HEADS * IDX_DIM, BF16), (KV_RANK, F32), (LANES, F32), (W_G, BF16))
    return pl.pallas_call(
        _proj_kernel,
        grid=(n // MOE_ROW_TILE,),
        in_specs=[row(d), _layer_block(w_packed, l), _layer_block(kv_g, l), _layer_block(brow, l)],
        out_specs=[row(width) for width, _ in outs],
        out_shape=[shp(width, dt) for width, dt in outs],
        compiler_params=_params(("parallel",)),
        name="in_proj",
    )(h, w_packed, kv_g, brow)


def _log_sigmoid(f):
    return jnp.minimum(f, 0.0) - jnp.log1p(jnp.exp(-jnp.abs(f)))


def _row_from_col(col):
    L = col.shape[0]
    eye = lax.broadcasted_iota(jnp.int32, (L, L), 0) == lax.broadcasted_iota(jnp.int32, (L, L), 1)
    return jnp.sum(jnp.where(eye, jnp.broadcast_to(col, (L, L)), 0.0), axis=0, keepdims=True)


def _mlstm_heads(qs, ks, vs, lis, li_rows, lf_rows, cts, ns, ms):
    L = qs[0].shape[0]
    heads = range(len(qs))
    r = lax.broadcasted_iota(jnp.int32, (L, L), 0)
    c = lax.broadcasted_iota(jnp.int32, (L, L), 1)
    tril = c <= r
    eye = c == r
    nt = (((1,), (1,)), ((), ()))
    tn = (((0,), (0,)), ((), ()))
    b_cols = [jnp.sum(jnp.where(tril, jnp.broadcast_to(lf_rows[h], (L, L)), 0.0), axis=1, keepdims=True)
              for h in heads]
    b_rows = [jnp.sum(jnp.where(eye, jnp.broadcast_to(b_cols[h], (L, L)), 0.0), axis=0, keepdims=True)
              for h in heads]
    qb = [qs[h].astype(BF16) for h in heads]
    kb = [ks[h].astype(BF16) for h in heads]
    vb = [vs[h].astype(BF16) for h in heads]
    qk = [lax.dot_general(qb[h], kb[h], nt, preferred_element_type=F32) for h in heads]
    qc = [jnp.dot(qb[h], cts[h].astype(BF16), preferred_element_type=F32) for h in heads]
    d = [jnp.where(tril, b_cols[h] - b_rows[h] + li_rows[h], -jnp.inf) for h in heads]
    inter = [b_cols[h] + ms[h] for h in heads]
    m_t = [jnp.maximum(inter[h], jnp.max(d[h], axis=1, keepdims=True)) for h in heads]
    a = [jnp.exp(inter[h] - m_t[h]) for h in heads]
    w = [jnp.exp(d[h] - m_t[h]) * qk[h] for h in heads]
    wv = [jnp.dot(w[h].astype(BF16), vb[h], preferred_element_type=F32) for h in heads]
    den = [a[h] * jnp.sum(qs[h] * ns[h], axis=1, keepdims=True) + jnp.sum(w[h], axis=1, keepdims=True)
           for h in heads]
    hs = [(a[h] * qc[h] + wv[h]) / jnp.maximum(jnp.abs(den[h]), jnp.exp(-m_t[h])) for h in heads]
    b_last = [b_cols[h][L - 1:L, :] for h in heads]
    g = [b_last[h] - b_cols[h] + lis[h] for h in heads]
    m_new = [jnp.maximum(b_last[h] + ms[h], jnp.max(g[h], axis=0, keepdims=True)) for h in heads]
    decay = [jnp.exp(b_last[h] + ms[h] - m_new[h]) for h in heads]
    kw = [ks[h] * jnp.exp(g[h] - m_new[h]) for h in heads]
    ct_new = [decay[h] * cts[h] + lax.dot_general(kw[h].astype(BF16), vb[h], tn, preferred_element_type=F32)
              for h in heads]
    n_new = [decay[h] * ns[h] + jnp.sum(kw[h], axis=0, keepdims=True) for h in heads]
    return hs, ct_new, n_new, m_new


def _mlstm_kernel(pm_ref, sm_ref, pmm_ref, smm_ref, cw_ref, mg_ref, out_ref, outm_ref,
                  ct_scr, n_scr, m_scr, x_scr):
    ci = pl.program_id(1)
    tail = 8

    def run_chunk(p_ref, s_ref, o_ref):
        L = p_ref.shape[0]
        heads = range(M_HEADS)
        x_scr[tail:tail + L, :] = p_ref[:, 0:2 * M_WIDTH].astype(F32)
        conv = cw_ref[0:1, :] * x_scr[tail - 3:tail - 3 + L, :]
        for j in range(1, CONV_WIDTH):
            conv = conv + cw_ref[j:j + 1, :] * x_scr[tail - 3 + j:tail - 3 + j + L, :]
        x_scr[0:tail, :] = x_scr[L:L + tail, :]
        qk = conv * _sigmoid(conv)
        cols = lambda base, h: slice(base + h * M_HEAD_DIM, base + (h + 1) * M_HEAD_DIM)
        lis = [s_ref[:, SM_IPRE + h:SM_IPRE + h + 1] for h in heads]
        if L % LANES == 0:
            gates_t = s_ref[...].T
            lf_all = _log_sigmoid(gates_t[SM_FPRE:SM_FPRE + M_HEADS, :])
            li_rows = [gates_t[SM_IPRE + h:SM_IPRE + h + 1, :] for h in heads]
            lf_rows = [lf_all[h:h + 1, :] for h in heads]
        else:
            li_rows = [_row_from_col(lis[h]) for h in heads]
            lf_rows = [_row_from_col(_log_sigmoid(s_ref[:, SM_FPRE + h:SM_FPRE + h + 1])) for h in heads]
        hs, ct_new, n_new, m_new = _mlstm_heads(
            [qk[:, cols(0, h)] * (M_HEAD_DIM ** -0.5) for h in heads],
            [qk[:, cols(M_WIDTH, h)] for h in heads],
            [p_ref[:, cols(2 * M_WIDTH, h)].astype(F32) for h in heads],
            lis, li_rows, lf_rows,
            [ct_scr[h] for h in heads], [n_scr[h:h + 1, :] for h in heads],
            [m_scr[h:h + 1, 0:1] for h in heads])
        for h in heads:
            ct_scr[h] = ct_new[h]
            n_scr[h:h + 1, :] = n_new[h]
            m_scr[h:h + 1, :] = jnp.broadcast_to(m_new[h], (1, LANES))
        mus = [jnp.mean(hs[h], axis=-1, keepdims=True) for h in heads]
        hcs = [hs[h] - mus[h] for h in heads]
        vars_ = [jnp.mean(hcs[h] * hcs[h], axis=-1, keepdims=True) for h in heads]
        for h in heads:
            o_gate = _sigmoid(p_ref[:, cols(3 * M_WIDTH, h)].astype(F32))
            o_ref[:, cols(0, h)] = (hcs[h] * lax.rsqrt(vars_[h] + 1e-5) * mg_ref[:, cols(0, h)]
                                    * o_gate).astype(o_ref.dtype)

    @pl.when(ci == 0)
    def _():
        ct_scr[...] = jnp.zeros_like(ct_scr)
        n_scr[...] = jnp.zeros_like(n_scr)
        m_scr[...] = jnp.zeros_like(m_scr)
        x_scr[...] = jnp.zeros_like(x_scr)
        run_chunk(pmm_ref, smm_ref, outm_ref)

    run_chunk(pm_ref, sm_ref, out_ref)


def _mlstm(pm, sm, conv_w, mnorm_g, l, batch, seq):
    n_real = batch * seq
    chunk = min(M_CHUNK, seq)
    nc = seq // chunk
    meta_blk = n_real // N_META
    out_real, out_meta = pl.pallas_call(
        _mlstm_kernel,
        grid=(batch, nc),
        in_specs=[pl.BlockSpec((chunk, 4 * M_WIDTH), lambda b, c: (b * nc + c, 0)),
                  pl.BlockSpec((chunk, LANES), lambda b, c: (b * nc + c, 0)),
                  pl.BlockSpec((N_META, 4 * M_WIDTH), lambda b, c: (meta_blk + b, 0)),
                  pl.BlockSpec((N_META, LANES), lambda b, c: (meta_blk + b, 0)),
                  _layer_block(conv_w, l), _layer_block(mnorm_g, l)],
        out_specs=[pl.BlockSpec((chunk, M_WIDTH), lambda b, c: (b * nc + c, 0)),
                   pl.BlockSpec((N_META, M_WIDTH), lambda b, c: (b, 0))],
        out_shape=[jax.ShapeDtypeStruct((n_real, M_WIDTH), BF16),
                   jax.ShapeDtypeStruct((batch * N_META, M_WIDTH), BF16)],
        scratch_shapes=[pltpu.VMEM((M_HEADS, M_HEAD_DIM, M_HEAD_DIM), F32),
                        pltpu.VMEM((8, LANES), F32),
                        pltpu.VMEM((8, LANES), F32),
                        pltpu.VMEM((chunk + 8, 2 * M_WIDTH), F32)],
        compiler_params=_params(("arbitrary", "arbitrary")),
        name="mlstm",
    )(pm, sm, pm, sm, conv_w, mnorm_g)
    return out_real, out_meta


def _rel_bucket_np(dist):
    n = np.maximum(dist, 0)
    nf = np.maximum(n, REL_MAX_EXACT).astype(np.float32)
    large = REL_MAX_EXACT + (np.log(nf / np.float32(REL_MAX_EXACT)) /
                             np.float32(math.log(REL_MAX_DIST / REL_MAX_EXACT))
                             * np.float32(REL_BUCKETS - REL_MAX_EXACT)).astype(np.int32)
    large = np.minimum(large, REL_BUCKETS - 1)
    return np.where(n < REL_MAX_EXACT, n, large).astype(np.int32)


def _bias_tables(rel_bias):
    q = np.arange(LANES)[:, None]
    k = np.arange(LANES)[None, :]
    far = 4 * LANES
    assert (_rel_bucket_np(np.arange(LANES + 1, far)) == REL_BUCKETS - 1).all()
    far_idx = np.full((LANES, LANES), REL_BUCKETS - 1, np.int32)
    near_idx = np.stack([_rel_bucket_np(q - k), _rel_bucket_np(LANES + q - k), far_idx])
    meta_idx = np.stack([_rel_bucket_np(q + N_META - np.minimum(k, N_META - 1)), far_idx])
    mq = np.arange(N_META)[:, None]
    mm_idx = _rel_bucket_np(mq - np.minimum(k, N_META - 1))
    rb = rel_bias.astype(F32)

    def lookup(idx, values):
        onehot = jnp.asarray(idx[..., None] == np.arange(REL_BUCKETS), F32)
        out = jnp.einsum('...b,bh->...h', onehot, values, precision=lax.Precision.HIGHEST)
        return jnp.moveaxis(out, -1, -3)

    rel_values = (rb - rb[REL_BUCKETS - 1][None, :]) * LOG2E
    return lookup(near_idx, rel_values), lookup(meta_idx, rel_values), lookup(mm_idx, rb)


def _dsa_kernel(top_k, qa_ref, qi_ref, sm_ref, cb_ref, smb_ref, cm_ref, qam_ref,
                wuk_ref, wuv_ref, near_ref, metab_ref, mmb_ref,
                out_ref, outm_ref,
                keys_scr, hi_scr, lo_scr, lg_scr, lgm_scr, rawa_scr, rawb_scr, pa_scr, pb_scr,
                caug_scr, ct_scr, kt_scr, cmaug_scr, cmt_scr, qs_scr, mx_scr, mrep_scr, acc_scr):
    i = pl.program_id(1)
    T = LANES
    H = A_HEADS
    col = lax.broadcasted_iota(jnp.int32, (T, T), 1)
    row = lax.broadcasted_iota(jnp.int32, (T, T), 0)
    nt = (((1,), (1,)), ((), ()))

    def q_latent(qa, hd, scale):
        ql = jnp.dot(qa[:, hd * A_HEAD_DIM:(hd + 1) * A_HEAD_DIM], wuk_ref[hd],
                     preferred_element_type=F32)
        return (ql * scale).astype(BF16)

    def ones_column(n):
        return jnp.where(lax.broadcasted_iota(jnp.int32, (n, T), 1) == 0, 1.0, 0.0).astype(BF16)

    @pl.when(i == 0)
    def _():
        caug_scr[:, 0:KV_RANK] = cb_ref[...].astype(BF16)
        caug_scr[:, KV_RANK:KV_RANK + T] = ones_column(caug_scr.shape[0])

        def transpose_keys(blk, carry):
            rows = pl.ds(pl.multiple_of(blk * KEY_SUB * T, KEY_SUB * T), KEY_SUB * T)
            ct_scr[blk] = cb_ref[rows, :].T.astype(BF16)
            kt_scr[blk] = smb_ref[rows, :].T[SM_KIDX:SM_KIDX + IDX_DIM, :].astype(BF16)
            return carry

        lax.fori_loop(0, ct_scr.shape[0], transpose_keys, 0)
        cm_pad = jnp.concatenate([cm_ref[...], jnp.zeros((T - N_META, KV_RANK), F32)], axis=0)
        cmaug_scr[:, 0:KV_RANK] = cm_pad.astype(BF16)
        cmt_scr[...] = cm_pad.T.astype(BF16)
        cmaug_scr[:, KV_RANK:KV_RANK + T] = ones_column(T)
        cmk = cmaug_scr[:, 0:KV_RANK]
        qam = qam_ref[...]
        mrow = lax.broadcasted_iota(jnp.int32, (N_META, T), 0)
        mcol = lax.broadcasted_iota(jnp.int32, (N_META, T), 1)
        for hd in range(H):
            lg = lax.dot_general(q_latent(qam, hd, A_HEAD_DIM ** -0.5), cmk, nt,
                                 preferred_element_type=F32) + mmb_ref[hd]
            lg = jnp.where(mcol <= mrow, lg, NEG_BIG)
            p = jnp.exp(lg - jnp.max(lg, axis=1, keepdims=True))
            p = p / jnp.sum(p, axis=1, keepdims=True)
            o = jnp.dot(p.astype(BF16), cmk, preferred_element_type=F32)
            outm_ref[:, hd * A_HEAD_DIM:(hd + 1) * A_HEAD_DIM] = jnp.dot(
                o.astype(BF16), wuv_ref[hd], preferred_element_type=F32).astype(outm_ref.dtype)

    qa = qa_ref[...]
    for hd in range(H):
        qs_scr[hd * T:(hd + 1) * T, :] = q_latent(qa, hd, A_HEAD_DIM ** -0.5 * LOG2E)
    qi = qi_ref[...]
    wv = sm_ref[:, SM_WIDX:SM_WIDX + IDX_HEADS] * IDX_SCALE
    t_col = i * T + lax.broadcasted_iota(jnp.int32, (T, 1), 0)
    n_chunks = (i + SCORE_CHUNK) // SCORE_CHUNK
    CW = SCORE_CHUNK * T

    def run_pipelined(n, produce, consume, buf_a, buf_b):
        last = n - 1
        produce(0, buf_a)

        def body(t, carry):
            s = 2 * t
            produce(jnp.minimum(s + 1, last), buf_b)
            consume(s, buf_a)
            produce(jnp.minimum(s + 2, last), buf_a)
            consume(s + 1, buf_b)
            return carry

        lax.fori_loop(0, n // 2, body, 0)

        @pl.when(n % 2 == 1)
        def _():
            consume(last, buf_a)

    def score_matmul(cix, buf):
        kc_t = kt_scr[cix]
        for hh in range(IDX_HEADS):
            buf[hh * T:(hh + 1) * T, :] = jnp.dot(
                qi[:, hh * IDX_DIM:(hh + 1) * IDX_DIM], kc_t, preferred_element_type=F32)

    def score_keys(cix, buf):
        acc = jnp.zeros((T, CW), F32)
        for hh in range(IDX_HEADS):
            acc = acc + wv[:, hh:hh + 1] * jnp.maximum(buf[hh * T:(hh + 1) * T, :], 0.0)
        acc = jnp.where(acc == 0.0, 0.0, acc)
        bits = lax.bitcast_convert_type(acc, jnp.int32)
        key = jnp.where(bits < 0, bits ^ jnp.int32(0x7FFFFFFF), bits)
        s_idx = cix * CW + lax.broadcasted_iota(jnp.int32, (T, CW), 1)
        key = jnp.where(s_idx <= t_col, key, jnp.int32(INT_MIN))
        for u in range(SCORE_CHUNK):
            tile = key[:, u * T:(u + 1) * T]
            keys_scr[cix * SCORE_CHUNK + u] = tile
            tile_t = tile.T
            hi_scr[cix * SCORE_CHUNK + u] = lax.shift_right_arithmetic(tile_t, 16).astype(jnp.int16)
            lo_scr[cix * SCORE_CHUNK + u] = ((tile_t & 0xFFFF) - HALF_BIAS).astype(jnp.int16)

    run_pipelined(n_chunks, score_matmul, score_keys, rawa_scr, rawb_scr)

    def rep16(row_i32):
        return jnp.broadcast_to(row_i32, (T, T)).astype(jnp.int16)

    def count16(src_scr, pred_fn):
        def body(cix, cnt):
            for u in range(SCORE_CHUNK):
                hit = pred_fn(src_scr[cix * SCORE_CHUNK + u])
                cnt = cnt + jnp.where(hit, jnp.int16(1), jnp.int16(0))
            return cnt
        cnt = lax.fori_loop(0, n_chunks, body, jnp.zeros((T, T), jnp.int16))
        return jnp.sum(cnt.astype(F32), axis=0, keepdims=True)

    def search16(src_scr, k_row):
        def bit_body(bi, carry):
            ans, above = carry
            cand_u = ans | lax.shift_left(jnp.int32(1), jnp.int32(15) - bi)
            cand = rep16(cand_u - HALF_BIAS)
            total = count16(src_scr, lambda x: x >= cand)
            take = total >= k_row
            return jnp.where(take, cand_u, ans), jnp.where(take, above, total)
        return lax.fori_loop(0, 16, bit_body, (jnp.zeros((1, T), jnp.int32), jnp.zeros((1, T), F32)))

    k_row = jnp.full((1, T), float(top_k), F32)
    hi_u, hi_above = search16(hi_scr, k_row)
    hi_s = hi_u - HALF_BIAS
    hi_rep = rep16(hi_s)
    k_low = k_row - hi_above

    def band_body(cix, carry):
        for u in range(SCORE_CHUNK):
            j = cix * SCORE_CHUNK + u
            lo_scr[j] = jnp.where(hi_scr[j] == hi_rep, lo_scr[j], jnp.int16(-HALF_BIAS))
        return carry

    lax.fori_loop(0, n_chunks, band_body, 0)
    lo_u, lo_above = search16(lo_scr, k_low)
    need_row = k_low - lo_above
    thr_row = lax.shift_left(hi_s, 16) | lo_u
    thr = jnp.broadcast_to(thr_row, (T, T)).T
    need = jnp.broadcast_to(need_row, (T, T)).T

    mx_scr[...] = jnp.full_like(mx_scr, NEG_BIG)
    acc_scr[...] = jnp.zeros_like(acc_scr)
    hg = H // ATT_GROUPS
    groups = [slice(g * hg * T, (g + 1) * hg * T) for g in range(ATT_GROUPS)]
    meta_sel = jnp.minimum(i, 1)

    def max_pass(ck_t, madds, bias_fns, store):
        lgs = [jnp.dot(qs_scr[rs, :], ck_t, preferred_element_type=F32) for rs in groups]
        for hd in range(H):
            rs = slice(hd * T, (hd + 1) * T)
            lo = (hd % hg) * T
            mx = mx_scr[rs, :]
            for u in range(len(madds)):
                x = lgs[hd // hg][lo:lo + T, u * T:(u + 1) * T] + madds[u]
                if bias_fns[u] is not None:
                    x = x + bias_fns[u](hd)
                store(rs, u, x)
                mx = jnp.maximum(mx, x)
            mx_scr[rs, :] = mx

    def sum_pass(c_aug, n_sub, load):
        for grp in groups:
            m_rep = mrep_scr[grp, :]
            ph = [jnp.exp2(load(grp, u) - m_rep).astype(BF16) for u in range(n_sub)]
            p = ph[0] if n_sub == 1 else jnp.concatenate(ph, axis=1)
            acc_scr[grp, :] += jnp.dot(p, c_aug, preferred_element_type=F32)

    def key_rows(step):
        return pl.ds(pl.multiple_of(step * KEY_SUB * T, KEY_SUB * T), KEY_SUB * T)

    upper = (row < col).astype(BF16)

    def mask_step(step, seen, near):
        madds, bias_fns = [], []
        for u in range(KEY_SUB):
            j = KEY_SUB * step + u
            kk = keys_scr[j]
            eq = kk == thr
            eqf = jnp.where(eq, 1.0, 0.0)
            before = jnp.dot(eqf.astype(BF16), upper, preferred_element_type=F32) + seen
            sel = (kk > thr) | (eq & (before < need))
            if near:
                sel = sel & ((j * T + col) <= (i * T + row))
                dsel = jnp.clip(i - j, 0, 2)
                bias_fns.append(lambda hd, dsel=dsel: near_ref[dsel, hd])
            else:
                bias_fns.append(None)
            seen = seen + jnp.sum(eqf, axis=1, keepdims=True)
            madds.append(jnp.where(sel, 0.0, NEG_BIG))

        def store(rs, u, x):
            lg_scr[step, rs, u * T:(u + 1) * T] = x

        max_pass(ct_scr[step], madds, bias_fns, store)
        return seen

    def weights_step(step, buf):
        for grp in groups:
            m_rep = mrep_scr[grp, :]
            for u in range(KEY_SUB):
                buf[grp, u * T:(u + 1) * T] = jnp.exp2(
                    lg_scr[step, grp, u * T:(u + 1) * T] - m_rep).astype(BF16)

    def accumulate_step(step, buf):
        c_aug = caug_scr[key_rows(step), :]
        for grp in groups:
            acc_scr[grp, :] += jnp.dot(buf[grp, :], c_aug, preferred_element_type=F32)

    def store_meta(rs, u, x):
        lgm_scr[rs, :] = x

    n_far = jnp.maximum(i - 1, 0) // KEY_SUB
    n_steps = (i + KEY_SUB) // KEY_SUB
    max_pass(cmt_scr[...], [jnp.where(col < N_META, 0.0, NEG_BIG)],
             [lambda hd: metab_ref[meta_sel, hd]], store_meta)
    seen = lax.fori_loop(0, n_far, lambda s, c: mask_step(s, c, False), jnp.zeros((T, 1), F32))
    lax.fori_loop(n_far, n_steps, lambda s, c: mask_step(s, c, True), seen)
    mrep_scr[...] = jnp.broadcast_to(jnp.max(mx_scr[...], axis=1, keepdims=True), mrep_scr.shape)
    sum_pass(cmaug_scr[...], 1, lambda grp, u: lgm_scr[grp, :])
    def sum_step(step, carry):
        weights_step(step, pa_scr)
        accumulate_step(step, pa_scr)
        return carry

    lax.fori_loop(0, n_steps, sum_step, 0)

    for hd in range(H):
        rs = slice(hd * T, (hd + 1) * T)
        o = acc_scr[rs, 0:KV_RANK] / acc_scr[rs, KV_RANK:KV_RANK + 1]
        out_ref[:, hd * A_HEAD_DIM:(hd + 1) * A_HEAD_DIM] = jnp.dot(
            o.astype(BF16), wuv_ref[hd], preferred_element_type=F32).astype(out_ref.dtype)


def _dsa(qa, qi, sm, c, wuk_t, wuv, tables, l, batch, seq):
    n_real = batch * seq
    nq = seq // LANES
    n_tiles = ((nq + SCORE_CHUNK - 1) // SCORE_CHUNK) * SCORE_CHUNK
    top_k = min(TOPK_MAX, seq // 4)
    meta_blk = n_real // N_META
    near, metab, mmb = tables
    full = lambda a: pl.BlockSpec(a.shape, lambda b, i: (0,) * a.ndim)
    assert seq % (SCORE_CHUNK * LANES) == 0 and nq % KEY_SUB == 0 and SCORE_CHUNK == KEY_SUB
    out_real, out_meta = pl.pallas_call(
        functools.partial(_dsa_kernel, top_k),
        grid=(batch, nq),
        in_specs=[pl.BlockSpec((LANES, A_WIDTH), lambda b, i: (b * nq + i, 0)),
                  pl.BlockSpec((LANES, IDX_HEADS * IDX_DIM), lambda b, i: (b * nq + i, 0)),
                  pl.BlockSpec((LANES, LANES), lambda b, i: (b * nq + i, 0)),
                  pl.BlockSpec((seq, KV_RANK), lambda b, i: (b, 0)),
                  pl.BlockSpec((seq, LANES), lambda b, i: (b, 0)),
                  pl.BlockSpec((N_META, KV_RANK), lambda b, i: (meta_blk + b, 0)),
                  pl.BlockSpec((N_META, A_WIDTH), lambda b, i: (meta_blk + b, 0)),
                  _layer_block(wuk_t, l), _layer_block(wuv, l), full(near), full(metab), full(mmb)],
        out_specs=[pl.BlockSpec((LANES, A_WIDTH), lambda b, i: (b * nq + i, 0)),
                   pl.BlockSpec((N_META, A_WIDTH), lambda b, i: (b, 0))],
        out_shape=[jax.ShapeDtypeStruct((n_real, A_WIDTH), BF16),
                   jax.ShapeDtypeStruct((batch * N_META, A_WIDTH), BF16)],
        scratch_shapes=[pltpu.VMEM((n_tiles, LANES, LANES), jnp.int32),
                        pltpu.VMEM((n_tiles, LANES, LANES), jnp.int16),
                        pltpu.VMEM((n_tiles, LANES, LANES), jnp.int16),
                        pltpu.VMEM((nq // KEY_SUB, A_HEADS * LANES, KEY_SUB * LANES), F32),
                        pltpu.VMEM((A_HEADS * LANES, LANES), F32),
                        pltpu.VMEM((IDX_HEADS * LANES, SCORE_CHUNK * LANES), F32),
                        pltpu.VMEM((IDX_HEADS * LANES, SCORE_CHUNK * LANES), F32),
                        pltpu.VMEM((A_HEADS * LANES, KEY_SUB * LANES), BF16),
                        pltpu.VMEM((A_HEADS * LANES, KEY_SUB * LANES), BF16),
                        pltpu.VMEM((seq, KV_RANK + LANES), BF16),
                        pltpu.VMEM((nq // KEY_SUB, KV_RANK, KEY_SUB * LANES), BF16),
                        pltpu.VMEM((nq // KEY_SUB, IDX_DIM, KEY_SUB * LANES), BF16),
                        pltpu.VMEM((LANES, KV_RANK + LANES), BF16),
                        pltpu.VMEM((KV_RANK, LANES), BF16),
                        pltpu.VMEM((A_HEADS * LANES, KV_RANK), BF16),
                        pltpu.VMEM((A_HEADS * LANES, LANES), F32),
                        pltpu.VMEM((A_HEADS * LANES, LANES), F32),
                        pltpu.VMEM((A_HEADS * LANES, KV_RANK + LANES), F32)],
        compiler_params=_params(("arbitrary", "arbitrary")),
        name="dsa",
    )(qa, qi, sm, c, sm, c, qa, wuk_t, wuv, near, metab, mmb)
    return out_real, out_meta


def _merge_kernel(alpha, h_ref, hm_ref, ha_ref, g_ref, wbm_ref, wba_ref, wo_ref, lg_ref, lb_ref,
                  wr_ref, br_ref, h1_ref, comb_ref, bgt_ref, cnt_ref):
    d = h_ref.shape[1]
    gm = _sigmoid(g_ref[:, 0:d].astype(F32))
    ga = _sigmoid(g_ref[:, d:2 * d].astype(F32))
    y = gm * jnp.dot(hm_ref[...], wbm_ref[...], preferred_element_type=F32) + \
        ga * jnp.dot(ha_ref[...], wba_ref[...], preferred_element_type=F32)
    z = alpha * h_ref[...] + jnp.dot(y.astype(BF16), wo_ref[...], preferred_element_type=F32)
    h1 = _layer_norm_rows(z, lg_ref[...], lb_ref[...], 1e-5)
    h1_ref[...] = h1

    tm = h1.shape[0]
    logits_t = lax.dot_general(wr_ref[...], h1.astype(BF16), (((1,), (1,)), ((), ())),
                               preferred_element_type=F32)
    scores = _sigmoid(logits_t[0:N_EXPERTS, :])
    sel = scores + br_ref[0:N_EXPERTS, :]
    best = None
    for gidx in range(N_GROUPS):
        r0, r1, r2, r3 = (sel[gidx * GROUP_SIZE + u:gidx * GROUP_SIZE + u + 1, :] for u in range(4))
        a, b = jnp.maximum(r0, r1), jnp.minimum(r0, r1)
        c, dd = jnp.maximum(r2, r3), jnp.minimum(r2, r3)
        gs = jnp.maximum(a, c) + jnp.maximum(jnp.minimum(a, c), jnp.maximum(b, dd))
        if best is None:
            best, bg = gs, jnp.zeros((1, tm), jnp.int32)
        else:
            upd = gs > best
            bg = jnp.where(upd, gidx, bg)
            best = jnp.where(upd, gs, best)
    eidx = lax.broadcasted_iota(jnp.int32, (N_EXPERTS, tm), 0)
    masked = jnp.where((eidx // GROUP_SIZE) == bg, sel, -jnp.inf)
    v1 = jnp.max(masked, axis=0, keepdims=True)
    i1 = jnp.min(jnp.where(masked == v1, eidx, N_EXPERTS), axis=0, keepdims=True)
    masked2 = jnp.where(eidx == i1, -jnp.inf, masked)
    v2 = jnp.max(masked2, axis=0, keepdims=True)
    i2 = jnp.min(jnp.where(masked2 == v2, eidx, N_EXPERTS), axis=0, keepdims=True)
    s1 = jnp.sum(jnp.where(eidx == i1, scores, 0.0), axis=0, keepdims=True)
    s2 = jnp.sum(jnp.where(eidx == i2, scores, 0.0), axis=0, keepdims=True)
    tot = s1 + s2
    comb_t = jnp.where(eidx == i1, s1 / tot, 0.0) + jnp.where(eidx == i2, s2 / tot, 0.0)
    comb_pad = jnp.concatenate([comb_t, jnp.zeros((LANES - N_EXPERTS, tm), F32)], axis=0)
    comb_ref[...] = comb_pad.T
    bgt_ref[...] = jnp.broadcast_to(bg, (8, tm))
    gidx8 = lax.broadcasted_iota(jnp.int32, (8, tm), 0)
    counts = jnp.sum(jnp.where(gidx8 == bg, 1.0, 0.0), axis=1, keepdims=True)
    cnt_ref[0] = jnp.broadcast_to(counts, (8, LANES)).astype(jnp.int32)


def _merge(h, hm, ha, g, w_bm, w_ba, w_o, ln_g, ln_b, wr_t, br, l, alpha):
    n, d = h.shape
    tm = MOE_ROW_TILE
    row = lambda width: pl.BlockSpec((tm, width), lambda r: (r, 0))
    full = lambda a: pl.BlockSpec(a.shape, lambda r: (0,) * a.ndim)
    args = (h, hm, ha, g, w_bm, w_ba, w_o, ln_g, ln_b, wr_t, br)
    return pl.pallas_call(
        functools.partial(_merge_kernel, alpha),
        grid=(n // tm,),
        in_specs=[row(d), row(M_WIDTH), row(A_WIDTH), row(2 * d)] +
                 [_layer_block(a, l) for a in args[4:9]] + [full(wr_t), full(br)],
        out_specs=[row(d), row(LANES),
                   pl.BlockSpec((8, tm), lambda r: (0, r)),
                   pl.BlockSpec((1, 8, LANES), lambda r: (r, 0, 0))],
        out_shape=[jax.ShapeDtypeStruct((n, d), F32), jax.ShapeDtypeStruct((n, LANES), F32),
                   jax.ShapeDtypeStruct((8, n), jnp.int32),
                   jax.ShapeDtypeStruct((n // tm, 8, LANES), jnp.int32)],
        compiler_params=_params(("parallel",)),
        name="merge",
    )(*args)


def _moe_kernel(alpha, cap, cnt_ref, h_ref, comb_ref, bgt_ref, wg_ref, wu_ref, wd_ref, lg_ref, lb_ref,
                out_ref, xb_scr, cs_scr, yt_scr, tri_scr):
    r = pl.program_id(0)
    g = pl.program_id(1)
    rows = h_ref.shape[0]

    @pl.when((r == 0) & (g == 0))
    def _():
        t0 = lax.broadcasted_iota(jnp.int32, (rows, rows), 0)
        t1 = lax.broadcasted_iota(jnp.int32, (rows, rows), 1)
        tri_scr[...] = (t0 < t1).astype(BF16)

    @pl.when(g == 0)
    def _():
        xb_scr[...] = h_ref[...].astype(BF16)
        yt_scr[...] = jnp.zeros_like(yt_scr)
        c = comb_ref[...]
        for part in range(2):
            cb = c.astype(BF16)
            cs_scr[part] = cb
            c = c - cb.astype(F32)

    member = bgt_ref[0:1, :] == g
    mem8 = jnp.broadcast_to(jnp.where(member, 1.0, 0.0), (8, rows)).astype(BF16)
    rank = jnp.dot(mem8, tri_scr[...], preferred_element_type=F32)[0:1, :].astype(jnp.int32)
    n_blocks = (cnt_ref[r * N_GROUPS + g] + cap - 1) // cap
    lane = lax.broadcasted_iota(jnp.int32, (cap, LANES), 1)
    tn = (((0,), (0,)), ((), ()))

    def block(b, carry):
        slot = lax.broadcasted_iota(jnp.int32, (cap, rows), 0) + b * cap
        onehot = jnp.where(member & (rank == slot), 1.0, 0.0).astype(BF16)
        xg = jnp.dot(onehot, xb_scr[...], preferred_element_type=F32).astype(BF16)
        cw = jnp.dot(onehot, cs_scr[0], preferred_element_type=F32)
        cw = cw + jnp.dot(onehot, cs_scr[1], preferred_element_type=F32)
        y = jnp.zeros((cap, out_ref.shape[1]), F32)
        for e in range(GROUP_SIZE):
            gate = jnp.dot(xg, wg_ref[e], preferred_element_type=F32)
            up = jnp.dot(xg, wu_ref[e], preferred_element_type=F32)
            he = gate * _sigmoid(gate) * up
            o = jnp.dot(he.astype(BF16), wd_ref[e], preferred_element_type=F32)
            ce = jnp.sum(jnp.where(lane == g * GROUP_SIZE + e, cw, 0.0), axis=1, keepdims=True)
            y = y + ce * o
        yt_scr[...] += lax.dot_general(onehot, y.astype(BF16), tn, preferred_element_type=F32)
        return carry

    lax.fori_loop(0, n_blocks, block, 0)

    @pl.when(g == pl.num_programs(1) - 1)
    def _():
        z = alpha * h_ref[...] + yt_scr[...]
        out_ref[...] = _layer_norm_rows(z, lg_ref[...], lb_ref[...], 1e-5)


def _moe(h, comb, bgt, counts, w_gate, w_up, w_down, ln_g, ln_b, l, alpha):
    n, d = h.shape
    de = w_gate.shape[-1]
    tm = MOE_ROW_TILE
    cnt = counts[:, 0:N_GROUPS, 0].reshape(-1)
    grid_spec = pltpu.PrefetchScalarGridSpec(
        num_scalar_prefetch=1,
        grid=(n // tm, N_GROUPS),
        in_specs=[pl.BlockSpec((tm, d), lambda r, g, c: (r, 0)),
                  pl.BlockSpec((tm, LANES), lambda r, g, c: (r, 0)),
                  pl.BlockSpec((8, tm), lambda r, g, c: (0, r)),
                  pl.BlockSpec((None, GROUP_SIZE, d, de), lambda r, g, c: (l, g, 0, 0)),
                  pl.BlockSpec((None, GROUP_SIZE, d, de), lambda r, g, c: (l, g, 0, 0)),
                  pl.BlockSpec((None, GROUP_SIZE, de, d), lambda r, g, c: (l, g, 0, 0)),
                  _layer_block(ln_g, l), _layer_block(ln_b, l)],
        out_specs=pl.BlockSpec((tm, d), lambda r, g, c: (r, 0)),
        scratch_shapes=[pltpu.VMEM((tm, d), BF16), pltpu.VMEM((2, tm, LANES), BF16),
                        pltpu.VMEM((tm, d), F32), pltpu.VMEM((tm, tm), BF16)])
    return pl.pallas_call(
        functools.partial(_moe_kernel, alpha, MOE_CAP),
        grid_spec=grid_spec,
        out_shape=jax.ShapeDtypeStruct((n, d), F32),
        compiler_params=_params(("arbitrary", "arbitrary")),
        name="moe",
    )(cnt, h, comb, bgt, w_gate, w_up, w_down, ln_g, ln_b)


def _with_meta(real, meta, n_pad):
    pad = n_pad - real.shape[0] - meta.shape[0]
    return jnp.concatenate([real, meta, jnp.zeros((pad, real.shape[1]), real.dtype)], axis=0)


def kernel(x, meta_tokens, ln_in_g, ln_in_b, w_in, conv_w, b_if, mnorm_g, kv_norm_g, w_uk, w_uv,
           w_branch_m, w_branch_a, w_out, ln1_g, ln1_b, w_router, b_router, w_gate, w_up, w_down,
           ln2_g, ln2_b, rel_bias):
    batch, seq, d = x.shape
    depth = w_in.shape[0]
    alpha = (2 * depth) ** 0.25
    n_real = batch * seq
    n_meta = batch * N_META
    tile = math.lcm(ROW_TILE, MOE_ROW_TILE)
    n_pad = -(-(n_real + n_meta) // tile) * tile
    assert seq % LANES == 0 and n_real % N_META == 0

    assert n_real % ROW_TILE == 0
    tail = _with_meta(jnp.tile(meta_tokens.astype(x.dtype), (batch, 1)), jnp.zeros((0, d), x.dtype),
                      n_pad - n_real)
    h = _input_ln(x.reshape(n_real, d), tail, ln_in_g, ln_in_b)
    tables = _bias_tables(rel_bias)
    row3 = lambda a: a.astype(F32)[:, None, :]
    w_packed, brow, kv_g = _pack_w_in(w_in), _pack_b_if(b_if), row3(kv_norm_g)
    conv_f, mnorm = conv_w.astype(F32), row3(mnorm_g)
    wuk_t = jnp.swapaxes(w_uk, 2, 3).astype(BF16)
    wuv = w_uv.astype(BF16)
    w_bm, w_ba, w_o = w_branch_m.astype(BF16), w_branch_a.astype(BF16), w_out.astype(BF16)
    wr_t = jnp.zeros((LANES, d), F32).at[0:N_EXPERTS].set(w_router.T).astype(BF16)
    br = jnp.zeros((LANES, 1), F32).at[0:N_EXPERTS, 0].set(b_router)
    wg, wu, wd = w_gate.astype(BF16), w_up.astype(BF16), w_down.astype(BF16)
    g1, b1, g2, b2 = row3(ln1_g), row3(ln1_b), row3(ln2_g), row3(ln2_b)
    for l in range(depth):
        pm, qa, qi, c, sm, g = _project(h, w_packed, kv_g, brow, l)
        hm_real, hm_meta = _mlstm(pm, sm, conv_f, mnorm, l, batch, seq)
        ha_real, ha_meta = _dsa(qa, qi, sm, c, wuk_t, wuv, tables, l, batch, seq)
        hm = _with_meta(hm_real, hm_meta, n_pad)
        ha = _with_meta(ha_real, ha_meta, n_pad)
        h1, comb, bgt, counts = _merge(h, hm, ha, g, w_bm, w_ba, w_o, g1, b1, wr_t, br, l, alpha)
        h = _moe(h1, comb, bgt, counts, wg, wu, wd, g2, b2, l, alpha)
    return h[:n_real].reshape(batch, seq, d)
```

```python
import functools
import math

import numpy as np
import jax
import jax.numpy as jnp
from jax import lax
from jax.experimental import pallas as pl
from jax.experimental.pallas import tpu as pltpu

F32 = jnp.float32
BF16 = jnp.bfloat16

N_META = 16
M_HEADS = 4
M_HEAD_DIM = 128
M_WIDTH = M_HEADS * M_HEAD_DIM
CONV_WIDTH = 4
A_HEADS = 8
A_HEAD_DIM = 64
A_WIDTH = A_HEADS * A_HEAD_DIM
KV_RANK = 128
IDX_HEADS = 4
IDX_DIM = 64
IDX_SCALE = (IDX_HEADS * IDX_DIM) ** -0.5
TOPK_MAX = 256
REL_BUCKETS = 32
REL_MAX_EXACT = 16
REL_MAX_DIST = 128
N_EXPERTS = 16
N_GROUPS = 4
GROUP_SIZE = N_EXPERTS // N_GROUPS

LANES = 128
ROW_TILE = 256
MOE_ROW_TILE = 768
MOE_CAP = 224
M_CHUNK = 256
SCORE_CHUNK = 4
KEY_SUB = 4
ATT_GROUPS = 2
VMEM_LIMIT = 56 * 1024 * 1024
NEG_BIG = -1e30
INT_MIN = -2 ** 31
HALF_BIAS = 2 ** 15
LOG2E = math.log2(math.e)

D_MODEL = 1024
IN_WIDTHS = (("q_m", M_WIDTH), ("k_m", M_WIDTH), ("v_m", M_WIDTH), ("o_m", M_WIDTH), ("i_pre", M_HEADS),
             ("f_pre", M_HEADS), ("q_a", A_WIDTH), ("c_kv", KV_RANK), ("q_idx", IDX_HEADS * IDX_DIM),
             ("k_idx", IDX_DIM), ("w_idx", IDX_HEADS), ("g_m", D_MODEL), ("g_a", D_MODEL))
W_PM = 4 * M_WIDTH
W_G = 2 * D_MODEL
PK_PM = 0
PK_QA = PK_PM + W_PM
PK_QI = PK_QA + A_WIDTH
PK_CKV = PK_QI + IDX_HEADS * IDX_DIM
PK_SM = PK_CKV + KV_RANK
PK_G = PK_SM + LANES
PK_TOTAL = PK_G + W_G
SM_KIDX = 0
SM_WIDX = SM_KIDX + IDX_DIM
SM_IPRE = SM_WIDX + IDX_HEADS
SM_FPRE = SM_IPRE + M_HEADS
SM_USED = SM_FPRE + M_HEADS


def _params(sem):
    return pltpu.CompilerParams(dimension_semantics=sem, vmem_limit_bytes=VMEM_LIMIT)


def _sigmoid(x):
    return 1.0 / (1.0 + jnp.exp(-x))


def _layer_norm_rows(x, g, b, eps):
    mu = jnp.mean(x, axis=-1, keepdims=True)
    xc = x - mu
    var = jnp.mean(xc * xc, axis=-1, keepdims=True)
    return xc * lax.rsqrt(var + eps) * g + b


def _ln_kernel(real_tiles, x_ref, m_ref, g_ref, b_ref, o_ref):
    r = pl.program_id(0)

    @pl.when(r < real_tiles)
    def _():
        o_ref[...] = _layer_norm_rows(x_ref[...], g_ref[...], b_ref[...], 1e-5)

    @pl.when(r >= real_tiles)
    def _():
        o_ref[...] = _layer_norm_rows(m_ref[...], g_ref[...], b_ref[...], 1e-5)


def _input_ln(x, tail, g, b):
    n_real, d = x.shape
    real_tiles = n_real // ROW_TILE
    tiles = real_tiles + tail.shape[0] // ROW_TILE
    return pl.pallas_call(
        functools.partial(_ln_kernel, real_tiles),
        grid=(tiles,),
        in_specs=[pl.BlockSpec((ROW_TILE, d), lambda r: (jnp.minimum(r, real_tiles - 1), 0)),
                  pl.BlockSpec((ROW_TILE, d), lambda r: (jnp.maximum(r - real_tiles, 0), 0)),
                  pl.BlockSpec((1, d), lambda r: (0, 0)),
                  pl.BlockSpec((1, d), lambda r: (0, 0))],
        out_specs=pl.BlockSpec((ROW_TILE, d), lambda r: (r, 0)),
        out_shape=jax.ShapeDtypeStruct((tiles * ROW_TILE, d), F32),
        compiler_params=_params(("arbitrary",)),
        name="input_ln",
    )(x, tail, g.reshape(1, d), b.reshape(1, d))


def _proj_kernel(h_ref, w_ref, kvg_ref, brow_ref, pm_ref, qa_ref, qi_ref, c_ref, sm_ref, g_ref):
    x = h_ref[...].astype(BF16)

    def mm(lo, width):
        return jnp.dot(x, w_ref[:, lo:lo + width], preferred_element_type=F32)

    pm_ref[...] = mm(PK_PM, W_PM).astype(BF16)
    qa_ref[...] = mm(PK_QA, A_WIDTH).astype(BF16)
    qi_ref[...] = mm(PK_QI, IDX_HEADS * IDX_DIM).astype(BF16)
    ckv = mm(PK_CKV, KV_RANK)
    c_ref[...] = ckv * lax.rsqrt(jnp.mean(ckv * ckv, axis=-1, keepdims=True) + 1e-6) * kvg_ref[...]
    sm_ref[...] = mm(PK_SM, LANES) + brow_ref[...]
    g_ref[...] = mm(PK_G, W_G).astype(BF16)


def _pack_w_in(w):
    part, start = {}, 0
    for name, width in IN_WIDTHS:
        part[name] = w[..., start:start + width]
        start += width
    assert start == w.shape[-1] and w.shape[-2] == D_MODEL
    order = ("q_m", "k_m", "v_m", "o_m", "q_a", "q_idx", "c_kv", "k_idx", "w_idx", "i_pre", "f_pre")
    cols = [part[name] for name in order]
    cols += [jnp.zeros(w.shape[:-1] + (LANES - SM_USED,), w.dtype), part["g_m"], part["g_a"]]
    return jnp.concatenate(cols, axis=-1).astype(BF16)


def _pack_b_if(b_if):
    depth = b_if.shape[0]
    return jnp.concatenate([jnp.zeros((depth, SM_IPRE), F32), b_if.astype(F32),
                            jnp.zeros((depth, LANES - SM_USED), F32)], axis=1)[:, None, :]


def _layer_block(arr, l):
    zeros = (0,) * (arr.ndim - 1)
    return pl.BlockSpec((None,) + arr.shape[1:], lambda *idx: (l,) + zeros)


def _project(h, w_packed, kv_g, brow, l):
    n, d = h.shape
    row = lambda width: pl.BlockSpec((MOE_ROW_TILE, width), lambda r: (r, 0))
    shp = lambda width, dt: jax.ShapeDtypeStruct((n, width), dt)
    outs = ((W_PM, BF16), (A_WIDTH, BF16), (IDX_HEADS * IDX_DIM, BF16), (KV_RANK, F32), (LANES, F32), (W_G, BF16))
    return pl.pallas_call(
        _proj_kernel,
        grid=(n // MOE_ROW_TILE,),
        in_specs=[row(d), _layer_block(w_packed, l), _layer_block(kv_g, l), _layer_block(brow, l)],
        out_specs=[row(width) for width, _ in outs],
        out_shape=[shp(width, dt) for width, dt in outs],
        compiler_params=_params(("parallel",)),
        name="in_proj",
    )(h, w_packed, kv_g, brow)


def _log_sigmoid(f):
    return jnp.minimum(f, 0.0) - jnp.log1p(jnp.exp(-jnp.abs(f)))


def _row_from_col(col):
    L = col.shape[0]
    eye = lax.broadcasted_iota(jnp.int32, (L, L), 0) == lax.broadcasted_iota(jnp.int32, (L, L), 1)
    return jnp.sum(jnp.where(eye, jnp.broadcast_to(col, (L, L)), 0.0), axis=0, keepdims=True)


def _mlstm_heads(qs, ks, vs, lis, li_rows, lf_rows, cts, ns, ms):
    L = qs[0].shape[0]
    heads = range(len(qs))
    r = lax.broadcasted_iota(jnp.int32, (L, L), 0)
    c = lax.broadcasted_iota(jnp.int32, (L, L), 1)
    tril = c <= r
    eye = c == r
    nt = (((1,), (1,)), ((), ()))
    tn = (((0,), (0,)), ((), ()))
    b_cols = [jnp.sum(jnp.where(tril, jnp.broadcast_to(lf_rows[h], (L, L)), 0.0), axis=1, keepdims=True)
              for h in heads]
    b_rows = [jnp.sum(jnp.where(eye, jnp.broadcast_to(b_cols[h], (L, L)), 0.0), axis=0, keepdims=True)
              for h in heads]
    qb = [qs[h].astype(BF16) for h in heads]
    kb = [ks[h].astype(BF16) for h in heads]
    vb = [vs[h].astype(BF16) for h in heads]
    qk = [lax.dot_general(qb[h], kb[h], nt, preferred_element_type=F32) for h in heads]
    qc = [jnp.dot(qb[h], cts[h].astype(BF16), preferred_element_type=F32) for h in heads]
    d = [jnp.where(tril, b_cols[h] - b_rows[h] + li_rows[h], -jnp.inf) for h in heads]
    inter = [b_cols[h] + ms[h] for h in heads]
    m_t = [jnp.maximum(inter[h], jnp.max(d[h], axis=1, keepdims=True)) for h in heads]
    a = [jnp.exp(inter[h] - m_t[h]) for h in heads]
    w = [jnp.exp(d[h] - m_t[h]) * qk[h] for h in heads]
    wv = [jnp.dot(w[h].astype(BF16), vb[h], preferred_element_type=F32) for h in heads]
    den = [a[h] * jnp.sum(qs[h] * ns[h], axis=1, keepdims=True) + jnp.sum(w[h], axis=1, keepdims=True)
           for h in heads]
    hs = [(a[h] * qc[h] + wv[h]) / jnp.maximum(jnp.abs(den[h]), jnp.exp(-m_t[h])) for h in heads]
    b_last = [b_cols[h][L - 1:L, :] for h in heads]
    g = [b_last[h] - b_cols[h] + lis[h] for h in heads]
    m_new = [jnp.maximum(b_last[h] + ms[h], jnp.max(g[h], axis=0, keepdims=True)) for h in heads]
    decay = [jnp.exp(b_last[h] + ms[h] - m_new[h]) for h in heads]
    kw = [ks[h] * jnp.exp(g[h] - m_new[h]) for h in heads]
    ct_new = [decay[h] * cts[h] + lax.dot_general(kw[h].astype(BF16), vb[h], tn, preferred_element_type=F32)
              for h in heads]
    n_new = [decay[h] * ns[h] + jnp.sum(kw[h], axis=0, keepdims=True) for h in heads]
    return hs, ct_new, n_new, m_new


def _mlstm_kernel(pm_ref, sm_ref, pmm_ref, smm_ref, cw_ref, mg_ref, out_ref, outm_ref,
                  ct_scr, n_scr, m_scr, x_scr):
    ci = pl.program_id(1)
    tail = 8

    def run_chunk(p_ref, s_ref, o_ref):
        L = p_ref.shape[0]
        heads = range(M_HEADS)
        x_scr[tail:tail + L, :] = p_ref[:, 0:2 * M_WIDTH].astype(F32)
        conv = cw_ref[0:1, :] * x_scr[tail - 3:tail - 3 + L, :]
        for j in range(1, CONV_WIDTH):
            conv = conv + cw_ref[j:j + 1, :] * x_scr[tail - 3 + j:tail - 3 + j + L, :]
        x_scr[0:tail, :] = x_scr[L:L + tail, :]
        qk = conv * _sigmoid(conv)
        cols = lambda base, h: slice(base + h * M_HEAD_DIM, base + (h + 1) * M_HEAD_DIM)
        lis = [s_ref[:, SM_IPRE + h:SM_IPRE + h + 1] for h in heads]
        if L % LANES == 0:
            gates_t = s_ref[...].T
            lf_all = _log_sigmoid(gates_t[SM_FPRE:SM_FPRE + M_HEADS, :])
            li_rows = [gates_t[SM_IPRE + h:SM_IPRE + h + 1, :] for h in heads]
            lf_rows = [lf_all[h:h + 1, :] for h in heads]
        else:
            li_rows = [_row_from_col(lis[h]) for h in heads]
            lf_rows = [_row_from_col(_log_sigmoid(s_ref[:, SM_FPRE + h:SM_FPRE + h + 1])) for h in heads]
        hs, ct_new, n_new, m_new = _mlstm_heads(
            [qk[:, cols(0, h)] * (M_HEAD_DIM ** -0.5) for h in heads],
            [qk[:, cols(M_WIDTH, h)] for h in heads],
            [p_ref[:, cols(2 * M_WIDTH, h)].astype(F32) for h in heads],
            lis, li_rows, lf_rows,
            [ct_scr[h] for h in heads], [n_scr[h:h + 1, :] for h in heads],
            [m_scr[h:h + 1, 0:1] for h in heads])
        for h in heads:
            ct_scr[h] = ct_new[h]
            n_scr[h:h + 1, :] = n_new[h]
            m_scr[h:h + 1, :] = jnp.broadcast_to(m_new[h], (1, LANES))
        mus = [jnp.mean(hs[h], axis=-1, keepdims=True) for h in heads]
        hcs = [hs[h] - mus[h] for h in heads]
        vars_ = [jnp.mean(hcs[h] * hcs[h], axis=-1, keepdims=True) for h in heads]
        for h in heads:
            o_gate = _sigmoid(p_ref[:, cols(3 * M_WIDTH, h)].astype(F32))
            o_ref[:, cols(0, h)] = (hcs[h] * lax.rsqrt(vars_[h] + 1e-5) * mg_ref[:, cols(0, h)]
                                    * o_gate).astype(o_ref.dtype)

    @pl.when(ci == 0)
    def _():
        ct_scr[...] = jnp.zeros_like(ct_scr)
        n_scr[...] = jnp.zeros_like(n_scr)
        m_scr[...] = jnp.zeros_like(m_scr)
        x_scr[...] = jnp.zeros_like(x_scr)
        run_chunk(pmm_ref, smm_ref, outm_ref)

    run_chunk(pm_ref, sm_ref, out_ref)


def _mlstm(pm, sm, conv_w, mnorm_g, l, batch, seq):
    n_real = batch * seq
    chunk = min(M_CHUNK, seq)
    nc = seq // chunk
    meta_blk = n_real // N_META
    out_real, out_meta = pl.pallas_call(
        _mlstm_kernel,
        grid=(batch, nc),
        in_specs=[pl.BlockSpec((chunk, 4 * M_WIDTH), lambda b, c: (b * nc + c, 0)),
                  pl.BlockSpec((chunk, LANES), lambda b, c: (b * nc + c, 0)),
                  pl.BlockSpec((N_META, 4 * M_WIDTH), lambda b, c: (meta_blk + b, 0)),
                  pl.BlockSpec((N_META, LANES), lambda b, c: (meta_blk + b, 0)),
                  _layer_block(conv_w, l), _layer_block(mnorm_g, l)],
        out_specs=[pl.BlockSpec((chunk, M_WIDTH), lambda b, c: (b * nc + c, 0)),
                   pl.BlockSpec((N_META, M_WIDTH), lambda b, c: (b, 0))],
        out_shape=[jax.ShapeDtypeStruct((n_real, M_WIDTH), BF16),
                   jax.ShapeDtypeStruct((batch * N_META, M_WIDTH), BF16)],
        scratch_shapes=[pltpu.VMEM((M_HEADS, M_HEAD_DIM, M_HEAD_DIM), F32),
                        pltpu.VMEM((8, LANES), F32),
                        pltpu.VMEM((8, LANES), F32),
                        pltpu.VMEM((chunk + 8, 2 * M_WIDTH), F32)],
        compiler_params=_params(("arbitrary", "arbitrary")),
        name="mlstm",
    )(pm, sm, pm, sm, conv_w, mnorm_g)
    return out_real, out_meta


def _rel_bucket_np(dist):
    n = np.maximum(dist, 0)
    nf = np.maximum(n, REL_MAX_EXACT).astype(np.float32)
    large = REL_MAX_EXACT + (np.log(nf / np.float32(REL_MAX_EXACT)) /
                             np.float32(math.log(REL_MAX_DIST / REL_MAX_EXACT))
                             * np.float32(REL_BUCKETS - REL_MAX_EXACT)).astype(np.int32)
    large = np.minimum(large, REL_BUCKETS - 1)
    return np.where(n < REL_MAX_EXACT, n, large).astype(np.int32)


def _bias_tables(rel_bias):
    q = np.arange(LANES)[:, None]
    k = np.arange(LANES)[None, :]
    far = 4 * LANES
    assert (_rel_bucket_np(np.arange(LANES + 1, far)) == REL_BUCKETS - 1).all()
    far_idx = np.full((LANES, LANES), REL_BUCKETS - 1, np.int32)
    near_idx = np.stack([_rel_bucket_np(q - k), _rel_bucket_np(LANES + q - k), far_idx])
    meta_idx = np.stack([_rel_bucket_np(q + N_META - np.minimum(k, N_META - 1)), far_idx])
    mq = np.arange(N_META)[:, None]
    mm_idx = _rel_bucket_np(mq - np.minimum(k, N_META - 1))
    rb = rel_bias.astype(F32)

    def lookup(idx, values):
        onehot = jnp.asarray(idx[..., None] == np.arange(REL_BUCKETS), F32)
        out = jnp.einsum('...b,bh->...h', onehot, values, precision=lax.Precision.HIGHEST)
        return jnp.moveaxis(out, -1, -3)

    rel_values = (rb - rb[REL_BUCKETS - 1][None, :]) * LOG2E
    return lookup(near_idx, rel_values), lookup(meta_idx, rel_values), lookup(mm_idx, rb)


def _dsa_kernel(top_k, qa_ref, qi_ref, sm_ref, cb_ref, smb_ref, cm_ref, qam_ref,
                wuk_ref, wuv_ref, near_ref, metab_ref, mmb_ref,
                out_ref, outm_ref,
                keys_scr, hi_scr, lo_scr, lg_scr, lgm_scr, rawa_scr, rawb_scr, pa_scr, pb_scr,
                caug_scr, ct_scr, kt_scr, cmaug_scr, cmt_scr, qs_scr, mx_scr, mrep_scr, acc_scr):
    i = pl.program_id(1)
    T = LANES
    H = A_HEADS
    col = lax.broadcasted_iota(jnp.int32, (T, T), 1)
    row = lax.broadcasted_iota(jnp.int32, (T, T), 0)
    nt = (((1,), (1,)), ((), ()))

    def q_latent(qa, hd, scale):
        ql = jnp.dot(qa[:, hd * A_HEAD_DIM:(hd + 1) * A_HEAD_DIM], wuk_ref[hd],
                     preferred_element_type=F32)
        return (ql * scale).astype(BF16)

    def ones_column(n):
        return jnp.where(lax.broadcasted_iota(jnp.int32, (n, T), 1) == 0, 1.0, 0.0).astype(BF16)

    @pl.when(i == 0)
    def _():
        caug_scr[:, 0:KV_RANK] = cb_ref[...].astype(BF16)
        caug_scr[:, KV_RANK:KV_RANK + T] = ones_column(caug_scr.shape[0])

        def transpose_keys(blk, carry):
            rows = pl.ds(pl.multiple_of(blk * KEY_SUB * T, KEY_SUB * T), KEY_SUB * T)
            ct_scr[blk] = cb_ref[rows, :].T.astype(BF16)
            kt_scr[blk] = smb_ref[rows, :].T[SM_KIDX:SM_KIDX + IDX_DIM, :].astype(BF16)
            return carry

        lax.fori_loop(0, ct_scr.shape[0], transpose_keys, 0)
        cm_pad = jnp.concatenate([cm_ref[...], jnp.zeros((T - N_META, KV_RANK), F32)], axis=0)
        cmaug_scr[:, 0:KV_RANK] = cm_pad.astype(BF16)
        cmt_scr[...] = cm_pad.T.astype(BF16)
        cmaug_scr[:, KV_RANK:KV_RANK + T] = ones_column(T)
        cmk = cmaug_scr[:, 0:KV_RANK]
        qam = qam_ref[...]
        mrow = lax.broadcasted_iota(jnp.int32, (N_META, T), 0)
        mcol = lax.broadcasted_iota(jnp.int32, (N_META, T), 1)
        for hd in range(H):
            lg = lax.dot_general(q_latent(qam, hd, A_HEAD_DIM ** -0.5), cmk, nt,
                                 preferred_element_type=F32) + mmb_ref[hd]
            lg = jnp.where(mcol <= mrow, lg, NEG_BIG)
            p = jnp.exp(lg - jnp.max(lg, axis=1, keepdims=True))
            p = p / jnp.sum(p, axis=1, keepdims=True)
            o = jnp.dot(p.astype(BF16), cmk, preferred_element_type=F32)
            outm_ref[:, hd * A_HEAD_DIM:(hd + 1) * A_HEAD_DIM] = jnp.dot(
                o.astype(BF16), wuv_ref[hd], preferred_element_type=F32).astype(outm_ref.dtype)

    qa = qa_ref[...]
    for hd in range(H):
        qs_scr[hd * T:(hd + 1) * T, :] = q_latent(qa, hd, A_HEAD_DIM ** -0.5 * LOG2E)
    qi = qi_ref[...]
    wv = sm_ref[:, SM_WIDX:SM_WIDX + IDX_HEADS] * IDX_SCALE
    t_col = i * T + lax.broadcasted_iota(jnp.int32, (T, 1), 0)
    n_chunks = (i + SCORE_CHUNK) // SCORE_CHUNK
    CW = SCORE_CHUNK * T

    def run_pipelined(n, produce, consume, buf_a, buf_b):
        last = n - 1
        produce(0, buf_a)

        def body(t, carry):
            s = 2 * t
            produce(jnp.minimum(s + 1, last), buf_b)
            consume(s, buf_a)
            produce(jnp.minimum(s + 2, last), buf_a)
            consume(s + 1, buf_b)
            return carry

        lax.fori_loop(0, n // 2, body, 0)

        @pl.when(n % 2 == 1)
        def _():
            consume(last, buf_a)

    def score_matmul(cix, buf):
        kc_t = kt_scr[cix]
        for hh in range(IDX_HEADS):
            buf[hh * T:(hh + 1) * T, :] = jnp.dot(
                qi[:, hh * IDX_DIM:(hh + 1) * IDX_DIM], kc_t, preferred_element_type=F32)

    def score_keys(cix, buf):
        acc = jnp.zeros((T, CW), F32)
        for hh in range(IDX_HEADS):
            acc = acc + wv[:, hh:hh + 1] * jnp.maximum(buf[hh * T:(hh + 1) * T, :], 0.0)
        acc = jnp.where(acc == 0.0, 0.0, acc)
        bits = lax.bitcast_convert_type(acc, jnp.int32)
        key = jnp.where(bits < 0, bits ^ jnp.int32(0x7FFFFFFF), bits)
        s_idx = cix * CW + lax.broadcasted_iota(jnp.int32, (T, CW), 1)
        key = jnp.where(s_idx <= t_col, key, jnp.int32(INT_MIN))
        for u in range(SCORE_CHUNK):
            tile = key[:, u * T:(u + 1) * T]
            keys_scr[cix * SCORE_CHUNK + u] = tile
            tile_t = tile.T
            hi_scr[cix * SCORE_CHUNK + u] = lax.shift_right_arithmetic(tile_t, 16).astype(jnp.int16)
            lo_scr[cix * SCORE_CHUNK + u] = ((tile_t & 0xFFFF) - HALF_BIAS).astype(jnp.int16)

    run_pipelined(n_chunks, score_matmul, score_keys, rawa_scr, rawb_scr)

    def rep16(row_i32):
        return jnp.broadcast_to(row_i32, (T, T)).astype(jnp.int16)

    def count16(src_scr, pred_fn):
        def body(cix, cnt):
            for u in range(SCORE_CHUNK):
                hit = pred_fn(src_scr[cix * SCORE_CHUNK + u])
                cnt = cnt + jnp.where(hit, jnp.int16(1), jnp.int16(0))
            return cnt
        cnt = lax.fori_loop(0, n_chunks, body, jnp.zeros((T, T), jnp.int16))
        return jnp.sum(cnt.astype(F32), axis=0, keepdims=True)

    def search16(src_scr, k_row):
        def bit_body(bi, carry):
            ans, above = carry
            cand_u = ans | lax.shift_left(jnp.int32(1), jnp.int32(15) - bi)
            cand = rep16(cand_u - HALF_BIAS)
            total = count16(src_scr, lambda x: x >= cand)
            take = total >= k_row
            return jnp.where(take, cand_u, ans), jnp.where(take, above, total)
        return lax.fori_loop(0, 16, bit_body, (jnp.zeros((1, T), jnp.int32), jnp.zeros((1, T), F32)))

    k_row = jnp.full((1, T), float(top_k), F32)
    hi_u, hi_above = search16(hi_scr, k_row)
    hi_s = hi_u - HALF_BIAS
    hi_rep = rep16(hi_s)
    k_low = k_row - hi_above

    def band_body(cix, carry):
        for u in range(SCORE_CHUNK):
            j = cix * SCORE_CHUNK + u
            lo_scr[j] = jnp.where(hi_scr[j] == hi_rep, lo_scr[j], jnp.int16(-HALF_BIAS))
        return carry

    lax.fori_loop(0, n_chunks, band_body, 0)
    lo_u, lo_above = search16(lo_scr, k_low)
    need_row = k_low - lo_above
    thr_row = lax.shift_left(hi_s, 16) | lo_u
    thr = jnp.broadcast_to(thr_row, (T, T)).T
    need = jnp.broadcast_to(need_row, (T, T)).T

    mx_scr[...] = jnp.full_like(mx_scr, NEG_BIG)
    acc_scr[...] = jnp.zeros_like(acc_scr)
    hg = H // ATT_GROUPS
    groups = [slice(g * hg * T, (g + 1) * hg * T) for g in range(ATT_GROUPS)]
    meta_sel = jnp.minimum(i, 1)

    def max_pass(ck_t, madds, bias_fns, store):
        lgs = [jnp.dot(qs_scr[rs, :], ck_t, preferred_element_type=F32) for rs in groups]
        for hd in range(H):
            rs = slice(hd * T, (hd + 1) * T)
            lo = (hd % hg) * T
            mx = mx_scr[rs, :]
            for u in range(len(madds)):
                x = lgs[hd // hg][lo:lo + T, u * T:(u + 1) * T] + madds[u]
                if bias_fns[u] is not None:
                    x = x + bias_fns[u](hd)
                store(rs, u, x)
                mx = jnp.maximum(mx, x)
            mx_scr[rs, :] = mx

    def sum_pass(c_aug, n_sub, load):
        for grp in groups:
            m_rep = mrep_scr[grp, :]
            ph = [jnp.exp2(load(grp, u) - m_rep).astype(BF16) for u in range(n_sub)]
            p = ph[0] if n_sub == 1 else jnp.concatenate(ph, axis=1)
            acc_scr[grp, :] += jnp.dot(p, c_aug, preferred_element_type=F32)

    def key_rows(step):
        return pl.ds(pl.multiple_of(step * KEY_SUB * T, KEY_SUB * T), KEY_SUB * T)

    upper = (row < col).astype(BF16)

    def mask_step(step, seen, near):
        madds, bias_fns = [], []
        for u in range(KEY_SUB):
            j = KEY_SUB * step + u
            kk = keys_scr[j]
            eq = kk == thr
            eqf = jnp.where(eq, 1.0, 0.0)
            before = jnp.dot(eqf.astype(BF16), upper, preferred_element_type=F32) + seen
            sel = (kk > thr) | (eq & (before < need))
            if near:
                sel = sel & ((j * T + col) <= (i * T + row))
                dsel = jnp.clip(i - j, 0, 2)
                bias_fns.append(lambda hd, dsel=dsel: near_ref[dsel, hd])
            else:
                bias_fns.append(None)
            seen = seen + jnp.sum(eqf, axis=1, keepdims=True)
            madds.append(jnp.where(sel, 0.0, NEG_BIG))

        def store(rs, u, x):
            lg_scr[step, rs, u * T:(u + 1) * T] = x

        max_pass(ct_scr[step], madds, bias_fns, store)
        return seen

    def weights_step(step, buf):
        for grp in groups:
            m_rep = mrep_scr[grp, :]
            for u in range(KEY_SUB):
                buf[grp, u * T:(u + 1) * T] = jnp.exp2(lg_scr[step, grp, u * T:(u + 1) * T] - m_rep)

    def accumulate_step(step, buf):
        c_aug = caug_scr[key_rows(step), :]
        for grp in groups:
            acc_scr[grp, :] += jnp.dot(buf[grp, :].astype(BF16), c_aug, preferred_element_type=F32)

    def store_meta(rs, u, x):
        lgm_scr[rs, :] = x

    n_far = jnp.maximum(i - 1, 0) // KEY_SUB
    n_steps = (i + KEY_SUB) // KEY_SUB
    max_pass(cmt_scr[...], [jnp.where(col < N_META, 0.0, NEG_BIG)],
             [lambda hd: metab_ref[meta_sel, hd]], store_meta)
    seen = lax.fori_loop(0, n_far, lambda s, c: mask_step(s, c, False), jnp.zeros((T, 1), F32))
    lax.fori_loop(n_far, n_steps, lambda s, c: mask_step(s, c, True), seen)
    mrep_scr[...] = jnp.broadcast_to(jnp.max(mx_scr[...], axis=1, keepdims=True), mrep_scr.shape)
    sum_pass(cmaug_scr[...], 1, lambda grp, u: lgm_scr[grp, :])
    run_pipelined(n_steps, weights_step, accumulate_step, pa_scr, pb_scr)

    for hd in range(H):
        rs = slice(hd * T, (hd + 1) * T)
        o = acc_scr[rs, 0:KV_RANK] / acc_scr[rs, KV_RANK:KV_RANK + 1]
        out_ref[:, hd * A_HEAD_DIM:(hd + 1) * A_HEAD_DIM] = jnp.dot(
            o.astype(BF16), wuv_ref[hd], preferred_element_type=F32).astype(out_ref.dtype)


def _dsa(qa, qi, sm, c, wuk_t, wuv, tables, l, batch, seq):
    n_real = batch * seq
    nq = seq // LANES
    n_tiles = ((nq + SCORE_CHUNK - 1) // SCORE_CHUNK) * SCORE_CHUNK
    top_k = min(TOPK_MAX, seq // 4)
    meta_blk = n_real // N_META
    near, metab, mmb = tables
    full = lambda a: pl.BlockSpec(a.shape, lambda b, i: (0,) * a.ndim)
    assert seq % (SCORE_CHUNK * LANES) == 0 and nq % KEY_SUB == 0 and SCORE_CHUNK == KEY_SUB
    out_real, out_meta = pl.pallas_call(
        functools.partial(_dsa_kernel, top_k),
        grid=(batch, nq),
        in_specs=[pl.BlockSpec((LANES, A_WIDTH), lambda b, i: (b * nq + i, 0)),
                  pl.BlockSpec((LANES, IDX_HEADS * IDX_DIM), lambda b, i: (b * nq + i, 0)),
                  pl.BlockSpec((LANES, LANES), lambda b, i: (b * nq + i, 0)),
                  pl.BlockSpec((seq, KV_RANK), lambda b, i: (b, 0)),
                  pl.BlockSpec((seq, LANES), lambda b, i: (b, 0)),
                  pl.BlockSpec((N_META, KV_RANK), lambda b, i: (meta_blk + b, 0)),
                  pl.BlockSpec((N_META, A_WIDTH), lambda b, i: (meta_blk + b, 0)),
                  _layer_block(wuk_t, l), _layer_block(wuv, l), full(near), full(metab), full(mmb)],
        out_specs=[pl.BlockSpec((LANES, A_WIDTH), lambda b, i: (b * nq + i, 0)),
                   pl.BlockSpec((N_META, A_WIDTH), lambda b, i: (b, 0))],
        out_shape=[jax.ShapeDtypeStruct((n_real, A_WIDTH), BF16),
                   jax.ShapeDtypeStruct((batch * N_META, A_WIDTH), BF16)],
        scratch_shapes=[pltpu.VMEM((n_tiles, LANES, LANES), jnp.int32),
                        pltpu.VMEM((n_tiles, LANES, LANES), jnp.int16),
                        pltpu.VMEM((n_tiles, LANES, LANES), jnp.int16),
                        pltpu.VMEM((nq // KEY_SUB, A_HEADS * LANES, KEY_SUB * LANES), F32),
                        pltpu.VMEM((A_HEADS * LANES, LANES), F32),
                        pltpu.VMEM((IDX_HEADS * LANES, SCORE_CHUNK * LANES), F32),
                        pltpu.VMEM((IDX_HEADS * LANES, SCORE_CHUNK * LANES), F32),
                        pltpu.VMEM((A_HEADS * LANES, KEY_SUB * LANES), F32),
                        pltpu.VMEM((A_HEADS * LANES, KEY_SUB * LANES), F32),
                        pltpu.VMEM((seq, KV_RANK + LANES), BF16),
                        pltpu.VMEM((nq // KEY_SUB, KV_RANK, KEY_SUB * LANES), BF16),
                        pltpu.VMEM((nq // KEY_SUB, IDX_DIM, KEY_SUB * LANES), BF16),
                        pltpu.VMEM((LANES, KV_RANK + LANES), BF16),
                        pltpu.VMEM((KV_RANK, LANES), BF16),
                        pltpu.VMEM((A_HEADS * LANES, KV_RANK), BF16),
                        pltpu.VMEM((A_HEADS * LANES, LANES), F32),
                        pltpu.VMEM((A_HEADS * LANES, LANES), F32),
                        pltpu.VMEM((A_HEADS * LANES, KV_RANK + LANES), F32)],
        compiler_params=_params(("arbitrary", "arbitrary")),
        name="dsa",
    )(qa, qi, sm, c, sm, c, qa, wuk_t, wuv, near, metab, mmb)
    return out_real, out_meta


def _merge_kernel(alpha, h_ref, hm_ref, ha_ref, g_ref, wbm_ref, wba_ref, wo_ref, lg_ref, lb_ref,
                  wr_ref, br_ref, h1_ref, comb_ref, bgt_ref, cnt_ref):
    d = h_ref.shape[1]
    gm = _sigmoid(g_ref[:, 0:d].astype(F32))
    ga = _sigmoid(g_ref[:, d:2 * d].astype(F32))
    y = gm * jnp.dot(hm_ref[...], wbm_ref[...], preferred_element_type=F32) + \
        ga * jnp.dot(ha_ref[...], wba_ref[...], preferred_element_type=F32)
    z = alpha * h_ref[...] + jnp.dot(y.astype(BF16), wo_ref[...], preferred_element_type=F32)
    h1 = _layer_norm_rows(z, lg_ref[...], lb_ref[...], 1e-5)
    h1_ref[...] = h1

    tm = h1.shape[0]
    logits_t = lax.dot_general(wr_ref[...], h1.astype(BF16), (((1,), (1,)), ((), ())),
                               preferred_element_type=F32)
    scores = _sigmoid(logits_t[0:N_EXPERTS, :])
    sel = scores + br_ref[0:N_EXPERTS, :]
    best = None
    for gidx in range(N_GROUPS):
        r0, r1, r2, r3 = (sel[gidx * GROUP_SIZE + u:gidx * GROUP_SIZE + u + 1, :] for u in range(4))
        a, b = jnp.maximum(r0, r1), jnp.minimum(r0, r1)
        c, dd = jnp.maximum(r2, r3), jnp.minimum(r2, r3)
        gs = jnp.maximum(a, c) + jnp.maximum(jnp.minimum(a, c), jnp.maximum(b, dd))
        if best is None:
            best, bg = gs, jnp.zeros((1, tm), jnp.int32)
        else:
            upd = gs > best
            bg = jnp.where(upd, gidx, bg)
            best = jnp.where(upd, gs, best)
    eidx = lax.broadcasted_iota(jnp.int32, (N_EXPERTS, tm), 0)
    masked = jnp.where((eidx // GROUP_SIZE) == bg, sel, -jnp.inf)
    v1 = jnp.max(masked, axis=0, keepdims=True)
    i1 = jnp.min(jnp.where(masked == v1, eidx, N_EXPERTS), axis=0, keepdims=True)
    masked2 = jnp.where(eidx == i1, -jnp.inf, masked)
    v2 = jnp.max(masked2, axis=0, keepdims=True)
    i2 = jnp.min(jnp.where(masked2 == v2, eidx, N_EXPERTS), axis=0, keepdims=True)
    s1 = jnp.sum(jnp.where(eidx == i1, scores, 0.0), axis=0, keepdims=True)
    s2 = jnp.sum(jnp.where(eidx == i2, scores, 0.0), axis=0, keepdims=True)
    tot = s1 + s2
    comb_t = jnp.where(eidx == i1, s1 / tot, 0.0) + jnp.where(eidx == i2, s2 / tot, 0.0)
    comb_pad = jnp.concatenate([comb_t, jnp.zeros((LANES - N_EXPERTS, tm), F32)], axis=0)
    comb_ref[...] = comb_pad.T
    bgt_ref[...] = jnp.broadcast_to(bg, (8, tm))
    gidx8 = lax.broadcasted_iota(jnp.int32, (8, tm), 0)
    counts = jnp.sum(jnp.where(gidx8 == bg, 1.0, 0.0), axis=1, keepdims=True)
    cnt_ref[0] = jnp.broadcast_to(counts, (8, LANES)).astype(jnp.int32)


def _merge(h, hm, ha, g, w_bm, w_ba, w_o, ln_g, ln_b, wr_t, br, l, alpha):
    n, d = h.shape
    tm = MOE_ROW_TILE
    row = lambda width: pl.BlockSpec((tm, width), lambda r: (r, 0))
    full = lambda a: pl.BlockSpec(a.shape, lambda r: (0,) * a.ndim)
    args = (h, hm, ha, g, w_bm, w_ba, w_o, ln_g, ln_b, wr_t, br)
    return pl.pallas_call(
        functools.partial(_merge_kernel, alpha),
        grid=(n // tm,),
        in_specs=[row(d), row(M_WIDTH), row(A_WIDTH), row(2 * d)] +
                 [_layer_block(a, l) for a in args[4:9]] + [full(wr_t), full(br)],
        out_specs=[row(d), row(LANES),
                   pl.BlockSpec((8, tm), lambda r: (0, r)),
                   pl.BlockSpec((1, 8, LANES), lambda r: (r, 0, 0))],
        out_shape=[jax.ShapeDtypeStruct((n, d), F32), jax.ShapeDtypeStruct((n, LANES), F32),
                   jax.ShapeDtypeStruct((8, n), jnp.int32),
                   jax.ShapeDtypeStruct((n // tm, 8, LANES), jnp.int32)],
        compiler_params=_params(("parallel",)),
        name="merge",
    )(*args)


def _moe_kernel(alpha, cap, cnt_ref, h_ref, comb_ref, bgt_ref, wg_ref, wu_ref, wd_ref, lg_ref, lb_ref,
                out_ref, xb_scr, cs_scr, yt_scr, tri_scr):
    r = pl.program_id(0)
    g = pl.program_id(1)
    rows = h_ref.shape[0]

    @pl.when((r == 0) & (g == 0))
    def _():
        t0 = lax.broadcasted_iota(jnp.int32, (rows, rows), 0)
        t1 = lax.broadcasted_iota(jnp.int32, (rows, rows), 1)
        tri_scr[...] = (t0 < t1).astype(BF16)

    @pl.when(g == 0)
    def _():
        xb_scr[...] = h_ref[...].astype(BF16)
        yt_scr[...] = jnp.zeros_like(yt_scr)
        c = comb_ref[...]
        for part in range(2):
            cb = c.astype(BF16)
            cs_scr[part] = cb
            c = c - cb.astype(F32)

    member = bgt_ref[0:1, :] == g
    mem8 = jnp.broadcast_to(jnp.where(member, 1.0, 0.0), (8, rows)).astype(BF16)
    rank = jnp.dot(mem8, tri_scr[...], preferred_element_type=F32)[0:1, :].astype(jnp.int32)
    n_blocks = (cnt_ref[r * N_GROUPS + g] + cap - 1) // cap
    lane = lax.broadcasted_iota(jnp.int32, (cap, LANES), 1)
    tn = (((0,), (0,)), ((), ()))

    def block(b, carry):
        slot = lax.broadcasted_iota(jnp.int32, (cap, rows), 0) + b * cap
        onehot = jnp.where(member & (rank == slot), 1.0, 0.0).astype(BF16)
        xg = jnp.dot(onehot, xb_scr[...], preferred_element_type=F32).astype(BF16)
        cw = jnp.dot(onehot, cs_scr[0], preferred_element_type=F32)
        cw = cw + jnp.dot(onehot, cs_scr[1], preferred_element_type=F32)
        y = jnp.zeros((cap, out_ref.shape[1]), F32)
        for e in range(GROUP_SIZE):
            gate = jnp.dot(xg, wg_ref[e], preferred_element_type=F32)
            up = jnp.dot(xg, wu_ref[e], preferred_element_type=F32)
            he = gate * _sigmoid(gate) * up
            o = jnp.dot(he.astype(BF16), wd_ref[e], preferred_element_type=F32)
            ce = jnp.sum(jnp.where(lane == g * GROUP_SIZE + e, cw, 0.0), axis=1, keepdims=True)
            y = y + ce * o
        yt_scr[...] += lax.dot_general(onehot, y.astype(BF16), tn, preferred_element_type=F32)
        return carry

    lax.fori_loop(0, n_blocks, block, 0)

    @pl.when(g == pl.num_programs(1) - 1)
    def _():
        z = alpha * h_ref[...] + yt_scr[...]
        out_ref[...] = _layer_norm_rows(z, lg_ref[...], lb_ref[...], 1e-5)


def _moe(h, comb, bgt, counts, w_gate, w_up, w_down, ln_g, ln_b, l, alpha):
    n, d = h.shape
    de = w_gate.shape[-1]
    tm = MOE_ROW_TILE
    cnt = counts[:, 0:N_GROUPS, 0].reshape(-1)
    grid_spec = pltpu.PrefetchScalarGridSpec(
        num_scalar_prefetch=1,
        grid=(n // tm, N_GROUPS),
        in_specs=[pl.BlockSpec((tm, d), lambda r, g, c: (r, 0)),
                  pl.BlockSpec((tm, LANES), lambda r, g, c: (r, 0)),
                  pl.BlockSpec((8, tm), lambda r, g, c: (0, r)),
                  pl.BlockSpec((None, GROUP_SIZE, d, de), lambda r, g, c: (l, g, 0, 0)),
                  pl.BlockSpec((None, GROUP_SIZE, d, de), lambda r, g, c: (l, g, 0, 0)),
                  pl.BlockSpec((None, GROUP_SIZE, de, d), lambda r, g, c: (l, g, 0, 0)),
                  _layer_block(ln_g, l), _layer_block(ln_b, l)],
        out_specs=pl.BlockSpec((tm, d), lambda r, g, c: (r, 0)),
        scratch_shapes=[pltpu.VMEM((tm, d), BF16), pltpu.VMEM((2, tm, LANES), BF16),
                        pltpu.VMEM((tm, d), F32), pltpu.VMEM((tm, tm), BF16)])
    return pl.pallas_call(
        functools.partial(_moe_kernel, alpha, MOE_CAP),
        grid_spec=grid_spec,
        out_shape=jax.ShapeDtypeStruct((n, d), F32),
        compiler_params=_params(("arbitrary", "arbitrary")),
        name="moe",
    )(cnt, h, comb, bgt, w_gate, w_up, w_down, ln_g, ln_b)


def _with_meta(real, meta, n_pad):
    pad = n_pad - real.shape[0] - meta.shape[0]
    return jnp.concatenate([real, meta, jnp.zeros((pad, real.shape[1]), real.dtype)], axis=0)


def kernel(x, meta_tokens, ln_in_g, ln_in_b, w_in, conv_w, b_if, mnorm_g, kv_norm_g, w_uk, w_uv,
           w_branch_m, w_branch_a, w_out, ln1_g, ln1_b, w_router, b_router, w_gate, w_up, w_down,
           ln2_g, ln2_b, rel_bias):
    batch, seq, d = x.shape
    depth = w_in.shape[0]
    alpha = (2 * depth) ** 0.25
    n_real = batch * seq
    n_meta = batch * N_META
    tile = math.lcm(ROW_TILE, MOE_ROW_TILE)
    n_pad = -(-(n_real + n_meta) // tile) * tile
    assert seq % LANES == 0 and n_real % N_META == 0

    assert n_real % ROW_TILE == 0
    tail = _with_meta(jnp.tile(meta_tokens.astype(x.dtype), (batch, 1)), jnp.zeros((0, d), x.dtype),
                      n_pad - n_real)
    h = _input_ln(x.reshape(n_real, d), tail, ln_in_g, ln_in_b)
    tables = _bias_tables(rel_bias)
    row3 = lambda a: a.astype(F32)[:, None, :]
    w_packed, brow, kv_g = _pack_w_in(w_in), _pack_b_if(b_if), row3(kv_norm_g)
    conv_f, mnorm = conv_w.astype(F32), row3(mnorm_g)
    wuk_t = jnp.swapaxes(w_uk, 2, 3).astype(BF16)
    wuv = w_uv.astype(BF16)
    w_bm, w_ba, w_o = w_branch_m.astype(BF16), w_branch_a.astype(BF16), w_out.astype(BF16)
    wr_t = jnp.zeros((LANES, d), F32).at[0:N_EXPERTS].set(w_router.T).astype(BF16)
    br = jnp.zeros((LANES, 1), F32).at[0:N_EXPERTS, 0].set(b_router)
    wg, wu, wd = w_gate.astype(BF16), w_up.astype(BF16), w_down.astype(BF16)
    g1, b1, g2, b2 = row3(ln1_g), row3(ln1_b), row3(ln2_g), row3(ln2_b)
    for l in range(depth):
        pm, qa, qi, c, sm, g = _project(h, w_packed, kv_g, brow, l)
        hm_real, hm_meta = _mlstm(pm, sm, conv_f, mnorm, l, batch, seq)
        ha_real, ha_meta = _dsa(qa, qi, sm, c, wuk_t, wuv, tables, l, batch, seq)
        hm = _with_meta(hm_real, hm_meta, n_pad)
        ha = _with_meta(ha_real, ha_meta, n_pad)
        h1, comb, bgt, counts = _merge(h, hm, ha, g, w_bm, w_ba, w_o, g1, b1, wr_t, br, l, alpha)
        h = _moe(h1, comb, bgt, counts, wg, wu, wd, g2, b2, l, alpha)
    return h[:n_real].reshape(batch, seq, d)
```

```python
import functools
import math

import numpy as np
import jax
import jax.numpy as jnp
from jax import lax
from jax.experimental import pallas as pl
from jax.experimental.pallas import tpu as pltpu

F32 = jnp.float32
BF16 = jnp.bfloat16

N_META = 16
M_HEADS = 4
M_HEAD_DIM = 128
M_WIDTH = M_HEADS * M_HEAD_DIM
CONV_WIDTH = 4
A_HEADS = 8
A_HEAD_DIM = 64
A_WIDTH = A_HEADS * A_HEAD_DIM
KV_RANK = 128
IDX_HEADS = 4
IDX_DIM = 64
IDX_SCALE = (IDX_HEADS * IDX_DIM) ** -0.5
TOPK_MAX = 256
REL_BUCKETS = 32
REL_MAX_EXACT = 16
REL_MAX_DIST = 128
N_EXPERTS = 16
N_GROUPS = 4
GROUP_SIZE = N_EXPERTS // N_GROUPS

LANES = 128
ROW_TILE = 256
MOE_ROW_TILE = 768
MOE_CAP = 224
M_CHUNK = 256
SCORE_CHUNK = 4
KEY_SUB = 4
ATT_GROUPS = 2
VMEM_LIMIT = 56 * 1024 * 1024
NEG_BIG = -1e30
INT_MIN = -2 ** 31
HALF_BIAS = 2 ** 15
LOG2E = math.log2(math.e)

D_MODEL = 1024
IN_WIDTHS = (("q_m", M_WIDTH), ("k_m", M_WIDTH), ("v_m", M_WIDTH), ("o_m", M_WIDTH), ("i_pre", M_HEADS),
             ("f_pre", M_HEADS), ("q_a", A_WIDTH), ("c_kv", KV_RANK), ("q_idx", IDX_HEADS * IDX_DIM),
             ("k_idx", IDX_DIM), ("w_idx", IDX_HEADS), ("g_m", D_MODEL), ("g_a", D_MODEL))
W_PM = 4 * M_WIDTH
W_G = 2 * D_MODEL
PK_PM = 0
PK_QA = PK_PM + W_PM
PK_QI = PK_QA + A_WIDTH
PK_CKV = PK_QI + IDX_HEADS * IDX_DIM
PK_SM = PK_CKV + KV_RANK
PK_G = PK_SM + LANES
PK_TOTAL = PK_G + W_G
SM_KIDX = 0
SM_WIDX = SM_KIDX + IDX_DIM
SM_IPRE = SM_WIDX + IDX_HEADS
SM_FPRE = SM_IPRE + M_HEADS
SM_USED = SM_FPRE + M_HEADS


def _params(sem):
    return pltpu.CompilerParams(dimension_semantics=sem, vmem_limit_bytes=VMEM_LIMIT)


def _sigmoid(x):
    return 1.0 / (1.0 + jnp.exp(-x))


def _layer_norm_rows(x, g, b, eps):
    mu = jnp.mean(x, axis=-1, keepdims=True)
    xc = x - mu
    var = jnp.mean(xc * xc, axis=-1, keepdims=True)
    return xc * lax.rsqrt(var + eps) * g + b


def _ln_kernel(real_tiles, x_ref, m_ref, g_ref, b_ref, o_ref):
    r = pl.program_id(0)

    @pl.when(r < real_tiles)
    def _():
        o_ref[...] = _layer_norm_rows(x_ref[...], g_ref[...], b_ref[...], 1e-5)

    @pl.when(r >= real_tiles)
    def _():
        o_ref[...] = _layer_norm_rows(m_ref[...], g_ref[...], b_ref[...], 1e-5)


def _input_ln(x, tail, g, b):
    n_real, d = x.shape
    real_tiles = n_real // ROW_TILE
    tiles = real_tiles + tail.shape[0] // ROW_TILE
    return pl.pallas_call(
        functools.partial(_ln_kernel, real_tiles),
        grid=(tiles,),
        in_specs=[pl.BlockSpec((ROW_TILE, d), lambda r: (jnp.minimum(r, real_tiles - 1), 0)),
                  pl.BlockSpec((ROW_TILE, d), lambda r: (jnp.maximum(r - real_tiles, 0), 0)),
                  pl.BlockSpec((1, d), lambda r: (0, 0)),
                  pl.BlockSpec((1, d), lambda r: (0, 0))],
        out_specs=pl.BlockSpec((ROW_TILE, d), lambda r: (r, 0)),
        out_shape=jax.ShapeDtypeStruct((tiles * ROW_TILE, d), F32),
        compiler_params=_params(("arbitrary",)),
        name="input_ln",
    )(x, tail, g.reshape(1, d), b.reshape(1, d))


def _proj_kernel(h_ref, w_ref, kvg_ref, brow_ref, pm_ref, qa_ref, qi_ref, c_ref, sm_ref, g_ref):
    x = h_ref[...].astype(BF16)

    def mm(lo, width):
        return jnp.dot(x, w_ref[:, lo:lo + width], preferred_element_type=F32)

    pm_ref[...] = mm(PK_PM, W_PM).astype(BF16)
    qa_ref[...] = mm(PK_QA, A_WIDTH).astype(BF16)
    qi_ref[...] = mm(PK_QI, IDX_HEADS * IDX_DIM).astype(BF16)
    ckv = mm(PK_CKV, KV_RANK)
    c_ref[...] = ckv * lax.rsqrt(jnp.mean(ckv * ckv, axis=-1, keepdims=True) + 1e-6) * kvg_ref[...]
    sm_ref[...] = mm(PK_SM, LANES) + brow_ref[...]
    g_ref[...] = mm(PK_G, W_G).astype(BF16)


def _pack_w_in(w):
    part, start = {}, 0
    for name, width in IN_WIDTHS:
        part[name] = w[..., start:start + width]
        start += width
    assert start == w.shape[-1] and w.shape[-2] == D_MODEL
    order = ("q_m", "k_m", "v_m", "o_m", "q_a", "q_idx", "c_kv", "k_idx", "w_idx", "i_pre", "f_pre")
    cols = [part[name] for name in order]
    cols += [jnp.zeros(w.shape[:-1] + (LANES - SM_USED,), w.dtype), part["g_m"], part["g_a"]]
    return jnp.concatenate(cols, axis=-1).astype(BF16)


def _pack_b_if(b_if):
    depth = b_if.shape[0]
    return jnp.concatenate([jnp.zeros((depth, SM_IPRE), F32), b_if.astype(F32),
                            jnp.zeros((depth, LANES - SM_USED), F32)], axis=1)[:, None, :]


def _layer_block(arr, l):
    zeros = (0,) * (arr.ndim - 1)
    return pl.BlockSpec((None,) + arr.shape[1:], lambda *idx: (l,) + zeros)


def _project(h, w_packed, kv_g, brow, l):
    n, d = h.shape
    row = lambda width: pl.BlockSpec((MOE_ROW_TILE, width), lambda r: (r, 0))
    shp = lambda width, dt: jax.ShapeDtypeStruct((n, width), dt)
    outs = ((W_PM, BF16), (A_WIDTH, BF16), (IDX_HEADS * IDX_DIM, BF16), (KV_RANK, F32), (LANES, F32), (W_G, BF16))
    return pl.pallas_call(
        _proj_kernel,
        grid=(n // MOE_ROW_TILE,),
        in_specs=[row(d), _layer_block(w_packed, l), _layer_block(kv_g, l), _layer_block(brow, l)],
        out_specs=[row(width) for width, _ in outs],
        out_shape=[shp(width, dt) for width, dt in outs],
        compiler_params=_params(("parallel",)),
        name="in_proj",
    )(h, w_packed, kv_g, brow)


def _log_sigmoid(f):
    return jnp.minimum(f, 0.0) - jnp.log1p(jnp.exp(-jnp.abs(f)))


def _row_from_col(col):
    L = col.shape[0]
    eye = lax.broadcasted_iota(jnp.int32, (L, L), 0) == lax.broadcasted_iota(jnp.int32, (L, L), 1)
    return jnp.sum(jnp.where(eye, jnp.broadcast_to(col, (L, L)), 0.0), axis=0, keepdims=True)


def _mlstm_heads(qs, ks, vs, lis, li_rows, lf_rows, cts, ns, ms):
    L = qs[0].shape[0]
    heads = range(len(qs))
    r = lax.broadcasted_iota(jnp.int32, (L, L), 0)
    c = lax.broadcasted_iota(jnp.int32, (L, L), 1)
    tril = c <= r
    eye = c == r
    nt = (((1,), (1,)), ((), ()))
    tn = (((0,), (0,)), ((), ()))
    b_cols = [jnp.sum(jnp.where(tril, jnp.broadcast_to(lf_rows[h], (L, L)), 0.0), axis=1, keepdims=True)
              for h in heads]
    b_rows = [jnp.sum(jnp.where(eye, jnp.broadcast_to(b_cols[h], (L, L)), 0.0), axis=0, keepdims=True)
              for h in heads]
    qb = [qs[h].astype(BF16) for h in heads]
    kb = [ks[h].astype(BF16) for h in heads]
    vb = [vs[h].astype(BF16) for h in heads]
    qk = [lax.dot_general(qb[h], kb[h], nt, preferred_element_type=F32) for h in heads]
    qc = [jnp.dot(qb[h], cts[h].astype(BF16), preferred_element_type=F32) for h in heads]
    d = [jnp.where(tril, b_cols[h] - b_rows[h] + li_rows[h], -jnp.inf) for h in heads]
    inter = [b_cols[h] + ms[h] for h in heads]
    m_t = [jnp.maximum(inter[h], jnp.max(d[h], axis=1, keepdims=True)) for h in heads]
    a = [jnp.exp(inter[h] - m_t[h]) for h in heads]
    w = [jnp.exp(d[h] - m_t[h]) * qk[h] for h in heads]
    wv = [jnp.dot(w[h].astype(BF16), vb[h], preferred_element_type=F32) for h in heads]
    den = [a[h] * jnp.sum(qs[h] * ns[h], axis=1, keepdims=True) + jnp.sum(w[h], axis=1, keepdims=True)
           for h in heads]
    hs = [(a[h] * qc[h] + wv[h]) / jnp.maximum(jnp.abs(den[h]), jnp.exp(-m_t[h])) for h in heads]
    b_last = [b_cols[h][L - 1:L, :] for h in heads]
    g = [b_last[h] - b_cols[h] + lis[h] for h in heads]
    m_new = [jnp.maximum(b_last[h] + ms[h], jnp.max(g[h], axis=0, keepdims=True)) for h in heads]
    decay = [jnp.exp(b_last[h] + ms[h] - m_new[h]) for h in heads]
    kw = [ks[h] * jnp.exp(g[h] - m_new[h]) for h in heads]
    ct_new = [decay[h] * cts[h] + lax.dot_general(kw[h].astype(BF16), vb[h], tn, preferred_element_type=F32)
              for h in heads]
    n_new = [decay[h] * ns[h] + jnp.sum(kw[h], axis=0, keepdims=True) for h in heads]
    return hs, ct_new, n_new, m_new


def _mlstm_kernel(pm_ref, sm_ref, pmm_ref, smm_ref, cw_ref, mg_ref, out_ref, outm_ref,
                  ct_scr, n_scr, m_scr, x_scr):
    ci = pl.program_id(1)
    tail = 8

    def run_chunk(p_ref, s_ref, o_ref):
        L = p_ref.shape[0]
        heads = range(M_HEADS)
        x_scr[tail:tail + L, :] = p_ref[:, 0:2 * M_WIDTH].astype(F32)
        conv = cw_ref[0:1, :] * x_scr[tail - 3:tail - 3 + L, :]
        for j in range(1, CONV_WIDTH):
            conv = conv + cw_ref[j:j + 1, :] * x_scr[tail - 3 + j:tail - 3 + j + L, :]
        x_scr[0:tail, :] = x_scr[L:L + tail, :]
        qk = conv * _sigmoid(conv)
        cols = lambda base, h: slice(base + h * M_HEAD_DIM, base + (h + 1) * M_HEAD_DIM)
        lis = [s_ref[:, SM_IPRE + h:SM_IPRE + h + 1] for h in heads]
        if L % LANES == 0:
            gates_t = s_ref[...].T
            lf_all = _log_sigmoid(gates_t[SM_FPRE:SM_FPRE + M_HEADS, :])
            li_rows = [gates_t[SM_IPRE + h:SM_IPRE + h + 1, :] for h in heads]
            lf_rows = [lf_all[h:h + 1, :] for h in heads]
        else:
            li_rows = [_row_from_col(lis[h]) for h in heads]
            lf_rows = [_row_from_col(_log_sigmoid(s_ref[:, SM_FPRE + h:SM_FPRE + h + 1])) for h in heads]
        hs, ct_new, n_new, m_new = _mlstm_heads(
            [qk[:, cols(0, h)] * (M_HEAD_DIM ** -0.5) for h in heads],
            [qk[:, cols(M_WIDTH, h)] for h in heads],
            [p_ref[:, cols(2 * M_WIDTH, h)].astype(F32) for h in heads],
            lis, li_rows, lf_rows,
            [ct_scr[h] for h in heads], [n_scr[h:h + 1, :] for h in heads],
            [m_scr[h:h + 1, 0:1] for h in heads])
        for h in heads:
            ct_scr[h] = ct_new[h]
            n_scr[h:h + 1, :] = n_new[h]
            m_scr[h:h + 1, :] = jnp.broadcast_to(m_new[h], (1, LANES))
        mus = [jnp.mean(hs[h], axis=-1, keepdims=True) for h in heads]
        hcs = [hs[h] - mus[h] for h in heads]
        vars_ = [jnp.mean(hcs[h] * hcs[h], axis=-1, keepdims=True) for h in heads]
        for h in heads:
            o_gate = _sigmoid(p_ref[:, cols(3 * M_WIDTH, h)].astype(F32))
            o_ref[:, cols(0, h)] = (hcs[h] * lax.rsqrt(vars_[h] + 1e-5) * mg_ref[:, cols(0, h)]
                                    * o_gate).astype(o_ref.dtype)

    @pl.when(ci == 0)
    def _():
        ct_scr[...] = jnp.zeros_like(ct_scr)
        n_scr[...] = jnp.zeros_like(n_scr)
        m_scr[...] = jnp.zeros_like(m_scr)
        x_scr[...] = jnp.zeros_like(x_scr)
        run_chunk(pmm_ref, smm_ref, outm_ref)

    run_chunk(pm_ref, sm_ref, out_ref)


def _mlstm(pm, sm, conv_w, mnorm_g, l, batch, seq):
    n_real = batch * seq
    chunk = min(M_CHUNK, seq)
    nc = seq // chunk
    meta_blk = n_real // N_META
    out_real, out_meta = pl.pallas_call(
        _mlstm_kernel,
        grid=(batch, nc),
        in_specs=[pl.BlockSpec((chunk, 4 * M_WIDTH), lambda b, c: (b * nc + c, 0)),
                  pl.BlockSpec((chunk, LANES), lambda b, c: (b * nc + c, 0)),
                  pl.BlockSpec((N_META, 4 * M_WIDTH), lambda b, c: (meta_blk + b, 0)),
                  pl.BlockSpec((N_META, LANES), lambda b, c: (meta_blk + b, 0)),
                  _layer_block(conv_w, l), _layer_block(mnorm_g, l)],
        out_specs=[pl.BlockSpec((chunk, M_WIDTH), lambda b, c: (b * nc + c, 0)),
                   pl.BlockSpec((N_META, M_WIDTH), lambda b, c: (b, 0))],
        out_shape=[jax.ShapeDtypeStruct((n_real, M_WIDTH), BF16),
                   jax.ShapeDtypeStruct((batch * N_META, M_WIDTH), BF16)],
        scratch_shapes=[pltpu.VMEM((M_HEADS, M_HEAD_DIM, M_HEAD_DIM), F32),
                        pltpu.VMEM((8, LANES), F32),
                        pltpu.VMEM((8, LANES), F32),
                        pltpu.VMEM((chunk + 8, 2 * M_WIDTH), F32)],
        compiler_params=_params(("arbitrary", "arbitrary")),
        name="mlstm",
    )(pm, sm, pm, sm, conv_w, mnorm_g)
    return out_real, out_meta


def _rel_bucket_np(dist):
    n = np.maximum(dist, 0)
    nf = np.maximum(n, REL_MAX_EXACT).astype(np.float32)
    large = REL_MAX_EXACT + (np.log(nf / np.float32(REL_MAX_EXACT)) /
                             np.float32(math.log(REL_MAX_DIST / REL_MAX_EXACT))
                             * np.float32(REL_BUCKETS - REL_MAX_EXACT)).astype(np.int32)
    large = np.minimum(large, REL_BUCKETS - 1)
    return np.where(n < REL_MAX_EXACT, n, large).astype(np.int32)


def _bias_tables(rel_bias):
    q = np.arange(LANES)[:, None]
    k = np.arange(LANES)[None, :]
    far = 4 * LANES
    assert (_rel_bucket_np(np.arange(LANES + 1, far)) == REL_BUCKETS - 1).all()
    far_idx = np.full((LANES, LANES), REL_BUCKETS - 1, np.int32)
    near_idx = np.stack([_rel_bucket_np(q - k), _rel_bucket_np(LANES + q - k), far_idx])
    meta_idx = np.stack([_rel_bucket_np(q + N_META - np.minimum(k, N_META - 1)), far_idx])
    mq = np.arange(N_META)[:, None]
    mm_idx = _rel_bucket_np(mq - np.minimum(k, N_META - 1))
    rb = rel_bias.astype(F32)

    def lookup(idx, values):
        onehot = jnp.asarray(idx[..., None] == np.arange(REL_BUCKETS), F32)
        out = jnp.einsum('...b,bh->...h', onehot, values, precision=lax.Precision.HIGHEST)
        return jnp.moveaxis(out, -1, -3)

    rel_values = (rb - rb[REL_BUCKETS - 1][None, :]) * LOG2E
    return lookup(near_idx, rel_values), lookup(meta_idx, rel_values), lookup(mm_idx, rb)


def _dsa_kernel(top_k, qa_ref, qi_ref, sm_ref, cb_ref, smb_ref, cm_ref, qam_ref,
                wuk_ref, wuv_ref, near_ref, metab_ref, mmb_ref,
                out_ref, outm_ref,
                keys_scr, hi_scr, lo_scr, lg_scr, lgm_scr, rawa_scr, rawb_scr, pa_scr, pb_scr,
                caug_scr, ct_scr, kt_scr, cmaug_scr, cmt_scr, qs_scr, mx_scr, mrep_scr, acc_scr):
    i = pl.program_id(1)
    T = LANES
    H = A_HEADS
    col = lax.broadcasted_iota(jnp.int32, (T, T), 1)
    row = lax.broadcasted_iota(jnp.int32, (T, T), 0)
    nt = (((1,), (1,)), ((), ()))

    def q_latent(qa, hd, scale):
        ql = jnp.dot(qa[:, hd * A_HEAD_DIM:(hd + 1) * A_HEAD_DIM], wuk_ref[hd],
                     preferred_element_type=F32)
        return (ql * scale).astype(BF16)

    def ones_column(n):
        return jnp.where(lax.broadcasted_iota(jnp.int32, (n, T), 1) == 0, 1.0, 0.0).astype(BF16)

    @pl.when(i == 0)
    def _():
        caug_scr[:, 0:KV_RANK] = cb_ref[...].astype(BF16)
        caug_scr[:, KV_RANK:KV_RANK + T] = ones_column(caug_scr.shape[0])

        def transpose_keys(blk, carry):
            rows = pl.ds(pl.multiple_of(blk * KEY_SUB * T, KEY_SUB * T), KEY_SUB * T)
            ct_scr[blk] = cb_ref[rows, :].T.astype(BF16)
            kt_scr[blk] = smb_ref[rows, :].T[SM_KIDX:SM_KIDX + IDX_DIM, :].astype(BF16)
            return carry

        lax.fori_loop(0, ct_scr.shape[0], transpose_keys, 0)
        cm_pad = jnp.concatenate([cm_ref[...], jnp.zeros((T - N_META, KV_RANK), F32)], axis=0)
        cmaug_scr[:, 0:KV_RANK] = cm_pad.astype(BF16)
        cmt_scr[...] = cm_pad.T.astype(BF16)
        cmaug_scr[:, KV_RANK:KV_RANK + T] = ones_column(T)
        cmk = cmaug_scr[:, 0:KV_RANK]
        qam = qam_ref[...]
        mrow = lax.broadcasted_iota(jnp.int32, (N_META, T), 0)
        mcol = lax.broadcasted_iota(jnp.int32, (N_META, T), 1)
        for hd in range(H):
            lg = lax.dot_general(q_latent(qam, hd, A_HEAD_DIM ** -0.5), cmk, nt,
                                 preferred_element_type=F32) + mmb_ref[hd]
            lg = jnp.where(mcol <= mrow, lg, NEG_BIG)
            p = jnp.exp(lg - jnp.max(lg, axis=1, keepdims=True))
            p = p / jnp.sum(p, axis=1, keepdims=True)
            o = jnp.dot(p.astype(BF16), cmk, preferred_element_type=F32)
            outm_ref[:, hd * A_HEAD_DIM:(hd + 1) * A_HEAD_DIM] = jnp.dot(
                o.astype(BF16), wuv_ref[hd], preferred_element_type=F32).astype(outm_ref.dtype)

    qa = qa_ref[...]
    for hd in range(H):
        qs_scr[hd * T:(hd + 1) * T, :] = q_latent(qa, hd, A_HEAD_DIM ** -0.5 * LOG2E)
    qi = qi_ref[...]
    wv = sm_ref[:, SM_WIDX:SM_WIDX + IDX_HEADS] * IDX_SCALE
    t_col = i * T + lax.broadcasted_iota(jnp.int32, (T, 1), 0)
    n_chunks = (i + SCORE_CHUNK) // SCORE_CHUNK
    CW = SCORE_CHUNK * T

    def run_pipelined(n, produce, consume, buf_a, buf_b):
        last = n - 1
        produce(0, buf_a)

        def body(t, carry):
            s = 2 * t
            produce(jnp.minimum(s + 1, last), buf_b)
            consume(s, buf_a)
            produce(jnp.minimum(s + 2, last), buf_a)
            consume(s + 1, buf_b)
            return carry

        lax.fori_loop(0, n // 2, body, 0)

        @pl.when(n % 2 == 1)
        def _():
            consume(last, buf_a)

    def score_matmul(cix, buf):
        kc_t = kt_scr[cix]
        for hh in range(IDX_HEADS):
            buf[hh * T:(hh + 1) * T, :] = jnp.dot(
                qi[:, hh * IDX_DIM:(hh + 1) * IDX_DIM], kc_t, preferred_element_type=F32)

    def score_keys(cix, buf):
        acc = jnp.zeros((T, CW), F32)
        for hh in range(IDX_HEADS):
            acc = acc + wv[:, hh:hh + 1] * jnp.maximum(buf[hh * T:(hh + 1) * T, :], 0.0)
        acc = jnp.where(acc == 0.0, 0.0, acc)
        bits = lax.bitcast_convert_type(acc, jnp.int32)
        key = jnp.where(bits < 0, bits ^ jnp.int32(0x7FFFFFFF), bits)
        s_idx = cix * CW + lax.broadcasted_iota(jnp.int32, (T, CW), 1)
        key = jnp.where(s_idx <= t_col, key, jnp.int32(INT_MIN))
        for u in range(SCORE_CHUNK):
            tile = key[:, u * T:(u + 1) * T]
            keys_scr[cix * SCORE_CHUNK + u] = tile
            tile_t = tile.T
            hi_scr[cix * SCORE_CHUNK + u] = lax.shift_right_arithmetic(tile_t, 16).astype(jnp.int16)
            lo_scr[cix * SCORE_CHUNK + u] = ((tile_t & 0xFFFF) - HALF_BIAS).astype(jnp.int16)

    run_pipelined(n_chunks, score_matmul, score_keys, rawa_scr, rawb_scr)

    def rep16(row_i32):
        return jnp.broadcast_to(row_i32, (T, T)).astype(jnp.int16)

    def count16(src_scr, pred_fn):
        def body(cix, cnt):
            for u in range(SCORE_CHUNK):
                hit = pred_fn(src_scr[cix * SCORE_CHUNK + u])
                cnt = cnt + jnp.where(hit, jnp.int16(1), jnp.int16(0))
            return cnt
        cnt = lax.fori_loop(0, n_chunks, body, jnp.zeros((T, T), jnp.int16))
        return jnp.sum(cnt.astype(F32), axis=0, keepdims=True)

    def search16(src_scr, k_row):
        def bit_body(bi, carry):
            ans, above = carry
            cand_u = ans | lax.shift_left(jnp.int32(1), jnp.int32(15) - bi)
            cand = rep16(cand_u - HALF_BIAS)
            total = count16(src_scr, lambda x: x >= cand)
            take = total >= k_row
            return jnp.where(take, cand_u, ans), jnp.where(take, above, total)
        return lax.fori_loop(0, 16, bit_body, (jnp.zeros((1, T), jnp.int32), jnp.zeros((1, T), F32)))

    k_row = jnp.full((1, T), float(top_k), F32)
    hi_u, hi_above = search16(hi_scr, k_row)
    hi_s = hi_u - HALF_BIAS
    hi_rep = rep16(hi_s)
    k_low = k_row - hi_above

    def band_body(cix, carry):
        for u in range(SCORE_CHUNK):
            j = cix * SCORE_CHUNK + u
            lo_scr[j] = jnp.where(hi_scr[j] == hi_rep, lo_scr[j], jnp.int16(-HALF_BIAS))
        return carry

    lax.fori_loop(0, n_chunks, band_body, 0)
    lo_u, lo_above = search16(lo_scr, k_low)
    need_row = k_low - lo_above
    thr_row = lax.shift_left(hi_s, 16) | lo_u
    thr = jnp.broadcast_to(thr_row, (T, T)).T
    need = jnp.broadcast_to(need_row, (T, T)).T

    mx_scr[...] = jnp.full_like(mx_scr, NEG_BIG)
    acc_scr[...] = jnp.zeros_like(acc_scr)
    hg = H // ATT_GROUPS
    groups = [slice(g * hg * T, (g + 1) * hg * T) for g in range(ATT_GROUPS)]
    meta_sel = jnp.minimum(i, 1)

    def max_pass(ck_t, madds, bias_fns, store):
        lgs = [jnp.dot(qs_scr[rs, :], ck_t, preferred_element_type=F32) for rs in groups]
        for hd in range(H):
            rs = slice(hd * T, (hd + 1) * T)
            lo = (hd % hg) * T
            mx = mx_scr[rs, :]
            for u in range(len(madds)):
                x = lgs[hd // hg][lo:lo + T, u * T:(u + 1) * T] + madds[u]
                if bias_fns[u] is not None:
                    x = x + bias_fns[u](hd)
                store(rs, u, x)
                mx = jnp.maximum(mx, x)
            mx_scr[rs, :] = mx

    def sum_pass(c_aug, n_sub, load):
        for grp in groups:
            m_rep = mrep_scr[grp, :]
            ph = [jnp.exp2(load(grp, u) - m_rep).astype(BF16) for u in range(n_sub)]
            p = ph[0] if n_sub == 1 else jnp.concatenate(ph, axis=1)
            acc_scr[grp, :] += jnp.dot(p, c_aug, preferred_element_type=F32)

    def key_rows(step):
        return pl.ds(pl.multiple_of(step * KEY_SUB * T, KEY_SUB * T), KEY_SUB * T)

    upper = (row < col).astype(BF16)

    def logits_step(step, buf):
        for grp in groups:
            buf[grp, :] = jnp.dot(qs_scr[grp, :], ct_scr[step], preferred_element_type=F32)

    def mask_step(step, buf, seen, near):
        madds = []
        for u in range(KEY_SUB):
            j = KEY_SUB * step + u
            kk = keys_scr[j]
            eq = kk == thr
            eqf = jnp.where(eq, 1.0, 0.0)
            before = jnp.dot(eqf.astype(BF16), upper, preferred_element_type=F32) + seen
            sel = (kk > thr) | (eq & (before < need))
            if near:
                sel = sel & ((j * T + col) <= (i * T + row))
            seen = seen + jnp.sum(eqf, axis=1, keepdims=True)
            madds.append(jnp.where(sel, 0.0, NEG_BIG))
        for hd in range(H):
            rs = slice(hd * T, (hd + 1) * T)
            mx = mx_scr[rs, :]
            for u in range(KEY_SUB):
                x = buf[rs, u * T:(u + 1) * T] + madds[u]
                if near:
                    x = x + near_ref[jnp.clip(i - (KEY_SUB * step + u), 0, 2), hd]
                lg_scr[step, rs, u * T:(u + 1) * T] = x
                mx = jnp.maximum(mx, x)
            mx_scr[rs, :] = mx
        return seen

    def run_pipelined_carry(start, n, produce, consume, buf_a, buf_b, carry):
        last = jnp.maximum(start + n - 1, start)
        produce(start, buf_a)

        def body(t, c):
            s = start + 2 * t
            produce(jnp.minimum(s + 1, last), buf_b)
            c = consume(s, buf_a, c)
            produce(jnp.minimum(s + 2, last), buf_a)
            return consume(s + 1, buf_b, c)

        carry = lax.fori_loop(0, n // 2, body, carry)
        return lax.cond(n % 2 == 1, lambda c: consume(last, buf_a, c), lambda c: c, carry)

    def weights_step(step, buf):
        for grp in groups:
            m_rep = mrep_scr[grp, :]
            for u in range(KEY_SUB):
                buf[grp, u * T:(u + 1) * T] = jnp.exp2(lg_scr[step, grp, u * T:(u + 1) * T] - m_rep)

    def accumulate_step(step, buf):
        c_aug = caug_scr[key_rows(step), :]
        for grp in groups:
            acc_scr[grp, :] += jnp.dot(buf[grp, :].astype(BF16), c_aug, preferred_element_type=F32)

    def store_meta(rs, u, x):
        lgm_scr[rs, :] = x

    n_far = jnp.maximum(i - 1, 0) // KEY_SUB
    n_steps = (i + KEY_SUB) // KEY_SUB
    max_pass(cmt_scr[...], [jnp.where(col < N_META, 0.0, NEG_BIG)],
             [lambda hd: metab_ref[meta_sel, hd]], store_meta)
    seen = run_pipelined_carry(0, n_far, logits_step, functools.partial(mask_step, near=False),
                               pa_scr, pb_scr, jnp.zeros((T, 1), F32))
    run_pipelined_carry(n_far, n_steps - n_far, logits_step, functools.partial(mask_step, near=True),
                        pa_scr, pb_scr, seen)
    mrep_scr[...] = jnp.broadcast_to(jnp.max(mx_scr[...], axis=1, keepdims=True), mrep_scr.shape)
    sum_pass(cmaug_scr[...], 1, lambda grp, u: lgm_scr[grp, :])
    run_pipelined(n_steps, weights_step, accumulate_step, pa_scr, pb_scr)

    for hd in range(H):
        rs = slice(hd * T, (hd + 1) * T)
        o = acc_scr[rs, 0:KV_RANK] / acc_scr[rs, KV_RANK:KV_RANK + 1]
        out_ref[:, hd * A_HEAD_DIM:(hd + 1) * A_HEAD_DIM] = jnp.dot(
            o.astype(BF16), wuv_ref[hd], preferred_element_type=F32).astype(out_ref.dtype)


def _dsa(qa, qi, sm, c, wuk_t, wuv, tables, l, batch, seq):
    n_real = batch * seq
    nq = seq // LANES
    n_tiles = ((nq + SCORE_CHUNK - 1) // SCORE_CHUNK) * SCORE_CHUNK
    top_k = min(TOPK_MAX, seq // 4)
    meta_blk = n_real // N_META
    near, metab, mmb = tables
    full = lambda a: pl.BlockSpec(a.shape, lambda b, i: (0,) * a.ndim)
    assert seq % (SCORE_CHUNK * LANES) == 0 and nq % KEY_SUB == 0 and SCORE_CHUNK == KEY_SUB
    out_real, out_meta = pl.pallas_call(
        functools.partial(_dsa_kernel, top_k),
        grid=(batch, nq),
        in_specs=[pl.BlockSpec((LANES, A_WIDTH), lambda b, i: (b * nq + i, 0)),
                  pl.BlockSpec((LANES, IDX_HEADS * IDX_DIM), lambda b, i: (b * nq + i, 0)),
                  pl.BlockSpec((LANES, LANES), lambda b, i: (b * nq + i, 0)),
                  pl.BlockSpec((seq, KV_RANK), lambda b, i: (b, 0)),
                  pl.BlockSpec((seq, LANES), lambda b, i: (b, 0)),
                  pl.BlockSpec((N_META, KV_RANK), lambda b, i: (meta_blk + b, 0)),
                  pl.BlockSpec((N_META, A_WIDTH), lambda b, i: (meta_blk + b, 0)),
                  _layer_block(wuk_t, l), _layer_block(wuv, l), full(near), full(metab), full(mmb)],
        out_specs=[pl.BlockSpec((LANES, A_WIDTH), lambda b, i: (b * nq + i, 0)),
                   pl.BlockSpec((N_META, A_WIDTH), lambda b, i: (b, 0))],
        out_shape=[jax.ShapeDtypeStruct((n_real, A_WIDTH), BF16),
                   jax.ShapeDtypeStruct((batch * N_META, A_WIDTH), BF16)],
        scratch_shapes=[pltpu.VMEM((n_tiles, LANES, LANES), jnp.int32),
                        pltpu.VMEM((n_tiles, LANES, LANES), jnp.int16),
                        pltpu.VMEM((n_tiles, LANES, LANES), jnp.int16),
                        pltpu.VMEM((nq // KEY_SUB, A_HEADS * LANES, KEY_SUB * LANES), F32),
                        pltpu.VMEM((A_HEADS * LANES, LANES), F32),
                        pltpu.VMEM((IDX_HEADS * LANES, SCORE_CHUNK * LANES), F32),
                        pltpu.VMEM((IDX_HEADS * LANES, SCORE_CHUNK * LANES), F32),
                        pltpu.VMEM((A_HEADS * LANES, KEY_SUB * LANES), F32),
                        pltpu.VMEM((A_HEADS * LANES, KEY_SUB * LANES), F32),
                        pltpu.VMEM((seq, KV_RANK + LANES), BF16),
                        pltpu.VMEM((nq // KEY_SUB, KV_RANK, KEY_SUB * LANES), BF16),
                        pltpu.VMEM((nq // KEY_SUB, IDX_DIM, KEY_SUB * LANES), BF16),
                        pltpu.VMEM((LANES, KV_RANK + LANES), BF16),
                        pltpu.VMEM((KV_RANK, LANES), BF16),
                        pltpu.VMEM((A_HEADS * LANES, KV_RANK), BF16),
                        pltpu.VMEM((A_HEADS * LANES, LANES), F32),
                        pltpu.VMEM((A_HEADS * LANES, LANES), F32),
                        pltpu.VMEM((A_HEADS * LANES, KV_RANK + LANES), F32)],
        compiler_params=_params(("arbitrary", "arbitrary")),
        name="dsa",
    )(qa, qi, sm, c, sm, c, qa, wuk_t, wuv, near, metab, mmb)
    return out_real, out_meta


def _merge_kernel(alpha, h_ref, hm_ref, ha_ref, g_ref, wbm_ref, wba_ref, wo_ref, lg_ref, lb_ref,
                  wr_ref, br_ref, h1_ref, comb_ref, bgt_ref, cnt_ref):
    d = h_ref.shape[1]
    gm = _sigmoid(g_ref[:, 0:d].astype(F32))
    ga = _sigmoid(g_ref[:, d:2 * d].astype(F32))
    y = gm * jnp.dot(hm_ref[...], wbm_ref[...], preferred_element_type=F32) + \
        ga * jnp.dot(ha_ref[...], wba_ref[...], preferred_element_type=F32)
    z = alpha * h_ref[...] + jnp.dot(y.astype(BF16), wo_ref[...], preferred_element_type=F32)
    h1 = _layer_norm_rows(z, lg_ref[...], lb_ref[...], 1e-5)
    h1_ref[...] = h1

    tm = h1.shape[0]
    logits_t = lax.dot_general(wr_ref[...], h1.astype(BF16), (((1,), (1,)), ((), ())),
                               preferred_element_type=F32)
    scores = _sigmoid(logits_t[0:N_EXPERTS, :])
    sel = scores + br_ref[0:N_EXPERTS, :]
    best = None
    for gidx in range(N_GROUPS):
        r0, r1, r2, r3 = (sel[gidx * GROUP_SIZE + u:gidx * GROUP_SIZE + u + 1, :] for u in range(4))
        a, b = jnp.maximum(r0, r1), jnp.minimum(r0, r1)
        c, dd = jnp.maximum(r2, r3), jnp.minimum(r2, r3)
        gs = jnp.maximum(a, c) + jnp.maximum(jnp.minimum(a, c), jnp.maximum(b, dd))
        if best is None:
            best, bg = gs, jnp.zeros((1, tm), jnp.int32)
        else:
            upd = gs > best
            bg = jnp.where(upd, gidx, bg)
            best = jnp.where(upd, gs, best)
    eidx = lax.broadcasted_iota(jnp.int32, (N_EXPERTS, tm), 0)
    masked = jnp.where((eidx // GROUP_SIZE) == bg, sel, -jnp.inf)
    v1 = jnp.max(masked, axis=0, keepdims=True)
    i1 = jnp.min(jnp.where(masked == v1, eidx, N_EXPERTS), axis=0, keepdims=True)
    masked2 = jnp.where(eidx == i1, -jnp.inf, masked)
    v2 = jnp.max(masked2, axis=0, keepdims=True)
    i2 = jnp.min(jnp.where(masked2 == v2, eidx, N_EXPERTS), axis=0, keepdims=True)
    s1 = jnp.sum(jnp.where(eidx == i1, scores, 0.0), axis=0, keepdims=True)
    s2 = jnp.sum(jnp.where(eidx == i2, scores, 0.0), axis=0, keepdims=True)
    tot = s1 + s2
    comb_t = jnp.where(eidx == i1, s1 / tot, 0.0) + jnp.where(eidx == i2, s2 / tot, 0.0)
    comb_pad = jnp.concatenate([comb_t, jnp.zeros((LANES - N_EXPERTS, tm), F32)], axis=0)
    comb_ref[...] = comb_pad.T
    bgt_ref[...] = jnp.broadcast_to(bg, (8, tm))
    gidx8 = lax.broadcasted_iota(jnp.int32, (8, tm), 0)
    counts = jnp.sum(jnp.where(gidx8 == bg, 1.0, 0.0), axis=1, keepdims=True)
    cnt_ref[0] = jnp.broadcast_to(counts, (8, LANES)).astype(jnp.int32)


def _merge(h, hm, ha, g, w_bm, w_ba, w_o, ln_g, ln_b, wr_t, br, l, alpha):
    n, d = h.shape
    tm = MOE_ROW_TILE
    row = lambda width: pl.BlockSpec((tm, width), lambda r: (r, 0))
    full = lambda a: pl.BlockSpec(a.shape, lambda r: (0,) * a.ndim)
    args = (h, hm, ha, g, w_bm, w_ba, w_o, ln_g, ln_b, wr_t, br)
    return pl.pallas_call(
        functools.partial(_merge_kernel, alpha),
        grid=(n // tm,),
        in_specs=[row(d), row(M_WIDTH), row(A_WIDTH), row(2 * d)] +
                 [_layer_block(a, l) for a in args[4:9]] + [full(wr_t), full(br)],
        out_specs=[row(d), row(LANES),
                   pl.BlockSpec((8, tm), lambda r: (0, r)),
                   pl.BlockSpec((1, 8, LANES), lambda r: (r, 0, 0))],
        out_shape=[jax.ShapeDtypeStruct((n, d), F32), jax.ShapeDtypeStruct((n, LANES), F32),
                   jax.ShapeDtypeStruct((8, n), jnp.int32),
                   jax.ShapeDtypeStruct((n // tm, 8, LANES), jnp.int32)],
        compiler_params=_params(("parallel",)),
        name="merge",
    )(*args)


def _moe_kernel(alpha, cap, cnt_ref, h_ref, comb_ref, bgt_ref, wg_ref, wu_ref, wd_ref, lg_ref, lb_ref,
                out_ref, xb_scr, cs_scr, yt_scr, tri_scr):
    r = pl.program_id(0)
    g = pl.program_id(1)
    rows = h_ref.shape[0]

    @pl.when((r == 0) & (g == 0))
    def _():
        t0 = lax.broadcasted_iota(jnp.int32, (rows, rows), 0)
        t1 = lax.broadcasted_iota(jnp.int32, (rows, rows), 1)
        tri_scr[...] = (t0 < t1).astype(BF16)

    @pl.when(g == 0)
    def _():
        xb_scr[...] = h_ref[...].astype(BF16)
        yt_scr[...] = jnp.zeros_like(yt_scr)
        c = comb_ref[...]
        for part in range(2):
            cb = c.astype(BF16)
            cs_scr[part] = cb
            c = c - cb.astype(F32)

    member = bgt_ref[0:1, :] == g
    mem8 = jnp.broadcast_to(jnp.where(member, 1.0, 0.0), (8, rows)).astype(BF16)
    rank = jnp.dot(mem8, tri_scr[...], preferred_element_type=F32)[0:1, :].astype(jnp.int32)
    n_blocks = (cnt_ref[r * N_GROUPS + g] + cap - 1) // cap
    lane = lax.broadcasted_iota(jnp.int32, (cap, LANES), 1)
    tn = (((0,), (0,)), ((), ()))

    def block(b, carry):
        slot = lax.broadcasted_iota(jnp.int32, (cap, rows), 0) + b * cap
        onehot = jnp.where(member & (rank == slot), 1.0, 0.0).astype(BF16)
        xg = jnp.dot(onehot, xb_scr[...], preferred_element_type=F32).astype(BF16)
        cw = jnp.dot(onehot, cs_scr[0], preferred_element_type=F32)
        cw = cw + jnp.dot(onehot, cs_scr[1], preferred_element_type=F32)
        y = jnp.zeros((cap, out_ref.shape[1]), F32)
        for e in range(GROUP_SIZE):
            gate = jnp.dot(xg, wg_ref[e], preferred_element_type=F32)
            up = jnp.dot(xg, wu_ref[e], preferred_element_type=F32)
            he = gate * _sigmoid(gate) * up
            o = jnp.dot(he.astype(BF16), wd_ref[e], preferred_element_type=F32)
            ce = jnp.sum(jnp.where(lane == g * GROUP_SIZE + e, cw, 0.0), axis=1, keepdims=True)
            y = y + ce * o
        yt_scr[...] += lax.dot_general(onehot, y.astype(BF16), tn, preferred_element_type=F32)
        return carry

    lax.fori_loop(0, n_blocks, block, 0)

    @pl.when(g == pl.num_programs(1) - 1)
    def _():
        z = alpha * h_ref[...] + yt_scr[...]
        out_ref[...] = _layer_norm_rows(z, lg_ref[...], lb_ref[...], 1e-5)


def _moe(h, comb, bgt, counts, w_gate, w_up, w_down, ln_g, ln_b, l, alpha):
    n, d = h.shape
    de = w_gate.shape[-1]
    tm = MOE_ROW_TILE
    cnt = counts[:, 0:N_GROUPS, 0].reshape(-1)
    grid_spec = pltpu.PrefetchScalarGridSpec(
        num_scalar_prefetch=1,
        grid=(n // tm, N_GROUPS),
        in_specs=[pl.BlockSpec((tm, d), lambda r, g, c: (r, 0)),
                  pl.BlockSpec((tm, LANES), lambda r, g, c: (r, 0)),
                  pl.BlockSpec((8, tm), lambda r, g, c: (0, r)),
                  pl.BlockSpec((None, GROUP_SIZE, d, de), lambda r, g, c: (l, g, 0, 0)),
                  pl.BlockSpec((None, GROUP_SIZE, d, de), lambda r, g, c: (l, g, 0, 0)),
                  pl.BlockSpec((None, GROUP_SIZE, de, d), lambda r, g, c: (l, g, 0, 0)),
                  _layer_block(ln_g, l), _layer_block(ln_b, l)],
        out_specs=pl.BlockSpec((tm, d), lambda r, g, c: (r, 0)),
        scratch_shapes=[pltpu.VMEM((tm, d), BF16), pltpu.VMEM((2, tm, LANES), BF16),
                        pltpu.VMEM((tm, d), F32), pltpu.VMEM((tm, tm), BF16)])
    return pl.pallas_call(
        functools.partial(_moe_kernel, alpha, MOE_CAP),
        grid_spec=grid_spec,
        out_shape=jax.ShapeDtypeStruct((n, d), F32),
        compiler_params=_params(("arbitrary", "arbitrary")),
        name="moe",
    )(cnt, h, comb, bgt, w_gate, w_up, w_down, ln_g, ln_b)


def _with_meta(real, meta, n_pad):
    pad = n_pad - real.shape[0] - meta.shape[0]
    return jnp.concatenate([real, meta, jnp.zeros((pad, real.shape[1]), real.dtype)], axis=0)


def kernel(x, meta_tokens, ln_in_g, ln_in_b, w_in, conv_w, b_if, mnorm_g, kv_norm_g, w_uk, w_uv,
           w_branch_m, w_branch_a, w_out, ln1_g, ln1_b, w_router, b_router, w_gate, w_up, w_down,
           ln2_g, ln2_b, rel_bias):
    batch, seq, d = x.shape
    depth = w_in.shape[0]
    alpha = (2 * depth) ** 0.25
    n_real = batch * seq
    n_meta = batch * N_META
    tile = math.lcm(ROW_TILE, MOE_ROW_TILE)
    n_pad = -(-(n_real + n_meta) // tile) * tile
    assert seq % LANES == 0 and n_real % N_META == 0

    assert n_real % ROW_TILE == 0
    tail = _with_meta(jnp.tile(meta_tokens.astype(x.dtype), (batch, 1)), jnp.zeros((0, d), x.dtype),
                      n_pad - n_real)
    h = _input_ln(x.reshape(n_real, d), tail, ln_in_g, ln_in_b)
    tables = _bias_tables(rel_bias)
    row3 = lambda a: a.astype(F32)[:, None, :]
    w_packed, brow, kv_g = _pack_w_in(w_in), _pack_b_if(b_if), row3(kv_norm_g)
    conv_f, mnorm = conv_w.astype(F32), row3(mnorm_g)
    wuk_t = jnp.swapaxes(w_uk, 2, 3).astype(BF16)
    wuv = w_uv.astype(BF16)
    w_bm, w_ba, w_o = w_branch_m.astype(BF16), w_branch_a.astype(BF16), w_out.astype(BF16)
    wr_t = jnp.zeros((LANES, d), F32).at[0:N_EXPERTS].set(w_router.T).astype(BF16)
    br = jnp.zeros((LANES, 1), F32).at[0:N_EXPERTS, 0].set(b_router)
    wg, wu, wd = w_gate.astype(BF16), w_up.astype(BF16), w_down.astype(BF16)
    g1, b1, g2, b2 = row3(ln1_g), row3(ln1_b), row3(ln2_g), row3(ln2_b)
    for l in range(depth):
        pm, qa, qi, c, sm, g = _project(h, w_packed, kv_g, brow, l)
        hm_real, hm_meta = _mlstm(pm, sm, conv_f, mnorm, l, batch, seq)
        ha_real, ha_meta = _dsa(qa, qi, sm, c, wuk_t, wuv, tables, l, batch, seq)
        hm = _with_meta(hm_real, hm_meta, n_pad)
        ha = _with_meta(ha_real, ha_meta, n_pad)
        h1, comb, bgt, counts = _merge(h, hm, ha, g, w_bm, w_ba, w_o, g1, b1, wr_t, br, l, alpha)
        h = _moe(h1, comb, bgt, counts, wg, wu, wd, g2, b2, l, alpha)
    return h[:n_real].reshape(batch, seq, d)
```

```python
import functools
import math

import numpy as np
import jax
import jax.numpy as jnp
from jax import lax
from jax.experimental import pallas as pl
from jax.experimental.pallas import tpu as pltpu

F32 = jnp.float32
BF16 = jnp.bfloat16

N_META = 16
M_HEADS = 4
M_HEAD_DIM = 128
M_WIDTH = M_HEADS * M_HEAD_DIM
CONV_WIDTH = 4
A_HEADS = 8
A_HEAD_DIM = 64
A_WIDTH = A_HEADS * A_HEAD_DIM
KV_RANK = 128
IDX_HEADS = 4
IDX_DIM = 64
IDX_SCALE = (IDX_HEADS * IDX_DIM) ** -0.5
TOPK_MAX = 256
REL_BUCKETS = 32
REL_MAX_EXACT = 16
REL_MAX_DIST = 128
N_EXPERTS = 16
N_GROUPS = 4
GROUP_SIZE = N_EXPERTS // N_GROUPS

LANES = 128
ROW_TILE = 256
MOE_ROW_TILE = 768
MOE_CAP = 224
M_CHUNK = 256
SCORE_CHUNK = 4
KEY_SUB = 4
ATT_GROUPS = 1
VMEM_LIMIT = 56 * 1024 * 1024
NEG_BIG = -1e30
INT_MIN = -2 ** 31
HALF_BIAS = 2 ** 15
LOG2E = math.log2(math.e)

D_MODEL = 1024
IN_WIDTHS = (("q_m", M_WIDTH), ("k_m", M_WIDTH), ("v_m", M_WIDTH), ("o_m", M_WIDTH), ("i_pre", M_HEADS),
             ("f_pre", M_HEADS), ("q_a", A_WIDTH), ("c_kv", KV_RANK), ("q_idx", IDX_HEADS * IDX_DIM),
             ("k_idx", IDX_DIM), ("w_idx", IDX_HEADS), ("g_m", D_MODEL), ("g_a", D_MODEL))
W_PM = 4 * M_WIDTH
W_G = 2 * D_MODEL
PK_PM = 0
PK_QA = PK_PM + W_PM
PK_QI = PK_QA + A_WIDTH
PK_CKV = PK_QI + IDX_HEADS * IDX_DIM
PK_SM = PK_CKV + KV_RANK
PK_G = PK_SM + LANES
PK_TOTAL = PK_G + W_G
SM_KIDX = 0
SM_WIDX = SM_KIDX + IDX_DIM
SM_IPRE = SM_WIDX + IDX_HEADS
SM_FPRE = SM_IPRE + M_HEADS
SM_USED = SM_FPRE + M_HEADS


def _params(sem):
    return pltpu.CompilerParams(dimension_semantics=sem, vmem_limit_bytes=VMEM_LIMIT)


def _sigmoid(x):
    return 1.0 / (1.0 + jnp.exp(-x))


def _layer_norm_rows(x, g, b, eps):
    mu = jnp.mean(x, axis=-1, keepdims=True)
    xc = x - mu
    var = jnp.mean(xc * xc, axis=-1, keepdims=True)
    return xc * lax.rsqrt(var + eps) * g + b


def _ln_kernel(real_tiles, x_ref, m_ref, g_ref, b_ref, o_ref):
    r = pl.program_id(0)

    @pl.when(r < real_tiles)
    def _():
        o_ref[...] = _layer_norm_rows(x_ref[...], g_ref[...], b_ref[...], 1e-5)

    @pl.when(r >= real_tiles)
    def _():
        o_ref[...] = _layer_norm_rows(m_ref[...], g_ref[...], b_ref[...], 1e-5)


def _input_ln(x, tail, g, b):
    n_real, d = x.shape
    real_tiles = n_real // ROW_TILE
    tiles = real_tiles + tail.shape[0] // ROW_TILE
    return pl.pallas_call(
        functools.partial(_ln_kernel, real_tiles),
        grid=(tiles,),
        in_specs=[pl.BlockSpec((ROW_TILE, d), lambda r: (jnp.minimum(r, real_tiles - 1), 0)),
                  pl.BlockSpec((ROW_TILE, d), lambda r: (jnp.maximum(r - real_tiles, 0), 0)),
                  pl.BlockSpec((1, d), lambda r: (0, 0)),
                  pl.BlockSpec((1, d), lambda r: (0, 0))],
        out_specs=pl.BlockSpec((ROW_TILE, d), lambda r: (r, 0)),
        out_shape=jax.ShapeDtypeStruct((tiles * ROW_TILE, d), F32),
        compiler_params=_params(("arbitrary",)),
        name="input_ln",
    )(x, tail, g.reshape(1, d), b.reshape(1, d))


def _proj_kernel(h_ref, w_ref, kvg_ref, brow_ref, pm_ref, qa_ref, qi_ref, c_ref, sm_ref, g_ref):
    x = h_ref[...].astype(BF16)

    def mm(lo, width):
        return jnp.dot(x, w_ref[:, lo:lo + width], preferred_element_type=F32)

    pm_ref[...] = mm(PK_PM, W_PM).astype(BF16)
    qa_ref[...] = mm(PK_QA, A_WIDTH).astype(BF16)
    qi_ref[...] = mm(PK_QI, IDX_HEADS * IDX_DIM).astype(BF16)
    ckv = mm(PK_CKV, KV_RANK)
    c_ref[...] = ckv * lax.rsqrt(jnp.mean(ckv * ckv, axis=-1, keepdims=True) + 1e-6) * kvg_ref[...]
    sm_ref[...] = mm(PK_SM, LANES) + brow_ref[...]
    g_ref[...] = mm(PK_G, W_G).astype(BF16)


def _pack_w_in(w):
    part, start = {}, 0
    for name, width in IN_WIDTHS:
        part[name] = w[..., start:start + width]
        start += width
    assert start == w.shape[-1] and w.shape[-2] == D_MODEL
    order = ("q_m", "k_m", "v_m", "o_m", "q_a", "q_idx", "c_kv", "k_idx", "w_idx", "i_pre", "f_pre")
    cols = [part[name] for name in order]
    cols += [jnp.zeros(w.shape[:-1] + (LANES - SM_USED,), w.dtype), part["g_m"], part["g_a"]]
    return jnp.concatenate(cols, axis=-1).astype(BF16)


def _pack_b_if(b_if):
    depth = b_if.shape[0]
    return jnp.concatenate([jnp.zeros((depth, SM_IPRE), F32), b_if.astype(F32),
                            jnp.zeros((depth, LANES - SM_USED), F32)], axis=1)[:, None, :]


def _layer_block(arr, l):
    zeros = (0,) * (arr.ndim - 1)
    return pl.BlockSpec((None,) + arr.shape[1:], lambda *idx: (l,) + zeros)


def _project(h, w_packed, kv_g, brow, l):
    n, d = h.shape
    row = lambda width: pl.BlockSpec((MOE_ROW_TILE, width), lambda r: (r, 0))
    shp = lambda width, dt: jax.ShapeDtypeStruct((n, width), dt)
    outs = ((W_PM, BF16), (A_WIDTH, BF16), (IDX_HEADS * IDX_DIM, BF16), (KV_RANK, F32), (LANES, F32), (W_G, BF16))
    return pl.pallas_call(
        _proj_kernel,
        grid=(n // MOE_ROW_TILE,),
        in_specs=[row(d), _layer_block(w_packed, l), _layer_block(kv_g, l), _layer_block(brow, l)],
        out_specs=[row(width) for width, _ in outs],
        out_shape=[shp(width, dt) for width, dt in outs],
        compiler_params=_params(("parallel",)),
        name="in_proj",
    )(h, w_packed, kv_g, brow)


def _log_sigmoid(f):
    return jnp.minimum(f, 0.0) - jnp.log1p(jnp.exp(-jnp.abs(f)))


def _row_from_col(col):
    L = col.shape[0]
    eye = lax.broadcasted_iota(jnp.int32, (L, L), 0) == lax.broadcasted_iota(jnp.int32, (L, L), 1)
    return jnp.sum(jnp.where(eye, jnp.broadcast_to(col, (L, L)), 0.0), axis=0, keepdims=True)


def _mlstm_heads(qs, ks, vs, lis, li_rows, lf_rows, cts, ns, ms):
    L = qs[0].shape[0]
    heads = range(len(qs))
    r = lax.broadcasted_iota(jnp.int32, (L, L), 0)
    c = lax.broadcasted_iota(jnp.int32, (L, L), 1)
    tril = c <= r
    eye = c == r
    nt = (((1,), (1,)), ((), ()))
    tn = (((0,), (0,)), ((), ()))
    b_cols = [jnp.sum(jnp.where(tril, jnp.broadcast_to(lf_rows[h], (L, L)), 0.0), axis=1, keepdims=True)
              for h in heads]
    b_rows = [jnp.sum(jnp.where(eye, jnp.broadcast_to(b_cols[h], (L, L)), 0.0), axis=0, keepdims=True)
              for h in heads]
    qb = [qs[h].astype(BF16) for h in heads]
    kb = [ks[h].astype(BF16) for h in heads]
    vb = [vs[h].astype(BF16) for h in heads]
    qk = [lax.dot_general(qb[h], kb[h], nt, preferred_element_type=F32) for h in heads]
    qc = [jnp.dot(qb[h], cts[h].astype(BF16), preferred_element_type=F32) for h in heads]
    d = [jnp.where(tril, b_cols[h] - b_rows[h] + li_rows[h], -jnp.inf) for h in heads]
    inter = [b_cols[h] + ms[h] for h in heads]
    m_t = [jnp.maximum(inter[h], jnp.max(d[h], axis=1, keepdims=True)) for h in heads]
    a = [jnp.exp(inter[h] - m_t[h]) for h in heads]
    w = [jnp.exp(d[h] - m_t[h]) * qk[h] for h in heads]
    wv = [jnp.dot(w[h].astype(BF16), vb[h], preferred_element_type=F32) for h in heads]
    den = [a[h] * jnp.sum(qs[h] * ns[h], axis=1, keepdims=True) + jnp.sum(w[h], axis=1, keepdims=True)
           for h in heads]
    hs = [(a[h] * qc[h] + wv[h]) / jnp.maximum(jnp.abs(den[h]), jnp.exp(-m_t[h])) for h in heads]
    b_last = [b_cols[h][L - 1:L, :] for h in heads]
    g = [b_last[h] - b_cols[h] + lis[h] for h in heads]
    m_new = [jnp.maximum(b_last[h] + ms[h], jnp.max(g[h], axis=0, keepdims=True)) for h in heads]
    decay = [jnp.exp(b_last[h] + ms[h] - m_new[h]) for h in heads]
    kw = [ks[h] * jnp.exp(g[h] - m_new[h]) for h in heads]
    ct_new = [decay[h] * cts[h] + lax.dot_general(kw[h].astype(BF16), vb[h], tn, preferred_element_type=F32)
              for h in heads]
    n_new = [decay[h] * ns[h] + jnp.sum(kw[h], axis=0, keepdims=True) for h in heads]
    return hs, ct_new, n_new, m_new


def _mlstm_kernel(pm_ref, sm_ref, pmm_ref, smm_ref, cw_ref, mg_ref, out_ref, outm_ref,
                  ct_scr, n_scr, m_scr, x_scr):
    ci = pl.program_id(1)
    tail = 8

    def run_chunk(p_ref, s_ref, o_ref):
        L = p_ref.shape[0]
        heads = range(M_HEADS)
        x_scr[tail:tail + L, :] = p_ref[:, 0:2 * M_WIDTH].astype(F32)
        conv = cw_ref[0:1, :] * x_scr[tail - 3:tail - 3 + L, :]
        for j in range(1, CONV_WIDTH):
            conv = conv + cw_ref[j:j + 1, :] * x_scr[tail - 3 + j:tail - 3 + j + L, :]
        x_scr[0:tail, :] = x_scr[L:L + tail, :]
        qk = conv * _sigmoid(conv)
        cols = lambda base, h: slice(base + h * M_HEAD_DIM, base + (h + 1) * M_HEAD_DIM)
        lis = [s_ref[:, SM_IPRE + h:SM_IPRE + h + 1] for h in heads]
        if L % LANES == 0:
            gates_t = s_ref[...].T
            lf_all = _log_sigmoid(gates_t[SM_FPRE:SM_FPRE + M_HEADS, :])
            li_rows = [gates_t[SM_IPRE + h:SM_IPRE + h + 1, :] for h in heads]
            lf_rows = [lf_all[h:h + 1, :] for h in heads]
        else:
            li_rows = [_row_from_col(lis[h]) for h in heads]
            lf_rows = [_row_from_col(_log_sigmoid(s_ref[:, SM_FPRE + h:SM_FPRE + h + 1])) for h in heads]
        hs, ct_new, n_new, m_new = _mlstm_heads(
            [qk[:, cols(0, h)] * (M_HEAD_DIM ** -0.5) for h in heads],
            [qk[:, cols(M_WIDTH, h)] for h in heads],
            [p_ref[:, cols(2 * M_WIDTH, h)].astype(F32) for h in heads],
            lis, li_rows, lf_rows,
            [ct_scr[h] for h in heads], [n_scr[h:h + 1, :] for h in heads],
            [m_scr[h:h + 1, 0:1] for h in heads])
        for h in heads:
            ct_scr[h] = ct_new[h]
            n_scr[h:h + 1, :] = n_new[h]
            m_scr[h:h + 1, :] = jnp.broadcast_to(m_new[h], (1, LANES))
        mus = [jnp.mean(hs[h], axis=-1, keepdims=True) for h in heads]
        hcs = [hs[h] - mus[h] for h in heads]
        vars_ = [jnp.mean(hcs[h] * hcs[h], axis=-1, keepdims=True) for h in heads]
        for h in heads:
            o_gate = _sigmoid(p_ref[:, cols(3 * M_WIDTH, h)].astype(F32))
            o_ref[:, cols(0, h)] = (hcs[h] * lax.rsqrt(vars_[h] + 1e-5) * mg_ref[:, cols(0, h)]
                                    * o_gate).astype(o_ref.dtype)

    @pl.when(ci == 0)
    def _():
        ct_scr[...] = jnp.zeros_like(ct_scr)
        n_scr[...] = jnp.zeros_like(n_scr)
        m_scr[...] = jnp.zeros_like(m_scr)
        x_scr[...] = jnp.zeros_like(x_scr)
        run_chunk(pmm_ref, smm_ref, outm_ref)

    run_chunk(pm_ref, sm_ref, out_ref)


def _mlstm(pm, sm, conv_w, mnorm_g, l, batch, seq):
    n_real = batch * seq
    chunk = min(M_CHUNK, seq)
    nc = seq // chunk
    meta_blk = n_real // N_META
    out_real, out_meta = pl.pallas_call(
        _mlstm_kernel,
        grid=(batch, nc),
        in_specs=[pl.BlockSpec((chunk, 4 * M_WIDTH), lambda b, c: (b * nc + c, 0)),
                  pl.BlockSpec((chunk, LANES), lambda b, c: (b * nc + c, 0)),
                  pl.BlockSpec((N_META, 4 * M_WIDTH), lambda b, c: (meta_blk + b, 0)),
                  pl.BlockSpec((N_META, LANES), lambda b, c: (meta_blk + b, 0)),
                  _layer_block(conv_w, l), _layer_block(mnorm_g, l)],
        out_specs=[pl.BlockSpec((chunk, M_WIDTH), lambda b, c: (b * nc + c, 0)),
                   pl.BlockSpec((N_META, M_WIDTH), lambda b, c: (b, 0))],
        out_shape=[jax.ShapeDtypeStruct((n_real, M_WIDTH), BF16),
                   jax.ShapeDtypeStruct((batch * N_META, M_WIDTH), BF16)],
        scratch_shapes=[pltpu.VMEM((M_HEADS, M_HEAD_DIM, M_HEAD_DIM), F32),
                        pltpu.VMEM((8, LANES), F32),
                        pltpu.VMEM((8, LANES), F32),
                        pltpu.VMEM((chunk + 8, 2 * M_WIDTH), F32)],
        compiler_params=_params(("arbitrary", "arbitrary")),
        name="mlstm",
    )(pm, sm, pm, sm, conv_w, mnorm_g)
    return out_real, out_meta


def _rel_bucket_np(dist):
    n = np.maximum(dist, 0)
    nf = np.maximum(n, REL_MAX_EXACT).astype(np.float32)
    large = REL_MAX_EXACT + (np.log(nf / np.float32(REL_MAX_EXACT)) /
                             np.float32(math.log(REL_MAX_DIST / REL_MAX_EXACT))
                             * np.float32(REL_BUCKETS - REL_MAX_EXACT)).astype(np.int32)
    large = np.minimum(large, REL_BUCKETS - 1)
    return np.where(n < REL_MAX_EXACT, n, large).astype(np.int32)


def _bias_tables(rel_bias):
    q = np.arange(LANES)[:, None]
    k = np.arange(LANES)[None, :]
    far = 4 * LANES
    assert (_rel_bucket_np(np.arange(LANES + 1, far)) == REL_BUCKETS - 1).all()
    far_idx = np.full((LANES, LANES), REL_BUCKETS - 1, np.int32)
    near_idx = np.stack([_rel_bucket_np(q - k), _rel_bucket_np(LANES + q - k), far_idx])
    meta_idx = np.stack([_rel_bucket_np(q + N_META - np.minimum(k, N_META - 1)), far_idx])
    mq = np.arange(N_META)[:, None]
    mm_idx = _rel_bucket_np(mq - np.minimum(k, N_META - 1))
    rb = rel_bias.astype(F32)

    def lookup(idx, values):
        onehot = jnp.asarray(idx[..., None] == np.arange(REL_BUCKETS), F32)
        out = jnp.einsum('...b,bh->...h', onehot, values, precision=lax.Precision.HIGHEST)
        return jnp.moveaxis(out, -1, -3)

    rel_values = (rb - rb[REL_BUCKETS - 1][None, :]) * LOG2E
    return lookup(near_idx, rel_values), lookup(meta_idx, rel_values), lookup(mm_idx, rb)


def _dsa_kernel(top_k, qa_ref, qi_ref, sm_ref, cb_ref, smb_ref, cm_ref, qam_ref,
                wuk_ref, wuv_ref, near_ref, metab_ref, mmb_ref,
                out_ref, outm_ref,
                keys_scr, hi_scr, lo_scr, lg_scr, lgm_scr, rawa_scr, rawb_scr, pa_scr, pb_scr,
                caug_scr, ct_scr, kt_scr, cmaug_scr, cmt_scr, qs_scr, mx_scr, mrep_scr, acc_scr):
    i = pl.program_id(1)
    T = LANES
    H = A_HEADS
    col = lax.broadcasted_iota(jnp.int32, (T, T), 1)
    row = lax.broadcasted_iota(jnp.int32, (T, T), 0)
    nt = (((1,), (1,)), ((), ()))

    def q_latent(qa, hd, scale):
        ql = jnp.dot(qa[:, hd * A_HEAD_DIM:(hd + 1) * A_HEAD_DIM], wuk_ref[hd],
                     preferred_element_type=F32)
        return (ql * scale).astype(BF16)

    def ones_column(n):
        return jnp.where(lax.broadcasted_iota(jnp.int32, (n, T), 1) == 0, 1.0, 0.0).astype(BF16)

    @pl.when(i == 0)
    def _():
        caug_scr[:, 0:KV_RANK] = cb_ref[...].astype(BF16)
        caug_scr[:, KV_RANK:KV_RANK + T] = ones_column(caug_scr.shape[0])

        def transpose_keys(blk, carry):
            rows = pl.ds(pl.multiple_of(blk * KEY_SUB * T, KEY_SUB * T), KEY_SUB * T)
            ct_scr[blk] = cb_ref[rows, :].T.astype(BF16)
            kt_scr[blk] = smb_ref[rows, :].T[SM_KIDX:SM_KIDX + IDX_DIM, :].astype(BF16)
            return carry

        lax.fori_loop(0, ct_scr.shape[0], transpose_keys, 0)
        cm_pad = jnp.concatenate([cm_ref[...], jnp.zeros((T - N_META, KV_RANK), F32)], axis=0)
        cmaug_scr[:, 0:KV_RANK] = cm_pad.astype(BF16)
        cmt_scr[...] = cm_pad.T.astype(BF16)
        cmaug_scr[:, KV_RANK:KV_RANK + T] = ones_column(T)
        cmk = cmaug_scr[:, 0:KV_RANK]
        qam = qam_ref[...]
        mrow = lax.broadcasted_iota(jnp.int32, (N_META, T), 0)
        mcol = lax.broadcasted_iota(jnp.int32, (N_META, T), 1)
        for hd in range(H):
            lg = lax.dot_general(q_latent(qam, hd, A_HEAD_DIM ** -0.5), cmk, nt,
                                 preferred_element_type=F32) + mmb_ref[hd]
            lg = jnp.where(mcol <= mrow, lg, NEG_BIG)
            p = jnp.exp(lg - jnp.max(lg, axis=1, keepdims=True))
            p = p / jnp.sum(p, axis=1, keepdims=True)
            o = jnp.dot(p.astype(BF16), cmk, preferred_element_type=F32)
            outm_ref[:, hd * A_HEAD_DIM:(hd + 1) * A_HEAD_DIM] = jnp.dot(
                o.astype(BF16), wuv_ref[hd], preferred_element_type=F32).astype(outm_ref.dtype)

    qa = qa_ref[...]
    for hd in range(H):
        qs_scr[hd * T:(hd + 1) * T, :] = q_latent(qa, hd, A_HEAD_DIM ** -0.5 * LOG2E)
    qi = qi_ref[...]
    wv = sm_ref[:, SM_WIDX:SM_WIDX + IDX_HEADS] * IDX_SCALE
    t_col = i * T + lax.broadcasted_iota(jnp.int32, (T, 1), 0)
    n_chunks = (i + SCORE_CHUNK) // SCORE_CHUNK
    CW = SCORE_CHUNK * T

    def run_pipelined(n, produce, consume, buf_a, buf_b):
        last = n - 1
        produce(0, buf_a)

        def body(t, carry):
            s = 2 * t
            produce(jnp.minimum(s + 1, last), buf_b)
            consume(s, buf_a)
            produce(jnp.minimum(s + 2, last), buf_a)
            consume(s + 1, buf_b)
            return carry

        lax.fori_loop(0, n // 2, body, 0)

        @pl.when(n % 2 == 1)
        def _():
            consume(last, buf_a)

    def score_matmul(cix, buf):
        kc_t = kt_scr[cix]
        for hh in range(IDX_HEADS):
            buf[hh * T:(hh + 1) * T, :] = jnp.dot(
                qi[:, hh * IDX_DIM:(hh + 1) * IDX_DIM], kc_t, preferred_element_type=F32)

    def score_keys(cix, buf):
        acc = jnp.zeros((T, CW), F32)
        for hh in range(IDX_HEADS):
            acc = acc + wv[:, hh:hh + 1] * jnp.maximum(buf[hh * T:(hh + 1) * T, :], 0.0)
        acc = jnp.where(acc == 0.0, 0.0, acc)
        bits = lax.bitcast_convert_type(acc, jnp.int32)
        key = jnp.where(bits < 0, bits ^ jnp.int32(0x7FFFFFFF), bits)
        s_idx = cix * CW + lax.broadcasted_iota(jnp.int32, (T, CW), 1)
        key = jnp.where(s_idx <= t_col, key, jnp.int32(INT_MIN))
        for u in range(SCORE_CHUNK):
            tile = key[:, u * T:(u + 1) * T]
            keys_scr[cix * SCORE_CHUNK + u] = tile
            tile_t = tile.T
            hi_scr[cix * SCORE_CHUNK + u] = lax.shift_right_arithmetic(tile_t, 16).astype(jnp.int16)
            lo_scr[cix * SCORE_CHUNK + u] = ((tile_t & 0xFFFF) - HALF_BIAS).astype(jnp.int16)

    run_pipelined(n_chunks, score_matmul, score_keys, rawa_scr, rawb_scr)

    def rep16(row_i32):
        return jnp.broadcast_to(row_i32, (T, T)).astype(jnp.int16)

    def count16(src_scr, pred_fn):
        def body(cix, cnt):
            for u in range(SCORE_CHUNK):
                hit = pred_fn(src_scr[cix * SCORE_CHUNK + u])
                cnt = cnt + jnp.where(hit, jnp.int16(1), jnp.int16(0))
            return cnt
        cnt = lax.fori_loop(0, n_chunks, body, jnp.zeros((T, T), jnp.int16))
        return jnp.sum(cnt.astype(F32), axis=0, keepdims=True)

    def search16(src_scr, k_row):
        def bit_body(bi, carry):
            ans, above = carry
            cand_u = ans | lax.shift_left(jnp.int32(1), jnp.int32(15) - bi)
            cand = rep16(cand_u - HALF_BIAS)
            total = count16(src_scr, lambda x: x >= cand)
            take = total >= k_row
            return jnp.where(take, cand_u, ans), jnp.where(take, above, total)
        return lax.fori_loop(0, 16, bit_body, (jnp.zeros((1, T), jnp.int32), jnp.zeros((1, T), F32)))

    k_row = jnp.full((1, T), float(top_k), F32)
    hi_u, hi_above = search16(hi_scr, k_row)
    hi_s = hi_u - HALF_BIAS
    hi_rep = rep16(hi_s)
    k_low = k_row - hi_above

    def band_body(cix, carry):
        for u in range(SCORE_CHUNK):
            j = cix * SCORE_CHUNK + u
            lo_scr[j] = jnp.where(hi_scr[j] == hi_rep, lo_scr[j], jnp.int16(-HALF_BIAS))
        return carry

    lax.fori_loop(0, n_chunks, band_body, 0)
    lo_u, lo_above = search16(lo_scr, k_low)
    need_row = k_low - lo_above
    thr_row = lax.shift_left(hi_s, 16) | lo_u
    thr = jnp.broadcast_to(thr_row, (T, T)).T
    need = jnp.broadcast_to(need_row, (T, T)).T

    mx_scr[...] = jnp.full_like(mx_scr, NEG_BIG)
    acc_scr[...] = jnp.zeros_like(acc_scr)
    hg = H // ATT_GROUPS
    groups = [slice(g * hg * T, (g + 1) * hg * T) for g in range(ATT_GROUPS)]
    meta_sel = jnp.minimum(i, 1)

    def max_pass(ck_t, madds, bias_fns, store):
        lgs = [jnp.dot(qs_scr[rs, :], ck_t, preferred_element_type=F32) for rs in groups]
        for hd in range(H):
            rs = slice(hd * T, (hd + 1) * T)
            lo = (hd % hg) * T
            mx = mx_scr[rs, :]
            for u in range(len(madds)):
                x = lgs[hd // hg][lo:lo + T, u * T:(u + 1) * T] + madds[u]
                if bias_fns[u] is not None:
                    x = x + bias_fns[u](hd)
                store(rs, u, x)
                mx = jnp.maximum(mx, x)
            mx_scr[rs, :] = mx

    def sum_pass(c_aug, n_sub, load):
        for grp in groups:
            m_rep = mrep_scr[grp, :]
            ph = [jnp.exp2(load(grp, u) - m_rep).astype(BF16) for u in range(n_sub)]
            p = ph[0] if n_sub == 1 else jnp.concatenate(ph, axis=1)
            acc_scr[grp, :] += jnp.dot(p, c_aug, preferred_element_type=F32)

    def key_rows(step):
        return pl.ds(pl.multiple_of(step * KEY_SUB * T, KEY_SUB * T), KEY_SUB * T)

    upper = (row < col).astype(BF16)

    def mask_step(step, seen, near):
        madds, bias_fns = [], []
        for u in range(KEY_SUB):
            j = KEY_SUB * step + u
            kk = keys_scr[j]
            eq = kk == thr
            eqf = jnp.where(eq, 1.0, 0.0)
            before = jnp.dot(eqf.astype(BF16), upper, preferred_element_type=F32) + seen
            sel = (kk > thr) | (eq & (before < need))
            if near:
                sel = sel & ((j * T + col) <= (i * T + row))
                dsel = jnp.clip(i - j, 0, 2)
                bias_fns.append(lambda hd, dsel=dsel: near_ref[dsel, hd])
            else:
                bias_fns.append(None)
            seen = seen + jnp.sum(eqf, axis=1, keepdims=True)
            madds.append(jnp.where(sel, 0.0, NEG_BIG))

        def store(rs, u, x):
            lg_scr[step, rs, u * T:(u + 1) * T] = x

        max_pass(ct_scr[step], madds, bias_fns, store)
        return seen

    def weights_step(step, buf):
        for grp in groups:
            m_rep = mrep_scr[grp, :]
            for u in range(KEY_SUB):
                buf[grp, u * T:(u + 1) * T] = jnp.exp2(lg_scr[step, grp, u * T:(u + 1) * T] - m_rep)

    def accumulate_step(step, buf):
        c_aug = caug_scr[key_rows(step), :]
        for grp in groups:
            acc_scr[grp, :] += jnp.dot(buf[grp, :].astype(BF16), c_aug, preferred_element_type=F32)

    def store_meta(rs, u, x):
        lgm_scr[rs, :] = x

    n_far = jnp.maximum(i - 1, 0) // KEY_SUB
    n_steps = (i + KEY_SUB) // KEY_SUB
    max_pass(cmt_scr[...], [jnp.where(col < N_META, 0.0, NEG_BIG)],
             [lambda hd: metab_ref[meta_sel, hd]], store_meta)
    seen = lax.fori_loop(0, n_far, lambda s, c: mask_step(s, c, False), jnp.zeros((T, 1), F32))
    lax.fori_loop(n_far, n_steps, lambda s, c: mask_step(s, c, True), seen)
    mrep_scr[...] = jnp.broadcast_to(jnp.max(mx_scr[...], axis=1, keepdims=True), mrep_scr.shape)
    sum_pass(cmaug_scr[...], 1, lambda grp, u: lgm_scr[grp, :])
    run_pipelined(n_steps, weights_step, accumulate_step, pa_scr, pb_scr)

    for hd in range(H):
        rs = slice(hd * T, (hd + 1) * T)
        o = acc_scr[rs, 0:KV_RANK] / acc_scr[rs, KV_RANK:KV_RANK + 1]
        out_ref[:, hd * A_HEAD_DIM:(hd + 1) * A_HEAD_DIM] = jnp.dot(
            o.astype(BF16), wuv_ref[hd], preferred_element_type=F32).astype(out_ref.dtype)


def _dsa(qa, qi, sm, c, wuk_t, wuv, tables, l, batch, seq):
    n_real = batch * seq
    nq = seq // LANES
    n_tiles = ((nq + SCORE_CHUNK - 1) // SCORE_CHUNK) * SCORE_CHUNK
    top_k = min(TOPK_MAX, seq // 4)
    meta_blk = n_real // N_META
    near, metab, mmb = tables
    full = lambda a: pl.BlockSpec(a.shape, lambda b, i: (0,) * a.ndim)
    assert seq % (SCORE_CHUNK * LANES) == 0 and nq % KEY_SUB == 0 and SCORE_CHUNK == KEY_SUB
    out_real, out_meta = pl.pallas_call(
        functools.partial(_dsa_kernel, top_k),
        grid=(batch, nq),
        in_specs=[pl.BlockSpec((LANES, A_WIDTH), lambda b, i: (b * nq + i, 0)),
                  pl.BlockSpec((LANES, IDX_HEADS * IDX_DIM), lambda b, i: (b * nq + i, 0)),
                  pl.BlockSpec((LANES, LANES), lambda b, i: (b * nq + i, 0)),
                  pl.BlockSpec((seq, KV_RANK), lambda b, i: (b, 0)),
                  pl.BlockSpec((seq, LANES), lambda b, i: (b, 0)),
                  pl.BlockSpec((N_META, KV_RANK), lambda b, i: (meta_blk + b, 0)),
                  pl.BlockSpec((N_META, A_WIDTH), lambda b, i: (meta_blk + b, 0)),
                  _layer_block(wuk_t, l), _layer_block(wuv, l), full(near), full(metab), full(mmb)],
        out_specs=[pl.BlockSpec((LANES, A_WIDTH), lambda b, i: (b * nq + i, 0)),
                   pl.BlockSpec((N_META, A_WIDTH), lambda b, i: (b, 0))],
        out_shape=[jax.ShapeDtypeStruct((n_real, A_WIDTH), BF16),
                   jax.ShapeDtypeStruct((batch * N_META, A_WIDTH), BF16)],
        scratch_shapes=[pltpu.VMEM((n_tiles, LANES, LANES), jnp.int32),
                        pltpu.VMEM((n_tiles, LANES, LANES), jnp.int16),
                        pltpu.VMEM((n_tiles, LANES, LANES), jnp.int16),
                        pltpu.VMEM((nq // KEY_SUB, A_HEADS * LANES, KEY_SUB * LANES), F32),
                        pltpu.VMEM((A_HEADS * LANES, LANES), F32),
                        pltpu.VMEM((IDX_HEADS * LANES, SCORE_CHUNK * LANES), F32),
                        pltpu.VMEM((IDX_HEADS * LANES, SCORE_CHUNK * LANES), F32),
                        pltpu.VMEM((A_HEADS * LANES, KEY_SUB * LANES), F32),
                        pltpu.VMEM((A_HEADS * LANES, KEY_SUB * LANES), F32),
                        pltpu.VMEM((seq, KV_RANK + LANES), BF16),
                        pltpu.VMEM((nq // KEY_SUB, KV_RANK, KEY_SUB * LANES), BF16),
                        pltpu.VMEM((nq // KEY_SUB, IDX_DIM, KEY_SUB * LANES), BF16),
                        pltpu.VMEM((LANES, KV_RANK + LANES), BF16),
                        pltpu.VMEM((KV_RANK, LANES), BF16),
                        pltpu.VMEM((A_HEADS * LANES, KV_RANK), BF16),
                        pltpu.VMEM((A_HEADS * LANES, LANES), F32),
                        pltpu.VMEM((A_HEADS * LANES, LANES), F32),
                        pltpu.VMEM((A_HEADS * LANES, KV_RANK + LANES), F32)],
        compiler_params=_params(("arbitrary", "arbitrary")),
        name="dsa",
    )(qa, qi, sm, c, sm, c, qa, wuk_t, wuv, near, metab, mmb)
    return out_real, out_meta


def _merge_kernel(alpha, h_ref, hm_ref, ha_ref, g_ref, wbm_ref, wba_ref, wo_ref, lg_ref, lb_ref,
                  wr_ref, br_ref, h1_ref, comb_ref, bgt_ref, cnt_ref):
    d = h_ref.shape[1]
    gm = _sigmoid(g_ref[:, 0:d].astype(F32))
    ga = _sigmoid(g_ref[:, d:2 * d].astype(F32))
    y = gm * jnp.dot(hm_ref[...], wbm_ref[...], preferred_element_type=F32) + \
        ga * jnp.dot(ha_ref[...], wba_ref[...], preferred_element_type=F32)
    z = alpha * h_ref[...] + jnp.dot(y.astype(BF16), wo_ref[...], preferred_element_type=F32)
    h1 = _layer_norm_rows(z, lg_ref[...], lb_ref[...], 1e-5)
    h1_ref[...] = h1

    tm = h1.shape[0]
    logits_t = lax.dot_general(wr_ref[...], h1.astype(BF16), (((1,), (1,)), ((), ())),
                               preferred_element_type=F32)
    scores = _sigmoid(logits_t[0:N_EXPERTS, :])
    sel = scores + br_ref[0:N_EXPERTS, :]
    best = None
    for gidx in range(N_GROUPS):
        r0, r1, r2, r3 = (sel[gidx * GROUP_SIZE + u:gidx * GROUP_SIZE + u + 1, :] for u in range(4))
        a, b = jnp.maximum(r0, r1), jnp.minimum(r0, r1)
        c, dd = jnp.maximum(r2, r3), jnp.minimum(r2, r3)
        gs = jnp.maximum(a, c) + jnp.maximum(jnp.minimum(a, c), jnp.maximum(b, dd))
        if best is None:
            best, bg = gs, jnp.zeros((1, tm), jnp.int32)
        else:
            upd = gs > best
            bg = jnp.where(upd, gidx, bg)
            best = jnp.where(upd, gs, best)
    eidx = lax.broadcasted_iota(jnp.int32, (N_EXPERTS, tm), 0)
    masked = jnp.where((eidx // GROUP_SIZE) == bg, sel, -jnp.inf)
    v1 = jnp.max(masked, axis=0, keepdims=True)
    i1 = jnp.min(jnp.where(masked == v1, eidx, N_EXPERTS), axis=0, keepdims=True)
    masked2 = jnp.where(eidx == i1, -jnp.inf, masked)
    v2 = jnp.max(masked2, axis=0, keepdims=True)
    i2 = jnp.min(jnp.where(masked2 == v2, eidx, N_EXPERTS), axis=0, keepdims=True)
    s1 = jnp.sum(jnp.where(eidx == i1, scores, 0.0), axis=0, keepdims=True)
    s2 = jnp.sum(jnp.where(eidx == i2, scores, 0.0), axis=0, keepdims=True)
    tot = s1 + s2
    comb_t = jnp.where(eidx == i1, s1 / tot, 0.0) + jnp.where(eidx == i2, s2 / tot, 0.0)
    comb_pad = jnp.concatenate([comb_t, jnp.zeros((LANES - N_EXPERTS, tm), F32)], axis=0)
    comb_ref[...] = comb_pad.T
    bgt_ref[...] = jnp.broadcast_to(bg, (8, tm))
    gidx8 = lax.broadcasted_iota(jnp.int32, (8, tm), 0)
    counts = jnp.sum(jnp.where(gidx8 == bg, 1.0, 0.0), axis=1, keepdims=True)
    cnt_ref[0] = jnp.broadcast_to(counts, (8, LANES)).astype(jnp.int32)


def _merge(h, hm, ha, g, w_bm, w_ba, w_o, ln_g, ln_b, wr_t, br, l, alpha):
    n, d = h.shape
    tm = MOE_ROW_TILE
    row = lambda width: pl.BlockSpec((tm, width), lambda r: (r, 0))
    full = lambda a: pl.BlockSpec(a.shape, lambda r: (0,) * a.ndim)
    args = (h, hm, ha, g, w_bm, w_ba, w_o, ln_g, ln_b, wr_t, br)
    return pl.pallas_call(
        functools.partial(_merge_kernel, alpha),
        grid=(n // tm,),
        in_specs=[row(d), row(M_WIDTH), row(A_WIDTH), row(2 * d)] +
                 [_layer_block(a, l) for a in args[4:9]] + [full(wr_t), full(br)],
        out_specs=[row(d), row(LANES),
                   pl.BlockSpec((8, tm), lambda r: (0, r)),
                   pl.BlockSpec((1, 8, LANES), lambda r: (r, 0, 0))],
        out_shape=[jax.ShapeDtypeStruct((n, d), F32), jax.ShapeDtypeStruct((n, LANES), F32),
                   jax.ShapeDtypeStruct((8, n), jnp.int32),
                   jax.ShapeDtypeStruct((n // tm, 8, LANES), jnp.int32)],
        compiler_params=_params(("parallel",)),
        name="merge",
    )(*args)


def _moe_kernel(alpha, cap, cnt_ref, h_ref, comb_ref, bgt_ref, wg_ref, wu_ref, wd_ref, lg_ref, lb_ref,
                out_ref, xb_scr, cs_scr, yt_scr, tri_scr):
    r = pl.program_id(0)
    g = pl.program_id(1)
    rows = h_ref.shape[0]

    @pl.when((r == 0) & (g == 0))
    def _():
        t0 = lax.broadcasted_iota(jnp.int32, (rows, rows), 0)
        t1 = lax.broadcasted_iota(jnp.int32, (rows, rows), 1)
        tri_scr[...] = (t0 < t1).astype(BF16)

    @pl.when(g == 0)
    def _():
        xb_scr[...] = h_ref[...].astype(BF16)
        yt_scr[...] = jnp.zeros_like(yt_scr)
        c = comb_ref[...]
        for part in range(2):
            cb = c.astype(BF16)
            cs_scr[part] = cb
            c = c - cb.astype(F32)

    member = bgt_ref[0:1, :] == g
    mem8 = jnp.broadcast_to(jnp.where(member, 1.0, 0.0), (8, rows)).astype(BF16)
    rank = jnp.dot(mem8, tri_scr[...], preferred_element_type=F32)[0:1, :].astype(jnp.int32)
    n_blocks = (cnt_ref[r * N_GROUPS + g] + cap - 1) // cap
    lane = lax.broadcasted_iota(jnp.int32, (cap, LANES), 1)
    tn = (((0,), (0,)), ((), ()))

    def block(b, carry):
        slot = lax.broadcasted_iota(jnp.int32, (cap, rows), 0) + b * cap
        onehot = jnp.where(member & (rank == slot), 1.0, 0.0).astype(BF16)
        xg = jnp.dot(onehot, xb_scr[...], preferred_element_type=F32).astype(BF16)
        cw = jnp.dot(onehot, cs_scr[0], preferred_element_type=F32)
        cw = cw + jnp.dot(onehot, cs_scr[1], preferred_element_type=F32)
        y = jnp.zeros((cap, out_ref.shape[1]), F32)
        for e in range(GROUP_SIZE):
            gate = jnp.dot(xg, wg_ref[e], preferred_element_type=F32)
            up = jnp.dot(xg, wu_ref[e], preferred_element_type=F32)
            he = gate * _sigmoid(gate) * up
            o = jnp.dot(he.astype(BF16), wd_ref[e], preferred_element_type=F32)
            ce = jnp.sum(jnp.where(lane == g * GROUP_SIZE + e, cw, 0.0), axis=1, keepdims=True)
            y = y + ce * o
        yt_scr[...] += lax.dot_general(onehot, y.astype(BF16), tn, preferred_element_type=F32)
        return carry

    lax.fori_loop(0, n_blocks, block, 0)

    @pl.when(g == pl.num_programs(1) - 1)
    def _():
        z = alpha * h_ref[...] + yt_scr[...]
        out_ref[...] = _layer_norm_rows(z, lg_ref[...], lb_ref[...], 1e-5)


def _moe(h, comb, bgt, counts, w_gate, w_up, w_down, ln_g, ln_b, l, alpha):
    n, d = h.shape
    de = w_gate.shape[-1]
    tm = MOE_ROW_TILE
    cnt = counts[:, 0:N_GROUPS, 0].reshape(-1)
    grid_spec = pltpu.PrefetchScalarGridSpec(
        num_scalar_prefetch=1,
        grid=(n // tm, N_GROUPS),
        in_specs=[pl.BlockSpec((tm, d), lambda r, g, c: (r, 0)),
                  pl.BlockSpec((tm, LANES), lambda r, g, c: (r, 0)),
                  pl.BlockSpec((8, tm), lambda r, g, c: (0, r)),
                  pl.BlockSpec((None, GROUP_SIZE, d, de), lambda r, g, c: (l, g, 0, 0)),
                  pl.BlockSpec((None, GROUP_SIZE, d, de), lambda r, g, c: (l, g, 0, 0)),
                  pl.BlockSpec((None, GROUP_SIZE, de, d), lambda r, g, c: (l, g, 0, 0)),
                  _layer_block(ln_g, l), _layer_block(ln_b, l)],
        out_specs=pl.BlockSpec((tm, d), lambda r, g, c: (r, 0)),
        scratch_shapes=[pltpu.VMEM((tm, d), BF16), pltpu.VMEM((2, tm, LANES), BF16),
                        pltpu.VMEM((tm, d), F32), pltpu.VMEM((tm, tm), BF16)])
    return pl.pallas_call(
        functools.partial(_moe_kernel, alpha, MOE_CAP),
        grid_spec=grid_spec,
        out_shape=jax.ShapeDtypeStruct((n, d), F32),
        compiler_params=_params(("arbitrary", "arbitrary")),
        name="moe",
    )(cnt, h, comb, bgt, w_gate, w_up, w_down, ln_g, ln_b)


def _with_meta(real, meta, n_pad):
    pad = n_pad - real.shape[0] - meta.shape[0]
    return jnp.concatenate([real, meta, jnp.zeros((pad, real.shape[1]), real.dtype)], axis=0)


def kernel(x, meta_tokens, ln_in_g, ln_in_b, w_in, conv_w, b_if, mnorm_g, kv_norm_g, w_uk, w_uv,
           w_branch_m, w_branch_a, w_out, ln1_g, ln1_b, w_router, b_router, w_gate, w_up, w_down,
           ln2_g, ln2_b, rel_bias):
    batch, seq, d = x.shape
    depth = w_in.shape[0]
    alpha = (2 * depth) ** 0.25
    n_real = batch * seq
    n_meta = batch * N_META
    tile = math.lcm(ROW_TILE, MOE_ROW_TILE)
    n_pad = -(-(n_real + n_meta) // tile) * tile
    assert seq % LANES == 0 and n_real % N_META == 0

    assert n_real % ROW_TILE == 0
    tail = _with_meta(jnp.tile(meta_tokens.astype(x.dtype), (batch, 1)), jnp.zeros((0, d), x.dtype),
                      n_pad - n_real)
    h = _input_ln(x.reshape(n_real, d), tail, ln_in_g, ln_in_b)
    tables = _bias_tables(rel_bias)
    row3 = lambda a: a.astype(F32)[:, None, :]
    w_packed, brow, kv_g = _pack_w_in(w_in), _pack_b_if(b_if), row3(kv_norm_g)
    conv_f, mnorm = conv_w.astype(F32), row3(mnorm_g)
    wuk_t = jnp.swapaxes(w_uk, 2, 3).astype(BF16)
    wuv = w_uv.astype(BF16)
    w_bm, w_ba, w_o = w_branch_m.astype(BF16), w_branch_a.astype(BF16), w_out.astype(BF16)
    wr_t = jnp.zeros((LANES, d), F32).at[0:N_EXPERTS].set(w_router.T).astype(BF16)
    br = jnp.zeros((LANES, 1), F32).at[0:N_EXPERTS, 0].set(b_router)
    wg, wu, wd = w_gate.astype(BF16), w_up.astype(BF16), w_down.astype(BF16)
    g1, b1, g2, b2 = row3(ln1_g), row3(ln1_b), row3(ln2_g), row3(ln2_b)
    for l in range(depth):
        pm, qa, qi, c, sm, g = _project(h, w_packed, kv_g, brow, l)
        hm_real, hm_meta = _mlstm(pm, sm, conv_f, mnorm, l, batch, seq)
        ha_real, ha_meta = _dsa(qa, qi, sm, c, wuk_t, wuv, tables, l, batch, seq)
        hm = _with_meta(hm_real, hm_meta, n_pad)
        ha = _with_meta(ha_real, ha_meta, n_pad)
        h1, comb, bgt, counts = _merge(h, hm, ha, g, w_bm, w_ba, w_o, g1, b1, wr_t, br, l, alpha)
        h = _moe(h1, comb, bgt, counts, wg, wu, wd, g2, b2, l, alpha)
    return h[:n_real].reshape(batch, seq, d)
```

```python
import functools
import math

import numpy as np
import jax
import jax.numpy as jnp
from jax import lax
from jax.experimental import pallas as pl
from jax.experimental.pallas import tpu as pltpu

F32 = jnp.float32
BF16 = jnp.bfloat16

N_META = 16
M_HEADS = 4
M_HEAD_DIM = 128
M_WIDTH = M_HEADS * M_HEAD_DIM
CONV_WIDTH = 4
A_HEADS = 8
A_HEAD_DIM = 64
A_WIDTH = A_HEADS * A_HEAD_DIM
KV_RANK = 128
IDX_HEADS = 4
IDX_DIM = 64
IDX_SCALE = (IDX_HEADS * IDX_DIM) ** -0.5
TOPK_MAX = 256
REL_BUCKETS = 32
REL_MAX_EXACT = 16
REL_MAX_DIST = 128
N_EXPERTS = 16
N_GROUPS = 4
GROUP_SIZE = N_EXPERTS // N_GROUPS

LANES = 128
ROW_TILE = 256
MOE_ROW_TILE = 768
MOE_CAP = 224
M_CHUNK = 256
SCORE_CHUNK = 4
KEY_SUB = 4
ATT_GROUPS = 2
VMEM_LIMIT = 56 * 1024 * 1024
NEG_BIG = -1e30
INT_MIN = -2 ** 31
HALF_BIAS = 2 ** 15
LOG2E = math.log2(math.e)

D_MODEL = 1024
IN_WIDTHS = (("q_m", M_WIDTH), ("k_m", M_WIDTH), ("v_m", M_WIDTH), ("o_m", M_WIDTH), ("i_pre", M_HEADS),
             ("f_pre", M_HEADS), ("q_a", A_WIDTH), ("c_kv", KV_RANK), ("q_idx", IDX_HEADS * IDX_DIM),
             ("k_idx", IDX_DIM), ("w_idx", IDX_HEADS), ("g_m", D_MODEL), ("g_a", D_MODEL))
W_PM = 4 * M_WIDTH
W_G = 2 * D_MODEL
PK_PM = 0
PK_QA = PK_PM + W_PM
PK_QI = PK_QA + A_WIDTH
PK_CKV = PK_QI + IDX_HEADS * IDX_DIM
PK_SM = PK_CKV + KV_RANK
PK_G = PK_SM + LANES
PK_TOTAL = PK_G + W_G
SM_KIDX = 0
SM_WIDX = SM_KIDX + IDX_DIM
SM_IPRE = SM_WIDX + IDX_HEADS
SM_FPRE = SM_IPRE + M_HEADS
SM_USED = SM_FPRE + M_HEADS


def _params(sem):
    return pltpu.CompilerParams(dimension_semantics=sem, vmem_limit_bytes=VMEM_LIMIT)


def _sigmoid(x):
    return 1.0 / (1.0 + jnp.exp(-x))


def _layer_norm_rows(x, g, b, eps):
    mu = jnp.mean(x, axis=-1, keepdims=True)
    xc = x - mu
    var = jnp.mean(xc * xc, axis=-1, keepdims=True)
    return xc * lax.rsqrt(var + eps) * g + b


def _ln_kernel(real_tiles, x_ref, m_ref, g_ref, b_ref, o_ref):
    r = pl.program_id(0)

    @pl.when(r < real_tiles)
    def _():
        o_ref[...] = _layer_norm_rows(x_ref[...], g_ref[...], b_ref[...], 1e-5)

    @pl.when(r >= real_tiles)
    def _():
        o_ref[...] = _layer_norm_rows(m_ref[...], g_ref[...], b_ref[...], 1e-5)


def _input_ln(x, tail, g, b):
    n_real, d = x.shape
    real_tiles = n_real // ROW_TILE
    tiles = real_tiles + tail.shape[0] // ROW_TILE
    return pl.pallas_call(
        functools.partial(_ln_kernel, real_tiles),
        grid=(tiles,),
        in_specs=[pl.BlockSpec((ROW_TILE, d), lambda r: (jnp.minimum(r, real_tiles - 1), 0)),
                  pl.BlockSpec((ROW_TILE, d), lambda r: (jnp.maximum(r - real_tiles, 0), 0)),
                  pl.BlockSpec((1, d), lambda r: (0, 0)),
                  pl.BlockSpec((1, d), lambda r: (0, 0))],
        out_specs=pl.BlockSpec((ROW_TILE, d), lambda r: (r, 0)),
        out_shape=jax.ShapeDtypeStruct((tiles * ROW_TILE, d), F32),
        compiler_params=_params(("arbitrary",)),
        name="input_ln",
    )(x, tail, g.reshape(1, d), b.reshape(1, d))


def _proj_kernel(h_ref, w_ref, kvg_ref, brow_ref, pm_ref, qa_ref, qi_ref, c_ref, sm_ref, g_ref):
    x = h_ref[...].astype(BF16)

    def mm(lo, width):
        return jnp.dot(x, w_ref[:, lo:lo + width], preferred_element_type=F32)

    pm_ref[...] = mm(PK_PM, W_PM).astype(BF16)
    qa_ref[...] = mm(PK_QA, A_WIDTH).astype(BF16)
    qi_ref[...] = mm(PK_QI, IDX_HEADS * IDX_DIM).astype(BF16)
    ckv = mm(PK_CKV, KV_RANK)
    c_ref[...] = ckv * lax.rsqrt(jnp.mean(ckv * ckv, axis=-1, keepdims=True) + 1e-6) * kvg_ref[...]
    sm_ref[...] = mm(PK_SM, LANES) + brow_ref[...]
    g_ref[...] = mm(PK_G, W_G).astype(BF16)


def _pack_w_in(w):
    part, start = {}, 0
    for name, width in IN_WIDTHS:
        part[name] = w[..., start:start + width]
        start += width
    assert start == w.shape[-1] and w.shape[-2] == D_MODEL
    order = ("q_m", "k_m", "v_m", "o_m", "q_a", "q_idx", "c_kv", "k_idx", "w_idx", "i_pre", "f_pre")
    cols = [part[name] for name in order]
    cols += [jnp.zeros(w.shape[:-1] + (LANES - SM_USED,), w.dtype), part["g_m"], part["g_a"]]
    return jnp.concatenate(cols, axis=-1).astype(BF16)


def _pack_b_if(b_if):
    depth = b_if.shape[0]
    return jnp.concatenate([jnp.zeros((depth, SM_IPRE), F32), b_if.astype(F32),
                            jnp.zeros((depth, LANES - SM_USED), F32)], axis=1)[:, None, :]


def _layer_block(arr, l):
    zeros = (0,) * (arr.ndim - 1)
    return pl.BlockSpec((None,) + arr.shape[1:], lambda *idx: (l,) + zeros)


def _project(h, w_packed, kv_g, brow, l):
    n, d = h.shape
    row = lambda width: pl.BlockSpec((MOE_ROW_TILE, width), lambda r: (r, 0))
    shp = lambda width, dt: jax.ShapeDtypeStruct((n, width), dt)
    outs = ((W_PM, BF16), (A_WIDTH, BF16), (IDX_HEADS * IDX_DIM, BF16), (KV_RANK, F32), (LANES, F32), (W_G, BF16))
    return pl.pallas_call(
        _proj_kernel,
        grid=(n // MOE_ROW_TILE,),
        in_specs=[row(d), _layer_block(w_packed, l), _layer_block(kv_g, l), _layer_block(brow, l)],
        out_specs=[row(width) for width, _ in outs],
        out_shape=[shp(width, dt) for width, dt in outs],
        compiler_params=_params(("parallel",)),
        name="in_proj",
    )(h, w_packed, kv_g, brow)


def _log_sigmoid(f):
    return jnp.minimum(f, 0.0) - jnp.log1p(jnp.exp(-jnp.abs(f)))


def _row_from_col(col):
    L = col.shape[0]
    eye = lax.broadcasted_iota(jnp.int32, (L, L), 0) == lax.broadcasted_iota(jnp.int32, (L, L), 1)
    return jnp.sum(jnp.where(eye, jnp.broadcast_to(col, (L, L)), 0.0), axis=0, keepdims=True)


def _mlstm_heads(qs, ks, vs, lis, li_rows, lf_rows, cts, ns, ms):
    L = qs[0].shape[0]
    heads = range(len(qs))
    r = lax.broadcasted_iota(jnp.int32, (L, L), 0)
    c = lax.broadcasted_iota(jnp.int32, (L, L), 1)
    tril = c <= r
    eye = c == r
    nt = (((1,), (1,)), ((), ()))
    tn = (((0,), (0,)), ((), ()))
    b_cols = [jnp.sum(jnp.where(tril, jnp.broadcast_to(lf_rows[h], (L, L)), 0.0), axis=1, keepdims=True)
              for h in heads]
    b_rows = [jnp.sum(jnp.where(eye, jnp.broadcast_to(b_cols[h], (L, L)), 0.0), axis=0, keepdims=True)
              for h in heads]
    qb = [qs[h].astype(BF16) for h in heads]
    kb = [ks[h].astype(BF16) for h in heads]
    vb = [vs[h].astype(BF16) for h in heads]
    qk = [lax.dot_general(qb[h], kb[h], nt, preferred_element_type=F32) for h in heads]
    qc = [jnp.dot(qb[h], cts[h].astype(BF16), preferred_element_type=F32) for h in heads]
    d = [jnp.where(tril, b_cols[h] - b_rows[h] + li_rows[h], -jnp.inf) for h in heads]
    inter = [b_cols[h] + ms[h] for h in heads]
    m_t = [jnp.maximum(inter[h], jnp.max(d[h], axis=1, keepdims=True)) for h in heads]
    a = [jnp.exp(inter[h] - m_t[h]) for h in heads]
    w = [jnp.exp(d[h] - m_t[h]) * qk[h] for h in heads]
    wv = [jnp.dot(w[h].astype(BF16), vb[h], preferred_element_type=F32) for h in heads]
    den = [a[h] * jnp.sum(qs[h] * ns[h], axis=1, keepdims=True) + jnp.sum(w[h], axis=1, keepdims=True)
           for h in heads]
    hs = [(a[h] * qc[h] + wv[h]) / jnp.maximum(jnp.abs(den[h]), jnp.exp(-m_t[h])) for h in heads]
    b_last = [b_cols[h][L - 1:L, :] for h in heads]
    g = [b_last[h] - b_cols[h] + lis[h] for h in heads]
    m_new = [jnp.maximum(b_last[h] + ms[h], jnp.max(g[h], axis=0, keepdims=True)) for h in heads]
    decay = [jnp.exp(b_last[h] + ms[h] - m_new[h]) for h in heads]
    kw = [ks[h] * jnp.exp(g[h] - m_new[h]) for h in heads]
    ct_new = [decay[h] * cts[h] + lax.dot_general(kw[h].astype(BF16), vb[h], tn, preferred_element_type=F32)
              for h in heads]
    n_new = [decay[h] * ns[h] + jnp.sum(kw[h], axis=0, keepdims=True) for h in heads]
    return hs, ct_new, n_new, m_new


def _mlstm_kernel(pm_ref, sm_ref, pmm_ref, smm_ref, cw_ref, mg_ref, out_ref, outm_ref,
                  ct_scr, n_scr, m_scr, x_scr):
    ci = pl.program_id(1)
    tail = 8

    def run_chunk(p_ref, s_ref, o_ref):
        L = p_ref.shape[0]
        heads = range(M_HEADS)
        x_scr[tail:tail + L, :] = p_ref[:, 0:2 * M_WIDTH].astype(F32)
        conv = cw_ref[0:1, :] * x_scr[tail - 3:tail - 3 + L, :]
        for j in range(1, CONV_WIDTH):
            conv = conv + cw_ref[j:j + 1, :] * x_scr[tail - 3 + j:tail - 3 + j + L, :]
        x_scr[0:tail, :] = x_scr[L:L + tail, :]
        qk = conv * _sigmoid(conv)
        cols = lambda base, h: slice(base + h * M_HEAD_DIM, base + (h + 1) * M_HEAD_DIM)
        lis = [s_ref[:, SM_IPRE + h:SM_IPRE + h + 1] for h in heads]
        if L % LANES == 0:
            gates_t = s_ref[...].T
            lf_all = _log_sigmoid(gates_t[SM_FPRE:SM_FPRE + M_HEADS, :])
            li_rows = [gates_t[SM_IPRE + h:SM_IPRE + h + 1, :] for h in heads]
            lf_rows = [lf_all[h:h + 1, :] for h in heads]
        else:
            li_rows = [_row_from_col(lis[h]) for h in heads]
            lf_rows = [_row_from_col(_log_sigmoid(s_ref[:, SM_FPRE + h:SM_FPRE + h + 1])) for h in heads]
        hs, ct_new, n_new, m_new = _mlstm_heads(
            [qk[:, cols(0, h)] * (M_HEAD_DIM ** -0.5) for h in heads],
            [qk[:, cols(M_WIDTH, h)] for h in heads],
            [p_ref[:, cols(2 * M_WIDTH, h)].astype(F32) for h in heads],
            lis, li_rows, lf_rows,
            [ct_scr[h] for h in heads], [n_scr[h:h + 1, :] for h in heads],
            [m_scr[h:h + 1, 0:1] for h in heads])
        for h in heads:
            ct_scr[h] = ct_new[h]
            n_scr[h:h + 1, :] = n_new[h]
            m_scr[h:h + 1, :] = jnp.broadcast_to(m_new[h], (1, LANES))
        mus = [jnp.mean(hs[h], axis=-1, keepdims=True) for h in heads]
        hcs = [hs[h] - mus[h] for h in heads]
        vars_ = [jnp.mean(hcs[h] * hcs[h], axis=-1, keepdims=True) for h in heads]
        for h in heads:
            o_gate = _sigmoid(p_ref[:, cols(3 * M_WIDTH, h)].astype(F32))
            o_ref[:, cols(0, h)] = (hcs[h] * lax.rsqrt(vars_[h] + 1e-5) * mg_ref[:, cols(0, h)]
                                    * o_gate).astype(o_ref.dtype)

    @pl.when(ci == 0)
    def _():
        ct_scr[...] = jnp.zeros_like(ct_scr)
        n_scr[...] = jnp.zeros_like(n_scr)
        m_scr[...] = jnp.zeros_like(m_scr)
        x_scr[...] = jnp.zeros_like(x_scr)
        run_chunk(pmm_ref, smm_ref, outm_ref)

    run_chunk(pm_ref, sm_ref, out_ref)


def _mlstm(pm, sm, conv_w, mnorm_g, l, batch, seq):
    n_real = batch * seq
    chunk = min(M_CHUNK, seq)
    nc = seq // chunk
    meta_blk = n_real // N_META
    out_real, out_meta = pl.pallas_call(
        _mlstm_kernel,
        grid=(batch, nc),
        in_specs=[pl.BlockSpec((chunk, 4 * M_WIDTH), lambda b, c: (b * nc + c, 0)),
                  pl.BlockSpec((chunk, LANES), lambda b, c: (b * nc + c, 0)),
                  pl.BlockSpec((N_META, 4 * M_WIDTH), lambda b, c: (meta_blk + b, 0)),
                  pl.BlockSpec((N_META, LANES), lambda b, c: (meta_blk + b, 0)),
                  _layer_block(conv_w, l), _layer_block(mnorm_g, l)],
        out_specs=[pl.BlockSpec((chunk, M_WIDTH), lambda b, c: (b * nc + c, 0)),
                   pl.BlockSpec((N_META, M_WIDTH), lambda b, c: (b, 0))],
        out_shape=[jax.ShapeDtypeStruct((n_real, M_WIDTH), BF16),
                   jax.ShapeDtypeStruct((batch * N_META, M_WIDTH), BF16)],
        scratch_shapes=[pltpu.VMEM((M_HEADS, M_HEAD_DIM, M_HEAD_DIM), F32),
                        pltpu.VMEM((8, LANES), F32),
                        pltpu.VMEM((8, LANES), F32),
                        pltpu.VMEM((chunk + 8, 2 * M_WIDTH), F32)],
        compiler_params=_params(("arbitrary", "arbitrary")),
        name="mlstm",
    )(pm, sm, pm, sm, conv_w, mnorm_g)
    return out_real, out_meta


def _rel_bucket_np(dist):
    n = np.maximum(dist, 0)
    nf = np.maximum(n, REL_MAX_EXACT).astype(np.float32)
    large = REL_MAX_EXACT + (np.log(nf / np.float32(REL_MAX_EXACT)) /
                             np.float32(math.log(REL_MAX_DIST / REL_MAX_EXACT))
                             * np.float32(REL_BUCKETS - REL_MAX_EXACT)).astype(np.int32)
    large = np.minimum(large, REL_BUCKETS - 1)
    return np.where(n < REL_MAX_EXACT, n, large).astype(np.int32)


def _bias_tables(rel_bias):
    q = np.arange(LANES)[:, None]
    k = np.arange(LANES)[None, :]
    far = 4 * LANES
    assert (_rel_bucket_np(np.arange(LANES + 1, far)) == REL_BUCKETS - 1).all()
    far_idx = np.full((LANES, LANES), REL_BUCKETS - 1, np.int32)
    near_idx = np.stack([_rel_bucket_np(q - k), _rel_bucket_np(LANES + q - k), far_idx])
    meta_idx = np.stack([_rel_bucket_np(q + N_META - np.minimum(k, N_META - 1)), far_idx])
    mq = np.arange(N_META)[:, None]
    mm_idx = _rel_bucket_np(mq - np.minimum(k, N_META - 1))
    rb = rel_bias.astype(F32)

    def lookup(idx, values):
        onehot = jnp.asarray(idx[..., None] == np.arange(REL_BUCKETS), F32)
        out = jnp.einsum('...b,bh->...h', onehot, values, precision=lax.Precision.HIGHEST)
        return jnp.moveaxis(out, -1, -3)

    rel_values = (rb - rb[REL_BUCKETS - 1][None, :]) * LOG2E
    return lookup(near_idx, rel_values), lookup(meta_idx, rel_values), lookup(mm_idx, rb)


def _dsa_kernel(top_k, qa_ref, qi_ref, sm_ref, cb_ref, smb_ref, cm_ref, qam_ref,
                wuk_ref, wuv_ref, near_ref, metab_ref, mmb_ref,
                out_ref, outm_ref,
                keys_scr, hi_scr, lo_scr, lg_scr, lgm_scr, rawa_scr, rawb_scr, pa_scr, pb_scr,
                caug_scr, ct_scr, kt_scr, cmaug_scr, cmt_scr, qs_scr, mx_scr, mrep_scr, acc_scr):
    i = pl.program_id(1)
    T = LANES
    H = A_HEADS
    col = lax.broadcasted_iota(jnp.int32, (T, T), 1)
    row = lax.broadcasted_iota(jnp.int32, (T, T), 0)
    nt = (((1,), (1,)), ((), ()))

    def q_latent(qa, hd, scale):
        ql = jnp.dot(qa[:, hd * A_HEAD_DIM:(hd + 1) * A_HEAD_DIM], wuk_ref[hd],
                     preferred_element_type=F32)
        return (ql * scale).astype(BF16)

    def ones_column(n):
        return jnp.where(lax.broadcasted_iota(jnp.int32, (n, T), 1) == 0, 1.0, 0.0).astype(BF16)

    @pl.when(i == 0)
    def _():
        caug_scr[:, 0:KV_RANK] = cb_ref[...].astype(BF16)
        caug_scr[:, KV_RANK:KV_RANK + T] = ones_column(caug_scr.shape[0])

        def transpose_keys(blk, carry):
            rows = pl.ds(pl.multiple_of(blk * KEY_SUB * T, KEY_SUB * T), KEY_SUB * T)
            ct_scr[blk] = cb_ref[rows, :].T.astype(BF16)
            kt_scr[blk] = smb_ref[rows, :].T[SM_KIDX:SM_KIDX + IDX_DIM, :].astype(BF16)
            return carry

        lax.fori_loop(0, ct_scr.shape[0], transpose_keys, 0)
        cm_pad = jnp.concatenate([cm_ref[...], jnp.zeros((T - N_META, KV_RANK), F32)], axis=0)
        cmaug_scr[:, 0:KV_RANK] = cm_pad.astype(BF16)
        cmt_scr[...] = cm_pad.T.astype(BF16)
        cmaug_scr[:, KV_RANK:KV_RANK + T] = ones_column(T)
        cmk = cmaug_scr[:, 0:KV_RANK]
        qam = qam_ref[...]
        mrow = lax.broadcasted_iota(jnp.int32, (N_META, T), 0)
        mcol = lax.broadcasted_iota(jnp.int32, (N_META, T), 1)
        for hd in range(H):
            lg = lax.dot_general(q_latent(qam, hd, A_HEAD_DIM ** -0.5), cmk, nt,
                                 preferred_element_type=F32) + mmb_ref[hd]
            lg = jnp.where(mcol <= mrow, lg, NEG_BIG)
            p = jnp.exp(lg - jnp.max(lg, axis=1, keepdims=True))
            p = p / jnp.sum(p, axis=1, keepdims=True)
            o = jnp.dot(p.astype(BF16), cmk, preferred_element_type=F32)
            outm_ref[:, hd * A_HEAD_DIM:(hd + 1) * A_HEAD_DIM] = jnp.dot(
                o.astype(BF16), wuv_ref[hd], preferred_element_type=F32).astype(outm_ref.dtype)

    qa = qa_ref[...]
    for hd in range(H):
        qs_scr[hd * T:(hd + 1) * T, :] = q_latent(qa, hd, A_HEAD_DIM ** -0.5 * LOG2E)
    qi = qi_ref[...]
    wv = sm_ref[:, SM_WIDX:SM_WIDX + IDX_HEADS] * IDX_SCALE
    t_col = i * T + lax.broadcasted_iota(jnp.int32, (T, 1), 0)
    n_chunks = (i + SCORE_CHUNK) // SCORE_CHUNK
    CW = SCORE_CHUNK * T

    def run_pipelined(n, produce, consume, buf_a, buf_b):
        last = n - 1
        produce(0, buf_a)

        def body(t, carry):
            s = 2 * t
            produce(jnp.minimum(s + 1, last), buf_b)
            consume(s, buf_a)
            produce(jnp.minimum(s + 2, last), buf_a)
            consume(s + 1, buf_b)
            return carry

        lax.fori_loop(0, n // 2, body, 0)

        @pl.when(n % 2 == 1)
        def _():
            consume(last, buf_a)

    def score_matmul(cix, buf):
        kc_t = kt_scr[cix]
        for hh in range(IDX_HEADS):
            buf[hh * T:(hh + 1) * T, :] = jnp.dot(
                qi[:, hh * IDX_DIM:(hh + 1) * IDX_DIM], kc_t, preferred_element_type=F32)

    def score_keys(cix, buf):
        acc = jnp.zeros((T, CW), F32)
        for hh in range(IDX_HEADS):
            acc = acc + wv[:, hh:hh + 1] * jnp.maximum(buf[hh * T:(hh + 1) * T, :], 0.0)
        acc = jnp.where(acc == 0.0, 0.0, acc)
        bits = lax.bitcast_convert_type(acc, jnp.int32)
        key = jnp.where(bits < 0, bits ^ jnp.int32(0x7FFFFFFF), bits)
        s_idx = cix * CW + lax.broadcasted_iota(jnp.int32, (T, CW), 1)
        key = jnp.where(s_idx <= t_col, key, jnp.int32(INT_MIN))
        for u in range(SCORE_CHUNK):
            tile = key[:, u * T:(u + 1) * T]
            keys_scr[cix * SCORE_CHUNK + u] = tile
            tile_t = tile.T
            hi_scr[cix * SCORE_CHUNK + u] = lax.shift_right_arithmetic(tile_t, 16).astype(jnp.int16)
            lo_scr[cix * SCORE_CHUNK + u] = ((tile_t & 0xFFFF) - HALF_BIAS).astype(jnp.int16)

    run_pipelined(n_chunks, score_matmul, score_keys, rawa_scr, rawb_scr)

    def rep16(row_i32):
        return jnp.broadcast_to(row_i32, (T, T)).astype(jnp.int16)

    def count16(src_scr, pred_fn):
        def body(cix, cnt):
            for u in range(SCORE_CHUNK):
                hit = pred_fn(src_scr[cix * SCORE_CHUNK + u])
                cnt = cnt + jnp.where(hit, jnp.int16(1), jnp.int16(0))
            return cnt
        cnt = lax.fori_loop(0, n_chunks, body, jnp.zeros((T, T), jnp.int16))
        return jnp.sum(cnt.astype(F32), axis=0, keepdims=True)

    def search16(src_scr, k_row):
        def bit_body(bi, carry):
            ans, above = carry
            cand_u = ans | lax.shift_left(jnp.int32(1), jnp.int32(15) - bi)
            cand = rep16(cand_u - HALF_BIAS)
            total = count16(src_scr, lambda x: x >= cand)
            take = total >= k_row
            return jnp.where(take, cand_u, ans), jnp.where(take, above, total)
        return lax.fori_loop(0, 16, bit_body, (jnp.zeros((1, T), jnp.int32), jnp.zeros((1, T), F32)))

    k_row = jnp.full((1, T), float(top_k), F32)
    hi_u, hi_above = search16(hi_scr, k_row)
    hi_s = hi_u - HALF_BIAS
    hi_rep = rep16(hi_s)
    k_low = k_row - hi_above

    def band_body(cix, carry):
        for u in range(SCORE_CHUNK):
            j = cix * SCORE_CHUNK + u
            lo_scr[j] = jnp.where(hi_scr[j] == hi_rep, lo_scr[j], jnp.int16(-HALF_BIAS))
        return carry

    lax.fori_loop(0, n_chunks, band_body, 0)
    lo_u, lo_above = search16(lo_scr, k_low)
    need_row = k_low - lo_above
    thr_row = lax.shift_left(hi_s, 16) | lo_u
    thr = jnp.broadcast_to(thr_row, (T, T)).T
    need = jnp.broadcast_to(need_row, (T, T)).T

    mx_scr[...] = jnp.full_like(mx_scr, NEG_BIG)
    acc_scr[...] = jnp.zeros_like(acc_scr)
    hg = H // ATT_GROUPS
    groups = [slice(g * hg * T, (g + 1) * hg * T) for g in range(ATT_GROUPS)]
    meta_sel = jnp.minimum(i, 1)

    def max_pass(ck_t, madds, bias_fns, store):
        lgs = [jnp.dot(qs_scr[rs, :], ck_t, preferred_element_type=F32) for rs in groups]
        for hd in range(H):
            rs = slice(hd * T, (hd + 1) * T)
            lo = (hd % hg) * T
            mx = mx_scr[rs, :]
            for u in range(len(madds)):
                x = lgs[hd // hg][lo:lo + T, u * T:(u + 1) * T] + madds[u]
                if bias_fns[u] is not None:
                    x = x + bias_fns[u](hd)
                store(rs, u, x)
                mx = jnp.maximum(mx, x)
            mx_scr[rs, :] = mx

    def sum_pass(c_aug, n_sub, load):
        for grp in groups:
            m_rep = mrep_scr[grp, :]
            ph = [jnp.exp2(load(grp, u) - m_rep).astype(BF16) for u in range(n_sub)]
            p = ph[0] if n_sub == 1 else jnp.concatenate(ph, axis=1)
            acc_scr[grp, :] += jnp.dot(p, c_aug, preferred_element_type=F32)

    def key_rows(step):
        return pl.ds(pl.multiple_of(step * KEY_SUB * T, KEY_SUB * T), KEY_SUB * T)

    upper = (row < col).astype(BF16)

    def mask_step(step, seen, near):
        madds, bias_fns = [], []
        for u in range(KEY_SUB):
            j = KEY_SUB * step + u
            kk = keys_scr[j]
            eq = kk == thr
            eqf = jnp.where(eq, 1.0, 0.0)
            before = jnp.dot(eqf.astype(BF16), upper, preferred_element_type=F32) + seen
            sel = (kk > thr) | (eq & (before < need))
            if near:
                sel = sel & ((j * T + col) <= (i * T + row))
                dsel = jnp.clip(i - j, 0, 2)
                bias_fns.append(lambda hd, dsel=dsel: near_ref[dsel, hd])
            else:
                bias_fns.append(None)
            seen = seen + jnp.sum(eqf, axis=1, keepdims=True)
            madds.append(jnp.where(sel, 0.0, NEG_BIG))

        def store(rs, u, x):
            lg_scr[step, rs, u * T:(u + 1) * T] = x

        max_pass(ct_scr[step], madds, bias_fns, store)
        return seen

    def weights_step(step, buf):
        for grp in groups:
            m_rep = mrep_scr[grp, :]
            for u in range(KEY_SUB):
                buf[grp, u * T:(u + 1) * T] = jnp.exp2(lg_scr[step, grp, u * T:(u + 1) * T] - m_rep)

    def accumulate_step(step, buf):
        c_aug = caug_scr[key_rows(step), :]
        for grp in groups:
            acc_scr[grp, :] += jnp.dot(buf[grp, :].astype(BF16), c_aug, preferred_element_type=F32)

    def store_meta(rs, u, x):
        lgm_scr[rs, :] = x

    n_far = jnp.maximum(i - 1, 0) // KEY_SUB
    n_steps = (i + KEY_SUB) // KEY_SUB
    max_pass(cmt_scr[...], [jnp.where(col < N_META, 0.0, NEG_BIG)],
             [lambda hd: metab_ref[meta_sel, hd]], store_meta)
    seen = lax.fori_loop(0, n_far, lambda s, c: mask_step(s, c, False), jnp.zeros((T, 1), F32))
    lax.fori_loop(n_far, n_steps, lambda s, c: mask_step(s, c, True), seen)
    mrep_scr[...] = jnp.broadcast_to(jnp.max(mx_scr[...], axis=1, keepdims=True), mrep_scr.shape)
    sum_pass(cmaug_scr[...], 1, lambda grp, u: lgm_scr[grp, :])
    run_pipelined(n_steps, weights_step, accumulate_step, pa_scr, pb_scr)

    for hd in range(H):
        rs = slice(hd * T, (hd + 1) * T)
        o = acc_scr[rs, 0:KV_RANK] / acc_scr[rs, KV_RANK:KV_RANK + 1]
        out_ref[:, hd * A_HEAD_DIM:(hd + 1) * A_HEAD_DIM] = jnp.dot(
            o.astype(BF16), wuv_ref[hd], preferred_element_type=F32).astype(out_ref.dtype)


def _dsa(qa, qi, sm, c, wuk_t, wuv, tables, l, batch, seq):
    n_real = batch * seq
    nq = seq // LANES
    n_tiles = ((nq + SCORE_CHUNK - 1) // SCORE_CHUNK) * SCORE_CHUNK
    top_k = min(TOPK_MAX, seq // 4)
    meta_blk = n_real // N_META
    near, metab, mmb = tables
    full = lambda a: pl.BlockSpec(a.shape, lambda b, i: (0,) * a.ndim)
    assert seq % (SCORE_CHUNK * LANES) == 0 and nq % KEY_SUB == 0 and SCORE_CHUNK == KEY_SUB
    out_real, out_meta = pl.pallas_call(
        functools.partial(_dsa_kernel, top_k),
        grid=(batch, nq),
        in_specs=[pl.BlockSpec((LANES, A_WIDTH), lambda b, i: (b * nq + i, 0)),
                  pl.BlockSpec((LANES, IDX_HEADS * IDX_DIM), lambda b, i: (b * nq + i, 0)),
                  pl.BlockSpec((LANES, LANES), lambda b, i: (b * nq + i, 0)),
                  pl.BlockSpec((seq, KV_RANK), lambda b, i: (b, 0)),
                  pl.BlockSpec((seq, LANES), lambda b, i: (b, 0)),
                  pl.BlockSpec((N_META, KV_RANK), lambda b, i: (meta_blk + b, 0)),
                  pl.BlockSpec((N_META, A_WIDTH), lambda b, i: (meta_blk + b, 0)),
                  _layer_block(wuk_t, l), _layer_block(wuv, l), full(near), full(metab), full(mmb)],
        out_specs=[pl.BlockSpec((LANES, A_WIDTH), lambda b, i: (b * nq + i, 0)),
                   pl.BlockSpec((N_META, A_WIDTH), lambda b, i: (b, 0))],
        out_shape=[jax.ShapeDtypeStruct((n_real, A_WIDTH), BF16),
                   jax.ShapeDtypeStruct((batch * N_META, A_WIDTH), BF16)],
        scratch_shapes=[pltpu.VMEM((n_tiles, LANES, LANES), jnp.int32),
                        pltpu.VMEM((n_tiles, LANES, LANES), jnp.int16),
                        pltpu.VMEM((n_tiles, LANES, LANES), jnp.int16),
                        pltpu.VMEM((nq // KEY_SUB, A_HEADS * LANES, KEY_SUB * LANES), F32),
                        pltpu.VMEM((A_HEADS * LANES, LANES), F32),
                        pltpu.VMEM((IDX_HEADS * LANES, SCORE_CHUNK * LANES), F32),
                        pltpu.VMEM((IDX_HEADS * LANES, SCORE_CHUNK * LANES), F32),
                        pltpu.VMEM((A_HEADS * LANES, KEY_SUB * LANES), F32),
                        pltpu.VMEM((A_HEADS * LANES, KEY_SUB * LANES), F32),
                        pltpu.VMEM((seq, KV_RANK + LANES), BF16),
                        pltpu.VMEM((nq // KEY_SUB, KV_RANK, KEY_SUB * LANES), BF16),
                        pltpu.VMEM((nq // KEY_SUB, IDX_DIM, KEY_SUB * LANES), BF16),
                        pltpu.VMEM((LANES, KV_RANK + LANES), BF16),
                        pltpu.VMEM((KV_RANK, LANES), BF16),
                        pltpu.VMEM((A_HEADS * LANES, KV_RANK), BF16),
                        pltpu.VMEM((A_HEADS * LANES, LANES), F32),
                        pltpu.VMEM((A_HEADS * LANES, LANES), F32),
                        pltpu.VMEM((A_HEADS * LANES, KV_RANK + LANES), F32)],
        compiler_params=_params(("arbitrary", "arbitrary")),
        name="dsa",
    )(qa, qi, sm, c, sm, c, qa, wuk_t, wuv, near, metab, mmb)
    return out_real, out_meta


def _merge_kernel(alpha, full_tiles, h_ref, hm_ref, hmt_ref, ha_ref, hat_ref, g_ref, wbm_ref, wba_ref, wo_ref,
                  lg_ref, lb_ref, wr_ref, br_ref, h1_ref, comb_ref, bgt_ref, cnt_ref):
    d = h_ref.shape[1]
    gm = _sigmoid(g_ref[:, 0:d].astype(F32))
    ga = _sigmoid(g_ref[:, d:2 * d].astype(F32))
    in_tail = pl.program_id(0) >= full_tiles
    hm = jnp.where(in_tail, hmt_ref[...], hm_ref[...])
    ha = jnp.where(in_tail, hat_ref[...], ha_ref[...])
    y = gm * jnp.dot(hm, wbm_ref[...], preferred_element_type=F32) + \
        ga * jnp.dot(ha, wba_ref[...], preferred_element_type=F32)
    z = alpha * h_ref[...] + jnp.dot(y.astype(BF16), wo_ref[...], preferred_element_type=F32)
    h1 = _layer_norm_rows(z, lg_ref[...], lb_ref[...], 1e-5)
    h1_ref[...] = h1

    tm = h1.shape[0]
    logits_t = lax.dot_general(wr_ref[...], h1.astype(BF16), (((1,), (1,)), ((), ())),
                               preferred_element_type=F32)
    scores = _sigmoid(logits_t[0:N_EXPERTS, :])
    sel = scores + br_ref[0:N_EXPERTS, :]
    best = None
    for gidx in range(N_GROUPS):
        r0, r1, r2, r3 = (sel[gidx * GROUP_SIZE + u:gidx * GROUP_SIZE + u + 1, :] for u in range(4))
        a, b = jnp.maximum(r0, r1), jnp.minimum(r0, r1)
        c, dd = jnp.maximum(r2, r3), jnp.minimum(r2, r3)
        gs = jnp.maximum(a, c) + jnp.maximum(jnp.minimum(a, c), jnp.maximum(b, dd))
        if best is None:
            best, bg = gs, jnp.zeros((1, tm), jnp.int32)
        else:
            upd = gs > best
            bg = jnp.where(upd, gidx, bg)
            best = jnp.where(upd, gs, best)
    eidx = lax.broadcasted_iota(jnp.int32, (N_EXPERTS, tm), 0)
    masked = jnp.where((eidx // GROUP_SIZE) == bg, sel, -jnp.inf)
    v1 = jnp.max(masked, axis=0, keepdims=True)
    i1 = jnp.min(jnp.where(masked == v1, eidx, N_EXPERTS), axis=0, keepdims=True)
    masked2 = jnp.where(eidx == i1, -jnp.inf, masked)
    v2 = jnp.max(masked2, axis=0, keepdims=True)
    i2 = jnp.min(jnp.where(masked2 == v2, eidx, N_EXPERTS), axis=0, keepdims=True)
    s1 = jnp.sum(jnp.where(eidx == i1, scores, 0.0), axis=0, keepdims=True)
    s2 = jnp.sum(jnp.where(eidx == i2, scores, 0.0), axis=0, keepdims=True)
    tot = s1 + s2
    comb_t = jnp.where(eidx == i1, s1 / tot, 0.0) + jnp.where(eidx == i2, s2 / tot, 0.0)
    comb_pad = jnp.concatenate([comb_t, jnp.zeros((LANES - N_EXPERTS, tm), F32)], axis=0)
    comb_ref[...] = comb_pad.T
    bgt_ref[...] = jnp.broadcast_to(bg, (8, tm))
    gidx8 = lax.broadcasted_iota(jnp.int32, (8, tm), 0)
    counts = jnp.sum(jnp.where(gidx8 == bg, 1.0, 0.0), axis=1, keepdims=True)
    cnt_ref[0] = jnp.broadcast_to(counts, (8, LANES)).astype(jnp.int32)


def _branch_tail(real, meta, n_pad, tm):
    start = (real.shape[0] // tm) * tm
    return _with_meta(real[start:], meta, n_pad - start)


def _merge(h, hm_real, hm_meta, ha_real, ha_meta, g, w_bm, w_ba, w_o, ln_g, ln_b, wr_t, br, l, alpha):
    n, d = h.shape
    tm = MOE_ROW_TILE
    full_tiles = hm_real.shape[0] // tm
    assert full_tiles >= 1
    hm_tail, ha_tail = _branch_tail(hm_real, hm_meta, n, tm), _branch_tail(ha_real, ha_meta, n, tm)
    row = lambda width: pl.BlockSpec((tm, width), lambda r: (r, 0))
    body = lambda width: pl.BlockSpec((tm, width), lambda r: (jnp.minimum(r, full_tiles - 1), 0))
    tail = lambda width: pl.BlockSpec((tm, width), lambda r: (jnp.maximum(r - full_tiles, 0), 0))
    full = lambda a: pl.BlockSpec(a.shape, lambda r: (0,) * a.ndim)
    args = (h, hm_real, hm_tail, ha_real, ha_tail, g, w_bm, w_ba, w_o, ln_g, ln_b, wr_t, br)
    return pl.pallas_call(
        functools.partial(_merge_kernel, alpha, full_tiles),
        grid=(n // tm,),
        in_specs=[row(d), body(M_WIDTH), tail(M_WIDTH), body(A_WIDTH), tail(A_WIDTH), row(2 * d)] +
                 [_layer_block(a, l) for a in args[6:11]] + [full(wr_t), full(br)],
        out_specs=[row(d), row(LANES),
                   pl.BlockSpec((8, tm), lambda r: (0, r)),
                   pl.BlockSpec((1, 8, LANES), lambda r: (r, 0, 0))],
        out_shape=[jax.ShapeDtypeStruct((n, d), F32), jax.ShapeDtypeStruct((n, LANES), F32),
                   jax.ShapeDtypeStruct((8, n), jnp.int32),
                   jax.ShapeDtypeStruct((n // tm, 8, LANES), jnp.int32)],
        compiler_params=_params(("parallel",)),
        name="merge",
    )(*args)


def _moe_kernel(alpha, cap, cnt_ref, h_ref, comb_ref, bgt_ref, wg_ref, wu_ref, wd_ref, lg_ref, lb_ref,
                out_ref, xb_scr, cs_scr, yt_scr, tri_scr):
    r = pl.program_id(0)
    g = pl.program_id(1)
    rows = h_ref.shape[0]

    @pl.when((r == 0) & (g == 0))
    def _():
        t0 = lax.broadcasted_iota(jnp.int32, (rows, rows), 0)
        t1 = lax.broadcasted_iota(jnp.int32, (rows, rows), 1)
        tri_scr[...] = (t0 < t1).astype(BF16)

    @pl.when(g == 0)
    def _():
        xb_scr[...] = h_ref[...].astype(BF16)
        yt_scr[...] = jnp.zeros_like(yt_scr)
        c = comb_ref[...]
        for part in range(2):
            cb = c.astype(BF16)
            cs_scr[part] = cb
            c = c - cb.astype(F32)

    member = bgt_ref[0:1, :] == g
    mem8 = jnp.broadcast_to(jnp.where(member, 1.0, 0.0), (8, rows)).astype(BF16)
    rank = jnp.dot(mem8, tri_scr[...], preferred_element_type=F32)[0:1, :].astype(jnp.int32)
    n_blocks = (cnt_ref[r * N_GROUPS + g] + cap - 1) // cap
    lane = lax.broadcasted_iota(jnp.int32, (cap, LANES), 1)
    tn = (((0,), (0,)), ((), ()))

    def block(b, carry):
        slot = lax.broadcasted_iota(jnp.int32, (cap, rows), 0) + b * cap
        onehot = jnp.where(member & (rank == slot), 1.0, 0.0).astype(BF16)
        xg = jnp.dot(onehot, xb_scr[...], preferred_element_type=F32).astype(BF16)
        cw = jnp.dot(onehot, cs_scr[0], preferred_element_type=F32)
        cw = cw + jnp.dot(onehot, cs_scr[1], preferred_element_type=F32)
        y = jnp.zeros((cap, out_ref.shape[1]), F32)
        for e in range(GROUP_SIZE):
            gate = jnp.dot(xg, wg_ref[e], preferred_element_type=F32)
            up = jnp.dot(xg, wu_ref[e], preferred_element_type=F32)
            he = gate * _sigmoid(gate) * up
            o = jnp.dot(he.astype(BF16), wd_ref[e], preferred_element_type=F32)
            ce = jnp.sum(jnp.where(lane == g * GROUP_SIZE + e, cw, 0.0), axis=1, keepdims=True)
            y = y + ce * o
        yt_scr[...] += lax.dot_general(onehot, y.astype(BF16), tn, preferred_element_type=F32)
        return carry

    lax.fori_loop(0, n_blocks, block, 0)

    @pl.when(g == pl.num_programs(1) - 1)
    def _():
        z = alpha * h_ref[...] + yt_scr[...]
        out_ref[...] = _layer_norm_rows(z, lg_ref[...], lb_ref[...], 1e-5)


def _moe(h, comb, bgt, counts, w_gate, w_up, w_down, ln_g, ln_b, l, alpha):
    n, d = h.shape
    de = w_gate.shape[-1]
    tm = MOE_ROW_TILE
    cnt = counts[:, 0:N_GROUPS, 0].reshape(-1)
    grid_spec = pltpu.PrefetchScalarGridSpec(
        num_scalar_prefetch=1,
        grid=(n // tm, N_GROUPS),
        in_specs=[pl.BlockSpec((tm, d), lambda r, g, c: (r, 0)),
                  pl.BlockSpec((tm, LANES), lambda r, g, c: (r, 0)),
                  pl.BlockSpec((8, tm), lambda r, g, c: (0, r)),
                  pl.BlockSpec((None, GROUP_SIZE, d, de), lambda r, g, c: (l, g, 0, 0)),
                  pl.BlockSpec((None, GROUP_SIZE, d, de), lambda r, g, c: (l, g, 0, 0)),
                  pl.BlockSpec((None, GROUP_SIZE, de, d), lambda r, g, c: (l, g, 0, 0)),
                  _layer_block(ln_g, l), _layer_block(ln_b, l)],
        out_specs=pl.BlockSpec((tm, d), lambda r, g, c: (r, 0)),
        scratch_shapes=[pltpu.VMEM((tm, d), BF16), pltpu.VMEM((2, tm, LANES), BF16),
                        pltpu.VMEM((tm, d), F32), pltpu.VMEM((tm, tm), BF16)])
    return pl.pallas_call(
        functools.partial(_moe_kernel, alpha, MOE_CAP),
        grid_spec=grid_spec,
        out_shape=jax.ShapeDtypeStruct((n, d), F32),
        compiler_params=_params(("arbitrary", "arbitrary")),
        name="moe",
    )(cnt, h, comb, bgt, w_gate, w_up, w_down, ln_g, ln_b)


def _with_meta(real, meta, n_pad):
    pad = n_pad - real.shape[0] - meta.shape[0]
    return jnp.concatenate([real, meta, jnp.zeros((pad, real.shape[1]), real.dtype)], axis=0)


def kernel(x, meta_tokens, ln_in_g, ln_in_b, w_in, conv_w, b_if, mnorm_g, kv_norm_g, w_uk, w_uv,
           w_branch_m, w_branch_a, w_out, ln1_g, ln1_b, w_router, b_router, w_gate, w_up, w_down,
           ln2_g, ln2_b, rel_bias):
    batch, seq, d = x.shape
    depth = w_in.shape[0]
    alpha = (2 * depth) ** 0.25
    n_real = batch * seq
    n_meta = batch * N_META
    tile = math.lcm(ROW_TILE, MOE_ROW_TILE)
    n_pad = -(-(n_real + n_meta) // tile) * tile
    assert seq % LANES == 0 and n_real % N_META == 0

    assert n_real % ROW_TILE == 0
    tail = _with_meta(jnp.tile(meta_tokens.astype(x.dtype), (batch, 1)), jnp.zeros((0, d), x.dtype),
                      n_pad - n_real)
    h = _input_ln(x.reshape(n_real, d), tail, ln_in_g, ln_in_b)
    tables = _bias_tables(rel_bias)
    row3 = lambda a: a.astype(F32)[:, None, :]
    w_packed, brow, kv_g = _pack_w_in(w_in), _pack_b_if(b_if), row3(kv_norm_g)
    conv_f, mnorm = conv_w.astype(F32), row3(mnorm_g)
    wuk_t = jnp.swapaxes(w_uk, 2, 3).astype(BF16)
    wuv = w_uv.astype(BF16)
    w_bm, w_ba, w_o = w_branch_m.astype(BF16), w_branch_a.astype(BF16), w_out.astype(BF16)
    wr_t = jnp.zeros((LANES, d), F32).at[0:N_EXPERTS].set(w_router.T).astype(BF16)
    br = jnp.zeros((LANES, 1), F32).at[0:N_EXPERTS, 0].set(b_router)
    wg, wu, wd = w_gate.astype(BF16), w_up.astype(BF16), w_down.astype(BF16)
    g1, b1, g2, b2 = row3(ln1_g), row3(ln1_b), row3(ln2_g), row3(ln2_b)
    for l in range(depth):
        pm, qa, qi, c, sm, g = _project(h, w_packed, kv_g, brow, l)
        hm_real, hm_meta = _mlstm(pm, sm, conv_f, mnorm, l, batch, seq)
        ha_real, ha_meta = _dsa(qa, qi, sm, c, wuk_t, wuv, tables, l, batch, seq)
        h1, comb, bgt, counts = _merge(h, hm_real, hm_meta, ha_real, ha_meta, g, w_bm, w_ba, w_o, g1, b1,
                                       wr_t, br, l, alpha)
        h = _moe(h1, comb, bgt, counts, wg, wu, wd, g2, b2, l, alpha)
    return h[:n_real].reshape(batch, seq, d)
```
